```python
import math
import jax, jax.numpy as jnp
from jax import lax
import numpy as np

D_MODEL = 2048
BATCH = 4
SEQ = 2048
DEPTH = 1
DEC_BATCH = 128
DEC_SEQ = 1
PAST_LEN = 16384
PAGE_SIZE = 128

CONV_W = D_MODEL // 2
CONV_K = 31
RWKV_W = D_MODEL - CONV_W
RWKV_HEAD = 64
RWKV_H = RWKV_W // RWKV_HEAD
DECAY_LORA = 64
AAA_LORA = 64
GATE_LORA = 160
SHIFT_W = 3 * RWKV_W + DECAY_LORA + AAA_LORA + GATE_LORA
IN_W = 2 * CONV_W + SHIFT_W
N_MEM = 256
X_HEADS = 4
X_HEAD = D_MODEL // X_HEADS
N_GROUPS = 4
EXP_PER_GROUP = 8
TOP_K_IN_GROUP = 2
D_EXPERT = D_MODEL // 4
RMS_EPS = 1e-6
LN_EPS = 1e-5
GN_EPS = 64e-5
DECAY_SCALE = math.exp(-0.5)

kernel_name = 'hymba_conformer_rwkv7_hmoe_step'

F32 = jnp.float32


def rmsnorm(x, g):
    xf = x.astype(F32)
    y = xf * lax.rsqrt(jnp.mean(xf * xf, axis=-1, keepdims=True) + RMS_EPS)
    return (y * g.astype(F32)).astype(x.dtype)


def conformer_conv(u, conv_buf, conv_w, conv_b, ln_g, ln_b):
    u_full = jnp.concatenate([conv_buf.astype(u.dtype), u], axis=1)
    c = lax.conv_general_dilated(
        u_full, conv_w[:, None, :].astype(u.dtype), (1,), 'VALID',
        dimension_numbers=('NWC', 'WIO', 'NWC'), feature_group_count=CONV_W)
    cf = (c + conv_b).astype(F32)
    mu = jnp.mean(cf, axis=-1, keepdims=True)
    var = jnp.mean(jnp.square(cf - mu), axis=-1, keepdims=True)
    cf = (cf - mu) * lax.rsqrt(var + LN_EPS) * ln_g.astype(F32) + ln_b.astype(F32)
    return jax.nn.silu(cf).astype(u.dtype), u_full[:, -(CONV_K - 1):]


def rwkv7_recurrence(S0, r, w, k, v, kk, a):
    def step(S, inp):
        r_t, w_t, k_t, v_t, kk_t, a_t = inp
        sa = jnp.einsum('bhvk,bhk->bhv', S, kk_t)
        S = (S * w_t[:, :, None, :]
             - sa[..., None] * (kk_t * a_t)[:, :, None, :]
             + v_t[..., None] * k_t[:, :, None, :])
        return S, jnp.einsum('bhvk,bhk->bhv', S, r_t)
    xs = tuple(jnp.moveaxis(t, 1, 0) for t in (r, w, k, v, kk, a))
    S, o = lax.scan(step, S0, xs)
    return S, jnp.moveaxis(o, 0, 1)


def rwkv7_mix(q, shift_buf, S0, p):
    B, T, _ = q.shape
    q_prev = jnp.concatenate([shift_buf[:, None, :].astype(q.dtype), q[:, :-1]], axis=1)
    qs = q + (q_prev - q) * p['shift_mu']
    splits = np.cumsum([RWKV_W, RWKV_W, RWKV_W, DECAY_LORA, AAA_LORA]).tolist()
    r, k, v, pw, pa, pg = jnp.split(qs, splits, axis=-1)
    hd = lambda t: t.reshape(B, T, RWKV_H, RWKV_HEAD)
    decay = jnp.exp(-DECAY_SCALE * jax.nn.sigmoid(
        (p['decay_bias'] + jnp.tanh(pw) @ p['w_decay_up']).astype(F32)))
    a = jax.nn.sigmoid((p['a_bias'] + pa @ p['w_a_up']).astype(F32))
    g = jax.nn.sigmoid(pg) @ p['w_g_up']
    r = r.astype(F32)
    k = k.astype(F32)
    v = v.astype(F32)
    kk = hd(k * p['k_k'].astype(F32))
    kk = kk / jnp.maximum(jnp.sqrt(jnp.sum(kk * kk, axis=-1, keepdims=True)), 1e-12)
    k = k * (1.0 + (a - 1.0) * p['k_a'].astype(F32))
    S, o = rwkv7_recurrence(S0, hd(r), hd(decay), hd(k), hd(v), kk, hd(a))
    mu = jnp.mean(o, axis=-1, keepdims=True)
    var = jnp.mean(jnp.square(o - mu), axis=-1, keepdims=True)
    o = ((o - mu) * lax.rsqrt(var + GN_EPS)).reshape(B, T, RWKV_W)
    o = o * p['lnx_g'].astype(F32) + p['lnx_b'].astype(F32)
    bonus = jnp.sum(hd(r) * hd(k) * p['r_k'].astype(F32), axis=-1, keepdims=True) * hd(v)
    o = (o + bonus.reshape(B, T, RWKV_W)) * g.astype(F32)
    return o.astype(q.dtype), S, q[:, -1]


def hier_moe(h, p):
    hf = h.astype(F32)
    lg = hf @ p['w_route_group'].astype(F32) + p['b_route_group'].astype(F32)
    pg = jax.nn.softmax(lg, axis=-1)
    g_val, g_idx = lax.top_k(pg, 1)
    le = jnp.einsum('nd,dge->nge', hf, p['w_route_expert'].astype(F32)) + p['b_route_expert'].astype(F32)
    le_sel = jnp.take_along_axis(le, g_idx[:, :, None], axis=1)[:, 0]
    e_val, e_idx = lax.top_k(le_sel, TOP_K_IN_GROUP)
    e_w = jax.nn.softmax(e_val, axis=-1) * g_val
    e_comb = jnp.sum(jax.nn.one_hot(e_idx, EXP_PER_GROUP, dtype=F32) * e_w[..., None], axis=1)
    combine = jax.nn.one_hot(g_idx[:, 0], N_GROUPS, dtype=F32)[:, :, None] * e_comb[:, None, :]
    out = jnp.zeros(h.shape, F32)
    for gi in range(N_GROUPS):
        hg = jnp.einsum('nd,edf->nef', h, p['w_gate'][gi])
        hu = jnp.einsum('nd,edf->nef', h, p['w_up'][gi])
        act = jax.nn.silu(hg) * hu * combine[:, gi, :, None].astype(h.dtype)
        out = out + jnp.einsum('nef,efd->nd', act, p['w_down'][gi]).astype(F32)
    return out.astype(h.dtype)


def layer(x, mem_k, mem_v, conv_buf, shift_buf, S0, p):
    B, T, _ = x.shape
    h = rmsnorm(x, p['norm_mix'])
    proj = h @ p['w_in']
    u = proj[..., :CONV_W] * jax.nn.sigmoid(proj[..., CONV_W:2 * CONV_W])
    c, new_conv = conformer_conv(u, conv_buf, p['conv_w'], p['conv_b'], p['conv_ln_g'], p['conv_ln_b'])
    o, new_S, new_shift = rwkv7_mix(proj[..., 2 * CONV_W:], shift_buf, S0, p)
    x = x + jnp.concatenate([c, o], axis=-1) @ p['w_out']
    q = (rmsnorm(x, p['norm_x']) @ p['w_cq']).reshape(B, T, X_HEADS, X_HEAD)
    s = jnp.einsum('bthd,bmhd->bhtm', q.astype(F32), mem_k.astype(F32)) * (X_HEAD ** -0.5)
    att = jax.nn.softmax(s, axis=-1)
    ctx = jnp.einsum('bhtm,bmhd->bthd', att, mem_v.astype(F32)).astype(x.dtype).reshape(B, T, D_MODEL)
    x = x + ctx @ p['w_co']
    h2 = rmsnorm(x, p['norm_ffn']).reshape(B * T, D_MODEL)
    x = x + hier_moe(h2, p).reshape(B, T, D_MODEL)
    return x, new_conv, new_shift, new_S


def setup_inputs(seed: int = 0) -> dict:
    key = jax.random.key(seed)
    ks = iter(jax.random.split(key, 48))
    L = DEPTH
    D = D_MODEL

    def nrm(shape, scale):
        return scale * jax.random.normal(next(ks), shape, F32)

    def gain(shape):
        return 1.0 + 0.05 * jax.random.normal(next(ks), shape, F32)

    return {
        'x_prompt': nrm((BATCH, SEQ, D), 1.0),
        'x_sample': nrm((DEC_BATCH, DEC_SEQ, D), 1.0),
        'mem_prompt': nrm((BATCH, N_MEM, D), 1.0),
        'cache_conv': nrm((L, DEC_BATCH, CONV_K - 1, CONV_W), 0.5),
        'state_shift': nrm((L, DEC_BATCH, SHIFT_W), 1.0),
        'state_rwkv': nrm((L, DEC_BATCH, RWKV_H, RWKV_HEAD, RWKV_HEAD), 0.3),
        'cache_mem_k': nrm((L, DEC_BATCH, N_MEM, X_HEADS, X_HEAD), 1.0),
        'cache_mem_v': nrm((L, DEC_BATCH, N_MEM, X_HEADS, X_HEAD), 1.0),
        'norm_mix': gain((L, D)),
        'w_in': nrm((L, D, IN_W), D ** -0.5),
        'conv_w': nrm((L, CONV_K, CONV_W), CONV_K ** -0.5),
        'conv_b': nrm((L, CONV_W), 0.02),
        'conv_ln_g': gain((L, CONV_W)),
        'conv_ln_b': nrm((L, CONV_W), 0.02),
        'shift_mu': jax.random.uniform(next(ks), (L, SHIFT_W), F32),
        'w_decay_up': nrm((L, DECAY_LORA, RWKV_W), DECAY_LORA ** -0.5),
        'decay_bias': nrm((L, RWKV_W), 1.0),
        'w_a_up': nrm((L, AAA_LORA, RWKV_W), AAA_LORA ** -0.5),
        'a_bias': nrm((L, RWKV_W), 0.5),
        'w_g_up': nrm((L, GATE_LORA, RWKV_W), GATE_LORA ** -0.5),
        'k_k': 0.85 + nrm((L, RWKV_W), 0.05),
        'k_a': gain((L, RWKV_W)),
        'r_k': nrm((L, RWKV_H, RWKV_HEAD), 0.1),
        'lnx_g': gain((L, RWKV_W)),
        'lnx_b': nrm((L, RWKV_W), 0.02),
        'w_out': nrm((L, D, D), D ** -0.5),
        'norm_x': gain((L, D)),
        'norm_mem': gain((L, D)),
        'w_cq': nrm((L, D, D), D ** -0.5),
        'w_ck': nrm((L, D, D), D ** -0.5),
        'w_cv': nrm((L, D, D), D ** -0.5),
        'w_co': nrm((L, D, D), D ** -0.5),
        'norm_ffn': gain((L, D)),
        'w_route_group': nrm((L, D, N_GROUPS), D ** -0.5),
        'b_route_group': nrm((L, N_GROUPS), 0.01),
        'w_route_expert': nrm((L, D, N_GROUPS, EXP_PER_GROUP), D ** -0.5),
        'b_route_expert': nrm((L, N_GROUPS, EXP_PER_GROUP), 0.01),
        'w_gate': nrm((L, N_GROUPS, EXP_PER_GROUP, D, D_EXPERT), D ** -0.5),
        'w_up': nrm((L, N_GROUPS, EXP_PER_GROUP, D, D_EXPERT), D ** -0.5),
        'w_down': nrm((L, N_GROUPS, EXP_PER_GROUP, D_EXPERT, D), D_EXPERT ** -0.5),
        'norm_final': gain((D,)),
    }


def reference(x_prompt, x_sample, mem_prompt, cache_conv, state_shift, state_rwkv,
              cache_mem_k, cache_mem_v, norm_mix, w_in, conv_w, conv_b, conv_ln_g, conv_ln_b,
              shift_mu, w_decay_up, decay_bias, w_a_up, a_bias, w_g_up, k_k, k_a, r_k,
              lnx_g, lnx_b, w_out, norm_x, norm_mem, w_cq, w_ck, w_cv, w_co, norm_ffn,
              w_route_group, b_route_group, w_route_expert, b_route_expert,
              w_gate, w_up, w_down, norm_final):
    xp = x_prompt
    xs = x_sample
    B = xp.shape[0]
    conv_p, shift_p, rwkv_p, memk_p, memv_p = [], [], [], [], []
    conv_s, shift_s, rwkv_s = [], [], []
    for l in range(DEPTH):
        p = dict(norm_mix=norm_mix[l], w_in=w_in[l], conv_w=conv_w[l], conv_b=conv_b[l],
                 conv_ln_g=conv_ln_g[l], conv_ln_b=conv_ln_b[l], shift_mu=shift_mu[l],
                 w_decay_up=w_decay_up[l], decay_bias=decay_bias[l], w_a_up=w_a_up[l],
                 a_bias=a_bias[l], w_g_up=w_g_up[l], k_k=k_k[l], k_a=k_a[l], r_k=r_k[l],
                 lnx_g=lnx_g[l], lnx_b=lnx_b[l], w_out=w_out[l], norm_x=norm_x[l],
                 w_cq=w_cq[l], w_co=w_co[l], norm_ffn=norm_ffn[l],
                 w_route_group=w_route_group[l], b_route_group=b_route_group[l],
                 w_route_expert=w_route_expert[l], b_route_expert=b_route_expert[l],
                 w_gate=w_gate[l], w_up=w_up[l], w_down=w_down[l])
        mn = rmsnorm(mem_prompt, norm_mem[l])
        mk = (mn @ w_ck[l]).reshape(B, N_MEM, X_HEADS, X_HEAD)
        mv = (mn @ w_cv[l]).reshape(B, N_MEM, X_HEADS, X_HEAD)
        xp, cp, sp, Sp = layer(xp, mk, mv,
                               jnp.zeros((B, CONV_K - 1, CONV_W), xp.dtype),
                               jnp.zeros((B, SHIFT_W), xp.dtype),
                               jnp.zeros((B, RWKV_H, RWKV_HEAD, RWKV_HEAD), F32), p)
        xs, cs, ss, Ss = layer(xs, cache_mem_k[l], cache_mem_v[l], cache_conv[l],
                               state_shift[l], state_rwkv[l].astype(F32), p)
        conv_p.append(cp)
        shift_p.append(sp)
        rwkv_p.append(Sp)
        memk_p.append(mk)
        memv_p.append(mv)
        conv_s.append(cs)
        shift_s.append(ss)
        rwkv_s.append(Ss)
    y_prompt = rmsnorm(xp, norm_final)
    y_sample = rmsnorm(xs, norm_final)
    return (y_prompt, y_sample, jnp.stack(conv_p), jnp.stack(shift_p), jnp.stack(rwkv_p),
            jnp.stack(memk_p), jnp.stack(memv_p), jnp.stack(conv_s), jnp.stack(shift_s),
            jnp.stack(rwkv_s))
```

```python
import functools
import math

import jax
import jax.numpy as jnp
from jax import lax
from jax.experimental import pallas as pl
from jax.experimental.pallas import tpu as pltpu

F32 = jnp.float32
BF16 = jnp.bfloat16

CONV_K = 31
HEAD = 64
PAIR = 2 * HEAD
CHUNK = 64
DECAY_LORA = 64
AAA_LORA = 64
GATE_LORA = 160
LORA_PAD = 512
N_MEM = 256
X_HEADS = 4
N_GROUPS = 4
EXP_PER_GROUP = 8
N_EXPERTS = N_GROUPS * EXP_PER_GROUP
RMS_EPS = 1e-6
LN_EPS = 1e-5
GN_EPS = 64e-5
DECAY_SCALE = math.exp(-0.5)
NEG_BIG = -1e30
LANES = 128
VMEM_LIMIT = 56 * 1024 * 1024


def _cparams(*sem):
    return pltpu.CompilerParams(dimension_semantics=sem, vmem_limit_bytes=VMEM_LIMIT)


def _dot(a, b):
    return jnp.dot(a, b, preferred_element_type=F32)


def _dot_nt(a, b):
    return lax.dot_general(a, b, (((1,), (1,)), ((), ())), preferred_element_type=F32)


def _split_dot(x, w_bf16):
    hi = x.astype(BF16)
    lo = (x - hi.astype(F32)).astype(BF16)
    return _dot(hi, w_bf16) + _dot(lo, w_bf16)


def _rms(x, g, eps=RMS_EPS):
    return x * lax.rsqrt(jnp.mean(x * x, axis=-1, keepdims=True) + eps) * g


def _norm_mm_kernel(x_ref, g_ref, w_ref, o_ref, xn_ref):
    @pl.when(pl.program_id(1) == 0)
    def _():
        xn_ref[...] = _rms(x_ref[...], g_ref[...]).astype(BF16)

    o_ref[...] = _dot(xn_ref[...], w_ref[...])


def _norm_matmul(x, g, w, bm, bn):
    m, k = x.shape
    n = w.shape[1]
    bm = min(bm, m)
    return pl.pallas_call(
        _norm_mm_kernel,
        grid=(m // bm, n // bn),
        in_specs=[pl.BlockSpec((bm, k), lambda i, j: (i, 0)),
                  pl.BlockSpec((1, k), lambda i, j: (0, 0)),
                  pl.BlockSpec((k, bn), lambda i, j: (0, j))],
        out_specs=pl.BlockSpec((bm, bn), lambda i, j: (i, j)),
        out_shape=jax.ShapeDtypeStruct((m, n), F32),
        scratch_shapes=[pltpu.VMEM((bm, k), BF16)],
        compiler_params=_cparams("parallel", "arbitrary"),
        name="norm_matmul",
    )(x, g.reshape(1, k), w)


def _mm_res_kernel(*refs, n_lhs):
    res_ref = refs[2 * n_lhs]
    o_ref = refs[2 * n_lhs + 1]
    acc = res_ref[...]
    for a_ref, w_ref in zip(refs[:n_lhs], refs[n_lhs:2 * n_lhs]):
        acc = acc + _dot(a_ref[...].astype(BF16), w_ref[...])
    o_ref[...] = acc


def _matmul_res(lhs, ws, res, bm):
    m, n = res.shape
    bm = min(bm, m)
    n_lhs = len(lhs)
    in_specs = [pl.BlockSpec((bm, a.shape[1]), lambda i: (i, 0)) for a in lhs]
    in_specs += [pl.BlockSpec(w.shape, lambda i: (0, 0)) for w in ws]
    in_specs += [pl.BlockSpec((bm, n), lambda i: (i, 0))]
    return pl.pallas_call(
        functools.partial(_mm_res_kernel, n_lhs=n_lhs),
        grid=(m // bm,),
        in_specs=in_specs,
        out_specs=pl.BlockSpec((bm, n), lambda i: (i, 0)),
        out_shape=jax.ShapeDtypeStruct((m, n), F32),
        compiler_params=_cparams("parallel"),
        name="matmul_res",
    )(*lhs, *ws, res)


def _ln_silu(cf, lg, lb):
    mu = jnp.mean(cf, axis=-1, keepdims=True)
    d = cf - mu
    var = jnp.mean(d * d, axis=-1, keepdims=True)
    y = d * lax.rsqrt(var + LN_EPS) * lg + lb
    return y * jax.nn.sigmoid(y)


def _conv_prefill_kernel(a_ref, g_ref, buf_ref, w_ref, cb_ref, lg_ref, lb_ref, c_ref, nc_ref,
                         uf_ref, cv_ref, *, tt, halo):
    t = pl.program_id(1)
    pad = 32 - halo

    @pl.when(t == 0)
    def _():
        uf_ref[pad:32, :] = buf_ref[0]

    @pl.when(t > 0)
    def _():
        uf_ref[pad:32, :] = uf_ref[tt + pad:tt + 32, :]

    uf_ref[32:32 + tt, :] = a_ref[...] * jax.nn.sigmoid(g_ref[...])

    width = uf_ref.shape[1]
    rb = 64
    for r0 in range(0, tt, rb):
        for l0 in range(0, width, LANES):
            acc = jnp.zeros((rb, LANES), F32)
            for j in range(CONV_K):
                acc = acc + (uf_ref[r0 + pad + j:r0 + pad + j + rb, l0:l0 + LANES]
                             * w_ref[j:j + 1, l0:l0 + LANES])
            cv_ref[r0:r0 + rb, l0:l0 + LANES] = acc

    c_ref[...] = _ln_silu(cv_ref[...] + cb_ref[...], lg_ref[...], lb_ref[...]).astype(c_ref.dtype)

    @pl.when(t == pl.num_programs(1) - 1)
    def _():
        nc_ref[0] = uf_ref[tt + pad:tt + 32, :]


def _conv_prefill(proj, conv_buf, conv_w, conv_b, ln_g, ln_b, batch, seq):
    cw = conv_w.shape[1]
    halo = CONV_K - 1
    tt = min(256, seq)
    nt = seq // tt
    row = lambda b, t: (b * nt + t, 0)
    vec = pl.BlockSpec((1, cw), lambda b, t: (0, 0))
    return pl.pallas_call(
        functools.partial(_conv_prefill_kernel, tt=tt, halo=halo),
        grid=(batch, nt),
        in_specs=[pl.BlockSpec((tt, cw), row),
                  pl.BlockSpec((tt, cw), lambda b, t: (b * nt + t, 1)),
                  pl.BlockSpec((1, halo, cw), lambda b, t: (b, 0, 0)),
                  pl.BlockSpec((CONV_K, cw), lambda b, t: (0, 0)),
                  vec, vec, vec],
        out_specs=[pl.BlockSpec((tt, cw), row),
                   pl.BlockSpec((1, halo, cw), lambda b, t: (b, 0, 0))],
        out_shape=[jax.ShapeDtypeStruct((batch * seq, cw), BF16),
                   jax.ShapeDtypeStruct((batch, halo, cw), F32)],
        scratch_shapes=[pltpu.VMEM((tt + 32, cw), F32), pltpu.VMEM((tt, cw), F32)],
        compiler_params=_cparams("parallel", "arbitrary"),
        name="conv_prefill",
    )(proj, proj, conv_buf, conv_w, conv_b.reshape(1, cw), ln_g.reshape(1, cw), ln_b.reshape(1, cw))


def _conv_decode_kernel(a_ref, g_ref, cache_ref, w_ref, cb_ref, lg_ref, lb_ref, c_ref, nc_ref):
    halo = CONV_K - 1
    u = a_ref[...] * jax.nn.sigmoid(g_ref[...])
    acc = u * w_ref[halo:halo + 1, :]
    for j in range(halo):
        acc = acc + cache_ref[:, j, :] * w_ref[j:j + 1, :]
    c_ref[...] = _ln_silu(acc + cb_ref[...], lg_ref[...], lb_ref[...]).astype(c_ref.dtype)
    nc_ref[:, 0:halo - 1, :] = cache_ref[:, 1:halo, :]
    nc_ref[:, halo - 1, :] = u


def _conv_decode(proj, cache, conv_w, conv_b, ln_g, ln_b):
    batch, halo, cw = cache.shape
    bb = 8
    vec = pl.BlockSpec((1, cw), lambda i: (0, 0))
    return pl.pallas_call(
        _conv_decode_kernel,
        grid=(batch // bb,),
        in_specs=[pl.BlockSpec((bb, cw), lambda i: (i, 0)),
                  pl.BlockSpec((bb, cw), lambda i: (i, 1)),
                  pl.BlockSpec((bb, halo, cw), lambda i: (i, 0, 0)),
                  pl.BlockSpec((CONV_K, cw), lambda i: (0, 0)),
                  vec, vec, vec],
        out_specs=[pl.BlockSpec((bb, cw), lambda i: (i, 0)),
                   pl.BlockSpec((bb, halo, cw), lambda i: (i, 0, 0))],
        out_shape=[jax.ShapeDtypeStruct((batch, cw), BF16),
                   jax.ShapeDtypeStruct((batch, halo, cw), F32)],
        compiler_params=_cparams("parallel"),
        name="conv_decode",
    )(proj, proj, cache, conv_w, conv_b.reshape(1, cw), ln_g.reshape(1, cw), ln_b.reshape(1, cw))


def _head_sum(x, bd_ref):
    blk = bd_ref.shape[0]
    parts = [_split_dot(x[:, l0:l0 + blk], bd_ref[...]) for l0 in range(0, x.shape[1], blk)]
    return jnp.concatenate(parts, axis=1)


def _prep_math(q, qp, mu_ref, wd_ref, wa_ref, wg_ref, db_ref, ab_ref, kk_ref, ka_ref, rk_ref, bd_ref):
    rw = q[0].shape[1]
    offs = (0, rw, 2 * rw, 3 * rw)
    r, k, v, lo = [x + (xp - x) * mu_ref[:, o:o + x.shape[1]] for x, xp, o in zip(q, qp, offs)]
    pwa = lo[:, 0:LANES]
    pg = lo[:, LANES:3 * LANES]
    dec_in = _dot(jnp.tanh(pwa).astype(BF16), wd_ref[...])
    a_in = _dot(pwa.astype(BF16), wa_ref[...])
    gate = _dot(jax.nn.sigmoid(pg).astype(BF16), wg_ref[...])
    logw = -DECAY_SCALE * jax.nn.sigmoid(db_ref[...] + dec_in)
    a = jax.nn.sigmoid(ab_ref[...] + a_in)
    kk = k * kk_ref[...]
    kk = kk / jnp.maximum(jnp.sqrt(_head_sum(kk * kk, bd_ref)), 1e-12)
    k2 = k * (1.0 + (a - 1.0) * ka_ref[...])
    bonus = _head_sum(r * k2 * rk_ref[...], bd_ref) * v
    return r, logw, k2, v, kk, a, bonus, gate


def _prep_prefill_kernel(r_ref, k_ref, v_ref, lo_ref, sb_ref, mu_ref, wd_ref, wa_ref, wg_ref, db_ref,
                         ab_ref, kk_ref, ka_ref, rk_ref, bd_ref, *rest):
    outs = rest[:8]
    carry_ref = rest[8]
    t = pl.program_id(1)

    @pl.when(t == 0)
    def _():
        carry_ref[0:1, :] = sb_ref[0]

    q = [r_ref[...], k_ref[...], v_ref[...], lo_ref[...]]
    tt = q[0].shape[0]
    first = lax.broadcasted_iota(jnp.int32, (tt, 1), 0) == 0
    qp = []
    off = 0
    for x in q:
        w = x.shape[1]
        qp.append(jnp.where(first, carry_ref[0:1, off:off + w], pltpu.roll(x, 1, 0)))
        off += w
    off = 0
    for x in q:
        w = x.shape[1]
        carry_ref[0:1, off:off + w] = x[tt - 1:tt, :]
        off += w
    res = _prep_math(q, qp, mu_ref, wd_ref, wa_ref, wg_ref, db_ref, ab_ref, kk_ref, ka_ref, rk_ref, bd_ref)
    for o_ref, val in zip(outs, res):
        o_ref[...] = val


def _prep_decode_kernel(r_ref, k_ref, v_ref, lo_ref, rp_ref, kp_ref, vp_ref, lop_ref, mu_ref, wd_ref,
                        wa_ref, wg_ref, db_ref, ab_ref, kk_ref, ka_ref, rk_ref, bd_ref, *outs):
    q = [r_ref[...], k_ref[...], v_ref[...], lo_ref[...]]
    qp = [rp_ref[...], kp_ref[...], vp_ref[...], lop_ref[...]]
    res = _prep_math(q, qp, mu_ref, wd_ref, wa_ref, wg_ref, db_ref, ab_ref, kk_ref, ka_ref, rk_ref, bd_ref)
    for o_ref, val in zip(outs, res):
        o_ref[...] = val


def _prep_param_specs(rw, idx):
    full = lambda shape: pl.BlockSpec(shape, idx)
    vec = full((1, rw))
    return [full((1, 3 * rw + LORA_PAD)), full((LANES, rw)), full((LANES, rw)), full((2 * LANES, rw)),
            vec, vec, vec, vec, vec, full((2 * LANES, 2 * LANES))]


def _rwkv_prep_prefill(proj, shift_buf, pp, batch, seq, rw):
    tt = min(256, seq)
    nt = seq // tt
    lora_blk = (2 * rw + 3 * rw) // LORA_PAD
    col = lambda c: (lambda b, t: (b * nt + t, c))
    qw = 3 * rw + LORA_PAD
    in_specs = [pl.BlockSpec((tt, rw), col(2)), pl.BlockSpec((tt, rw), col(3)),
                pl.BlockSpec((tt, rw), col(4)), pl.BlockSpec((tt, LORA_PAD), col(lora_blk)),
                pl.BlockSpec((1, 1, qw), lambda b, t: (b, 0, 0))]
    in_specs += _prep_param_specs(rw, lambda b, t: (0, 0))
    out_spec = pl.BlockSpec((tt, rw), col(0))
    return pl.pallas_call(
        _prep_prefill_kernel,
        grid=(batch, nt),
        in_specs=in_specs,
        out_specs=[out_spec] * 8,
        out_shape=[jax.ShapeDtypeStruct((batch * seq, rw), F32)] * 8,
        scratch_shapes=[pltpu.VMEM((8, qw), F32)],
        compiler_params=_cparams("parallel", "arbitrary"),
        name="rwkv_prep_prefill",
    )(proj, proj, proj, proj, shift_buf.reshape(batch, 1, qw), *pp)


def _rwkv_prep_decode(proj, shift_state, pp, rw):
    batch = proj.shape[0]
    bb = min(128, batch)
    lora_blk = (2 * rw + 3 * rw) // LORA_PAD
    col = lambda c: (lambda i: (i, c))
    in_specs = [pl.BlockSpec((bb, rw), col(2)), pl.BlockSpec((bb, rw), col(3)),
                pl.BlockSpec((bb, rw), col(4)), pl.BlockSpec((bb, LORA_PAD), col(lora_blk)),
                pl.BlockSpec((bb, rw), col(0)), pl.BlockSpec((bb, rw), col(1)),
                pl.BlockSpec((bb, rw), col(2)), pl.BlockSpec((bb, LORA_PAD), col(3 * rw // LORA_PAD))]
    in_specs += _prep_param_specs(rw, lambda i: (0, 0))
    return pl.pallas_call(
        _prep_decode_kernel,
        grid=(batch // bb,),
        in_specs=in_specs,
        out_specs=[pl.BlockSpec((bb, rw), col(0))] * 8,
        out_shape=[jax.ShapeDtypeStruct((batch, rw), F32)] * 8,
        compiler_params=_cparams("parallel"),
        name="rwkv_prep_decode",
    )(proj, proj, proj, proj, shift_state, shift_state, shift_state, shift_state, *pp)


def _stack2(x, smask):
    return jnp.where(smask, jnp.concatenate([x, x], axis=0), 0.0)


def _rwkv_chunk_kernel(r_ref, lw_ref, k_ref, v_ref, kk_ref, a_ref, s0_ref, o_ref, so_ref, s_ref):
    c = pl.program_id(1)
    cs = r_ref.shape[0]
    n_pairs = r_ref.shape[1] // PAIR
    two = 2 * cs

    @pl.when(c == 0)
    def _():
        z = jnp.zeros((HEAD, HEAD), F32)
        for p in range(n_pairs):
            top = jnp.concatenate([s0_ref[0, 2 * p], z], axis=1)
            bot = jnp.concatenate([z, s0_ref[0, 2 * p + 1]], axis=1)
            s_ref[p] = jnp.concatenate([top, bot], axis=0)

    ri = lax.broadcasted_iota(jnp.int32, (two, two), 0)
    ci = lax.broadcasted_iota(jnp.int32, (two, two), 1)
    strict = ci < ri
    incl = ci <= ri
    eye = (ci == ri).astype(F32)
    smask = (lax.broadcasted_iota(jnp.int32, (two, PAIR), 0) < cs) == (
        lax.broadcasted_iota(jnp.int32, (two, PAIR), 1) < HEAD)
    tri = (lax.broadcasted_iota(jnp.int32, (cs, cs), 1)
           <= lax.broadcasted_iota(jnp.int32, (cs, cs), 0)).astype(BF16)

    lw_all = lw_ref[...]
    lw_hi = lw_all.astype(BF16)
    lw_lo = (lw_all - lw_hi.astype(F32)).astype(BF16)
    cum_all = _dot(tri, lw_hi) + _dot(tri, lw_lo)

    for p in range(n_pairs):
        sl = slice(p * PAIR, (p + 1) * PAIR)
        lw = lw_all[:, sl]
        cum = cum_all[:, sl]
        tot = cum[cs - 1:cs, :]
        g_in = jnp.exp(cum)
        g_ex = jnp.exp(cum - lw)
        g_inv = jnp.exp(-cum)
        g_end = jnp.exp(tot - cum)
        g_tot = jnp.exp(tot)
        kk = kk_ref[:, sl]
        k2 = k_ref[:, sl]
        bb = kk * a_ref[:, sl]
        a_s = _stack2(kk * g_ex, smask)
        bu_s = _stack2(bb * g_inv, smask).astype(BF16)
        ku_s = _stack2(k2 * g_inv, smask).astype(BF16)
        r_s = _stack2(r_ref[:, sl] * g_in, smask)
        v_s = _stack2(v_ref[:, sl], smask).astype(BF16)
        bg_s = _stack2(bb * g_end, smask).astype(BF16)
        kg_s = _stack2(k2 * g_end, smask).astype(BF16)
        a_b = a_s.astype(BF16)
        r_b = r_s.astype(BF16)

        l_ab = jnp.where(strict, _dot_nt(a_b, bu_s), 0.0)
        l_ak = jnp.where(strict, _dot_nt(a_b, ku_s), 0.0)
        m_rb = jnp.where(incl, _dot_nt(r_b, bu_s), 0.0)
        m_rk = jnp.where(incl, _dot_nt(r_b, ku_s), 0.0)

        tm = eye - l_ab
        pw = l_ab
        n = 1
        while 2 * n < cs:
            pb = pw.astype(BF16)
            pw = _dot(pb, pb)
            tm = tm + _dot(tm.astype(BF16), pw.astype(BF16))
            n *= 2
        tm_b = tm.astype(BF16)

        w1 = _dot(l_ak.astype(BF16), v_s)
        u0 = _dot(tm_b, w1.astype(BF16))
        at = _dot(tm_b, a_b)
        m_rb_b = m_rb.astype(BF16)
        u0_b = u0.astype(BF16)
        at_b = at.astype(BF16)
        o0 = _dot(m_rk.astype(BF16), v_s) - _dot(m_rb_b, u0_b)
        rt = r_s - _dot(m_rb_b, at_b)

        s_old = s_ref[p]
        s_b = s_old.astype(BF16)
        att_bg = _dot(at.T.astype(BF16), bg_s)
        u0t_bg = _dot(u0.T.astype(BF16), bg_s)
        vt_kg = _dot(v_s.astype(F32).T.astype(BF16), kg_s)
        s_ref[p] = s_old * g_tot - _dot(s_b, att_bg.astype(BF16)) + vt_kg - u0t_bg

        o_st = o0 + _dot_nt(rt.astype(BF16), s_b)
        o_ref[:, sl] = o_st[0:cs, :] + o_st[cs:two, :]

    @pl.when(c == pl.num_programs(1) - 1)
    def _():
        for p in range(n_pairs):
            s = s_ref[p]
            so_ref[0, 2 * p] = s[0:HEAD, 0:HEAD]
            so_ref[0, 2 * p + 1] = s[HEAD:PAIR, HEAD:PAIR]


def _rwkv_chunked(r, logw, k2, v, kk, a, s0, batch, seq):
    rw = r.shape[1]
    nc = seq // CHUNK
    heads = rw // HEAD
    row = pl.BlockSpec((CHUNK, rw), lambda b, c: (b * nc + c, 0))
    st = pl.BlockSpec((1, heads, HEAD, HEAD), lambda b, c: (b, 0, 0, 0))
    return pl.pallas_call(
        _rwkv_chunk_kernel,
        grid=(batch, nc),
        in_specs=[row] * 6 + [st],
        out_specs=[row, st],
        out_shape=[jax.ShapeDtypeStruct((batch * seq, rw), F32),
                   jax.ShapeDtypeStruct((batch, heads, HEAD, HEAD), F32)],
        scratch_shapes=[pltpu.VMEM((rw // PAIR, PAIR, PAIR), F32)],
        compiler_params=_cparams("parallel", "arbitrary"),
        name="rwkv_chunked",
    )(r, logw, k2, v, kk, a, s0)


def _rwkv_step_kernel(r_ref, lw_ref, k_ref, v_ref, kk_ref, a_ref, s_ref, o_ref, so_ref, *, heads):
    bb = s_ref.shape[0]
    eye = (lax.broadcasted_iota(jnp.int32, (HEAD, HEAD), 0)
           == lax.broadcasted_iota(jnp.int32, (HEAD, HEAD), 1))

    def body(bi, carry):
        for h in range(heads):
            row = pl.ds(bi * heads + h, 1)
            kk = kk_ref[row, :]
            s = s_ref[bi, h]
            sa = jnp.sum(s * kk, axis=1, keepdims=True)
            v_col = jnp.sum(jnp.where(eye, v_ref[row, :], 0.0), axis=1, keepdims=True)
            s_new = (s * jnp.exp(lw_ref[row, :]) - sa * (kk * a_ref[row, :]) + v_col * k_ref[row, :])
            so_ref[bi, h] = s_new
            o_col = jnp.sum(s_new * r_ref[row, :], axis=1, keepdims=True)
            o_ref[row, :] = jnp.sum(jnp.where(eye, o_col, 0.0), axis=0, keepdims=True)
        return carry

    lax.fori_loop(0, bb, body, 0)


def _rwkv_step(r, logw, k2, v, kk, a, s0):
    batch, rw = r.shape
    heads = rw // HEAD
    bb = 8
    flat = lambda x: x.reshape(batch * heads, HEAD)
    row = pl.BlockSpec((bb * heads, HEAD), lambda i: (i, 0))
    st = pl.BlockSpec((bb, heads, HEAD, HEAD), lambda i: (i, 0, 0, 0))
    o, s_new = pl.pallas_call(
        functools.partial(_rwkv_step_kernel, heads=heads),
        grid=(batch // bb,),
        in_specs=[row] * 6 + [st],
        out_specs=[row, st],
        out_shape=[jax.ShapeDtypeStruct((batch * heads, HEAD), F32),
                   jax.ShapeDtypeStruct(s0.shape, F32)],
        compiler_params=_cparams("parallel"),
        name="rwkv_step",
    )(flat(r), flat(logw), flat(k2), flat(v), flat(kk), flat(a), s0)
    return o.reshape(batch, rw), s_new


def _rwkv_post_kernel(o_ref, bonus_ref, gate_ref, lg_ref, lb_ref, bd_ref, y_ref):
    o = o_ref[...]
    mu = _head_sum(o, bd_ref) * (1.0 / HEAD)
    d = o - mu
    var = _head_sum(d * d, bd_ref) * (1.0 / HEAD)
    y = d * lax.rsqrt(var + GN_EPS) * lg_ref[...] + lb_ref[...]
    y_ref[...] = ((y + bonus_ref[...]) * gate_ref[...]).astype(y_ref.dtype)


def _rwkv_post(o, bonus, gate, lnx_g, lnx_b, bd):
    m, rw = o.shape
    bm = min(512, m)
    row = pl.BlockSpec((bm, rw), lambda i: (i, 0))
    vec = pl.BlockSpec((1, rw), lambda i: (0, 0))
    return pl.pallas_call(
        _rwkv_post_kernel,
        grid=(m // bm,),
        in_specs=[row, row, row, vec, vec, pl.BlockSpec(bd.shape, lambda i: (0, 0))],
        out_specs=row,
        out_shape=jax.ShapeDtypeStruct((m, rw), BF16),
        compiler_params=_cparams("parallel"),
        name="rwkv_post",
    )(o, bonus, gate, lnx_g.reshape(1, rw), lnx_b.reshape(1, rw), bd)


def _attn_prefill_kernel(q_ref, k_ref, v_ref, o_ref, *, n_heads):
    d = q_ref.shape[1] // n_heads
    scale = d ** -0.5
    for h in range(n_heads):
        sl = slice(h * d, (h + 1) * d)
        s = _dot_nt(q_ref[:, sl].astype(BF16), k_ref[:, sl].astype(BF16)) * scale
        p = jnp.exp(s - jnp.max(s, axis=-1, keepdims=True))
        att = p / jnp.sum(p, axis=-1, keepdims=True)
        o_ref[:, sl] = _dot(att.astype(BF16), v_ref[:, sl].astype(BF16)).astype(o_ref.dtype)


def _attn_prefill(q, mem_k, mem_v, batch, seq):
    d = q.shape[1]
    tt = min(512, seq)
    nt = seq // tt
    kv = pl.BlockSpec((N_MEM, d), lambda b, t: (b, 0))
    row = pl.BlockSpec((tt, d), lambda b, t: (b * nt + t, 0))
    return pl.pallas_call(
        functools.partial(_attn_prefill_kernel, n_heads=X_HEADS),
        grid=(batch, nt),
        in_specs=[row, kv, kv],
        out_specs=row,
        out_shape=jax.ShapeDtypeStruct((batch * seq, d), BF16),
        compiler_params=_cparams("parallel", "arbitrary"),
        name="attn_prefill",
    )(q, mem_k, mem_v)


def _attn_decode_kernel(q_ref, k_ref, v_ref, o_ref, *, n_heads):
    bb = q_ref.shape[0]
    d = q_ref.shape[2] // n_heads
    scale = d ** -0.5
    for bi in range(bb):
        q = q_ref[bi]
        for h in range(n_heads):
            sl = slice(h * d, (h + 1) * d)
            s = jnp.sum(k_ref[bi, :, sl] * q[:, sl], axis=1, keepdims=True) * scale
            p = jnp.exp(s - jnp.max(s, axis=0, keepdims=True))
            att = p / jnp.sum(p, axis=0, keepdims=True)
            o_ref[bi, :, sl] = jnp.sum(att * v_ref[bi, :, sl], axis=0, keepdims=True)


def _attn_decode(q, cache_k, cache_v):
    batch, d = q.shape
    bb = 2
    kv = pl.BlockSpec((bb, N_MEM, d), lambda i: (i, 0, 0))
    row = pl.BlockSpec((bb, 1, d), lambda i: (i, 0, 0))
    out = pl.pallas_call(
        functools.partial(_attn_decode_kernel, n_heads=X_HEADS),
        grid=(batch // bb,),
        in_specs=[row, kv, kv],
        out_specs=row,
        out_shape=jax.ShapeDtypeStruct((batch, 1, d), F32),
        compiler_params=_cparams("parallel"),
        name="attn_decode",
    )(q.reshape(batch, 1, d), cache_k, cache_v)
    return out.reshape(batch, d)


def _router_kernel(x_ref, g_ref, wh_ref, wl_ref, b_ref, h_ref, comb_ref):
    h = _rms(x_ref[...], g_ref[...])
    h_ref[...] = h.astype(BF16)
    hh = h.astype(BF16)
    hl = (h - hh.astype(F32)).astype(BF16)
    logits = _dot(hh, wh_ref[...]) + _dot(hl, wh_ref[...]) + _dot(hh, wl_ref[...]) + b_ref[...]
    lane = lax.broadcasted_iota(jnp.int32, (1, LANES), 1).astype(F32)
    is_g = (lane >= N_EXPERTS) & (lane < N_EXPERTS + N_GROUPS)
    lgm = jnp.where(is_g, logits, NEG_BIG)
    gmax = jnp.max(lgm, axis=1, keepdims=True)
    gsum = jnp.sum(jnp.where(is_g, jnp.exp(lgm - gmax), 0.0), axis=1, keepdims=True)
    g_val = 1.0 / gsum
    g_idx = jnp.min(jnp.where(is_g & (lgm == gmax), lane - N_EXPERTS, 1e9), axis=1, keepdims=True)
    in_grp = (lane < N_EXPERTS) & (jnp.floor(lane * (1.0 / EXP_PER_GROUP)) == g_idx)
    le = jnp.where(in_grp, logits, NEG_BIG)
    m1 = jnp.max(le, axis=1, keepdims=True)
    i1 = jnp.min(jnp.where(in_grp & (le == m1), lane, 1e9), axis=1, keepdims=True)
    rest = in_grp & (lane != i1)
    le2 = jnp.where(rest, logits, NEG_BIG)
    m2 = jnp.max(le2, axis=1, keepdims=True)
    i2 = jnp.min(jnp.where(rest & (le2 == m2), lane, 1e9), axis=1, keepdims=True)
    e2 = jnp.exp(m2 - m1)
    den = 1.0 + e2
    w1 = (1.0 / den) * g_val
    w2 = (e2 / den) * g_val
    comb_ref[...] = jnp.where(lane == i1, w1, 0.0) + jnp.where(lane == i2, w2, 0.0)


def _router(x, g, wh, wl, bias):
    m, d = x.shape
    bm = min(512, m)
    return pl.pallas_call(
        _router_kernel,
        grid=(m // bm,),
        in_specs=[pl.BlockSpec((bm, d), lambda i: (i, 0)),
                  pl.BlockSpec((1, d), lambda i: (0, 0)),
                  pl.BlockSpec((d, LANES), lambda i: (0, 0)),
                  pl.BlockSpec((d, LANES), lambda i: (0, 0)),
                  pl.BlockSpec((1, LANES), lambda i: (0, 0))],
        out_specs=[pl.BlockSpec((bm, d), lambda i: (i, 0)),
                   pl.BlockSpec((bm, LANES), lambda i: (i, 0))],
        out_shape=[jax.ShapeDtypeStruct((m, d), BF16),
                   jax.ShapeDtypeStruct((m, LANES), F32)],
        compiler_params=_cparams("parallel"),
        name="moe_router",
    )(x, g.reshape(1, d), wh, wl, bias)


def _moe_kernel(h_ref, comb_ref, wg_ref, wu_ref, wd_ref, x_ref, nf_ref, y_ref, acc_ref):
    e = pl.program_id(1)

    @pl.when(e == 0)
    def _():
        acc_ref[...] = x_ref[...]

    h = h_ref[...]
    hg = _dot(h, wg_ref[0])
    hu = _dot(h, wu_ref[0])
    lane = lax.broadcasted_iota(jnp.int32, (1, LANES), 1)
    cw = jnp.sum(jnp.where(lane == e, comb_ref[...], 0.0), axis=1, keepdims=True)
    act = hg * jax.nn.sigmoid(hg) * hu * cw
    acc_ref[...] += _dot(act.astype(BF16), wd_ref[0])

    @pl.when(e == pl.num_programs(1) - 1)
    def _():
        y_ref[...] = _rms(acc_ref[...], nf_ref[...])


def _moe_dense(h, comb, wg, wu, wd, x, norm_final):
    m, d = x.shape
    de = wg.shape[2]
    bm = min(512, m)
    return pl.pallas_call(
        _moe_kernel,
        grid=(m // bm, N_EXPERTS),
        in_specs=[pl.BlockSpec((bm, d), lambda i, e: (i, 0)),
                  pl.BlockSpec((bm, LANES), lambda i, e: (i, 0)),
                  pl.BlockSpec((1, d, de), lambda i, e: (e, 0, 0)),
                  pl.BlockSpec((1, d, de), lambda i, e: (e, 0, 0)),
                  pl.BlockSpec((1, de, d), lambda i, e: (e, 0, 0)),
                  pl.BlockSpec((bm, d), lambda i, e: (i, 0)),
                  pl.BlockSpec((1, d), lambda i, e: (0, 0))],
        out_specs=pl.BlockSpec((bm, d), lambda i, e: (i, 0)),
        out_shape=jax.ShapeDtypeStruct((m, d), F32),
        scratch_shapes=[pltpu.VMEM((bm, d), F32)],
        compiler_params=_cparams("parallel", "arbitrary"),
        name="moe_experts",
    )(h, comb, wg, wu, wd, x, norm_final.reshape(1, d))


def _pad_cols(x, n):
    return jnp.pad(x, ((0, 0), (0, n - x.shape[1])))


def _block_diag_ones(n, blk):
    i = jnp.arange(n) // blk
    return (i[:, None] == i[None, :]).astype(BF16)


def kernel(x_prompt, x_sample, mem_prompt, cache_conv, state_shift, state_rwkv, cache_mem_k, cache_mem_v,
           norm_mix, w_in, conv_w, conv_b, conv_ln_g, conv_ln_b, shift_mu, w_decay_up, decay_bias, w_a_up,
           a_bias, w_g_up, k_k, k_a, r_k, lnx_g, lnx_b, w_out, norm_x, norm_mem, w_cq, w_ck, w_cv, w_co,
           norm_ffn, w_route_group, b_route_group, w_route_expert, b_route_expert, w_gate, w_up, w_down,
           norm_final):
    depth = w_in.shape[0]
    batch, seq, d = x_prompt.shape
    dec_batch = x_sample.shape[0]
    assert depth == 1
    assert x_sample.shape[1] == 1 and seq % CHUNK == 0 and seq >= CONV_K - 1
    cw = conv_w.shape[2]
    rw = w_decay_up.shape[2]
    heads = rw // HEAD
    shift_w = shift_mu.shape[1]
    in_w = w_in.shape[2]
    assert in_w == 2 * cw + shift_w and shift_w == 3 * rw + DECAY_LORA + AAA_LORA + GATE_LORA
    assert cw == rw and rw % LORA_PAD == 0
    in_pad = 2 * cw + 3 * rw + LORA_PAD
    qw = 3 * rw + LORA_PAD

    xp = x_prompt.reshape(batch * seq, d)
    xs = x_sample.reshape(dec_batch, d)
    outs = {k: [] for k in ("conv_p", "shift_p", "rwkv_p", "memk_p", "memv_p", "conv_s", "shift_s", "rwkv_s")}
    bd = _block_diag_ones(2 * LANES, HEAD)

    for l in range(depth):
        w_in_b = _pad_cols(w_in[l], in_pad).astype(BF16)
        w_out_c = w_out[l, :cw].astype(BF16)
        w_out_r = w_out[l, cw:].astype(BF16)
        w_cq_b = w_cq[l].astype(BF16)
        w_ck_b = w_ck[l].astype(BF16)
        w_cv_b = w_cv[l].astype(BF16)
        w_co_b = w_co[l].astype(BF16)
        zeros_l = jnp.zeros((DECAY_LORA, rw), F32)
        wd_pad = jnp.concatenate([w_decay_up[l], zeros_l], axis=0).astype(BF16)
        wa_pad = jnp.concatenate([zeros_l, w_a_up[l]], axis=0).astype(BF16)
        wg_pad = jnp.pad(w_g_up[l], ((0, 2 * LANES - GATE_LORA), (0, 0))).astype(BF16)
        mu_pad = _pad_cols(shift_mu[l].reshape(1, shift_w), qw)
        vec = lambda x: x.reshape(1, rw)
        pp = (mu_pad, wd_pad, wa_pad, wg_pad, vec(decay_bias[l]), vec(a_bias[l]), vec(k_k[l]), vec(k_a[l]),
              vec(r_k[l]), bd)
        w_route = jnp.concatenate([w_route_expert[l].reshape(d, N_EXPERTS), w_route_group[l]], axis=1)
        w_route = _pad_cols(w_route, LANES)
        wr_hi = w_route.astype(BF16)
        wr_lo = (w_route - wr_hi.astype(F32)).astype(BF16)
        b_route = _pad_cols(jnp.concatenate([b_route_expert[l].reshape(1, N_EXPERTS),
                                             b_route_group[l].reshape(1, N_GROUPS)], axis=1), LANES)
        de = w_gate.shape[-1]
        wg_e = w_gate[l].reshape(N_EXPERTS, d, de).astype(BF16)
        wu_e = w_up[l].reshape(N_EXPERTS, d, de).astype(BF16)
        wd_e = w_down[l].reshape(N_EXPERTS, de, d).astype(BF16)
        nf = norm_final

        mem2 = mem_prompt.reshape(batch * N_MEM, d)
        mk = _norm_matmul(mem2, norm_mem[l], w_ck_b, 512, 1024)
        mv = _norm_matmul(mem2, norm_mem[l], w_cv_b, 512, 1024)
        proj = _norm_matmul(xp, norm_mix[l], w_in_b, 1024, 512)
        c_p, conv_new = _conv_prefill(proj, jnp.zeros((batch, CONV_K - 1, cw), F32), conv_w[l], conv_b[l],
                                      conv_ln_g[l], conv_ln_b[l], batch, seq)
        prep = _rwkv_prep_prefill(proj, jnp.zeros((batch, qw), F32), pp, batch, seq, rw)
        o_p, s_p = _rwkv_chunked(*prep[:6], jnp.zeros((batch, heads, HEAD, HEAD), F32), batch, seq)
        o_p = _rwkv_post(o_p, prep[6], prep[7], lnx_g[l], lnx_b[l], bd)
        shift_new = proj.reshape(batch, seq, in_pad)[:, -1, 2 * cw:2 * cw + shift_w]
        xp = _matmul_res([c_p, o_p], [w_out_c, w_out_r], xp, 512)
        qx = _norm_matmul(xp, norm_x[l], w_cq_b, 1024, 512)
        ctx = _attn_prefill(qx, mk, mv, batch, seq)
        xp = _matmul_res([ctx], [w_co_b], xp, 512)
        h2, comb = _router(xp, norm_ffn[l], wr_hi, wr_lo, b_route)
        xp_new = _moe_dense(h2, comb, wg_e, wu_e, wd_e, xp, nf)
        outs["conv_p"].append(conv_new)
        outs["shift_p"].append(shift_new)
        outs["rwkv_p"].append(s_p)
        outs["memk_p"].append(mk.reshape(batch, N_MEM, X_HEADS, d // X_HEADS))
        outs["memv_p"].append(mv.reshape(batch, N_MEM, X_HEADS, d // X_HEADS))
        xp = xp_new

        proj_s = _norm_matmul(xs, norm_mix[l], w_in_b, 128, 512)
        c_s, conv_new_s = _conv_decode(proj_s, cache_conv[l], conv_w[l], conv_b[l], conv_ln_g[l],
                                       conv_ln_b[l])
        prep_s = _rwkv_prep_decode(proj_s, _pad_cols(state_shift[l], qw), pp, rw)
        o_s, s_s = _rwkv_step(*prep_s[:6], state_rwkv[l].astype(F32))
        o_s = _rwkv_post(o_s, prep_s[6], prep_s[7], lnx_g[l], lnx_b[l], bd)
        xs = _matmul_res([c_s, o_s], [w_out_c, w_out_r], xs, 128)
        qs = _norm_matmul(xs, norm_x[l], w_cq_b, 128, 512)
        ctx_s = _attn_decode(qs, cache_mem_k[l].reshape(dec_batch, N_MEM, d),
                             cache_mem_v[l].reshape(dec_batch, N_MEM, d))
        xs = _matmul_res([ctx_s], [w_co_b], xs, 128)
        h2s, comb_s = _router(xs, norm_ffn[l], wr_hi, wr_lo, b_route)
        xs = _moe_dense(h2s, comb_s, wg_e, wu_e, wd_e, xs, nf)
        outs["conv_s"].append(conv_new_s)
        outs["shift_s"].append(proj_s[:, 2 * cw:2 * cw + shift_w])
        outs["rwkv_s"].append(s_s)

    y_prompt = xp.reshape(batch, seq, d)
    y_sample = xs.reshape(dec_batch, 1, d)
    st = lambda k: jnp.stack(outs[k])
    return (y_prompt, y_sample, st("conv_p"), st("shift_p"), st("rwkv_p"), st("memk_p"), st("memv_p"),
            st("conv_s"), st("shift_s"), st("rwkv_s"))
```

```python
import functools
import math

import jax
import jax.numpy as jnp
from jax import lax
from jax.experimental import pallas as pl
from jax.experimental.pallas import tpu as pltpu

F32 = jnp.float32
BF16 = jnp.bfloat16

CONV_K = 31
HEAD = 64
PAIR = 2 * HEAD
CHUNK = 64
DECAY_LORA = 64
AAA_LORA = 64
GATE_LORA = 160
LORA_PAD = 512
N_MEM = 256
X_HEADS = 4
N_GROUPS = 4
EXP_PER_GROUP = 8
N_EXPERTS = N_GROUPS * EXP_PER_GROUP
RMS_EPS = 1e-6
LN_EPS = 1e-5
GN_EPS = 64e-5
DECAY_SCALE = math.exp(-0.5)
NEG_BIG = -1e30
MOE_BM = 256
LANES = 128
VMEM_LIMIT = 56 * 1024 * 1024


def _cparams(*sem):
    return pltpu.CompilerParams(dimension_semantics=sem, vmem_limit_bytes=VMEM_LIMIT)


def _dot(a, b):
    return jnp.dot(a, b, preferred_element_type=F32)


def _dot_nt(a, b):
    return lax.dot_general(a, b, (((1,), (1,)), ((), ())), preferred_element_type=F32)


def _split_dot(x, w_bf16):
    hi = x.astype(BF16)
    lo = (x - hi.astype(F32)).astype(BF16)
    return _dot(hi, w_bf16) + _dot(lo, w_bf16)


def _rms(x, g, eps=RMS_EPS):
    return x * lax.rsqrt(jnp.mean(x * x, axis=-1, keepdims=True) + eps) * g


def _norm_mm_kernel(x_ref, g_ref, w_ref, o_ref, xn_ref):
    @pl.when(pl.program_id(1) == 0)
    def _():
        xn_ref[...] = _rms(x_ref[...], g_ref[...]).astype(BF16)

    o_ref[...] = _dot(xn_ref[...], w_ref[...])


def _norm_matmul(x, g, w, bm, bn):
    m, k = x.shape
    n = w.shape[1]
    bm = min(bm, m)
    return pl.pallas_call(
        _norm_mm_kernel,
        grid=(m // bm, n // bn),
        in_specs=[pl.BlockSpec((bm, k), lambda i, j: (i, 0)),
                  pl.BlockSpec((1, k), lambda i, j: (0, 0)),
                  pl.BlockSpec((k, bn), lambda i, j: (0, j))],
        out_specs=pl.BlockSpec((bm, bn), lambda i, j: (i, j)),
        out_shape=jax.ShapeDtypeStruct((m, n), F32),
        scratch_shapes=[pltpu.VMEM((bm, k), BF16)],
        compiler_params=_cparams("parallel", "arbitrary"),
        name="norm_matmul",
    )(x, g.reshape(1, k), w)


def _norm_mm_heads_kernel(x_ref, g_ref, w_ref, o_ref):
    res = _dot(_rms(x_ref[...], g_ref[...]).astype(BF16), w_ref[...])
    dh = o_ref.shape[2]
    for h in range(o_ref.shape[1]):
        o_ref[:, h, :] = res[:, h * dh:(h + 1) * dh]


def _norm_matmul_heads(x, g, w, n_heads, bm):
    m, k = x.shape
    n = w.shape[1]
    bm = min(bm, m)
    return pl.pallas_call(
        _norm_mm_heads_kernel,
        grid=(m // bm,),
        in_specs=[pl.BlockSpec((bm, k), lambda i: (i, 0)),
                  pl.BlockSpec((1, k), lambda i: (0, 0)),
                  pl.BlockSpec((k, n), lambda i: (0, 0))],
        out_specs=pl.BlockSpec((bm, n_heads, n // n_heads), lambda i: (i, 0, 0)),
        out_shape=jax.ShapeDtypeStruct((m, n_heads, n // n_heads), F32),
        compiler_params=_cparams("parallel"),
        name="norm_matmul_heads",
    )(x, g.reshape(1, k), w)


def _mm_res_kernel(*refs, n_lhs):
    res_ref = refs[2 * n_lhs]
    o_ref = refs[2 * n_lhs + 1]
    acc = res_ref[...]
    for a_ref, w_ref in zip(refs[:n_lhs], refs[n_lhs:2 * n_lhs]):
        acc = acc + _dot(a_ref[...].astype(BF16), w_ref[...])
    o_ref[...] = acc


def _matmul_res(lhs, ws, res, bm):
    m, n = res.shape
    bm = min(bm, m)
    n_lhs = len(lhs)
    in_specs = [pl.BlockSpec((bm, a.shape[1]), lambda i: (i, 0)) for a in lhs]
    in_specs += [pl.BlockSpec(w.shape, lambda i: (0, 0)) for w in ws]
    in_specs += [pl.BlockSpec((bm, n), lambda i: (i, 0))]
    return pl.pallas_call(
        functools.partial(_mm_res_kernel, n_lhs=n_lhs),
        grid=(m // bm,),
        in_specs=in_specs,
        out_specs=pl.BlockSpec((bm, n), lambda i: (i, 0)),
        out_shape=jax.ShapeDtypeStruct((m, n), F32),
        compiler_params=_cparams("parallel"),
        name="matmul_res",
    )(*lhs, *ws, res)


def _ln_silu(cf, lg, lb):
    mu = jnp.mean(cf, axis=-1, keepdims=True)
    d = cf - mu
    var = jnp.mean(d * d, axis=-1, keepdims=True)
    y = d * lax.rsqrt(var + LN_EPS) * lg + lb
    return y * jax.nn.sigmoid(y)


def _conv_prefill_kernel(a_ref, g_ref, buf_ref, w_ref, cb_ref, lg_ref, lb_ref, c_ref, nc_ref,
                         uf_ref, cv_ref, *, tt, halo):
    t = pl.program_id(1)
    pad = 32 - halo

    @pl.when(t == 0)
    def _():
        uf_ref[pad:32, :] = buf_ref[0]

    @pl.when(t > 0)
    def _():
        uf_ref[pad:32, :] = uf_ref[tt + pad:tt + 32, :]

    uf_ref[32:32 + tt, :] = a_ref[...] * jax.nn.sigmoid(g_ref[...])

    width = uf_ref.shape[1]
    rb = 64
    for r0 in range(0, tt, rb):
        for l0 in range(0, width, LANES):
            acc = jnp.zeros((rb, LANES), F32)
            for j in range(CONV_K):
                acc = acc + (uf_ref[r0 + pad + j:r0 + pad + j + rb, l0:l0 + LANES]
                             * w_ref[j:j + 1, l0:l0 + LANES])
            cv_ref[r0:r0 + rb, l0:l0 + LANES] = acc

    c_ref[...] = _ln_silu(cv_ref[...] + cb_ref[...], lg_ref[...], lb_ref[...]).astype(c_ref.dtype)

    @pl.when(t == pl.num_programs(1) - 1)
    def _():
        nc_ref[0] = uf_ref[tt + pad:tt + 32, :]


def _conv_prefill(proj, conv_buf, conv_w, conv_b, ln_g, ln_b, batch, seq):
    cw = conv_w.shape[1]
    halo = CONV_K - 1
    tt = min(256, seq)
    nt = seq // tt
    row = lambda b, t: (b * nt + t, 0)
    vec = pl.BlockSpec((1, cw), lambda b, t: (0, 0))
    return pl.pallas_call(
        functools.partial(_conv_prefill_kernel, tt=tt, halo=halo),
        grid=(batch, nt),
        in_specs=[pl.BlockSpec((tt, cw), row),
                  pl.BlockSpec((tt, cw), lambda b, t: (b * nt + t, 1)),
                  pl.BlockSpec((1, halo, cw), lambda b, t: (b, 0, 0)),
                  pl.BlockSpec((CONV_K, cw), lambda b, t: (0, 0)),
                  vec, vec, vec],
        out_specs=[pl.BlockSpec((tt, cw), row),
                   pl.BlockSpec((1, halo, cw), lambda b, t: (b, 0, 0))],
        out_shape=[jax.ShapeDtypeStruct((batch * seq, cw), BF16),
                   jax.ShapeDtypeStruct((batch, halo, cw), F32)],
        scratch_shapes=[pltpu.VMEM((tt + 32, cw), F32), pltpu.VMEM((tt, cw), F32)],
        compiler_params=_cparams("parallel", "arbitrary"),
        name="conv_prefill",
    )(proj, proj, conv_buf, conv_w, conv_b.reshape(1, cw), ln_g.reshape(1, cw), ln_b.reshape(1, cw))


def _conv_decode_kernel(a_ref, g_ref, cache_ref, w_ref, cb_ref, lg_ref, lb_ref, c_ref, nc_ref):
    halo = CONV_K - 1
    u = a_ref[...] * jax.nn.sigmoid(g_ref[...])
    acc = u * w_ref[halo:halo + 1, :]
    for j in range(halo):
        acc = acc + cache_ref[:, j, :] * w_ref[j:j + 1, :]
    c_ref[...] = _ln_silu(acc + cb_ref[...], lg_ref[...], lb_ref[...]).astype(c_ref.dtype)
    nc_ref[:, 0:halo - 1, :] = cache_ref[:, 1:halo, :]
    nc_ref[:, halo - 1, :] = u


def _conv_decode(proj, cache, conv_w, conv_b, ln_g, ln_b):
    batch, halo, cw = cache.shape
    bb = 8
    vec = pl.BlockSpec((1, cw), lambda i: (0, 0))
    return pl.pallas_call(
        _conv_decode_kernel,
        grid=(batch // bb,),
        in_specs=[pl.BlockSpec((bb, cw), lambda i: (i, 0)),
                  pl.BlockSpec((bb, cw), lambda i: (i, 1)),
                  pl.BlockSpec((bb, halo, cw), lambda i: (i, 0, 0)),
                  pl.BlockSpec((CONV_K, cw), lambda i: (0, 0)),
                  vec, vec, vec],
        out_specs=[pl.BlockSpec((bb, cw), lambda i: (i, 0)),
                   pl.BlockSpec((bb, halo, cw), lambda i: (i, 0, 0))],
        out_shape=[jax.ShapeDtypeStruct((batch, cw), BF16),
                   jax.ShapeDtypeStruct((batch, halo, cw), F32)],
        compiler_params=_cparams("parallel"),
        name="conv_decode",
    )(proj, proj, cache, conv_w, conv_b.reshape(1, cw), ln_g.reshape(1, cw), ln_b.reshape(1, cw))


def _head_sum(x, bd_ref):
    blk = bd_ref.shape[0]
    parts = [_split_dot(x[:, l0:l0 + blk], bd_ref[...]) for l0 in range(0, x.shape[1], blk)]
    return jnp.concatenate(parts, axis=1)


def _prep_math(q, qp, mu_ref, wd_ref, wa_ref, wg_ref, db_ref, ab_ref, kk_ref, ka_ref, rk_ref, bd_ref):
    rw = q[0].shape[1]
    offs = (0, rw, 2 * rw, 3 * rw)
    r, k, v, lo = [x + (xp - x) * mu_ref[:, o:o + x.shape[1]] for x, xp, o in zip(q, qp, offs)]
    pwa = lo[:, 0:LANES]
    pg = lo[:, LANES:3 * LANES]
    dec_in = _dot(jnp.tanh(pwa).astype(BF16), wd_ref[...])
    a_in = _dot(pwa.astype(BF16), wa_ref[...])
    gate = _dot(jax.nn.sigmoid(pg).astype(BF16), wg_ref[...])
    logw = -DECAY_SCALE * jax.nn.sigmoid(db_ref[...] + dec_in)
    a = jax.nn.sigmoid(ab_ref[...] + a_in)
    kk = k * kk_ref[...]
    kk = kk / jnp.maximum(jnp.sqrt(_head_sum(kk * kk, bd_ref)), 1e-12)
    k2 = k * (1.0 + (a - 1.0) * ka_ref[...])
    bonus = _head_sum(r * k2 * rk_ref[...], bd_ref) * v
    return r, logw, k2, v, kk, a, bonus, gate


def _prep_prefill_kernel(r_ref, k_ref, v_ref, lo_ref, sb_ref, mu_ref, wd_ref, wa_ref, wg_ref, db_ref,
                         ab_ref, kk_ref, ka_ref, rk_ref, bd_ref, *rest):
    outs = rest[:8]
    carry_ref = rest[8]
    t = pl.program_id(1)

    @pl.when(t == 0)
    def _():
        carry_ref[0:1, :] = sb_ref[0]

    q = [r_ref[...], k_ref[...], v_ref[...], lo_ref[...]]
    tt = q[0].shape[0]
    first = lax.broadcasted_iota(jnp.int32, (tt, 1), 0) == 0
    qp = []
    off = 0
    for x in q:
        w = x.shape[1]
        qp.append(jnp.where(first, carry_ref[0:1, off:off + w], pltpu.roll(x, 1, 0)))
        off += w
    off = 0
    for x in q:
        w = x.shape[1]
        carry_ref[0:1, off:off + w] = x[tt - 1:tt, :]
        off += w
    res = _prep_math(q, qp, mu_ref, wd_ref, wa_ref, wg_ref, db_ref, ab_ref, kk_ref, ka_ref, rk_ref, bd_ref)
    for o_ref, val in zip(outs, res):
        o_ref[...] = val


def _prep_decode_kernel(r_ref, k_ref, v_ref, lo_ref, rp_ref, kp_ref, vp_ref, lop_ref, mu_ref, wd_ref,
                        wa_ref, wg_ref, db_ref, ab_ref, kk_ref, ka_ref, rk_ref, bd_ref, *outs):
    q = [r_ref[...], k_ref[...], v_ref[...], lo_ref[...]]
    qp = [rp_ref[...], kp_ref[...], vp_ref[...], lop_ref[...]]
    res = _prep_math(q, qp, mu_ref, wd_ref, wa_ref, wg_ref, db_ref, ab_ref, kk_ref, ka_ref, rk_ref, bd_ref)
    for o_ref, val in zip(outs, res):
        o_ref[...] = val


def _prep_param_specs(rw, idx):
    full = lambda shape: pl.BlockSpec(shape, idx)
    vec = full((1, rw))
    return [full((1, 3 * rw + LORA_PAD)), full((LANES, rw)), full((LANES, rw)), full((2 * LANES, rw)),
            vec, vec, vec, vec, vec, full((2 * LANES, 2 * LANES))]


def _rwkv_prep_prefill(proj, shift_buf, pp, batch, seq, rw):
    tt = min(256, seq)
    nt = seq // tt
    lora_blk = (2 * rw + 3 * rw) // LORA_PAD
    col = lambda c: (lambda b, t: (b * nt + t, c))
    qw = 3 * rw + LORA_PAD
    in_specs = [pl.BlockSpec((tt, rw), col(2)), pl.BlockSpec((tt, rw), col(3)),
                pl.BlockSpec((tt, rw), col(4)), pl.BlockSpec((tt, LORA_PAD), col(lora_blk)),
                pl.BlockSpec((1, 1, qw), lambda b, t: (b, 0, 0))]
    in_specs += _prep_param_specs(rw, lambda b, t: (0, 0))
    out_spec = pl.BlockSpec((tt, rw), col(0))
    return pl.pallas_call(
        _prep_prefill_kernel,
        grid=(batch, nt),
        in_specs=in_specs,
        out_specs=[out_spec] * 8,
        out_shape=[jax.ShapeDtypeStruct((batch * seq, rw), F32)] * 8,
        scratch_shapes=[pltpu.VMEM((8, qw), F32)],
        compiler_params=_cparams("parallel", "arbitrary"),
        name="rwkv_prep_prefill",
    )(proj, proj, proj, proj, shift_buf.reshape(batch, 1, qw), *pp)


def _rwkv_prep_decode(proj, shift_state, pp, rw):
    batch = proj.shape[0]
    bb = min(128, batch)
    lora_blk = (2 * rw + 3 * rw) // LORA_PAD
    col = lambda c: (lambda i: (i, c))
    in_specs = [pl.BlockSpec((bb, rw), col(2)), pl.BlockSpec((bb, rw), col(3)),
                pl.BlockSpec((bb, rw), col(4)), pl.BlockSpec((bb, LORA_PAD), col(lora_blk)),
                pl.BlockSpec((bb, rw), col(0)), pl.BlockSpec((bb, rw), col(1)),
                pl.BlockSpec((bb, rw), col(2)), pl.BlockSpec((bb, LORA_PAD), col(3 * rw // LORA_PAD))]
    in_specs += _prep_param_specs(rw, lambda i: (0, 0))
    return pl.pallas_call(
        _prep_decode_kernel,
        grid=(batch // bb,),
        in_specs=in_specs,
        out_specs=[pl.BlockSpec((bb, rw), col(0))] * 8,
        out_shape=[jax.ShapeDtypeStruct((batch, rw), F32)] * 8,
        compiler_params=_cparams("parallel"),
        name="rwkv_prep_decode",
    )(proj, proj, proj, proj, shift_state, shift_state, shift_state, shift_state, *pp)


def _stack2(x, smask):
    return jnp.where(smask, jnp.concatenate([x, x], axis=0), 0.0)


def _rwkv_chunk_kernel(r_ref, lw_ref, k_ref, v_ref, kk_ref, a_ref, s0_ref, o_ref, so_ref, s_ref):
    c = pl.program_id(1)
    cs = r_ref.shape[0]
    n_pairs = r_ref.shape[1] // PAIR
    two = 2 * cs

    @pl.when(c == 0)
    def _():
        z = jnp.zeros((HEAD, HEAD), F32)
        for p in range(n_pairs):
            top = jnp.concatenate([s0_ref[0, 2 * p], z], axis=1)
            bot = jnp.concatenate([z, s0_ref[0, 2 * p + 1]], axis=1)
            s_ref[p] = jnp.concatenate([top, bot], axis=0)

    ri = lax.broadcasted_iota(jnp.int32, (two, two), 0)
    ci = lax.broadcasted_iota(jnp.int32, (two, two), 1)
    strict = ci < ri
    incl = ci <= ri
    eye = (ci == ri).astype(F32)
    smask = (lax.broadcasted_iota(jnp.int32, (two, PAIR), 0) < cs) == (
        lax.broadcasted_iota(jnp.int32, (two, PAIR), 1) < HEAD)
    tri = (lax.broadcasted_iota(jnp.int32, (cs, cs), 1)
           <= lax.broadcasted_iota(jnp.int32, (cs, cs), 0)).astype(BF16)

    lw_all = lw_ref[...]
    lw_hi = lw_all.astype(BF16)
    lw_lo = (lw_all - lw_hi.astype(F32)).astype(BF16)
    cum_all = _dot(tri, lw_hi) + _dot(tri, lw_lo)

    pairs = range(n_pairs)
    cat = jnp.concatenate
    prep = []
    for p in pairs:
        sl = slice(p * PAIR, (p + 1) * PAIR)
        lw = lw_all[:, sl]
        cum = cum_all[:, sl]
        tot = cum[cs - 1:cs, :]
        g_inv = jnp.exp(-cum)
        g_end = jnp.exp(tot - cum)
        kk = kk_ref[:, sl]
        k2 = k_ref[:, sl]
        bb = kk * a_ref[:, sl]
        prep.append(dict(
            g_tot=jnp.exp(tot),
            a_b=_stack2(kk * jnp.exp(cum - lw), smask).astype(BF16),
            r_s=_stack2(r_ref[:, sl] * jnp.exp(cum), smask),
            bk=cat([_stack2(bb * g_inv, smask), _stack2(k2 * g_inv, smask)], axis=0).astype(BF16),
            v_s=_stack2(v_ref[:, sl], smask),
            bg_s=_stack2(bb * g_end, smask).astype(BF16),
            kg_s=_stack2(k2 * g_end, smask).astype(BF16)))
    a_b = [q["a_b"] for q in prep]
    r_s = [q["r_s"] for q in prep]
    v_s = [q["v_s"] for q in prep]
    v_b = [x.astype(BF16) for x in v_s]

    gram = [_dot_nt(cat([a_b[p], r_s[p].astype(BF16)], axis=0), prep[p]["bk"]) for p in pairs]
    l_ab = [jnp.where(strict, g[0:two, 0:two], 0.0) for g in gram]
    l_ak = [jnp.where(strict, g[0:two, two:], 0.0).astype(BF16) for g in gram]
    m_rb = [jnp.where(incl, g[two:, 0:two], 0.0).astype(BF16) for g in gram]
    m_rk = [jnp.where(incl, g[two:, two:], 0.0).astype(BF16) for g in gram]

    tm = [eye - x for x in l_ab]
    pw = l_ab
    n = 1
    while 2 * n < cs:
        pw = [_dot(x.astype(BF16), x.astype(BF16)) for x in pw]
        tm = [t + _dot(t.astype(BF16), x.astype(BF16)) for t, x in zip(tm, pw)]
        n *= 2
    tm_b = [t.astype(BF16) for t in tm]

    w1 = [_dot(l_ak[p], v_b[p]) for p in pairs]
    ua = [_dot(tm_b[p], cat([w1[p].astype(BF16), a_b[p]], axis=1)) for p in pairs]
    ua_b = [x.astype(BF16) for x in ua]
    mrb_ua = [_dot(m_rb[p], ua_b[p]) for p in pairs]
    o0 = [_dot(m_rk[p], v_b[p]) - mrb_ua[p][:, 0:PAIR] for p in pairs]
    rt = [(r_s[p] - mrb_ua[p][:, PAIR:]).astype(BF16) for p in pairs]

    s_old = [s_ref[p] for p in pairs]
    s_b = [x.astype(BF16) for x in s_old]
    ua_t = [cat([x[:, 0:PAIR].T, x[:, PAIR:].T], axis=0).astype(BF16) for x in ua]
    uat_bg = [_dot(ua_t[p], prep[p]["bg_s"]) for p in pairs]
    vt_kg = [_dot(v_s[p].T.astype(BF16), prep[p]["kg_s"]) for p in pairs]
    s_new = [s_old[p] * prep[p]["g_tot"] - _dot(s_b[p], uat_bg[p][PAIR:, :].astype(BF16))
             + vt_kg[p] - uat_bg[p][0:PAIR, :] for p in pairs]
    o_st = [o0[p] + _dot_nt(rt[p], s_b[p]) for p in pairs]

    o_ref[...] = cat([x[0:cs, :] + x[cs:two, :] for x in o_st], axis=1)
    s_ref[...] = jnp.stack(s_new, axis=0)

    @pl.when(c == pl.num_programs(1) - 1)
    def _():
        for p in range(n_pairs):
            s = s_ref[p]
            so_ref[0, 2 * p] = s[0:HEAD, 0:HEAD]
            so_ref[0, 2 * p + 1] = s[HEAD:PAIR, HEAD:PAIR]


def _rwkv_chunked(r, logw, k2, v, kk, a, s0, batch, seq):
    rw = r.shape[1]
    nc = seq // CHUNK
    heads = rw // HEAD
    row = pl.BlockSpec((CHUNK, rw), lambda b, c: (b * nc + c, 0))
    st = pl.BlockSpec((1, heads, HEAD, HEAD), lambda b, c: (b, 0, 0, 0))
    return pl.pallas_call(
        _rwkv_chunk_kernel,
        grid=(batch, nc),
        in_specs=[row] * 6 + [st],
        out_specs=[row, st],
        out_shape=[jax.ShapeDtypeStruct((batch * seq, rw), F32),
                   jax.ShapeDtypeStruct((batch, heads, HEAD, HEAD), F32)],
        scratch_shapes=[pltpu.VMEM((rw // PAIR, PAIR, PAIR), F32)],
        compiler_params=_cparams("parallel", "arbitrary"),
        name="rwkv_chunked",
    )(r, logw, k2, v, kk, a, s0)


def _rwkv_step_kernel(r_ref, lw_ref, k_ref, v_ref, kk_ref, a_ref, s_ref, o_ref, so_ref, *, heads):
    bb = s_ref.shape[0]
    eye = (lax.broadcasted_iota(jnp.int32, (HEAD, HEAD), 0)
           == lax.broadcasted_iota(jnp.int32, (HEAD, HEAD), 1))

    def body(bi, carry):
        for h in range(heads):
            row = pl.ds(bi * heads + h, 1)
            kk = kk_ref[row, :]
            s = s_ref[bi, h]
            sa = jnp.sum(s * kk, axis=1, keepdims=True)
            v_col = jnp.sum(jnp.where(eye, v_ref[row, :], 0.0), axis=1, keepdims=True)
            s_new = (s * jnp.exp(lw_ref[row, :]) - sa * (kk * a_ref[row, :]) + v_col * k_ref[row, :])
            so_ref[bi, h] = s_new
            o_col = jnp.sum(s_new * r_ref[row, :], axis=1, keepdims=True)
            o_ref[row, :] = jnp.sum(jnp.where(eye, o_col, 0.0), axis=0, keepdims=True)
        return carry

    lax.fori_loop(0, bb, body, 0)


def _rwkv_step(r, logw, k2, v, kk, a, s0):
    batch, rw = r.shape
    heads = rw // HEAD
    bb = 8
    flat = lambda x: x.reshape(batch * heads, HEAD)
    row = pl.BlockSpec((bb * heads, HEAD), lambda i: (i, 0))
    st = pl.BlockSpec((bb, heads, HEAD, HEAD), lambda i: (i, 0, 0, 0))
    o, s_new = pl.pallas_call(
        functools.partial(_rwkv_step_kernel, heads=heads),
        grid=(batch // bb,),
        in_specs=[row] * 6 + [st],
        out_specs=[row, st],
        out_shape=[jax.ShapeDtypeStruct((batch * heads, HEAD), F32),
                   jax.ShapeDtypeStruct(s0.shape, F32)],
        compiler_params=_cparams("parallel"),
        name="rwkv_step",
    )(flat(r), flat(logw), flat(k2), flat(v), flat(kk), flat(a), s0)
    return o.reshape(batch, rw), s_new


def _rwkv_post_kernel(o_ref, bonus_ref, gate_ref, lg_ref, lb_ref, bd_ref, y_ref):
    o = o_ref[...]
    mu = _head_sum(o, bd_ref) * (1.0 / HEAD)
    d = o - mu
    var = _head_sum(d * d, bd_ref) * (1.0 / HEAD)
    y = d * lax.rsqrt(var + GN_EPS) * lg_ref[...] + lb_ref[...]
    y_ref[...] = ((y + bonus_ref[...]) * gate_ref[...]).astype(y_ref.dtype)


def _rwkv_post(o, bonus, gate, lnx_g, lnx_b, bd):
    m, rw = o.shape
    bm = min(512, m)
    row = pl.BlockSpec((bm, rw), lambda i: (i, 0))
    vec = pl.BlockSpec((1, rw), lambda i: (0, 0))
    return pl.pallas_call(
        _rwkv_post_kernel,
        grid=(m // bm,),
        in_specs=[row, row, row, vec, vec, pl.BlockSpec(bd.shape, lambda i: (0, 0))],
        out_specs=row,
        out_shape=jax.ShapeDtypeStruct((m, rw), BF16),
        compiler_params=_cparams("parallel"),
        name="rwkv_post",
    )(o, bonus, gate, lnx_g.reshape(1, rw), lnx_b.reshape(1, rw), bd)


def _attn_prefill_kernel(q_ref, k_ref, v_ref, o_ref, *, n_heads):
    d = q_ref.shape[1] // n_heads
    scale = d ** -0.5
    for h in range(n_heads):
        sl = slice(h * d, (h + 1) * d)
        s = _dot_nt(q_ref[:, sl].astype(BF16), k_ref[:, h, :].astype(BF16)) * scale
        p = jnp.exp(s - jnp.max(s, axis=-1, keepdims=True))
        att = p / jnp.sum(p, axis=-1, keepdims=True)
        o_ref[:, sl] = _dot(att.astype(BF16), v_ref[:, h, :].astype(BF16)).astype(o_ref.dtype)


def _attn_prefill(q, mem_k, mem_v, batch, seq):
    d = q.shape[1]
    tt = min(512, seq)
    nt = seq // tt
    kv = pl.BlockSpec((N_MEM, X_HEADS, d // X_HEADS), lambda b, t: (b, 0, 0))
    row = pl.BlockSpec((tt, d), lambda b, t: (b * nt + t, 0))
    return pl.pallas_call(
        functools.partial(_attn_prefill_kernel, n_heads=X_HEADS),
        grid=(batch, nt),
        in_specs=[row, kv, kv],
        out_specs=row,
        out_shape=jax.ShapeDtypeStruct((batch * seq, d), BF16),
        compiler_params=_cparams("parallel", "arbitrary"),
        name="attn_prefill",
    )(q, mem_k, mem_v)


def _attn_decode_kernel(q_ref, k_ref, v_ref, o_ref, *, n_heads):
    bb = q_ref.shape[0]
    d = q_ref.shape[2] // n_heads
    scale = d ** -0.5
    for bi in range(bb):
        q = q_ref[bi]
        for h in range(n_heads):
            sl = slice(h * d, (h + 1) * d)
            s = jnp.sum(k_ref[bi, :, h, :] * q[:, sl], axis=1, keepdims=True) * scale
            p = jnp.exp(s - jnp.max(s, axis=0, keepdims=True))
            att = p / jnp.sum(p, axis=0, keepdims=True)
            o_ref[bi, :, sl] = jnp.sum(att * v_ref[bi, :, h, :], axis=0, keepdims=True)


def _attn_decode(q, cache_k, cache_v):
    batch, d = q.shape
    bb = 2
    kv = pl.BlockSpec((bb, N_MEM, X_HEADS, d // X_HEADS), lambda i: (i, 0, 0, 0))
    row = pl.BlockSpec((bb, 1, d), lambda i: (i, 0, 0))
    out = pl.pallas_call(
        functools.partial(_attn_decode_kernel, n_heads=X_HEADS),
        grid=(batch // bb,),
        in_specs=[row, kv, kv],
        out_specs=row,
        out_shape=jax.ShapeDtypeStruct((batch, 1, d), F32),
        compiler_params=_cparams("parallel"),
        name="attn_decode",
    )(q.reshape(batch, 1, d), cache_k, cache_v)
    return out.reshape(batch, d)


def _route(x_ref, g_ref, wh_ref, wl_ref, b_ref):
    h = _rms(x_ref[...], g_ref[...])
    hh = h.astype(BF16)
    hl = (h - hh.astype(F32)).astype(BF16)
    logits = _dot(hh, wh_ref[...]) + _dot(hl, wh_ref[...]) + _dot(hh, wl_ref[...]) + b_ref[...]
    lane = lax.broadcasted_iota(jnp.int32, (1, LANES), 1).astype(F32)
    is_g = (lane >= N_EXPERTS) & (lane < N_EXPERTS + N_GROUPS)
    lgm = jnp.where(is_g, logits, NEG_BIG)
    gmax = jnp.max(lgm, axis=1, keepdims=True)
    gsum = jnp.sum(jnp.where(is_g, jnp.exp(lgm - gmax), 0.0), axis=1, keepdims=True)
    g_val = 1.0 / gsum
    g_idx = jnp.min(jnp.where(is_g & (lgm == gmax), lane - N_EXPERTS, 1e9), axis=1, keepdims=True)
    in_grp = (lane < N_EXPERTS) & (jnp.floor(lane * (1.0 / EXP_PER_GROUP)) == g_idx)
    le = jnp.where(in_grp, logits, NEG_BIG)
    m1 = jnp.max(le, axis=1, keepdims=True)
    i1 = jnp.min(jnp.where(in_grp & (le == m1), lane, 1e9), axis=1, keepdims=True)
    rest = in_grp & (lane != i1)
    le2 = jnp.where(rest, logits, NEG_BIG)
    m2 = jnp.max(le2, axis=1, keepdims=True)
    i2 = jnp.min(jnp.where(rest & (le2 == m2), lane, 1e9), axis=1, keepdims=True)
    e2 = jnp.exp(m2 - m1)
    den = 1.0 + e2
    w1 = (1.0 / den) * g_val
    w2 = (e2 / den) * g_val
    return h, lane, i1, i2, w1, w2


def _router_kernel(x_ref, g_ref, wh_ref, wl_ref, b_ref, h_ref, comb_ref):
    h, lane, i1, i2, w1, w2 = _route(x_ref, g_ref, wh_ref, wl_ref, b_ref)
    h_ref[...] = h.astype(BF16)
    comb_ref[...] = jnp.where(lane == i1, w1, 0.0) + jnp.where(lane == i2, w2, 0.0)


def _router_sorted_kernel(x_ref, g_ref, wh_ref, wl_ref, b_ref, h_ref, route_ref, cnt_ref, run_ref):
    @pl.when(pl.program_id(0) == 0)
    def _():
        run_ref[...] = jnp.zeros_like(run_ref)

    h, lane, i1, i2, w1, w2 = _route(x_ref, g_ref, wh_ref, wl_ref, b_ref)
    h_ref[...] = h
    bm = h.shape[0]
    oh1 = lane == i1
    oh2 = lane == i2
    sel = (oh1 | oh2).astype(BF16)
    before = (lax.broadcasted_iota(jnp.int32, (bm, bm), 1)
              < lax.broadcasted_iota(jnp.int32, (bm, bm), 0)).astype(BF16)
    base = run_ref[0:1, :] + _dot(before, sel)
    rank1 = jnp.sum(jnp.where(oh1, base, 0.0), axis=1, keepdims=True)
    rank2 = jnp.sum(jnp.where(oh2, base, 0.0), axis=1, keepdims=True)
    total = run_ref[0:1, :] + jnp.sum(sel.astype(F32), axis=0, keepdims=True)
    run_ref[0:1, :] = total
    cnt_ref[...] = total
    vals = (i1, i2, w1, w2, rank1, rank2)
    route = jnp.zeros((bm, LANES), F32)
    for idx, val in enumerate(vals):
        route = jnp.where(lane == idx, val, route)
    route_ref[...] = route


def _router_sorted(x, g, wh, wl, bias):
    m, d = x.shape
    bm = min(512, m)
    return pl.pallas_call(
        _router_sorted_kernel,
        grid=(m // bm,),
        in_specs=[pl.BlockSpec((bm, d), lambda i: (i, 0)),
                  pl.BlockSpec((1, d), lambda i: (0, 0)),
                  pl.BlockSpec((d, LANES), lambda i: (0, 0)),
                  pl.BlockSpec((d, LANES), lambda i: (0, 0)),
                  pl.BlockSpec((1, LANES), lambda i: (0, 0))],
        out_specs=[pl.BlockSpec((bm, d), lambda i: (i, 0)),
                   pl.BlockSpec((bm, LANES), lambda i: (i, 0)),
                   pl.BlockSpec((1, LANES), lambda i: (0, 0))],
        out_shape=[jax.ShapeDtypeStruct((m, d), F32),
                   jax.ShapeDtypeStruct((m, LANES), F32),
                   jax.ShapeDtypeStruct((1, LANES), F32)],
        scratch_shapes=[pltpu.VMEM((8, LANES), F32)],
        compiler_params=_cparams("arbitrary"),
        name="moe_router_sorted",
    )(x, g.reshape(1, d), wh, wl, bias)


def _row_copy(src_hbm, src_row, dst, dst_row, sem):
    return pltpu.make_async_copy(src_hbm.at[pl.ds(src_row, 1)], dst.at[pl.ds(dst_row, 1)], sem)


def _dispatch_kernel(src_ref, nrow_ref, h_hbm, xs_hbm, zero_ref, sem, zsem):
    n_chunks = nrow_ref[0] // MOE_BM
    n_max = xs_hbm.shape[0] // MOE_BM
    zero_ref[...] = jnp.zeros_like(zero_ref)

    def zero_copy(c):
        return pltpu.make_async_copy(zero_ref, xs_hbm.at[pl.ds(c * MOE_BM, MOE_BM)], zsem)

    def zero_start(c, carry):
        zero_copy(c).start()
        return carry

    lax.fori_loop(n_chunks, n_max, zero_start, 0)

    def wait_chunk():
        pltpu.make_async_copy(h_hbm.at[pl.ds(0, MOE_BM)], xs_hbm.at[pl.ds(0, MOE_BM)], sem).wait()

    def chunk(c, carry):
        def issue(r, carry2):
            p = c * MOE_BM + r
            _row_copy(h_hbm, src_ref[p], xs_hbm, p, sem).start()
            return carry2

        lax.fori_loop(0, MOE_BM, issue, 0, unroll=8)

        @pl.when(c > 0)
        def _():
            wait_chunk()

        return carry

    lax.fori_loop(0, n_chunks, chunk, 0)

    @pl.when(n_chunks > 0)
    def _():
        wait_chunk()

    def zero_wait(c, carry):
        zero_copy(c).wait()
        return carry

    lax.fori_loop(n_chunks, n_max, zero_wait, 0)


def _dispatch(h, src, n_rows_used, n_rows_max):
    d = h.shape[1]
    return pl.pallas_call(
        _dispatch_kernel,
        grid_spec=pltpu.PrefetchScalarGridSpec(
            num_scalar_prefetch=2, grid=(1,),
            in_specs=[pl.BlockSpec(memory_space=pl.ANY)],
            out_specs=pl.BlockSpec(memory_space=pl.ANY),
            scratch_shapes=[pltpu.VMEM((MOE_BM, d), F32), pltpu.SemaphoreType.DMA(()),
                            pltpu.SemaphoreType.DMA(())]),
        out_shape=jax.ShapeDtypeStruct((n_rows_max, d), F32),
        compiler_params=_cparams("arbitrary"),
        name="moe_dispatch",
    )(src, n_rows_used, h)


def _experts_sorted_kernel(te_ref, nt_ref, xs_ref, wg_ref, wu_ref, wd_ref, ys_ref):
    @pl.when(pl.program_id(0) < nt_ref[0])
    def _():
        x = xs_ref[...].astype(BF16)
        hg = _dot(x, wg_ref[0])
        hu = _dot(x, wu_ref[0])
        act = hg * jax.nn.sigmoid(hg) * hu
        ys_ref[...] = _dot(act.astype(BF16), wd_ref[0])

    @pl.when(pl.program_id(0) >= nt_ref[0])
    def _():
        ys_ref[...] = jnp.zeros_like(ys_ref)


def _experts_sorted(xs, tile_expert, n_tiles_used, wg, wu, wd):
    n_rows, d = xs.shape
    de = wg.shape[2]
    n_tiles = n_rows // MOE_BM
    row_in = lambda j, te, nt: (jnp.minimum(j, nt[0] - 1), 0)
    wsel = lambda j, te, nt: (te[j], 0, 0)
    return pl.pallas_call(
        _experts_sorted_kernel,
        grid_spec=pltpu.PrefetchScalarGridSpec(
            num_scalar_prefetch=2, grid=(n_tiles,),
            in_specs=[pl.BlockSpec((MOE_BM, d), row_in),
                      pl.BlockSpec((1, d, de), wsel),
                      pl.BlockSpec((1, d, de), wsel),
                      pl.BlockSpec((1, de, d), wsel)],
            out_specs=pl.BlockSpec((MOE_BM, d), lambda j, te, nt: (j, 0))),
        out_shape=jax.ShapeDtypeStruct((n_rows, d), F32),
        compiler_params=_cparams("arbitrary"),
        name="moe_experts_sorted",
    )(tile_expert, n_tiles_used, xs, wg, wu, wd)


def _combine_kernel(p1_ref, p2_ref, ys_hbm, x_ref, route_ref, nf_ref, y_ref, buf_ref, sem):
    i = pl.program_id(0)
    n = pl.num_programs(0)
    bm = x_ref.shape[0]

    def issue(tile, slot):
        def body(r, carry):
            t = tile * bm + r
            _row_copy(ys_hbm, p1_ref[t], buf_ref.at[slot, 0], r, sem.at[slot]).start()
            _row_copy(ys_hbm, p2_ref[t], buf_ref.at[slot, 1], r, sem.at[slot]).start()
            return carry

        lax.fori_loop(0, bm, body, 0, unroll=8)

    @pl.when(i == 0)
    def _():
        issue(0, 0)

    @pl.when(i + 1 < n)
    def _():
        issue(i + 1, (i + 1) % 2)

    slot = i % 2
    for k in range(2):
        pltpu.make_async_copy(ys_hbm.at[pl.ds(0, bm)], buf_ref.at[slot, k], sem.at[slot]).wait()
    lane = lax.broadcasted_iota(jnp.int32, (1, LANES), 1)
    route = route_ref[...]
    w1 = jnp.sum(jnp.where(lane == 2, route, 0.0), axis=1, keepdims=True)
    w2 = jnp.sum(jnp.where(lane == 3, route, 0.0), axis=1, keepdims=True)
    x3 = x_ref[...] + w1 * buf_ref[slot, 0] + w2 * buf_ref[slot, 1]
    y_ref[...] = _rms(x3, nf_ref[...])


def _combine(ys, pos1, pos2, x, route, norm_final):
    m, d = x.shape
    bm = min(MOE_BM, m)
    return pl.pallas_call(
        _combine_kernel,
        grid_spec=pltpu.PrefetchScalarGridSpec(
            num_scalar_prefetch=2, grid=(m // bm,),
            in_specs=[pl.BlockSpec(memory_space=pl.ANY),
                      pl.BlockSpec((bm, d), lambda i, p1, p2: (i, 0)),
                      pl.BlockSpec((bm, LANES), lambda i, p1, p2: (i, 0)),
                      pl.BlockSpec((1, d), lambda i, p1, p2: (0, 0))],
            out_specs=pl.BlockSpec((bm, d), lambda i, p1, p2: (i, 0)),
            scratch_shapes=[pltpu.VMEM((2, 2, bm, d), F32), pltpu.SemaphoreType.DMA((2,))]),
        out_shape=jax.ShapeDtypeStruct((m, d), F32),
        compiler_params=_cparams("arbitrary"),
        name="moe_combine",
    )(pos1, pos2, ys, x, route, norm_final.reshape(1, d))


def _moe_sorted(x, g, wh, wl, bias, wg, wu, wd, norm_final):
    m, d = x.shape
    h, route, counts = _router_sorted(x, g, wh, wl, bias)
    e1 = route[:, 0].astype(jnp.int32)
    e2 = route[:, 1].astype(jnp.int32)
    rank1 = route[:, 4].astype(jnp.int32)
    rank2 = route[:, 5].astype(jnp.int32)
    cnt = counts[0, :N_EXPERTS].astype(jnp.int32)
    padded = (cnt + MOE_BM - 1) // MOE_BM * MOE_BM
    seg_end = jnp.cumsum(padded)
    seg_start = seg_end - padded
    pos1 = seg_start[e1] + rank1
    pos2 = seg_start[e2] + rank2
    n_rows_max = (2 * m // MOE_BM + N_EXPERTS) * MOE_BM
    n_tiles_max = n_rows_max // MOE_BM
    n_rows_used = seg_end[-1:]
    n_tiles_used = n_rows_used // MOE_BM
    tile_start = jnp.arange(n_tiles_max, dtype=jnp.int32) * MOE_BM
    tile_expert = jnp.searchsorted(seg_end, tile_start, side="right").astype(jnp.int32)
    last_expert = jnp.max(jnp.where(cnt > 0, jnp.arange(N_EXPERTS, dtype=jnp.int32), 0))
    tile_expert = jnp.minimum(tile_expert, last_expert)
    tok = jnp.arange(m, dtype=jnp.int32)
    src = jnp.zeros((n_rows_max,), jnp.int32).at[pos1].set(tok).at[pos2].set(tok)
    xs = _dispatch(h, src, n_rows_used, n_rows_max)
    ys = _experts_sorted(xs, tile_expert, n_tiles_used, wg, wu, wd)
    return _combine(ys, pos1, pos2, x, route, norm_final)


def _router(x, g, wh, wl, bias):
    m, d = x.shape
    bm = min(512, m)
    return pl.pallas_call(
        _router_kernel,
        grid=(m // bm,),
        in_specs=[pl.BlockSpec((bm, d), lambda i: (i, 0)),
                  pl.BlockSpec((1, d), lambda i: (0, 0)),
                  pl.BlockSpec((d, LANES), lambda i: (0, 0)),
                  pl.BlockSpec((d, LANES), lambda i: (0, 0)),
                  pl.BlockSpec((1, LANES), lambda i: (0, 0))],
        out_specs=[pl.BlockSpec((bm, d), lambda i: (i, 0)),
                   pl.BlockSpec((bm, LANES), lambda i: (i, 0))],
        out_shape=[jax.ShapeDtypeStruct((m, d), BF16),
                   jax.ShapeDtypeStruct((m, LANES), F32)],
        compiler_params=_cparams("parallel"),
        name="moe_router",
    )(x, g.reshape(1, d), wh, wl, bias)


def _moe_kernel(h_ref, comb_ref, wg_ref, wu_ref, wd_ref, x_ref, nf_ref, y_ref, acc_ref):
    e = pl.program_id(1)

    @pl.when(e == 0)
    def _():
        acc_ref[...] = x_ref[...]

    h = h_ref[...]
    hg = _dot(h, wg_ref[0])
    hu = _dot(h, wu_ref[0])
    lane = lax.broadcasted_iota(jnp.int32, (1, LANES), 1)
    cw = jnp.sum(jnp.where(lane == e, comb_ref[...], 0.0), axis=1, keepdims=True)
    act = hg * jax.nn.sigmoid(hg) * hu * cw
    acc_ref[...] += _dot(act.astype(BF16), wd_ref[0])

    @pl.when(e == pl.num_programs(1) - 1)
    def _():
        y_ref[...] = _rms(acc_ref[...], nf_ref[...])


def _moe_dense(h, comb, wg, wu, wd, x, norm_final):
    m, d = x.shape
    de = wg.shape[2]
    bm = min(512, m)
    return pl.pallas_call(
        _moe_kernel,
        grid=(m // bm, N_EXPERTS),
        in_specs=[pl.BlockSpec((bm, d), lambda i, e: (i, 0)),
                  pl.BlockSpec((bm, LANES), lambda i, e: (i, 0)),
                  pl.BlockSpec((1, d, de), lambda i, e: (e, 0, 0)),
                  pl.BlockSpec((1, d, de), lambda i, e: (e, 0, 0)),
                  pl.BlockSpec((1, de, d), lambda i, e: (e, 0, 0)),
                  pl.BlockSpec((bm, d), lambda i, e: (i, 0)),
                  pl.BlockSpec((1, d), lambda i, e: (0, 0))],
        out_specs=pl.BlockSpec((bm, d), lambda i, e: (i, 0)),
        out_shape=jax.ShapeDtypeStruct((m, d), F32),
        scratch_shapes=[pltpu.VMEM((bm, d), F32)],
        compiler_params=_cparams("parallel", "arbitrary"),
        name="moe_experts",
    )(h, comb, wg, wu, wd, x, norm_final.reshape(1, d))


def _pad_cols(x, n):
    return jnp.pad(x, ((0, 0), (0, n - x.shape[1])))


def _block_diag_ones(n, blk):
    i = jnp.arange(n) // blk
    return (i[:, None] == i[None, :]).astype(BF16)


def kernel(x_prompt, x_sample, mem_prompt, cache_conv, state_shift, state_rwkv, cache_mem_k, cache_mem_v,
           norm_mix, w_in, conv_w, conv_b, conv_ln_g, conv_ln_b, shift_mu, w_decay_up, decay_bias, w_a_up,
           a_bias, w_g_up, k_k, k_a, r_k, lnx_g, lnx_b, w_out, norm_x, norm_mem, w_cq, w_ck, w_cv, w_co,
           norm_ffn, w_route_group, b_route_group, w_route_expert, b_route_expert, w_gate, w_up, w_down,
           norm_final):
    depth = w_in.shape[0]
    batch, seq, d = x_prompt.shape
    dec_batch = x_sample.shape[0]
    assert depth == 1
    assert x_sample.shape[1] == 1 and seq % CHUNK == 0 and seq >= CONV_K - 1
    cw = conv_w.shape[2]
    rw = w_decay_up.shape[2]
    heads = rw // HEAD
    shift_w = shift_mu.shape[1]
    in_w = w_in.shape[2]
    assert in_w == 2 * cw + shift_w and shift_w == 3 * rw + DECAY_LORA + AAA_LORA + GATE_LORA
    assert cw == rw and rw % LORA_PAD == 0
    in_pad = 2 * cw + 3 * rw + LORA_PAD
    qw = 3 * rw + LORA_PAD

    xp = x_prompt.reshape(batch * seq, d)
    xs = x_sample.reshape(dec_batch, d)
    outs = {k: [] for k in ("conv_p", "shift_p", "rwkv_p", "memk_p", "memv_p", "conv_s", "shift_s", "rwkv_s")}
    bd = _block_diag_ones(2 * LANES, HEAD)

    for l in range(depth):
        w_in_b = _pad_cols(w_in[l], in_pad).astype(BF16)
        w_out_c = w_out[l, :cw].astype(BF16)
        w_out_r = w_out[l, cw:].astype(BF16)
        w_cq_b = w_cq[l].astype(BF16)
        w_ck_b = w_ck[l].astype(BF16)
        w_cv_b = w_cv[l].astype(BF16)
        w_co_b = w_co[l].astype(BF16)
        zeros_l = jnp.zeros((DECAY_LORA, rw), F32)
        wd_pad = jnp.concatenate([w_decay_up[l], zeros_l], axis=0).astype(BF16)
        wa_pad = jnp.concatenate([zeros_l, w_a_up[l]], axis=0).astype(BF16)
        wg_pad = jnp.pad(w_g_up[l], ((0, 2 * LANES - GATE_LORA), (0, 0))).astype(BF16)
        mu_pad = _pad_cols(shift_mu[l].reshape(1, shift_w), qw)
        vec = lambda x: x.reshape(1, rw)
        pp = (mu_pad, wd_pad, wa_pad, wg_pad, vec(decay_bias[l]), vec(a_bias[l]), vec(k_k[l]), vec(k_a[l]),
              vec(r_k[l]), bd)
        w_route = jnp.concatenate([w_route_expert[l].reshape(d, N_EXPERTS), w_route_group[l]], axis=1)
        w_route = _pad_cols(w_route, LANES)
        wr_hi = w_route.astype(BF16)
        wr_lo = (w_route - wr_hi.astype(F32)).astype(BF16)
        b_route = _pad_cols(jnp.concatenate([b_route_expert[l].reshape(1, N_EXPERTS),
                                             b_route_group[l].reshape(1, N_GROUPS)], axis=1), LANES)
        de = w_gate.shape[-1]
        wg_e = w_gate[l].reshape(N_EXPERTS, d, de).astype(BF16)
        wu_e = w_up[l].reshape(N_EXPERTS, d, de).astype(BF16)
        wd_e = w_down[l].reshape(N_EXPERTS, de, d).astype(BF16)
        nf = norm_final

        mem2 = mem_prompt.reshape(batch * N_MEM, d)
        mk = _norm_matmul_heads(mem2, norm_mem[l], w_ck_b, X_HEADS, 256)
        mv = _norm_matmul_heads(mem2, norm_mem[l], w_cv_b, X_HEADS, 256)
        proj = _norm_matmul(xp, norm_mix[l], w_in_b, 1024, 512)
        c_p, conv_new = _conv_prefill(proj, jnp.zeros((batch, CONV_K - 1, cw), F32), conv_w[l], conv_b[l],
                                      conv_ln_g[l], conv_ln_b[l], batch, seq)
        prep = _rwkv_prep_prefill(proj, jnp.zeros((batch, qw), F32), pp, batch, seq, rw)
        o_p, s_p = _rwkv_chunked(*prep[:6], jnp.zeros((batch, heads, HEAD, HEAD), F32), batch, seq)
        o_p = _rwkv_post(o_p, prep[6], prep[7], lnx_g[l], lnx_b[l], bd)
        shift_new = proj.reshape(batch, seq, in_pad)[:, -1, 2 * cw:2 * cw + shift_w]
        xp = _matmul_res([c_p, o_p], [w_out_c, w_out_r], xp, 512)
        qx = _norm_matmul(xp, norm_x[l], w_cq_b, 1024, 512)
        ctx = _attn_prefill(qx, mk, mv, batch, seq)
        xp = _matmul_res([ctx], [w_co_b], xp, 512)
        xp_new = _moe_sorted(xp, norm_ffn[l], wr_hi, wr_lo, b_route, wg_e, wu_e, wd_e, nf)
        outs["conv_p"].append(conv_new)
        outs["shift_p"].append(shift_new)
        outs["rwkv_p"].append(s_p)
        outs["memk_p"].append(mk.reshape(batch, N_MEM, X_HEADS, d // X_HEADS))
        outs["memv_p"].append(mv.reshape(batch, N_MEM, X_HEADS, d // X_HEADS))
        xp = xp_new

        proj_s = _norm_matmul(xs, norm_mix[l], w_in_b, 128, 512)
        c_s, conv_new_s = _conv_decode(proj_s, cache_conv[l], conv_w[l], conv_b[l], conv_ln_g[l],
                                       conv_ln_b[l])
        prep_s = _rwkv_prep_decode(proj_s, _pad_cols(state_shift[l], qw), pp, rw)
        o_s, s_s = _rwkv_step(*prep_s[:6], state_rwkv[l].astype(F32))
        o_s = _rwkv_post(o_s, prep_s[6], prep_s[7], lnx_g[l], lnx_b[l], bd)
        xs = _matmul_res([c_s, o_s], [w_out_c, w_out_r], xs, 128)
        qs = _norm_matmul(xs, norm_x[l], w_cq_b, 128, 512)
        ctx_s = _attn_decode(qs, cache_mem_k[l], cache_mem_v[l])
        xs = _matmul_res([ctx_s], [w_co_b], xs, 128)
        h2s, comb_s = _router(xs, norm_ffn[l], wr_hi, wr_lo, b_route)
        xs = _moe_dense(h2s, comb_s, wg_e, wu_e, wd_e, xs, nf)
        outs["conv_s"].append(conv_new_s)
        outs["shift_s"].append(proj_s[:, 2 * cw:2 * cw + shift_w])
        outs["rwkv_s"].append(s_s)

    y_prompt = xp.reshape(batch, seq, d)
    y_sample = xs.reshape(dec_batch, 1, d)
    st = lambda k: jnp.stack(outs[k])
    return (y_prompt, y_sample, st("conv_p"), st("shift_p"), st("rwkv_p"), st("memk_p"), st("memv_p"),
            st("conv_s"), st("shift_s"), st("rwkv_s"))
```

```python
import functools
import math

import jax
import jax.numpy as jnp
from jax import lax
from jax.experimental import pallas as pl
from jax.experimental.pallas import tpu as pltpu

F32 = jnp.float32
BF16 = jnp.bfloat16

CONV_K = 31
HEAD = 64
PAIR = 2 * HEAD
CHUNK = 64
DECAY_LORA = 64
AAA_LORA = 64
GATE_LORA = 160
LORA_PAD = 512
N_MEM = 256
X_HEADS = 4
N_GROUPS = 4
EXP_PER_GROUP = 8
N_EXPERTS = N_GROUPS * EXP_PER_GROUP
RMS_EPS = 1e-6
LN_EPS = 1e-5
GN_EPS = 64e-5
DECAY_SCALE = math.exp(-0.5)
NEG_BIG = -1e30
MOE_BM = 256
LANES = 128
VMEM_LIMIT = 56 * 1024 * 1024


def _cparams(*sem):
    return pltpu.CompilerParams(dimension_semantics=sem, vmem_limit_bytes=VMEM_LIMIT)


def _dot(a, b):
    return jnp.dot(a, b, preferred_element_type=F32)


def _dot_nt(a, b):
    return lax.dot_general(a, b, (((1,), (1,)), ((), ())), preferred_element_type=F32)


def _split_dot(x, w_bf16):
    hi = x.astype(BF16)
    lo = (x - hi.astype(F32)).astype(BF16)
    return _dot(hi, w_bf16) + _dot(lo, w_bf16)


def _rms(x, g, eps=RMS_EPS):
    return x * lax.rsqrt(jnp.mean(x * x, axis=-1, keepdims=True) + eps) * g


def _norm_mm_kernel(x_ref, g_ref, w_ref, o_ref, xn_ref):
    @pl.when(pl.program_id(1) == 0)
    def _():
        xn_ref[...] = _rms(x_ref[...], g_ref[...]).astype(BF16)

    o_ref[...] = _dot(xn_ref[...], w_ref[...])


def _norm_matmul(x, g, w, bm, bn):
    m, k = x.shape
    n = w.shape[1]
    bm = min(bm, m)
    return pl.pallas_call(
        _norm_mm_kernel,
        grid=(m // bm, n // bn),
        in_specs=[pl.BlockSpec((bm, k), lambda i, j: (i, 0)),
                  pl.BlockSpec((1, k), lambda i, j: (0, 0)),
                  pl.BlockSpec((k, bn), lambda i, j: (0, j))],
        out_specs=pl.BlockSpec((bm, bn), lambda i, j: (i, j)),
        out_shape=jax.ShapeDtypeStruct((m, n), F32),
        scratch_shapes=[pltpu.VMEM((bm, k), BF16)],
        compiler_params=_cparams("parallel", "arbitrary"),
        name="norm_matmul",
    )(x, g.reshape(1, k), w)


def _norm_mm_heads_kernel(x_ref, g_ref, w_ref, o_ref):
    res = _dot(_rms(x_ref[...], g_ref[...]).astype(BF16), w_ref[...])
    dh = o_ref.shape[2]
    for h in range(o_ref.shape[1]):
        o_ref[:, h, :] = res[:, h * dh:(h + 1) * dh]


def _norm_matmul_heads(x, g, w, n_heads, bm):
    m, k = x.shape
    n = w.shape[1]
    bm = min(bm, m)
    return pl.pallas_call(
        _norm_mm_heads_kernel,
        grid=(m // bm,),
        in_specs=[pl.BlockSpec((bm, k), lambda i: (i, 0)),
                  pl.BlockSpec((1, k), lambda i: (0, 0)),
                  pl.BlockSpec((k, n), lambda i: (0, 0))],
        out_specs=pl.BlockSpec((bm, n_heads, n // n_heads), lambda i: (i, 0, 0)),
        out_shape=jax.ShapeDtypeStruct((m, n_heads, n // n_heads), F32),
        compiler_params=_cparams("parallel"),
        name="norm_matmul_heads",
    )(x, g.reshape(1, k), w)


def _mm_res_kernel(*refs, n_lhs):
    res_ref = refs[2 * n_lhs]
    o_ref = refs[2 * n_lhs + 1]
    acc = res_ref[...]
    for a_ref, w_ref in zip(refs[:n_lhs], refs[n_lhs:2 * n_lhs]):
        acc = acc + _dot(a_ref[...].astype(BF16), w_ref[...])
    o_ref[...] = acc


def _matmul_res(lhs, ws, res, bm):
    m, n = res.shape
    bm = min(bm, m)
    n_lhs = len(lhs)
    in_specs = [pl.BlockSpec((bm, a.shape[1]), lambda i: (i, 0)) for a in lhs]
    in_specs += [pl.BlockSpec(w.shape, lambda i: (0, 0)) for w in ws]
    in_specs += [pl.BlockSpec((bm, n), lambda i: (i, 0))]
    return pl.pallas_call(
        functools.partial(_mm_res_kernel, n_lhs=n_lhs),
        grid=(m // bm,),
        in_specs=in_specs,
        out_specs=pl.BlockSpec((bm, n), lambda i: (i, 0)),
        out_shape=jax.ShapeDtypeStruct((m, n), F32),
        compiler_params=_cparams("parallel"),
        name="matmul_res",
    )(*lhs, *ws, res)


def _ln_silu(cf, lg, lb):
    mu = jnp.mean(cf, axis=-1, keepdims=True)
    d = cf - mu
    var = jnp.mean(d * d, axis=-1, keepdims=True)
    y = d * lax.rsqrt(var + LN_EPS) * lg + lb
    return y * jax.nn.sigmoid(y)


def _conv_prefill_kernel(a_ref, g_ref, buf_ref, w_ref, cb_ref, lg_ref, lb_ref, c_ref, nc_ref,
                         uf_ref, cv_ref, *, tt, halo):
    t = pl.program_id(1)
    pad = 32 - halo

    @pl.when(t == 0)
    def _():
        uf_ref[pad:32, :] = buf_ref[0]

    @pl.when(t > 0)
    def _():
        uf_ref[pad:32, :] = uf_ref[tt + pad:tt + 32, :]

    uf_ref[32:32 + tt, :] = a_ref[...] * jax.nn.sigmoid(g_ref[...])

    width = uf_ref.shape[1]
    rb = 64
    for r0 in range(0, tt, rb):
        for l0 in range(0, width, LANES):
            acc = jnp.zeros((rb, LANES), F32)
            for j in range(CONV_K):
                acc = acc + (uf_ref[r0 + pad + j:r0 + pad + j + rb, l0:l0 + LANES]
                             * w_ref[j:j + 1, l0:l0 + LANES])
            cv_ref[r0:r0 + rb, l0:l0 + LANES] = acc

    c_ref[...] = _ln_silu(cv_ref[...] + cb_ref[...], lg_ref[...], lb_ref[...]).astype(c_ref.dtype)

    @pl.when(t == pl.num_programs(1) - 1)
    def _():
        nc_ref[0] = uf_ref[tt + pad:tt + 32, :]


def _conv_prefill(proj, conv_buf, conv_w, conv_b, ln_g, ln_b, batch, seq):
    cw = conv_w.shape[1]
    halo = CONV_K - 1
    tt = min(256, seq)
    nt = seq // tt
    row = lambda b, t: (b * nt + t, 0)
    vec = pl.BlockSpec((1, cw), lambda b, t: (0, 0))
    return pl.pallas_call(
        functools.partial(_conv_prefill_kernel, tt=tt, halo=halo),
        grid=(batch, nt),
        in_specs=[pl.BlockSpec((tt, cw), row),
                  pl.BlockSpec((tt, cw), lambda b, t: (b * nt + t, 1)),
                  pl.BlockSpec((1, halo, cw), lambda b, t: (b, 0, 0)),
                  pl.BlockSpec((CONV_K, cw), lambda b, t: (0, 0)),
                  vec, vec, vec],
        out_specs=[pl.BlockSpec((tt, cw), row),
                   pl.BlockSpec((1, halo, cw), lambda b, t: (b, 0, 0))],
        out_shape=[jax.ShapeDtypeStruct((batch * seq, cw), BF16),
                   jax.ShapeDtypeStruct((batch, halo, cw), F32)],
        scratch_shapes=[pltpu.VMEM((tt + 32, cw), F32), pltpu.VMEM((tt, cw), F32)],
        compiler_params=_cparams("parallel", "arbitrary"),
        name="conv_prefill",
    )(proj, proj, conv_buf, conv_w, conv_b.reshape(1, cw), ln_g.reshape(1, cw), ln_b.reshape(1, cw))


def _conv_decode_kernel(a_ref, g_ref, cache_ref, w_ref, cb_ref, lg_ref, lb_ref, c_ref, nc_ref):
    halo = CONV_K - 1
    u = a_ref[...] * jax.nn.sigmoid(g_ref[...])
    acc = u * w_ref[halo:halo + 1, :]
    for j in range(halo):
        acc = acc + cache_ref[:, j, :] * w_ref[j:j + 1, :]
    c_ref[...] = _ln_silu(acc + cb_ref[...], lg_ref[...], lb_ref[...]).astype(c_ref.dtype)
    nc_ref[:, 0:halo - 1, :] = cache_ref[:, 1:halo, :]
    nc_ref[:, halo - 1, :] = u


def _conv_decode(proj, cache, conv_w, conv_b, ln_g, ln_b):
    batch, halo, cw = cache.shape
    bb = 8
    vec = pl.BlockSpec((1, cw), lambda i: (0, 0))
    return pl.pallas_call(
        _conv_decode_kernel,
        grid=(batch // bb,),
        in_specs=[pl.BlockSpec((bb, cw), lambda i: (i, 0)),
                  pl.BlockSpec((bb, cw), lambda i: (i, 1)),
                  pl.BlockSpec((bb, halo, cw), lambda i: (i, 0, 0)),
                  pl.BlockSpec((CONV_K, cw), lambda i: (0, 0)),
                  vec, vec, vec],
        out_specs=[pl.BlockSpec((bb, cw), lambda i: (i, 0)),
                   pl.BlockSpec((bb, halo, cw), lambda i: (i, 0, 0))],
        out_shape=[jax.ShapeDtypeStruct((batch, cw), BF16),
                   jax.ShapeDtypeStruct((batch, halo, cw), F32)],
        compiler_params=_cparams("parallel"),
        name="conv_decode",
    )(proj, proj, cache, conv_w, conv_b.reshape(1, cw), ln_g.reshape(1, cw), ln_b.reshape(1, cw))


def _head_sum(x, bd_ref):
    blk = bd_ref.shape[0]
    parts = [_split_dot(x[:, l0:l0 + blk], bd_ref[...]) for l0 in range(0, x.shape[1], blk)]
    return jnp.concatenate(parts, axis=1)


def _prep_math(q, qp, mu_ref, wd_ref, wa_ref, wg_ref, db_ref, ab_ref, kk_ref, ka_ref, rk_ref, bd_ref):
    rw = q[0].shape[1]
    offs = (0, rw, 2 * rw, 3 * rw)
    r, k, v, lo = [x + (xp - x) * mu_ref[:, o:o + x.shape[1]] for x, xp, o in zip(q, qp, offs)]
    pwa = lo[:, 0:LANES]
    pg = lo[:, LANES:3 * LANES]
    dec_in = _dot(jnp.tanh(pwa).astype(BF16), wd_ref[...])
    a_in = _dot(pwa.astype(BF16), wa_ref[...])
    gate = _dot(jax.nn.sigmoid(pg).astype(BF16), wg_ref[...])
    logw = -DECAY_SCALE * jax.nn.sigmoid(db_ref[...] + dec_in)
    a = jax.nn.sigmoid(ab_ref[...] + a_in)
    kk = k * kk_ref[...]
    kk = kk / jnp.maximum(jnp.sqrt(_head_sum(kk * kk, bd_ref)), 1e-12)
    k2 = k * (1.0 + (a - 1.0) * ka_ref[...])
    bonus = _head_sum(r * k2 * rk_ref[...], bd_ref) * v
    return r, logw, k2, v, kk, a, bonus, gate


def _prep_prefill_kernel(r_ref, k_ref, v_ref, lo_ref, sb_ref, mu_ref, wd_ref, wa_ref, wg_ref, db_ref,
                         ab_ref, kk_ref, ka_ref, rk_ref, bd_ref, *rest):
    outs = rest[:8]
    carry_ref = rest[8]
    t = pl.program_id(1)

    @pl.when(t == 0)
    def _():
        carry_ref[0:1, :] = sb_ref[0]

    q = [r_ref[...], k_ref[...], v_ref[...], lo_ref[...]]
    tt = q[0].shape[0]
    first = lax.broadcasted_iota(jnp.int32, (tt, 1), 0) == 0
    qp = []
    off = 0
    for x in q:
        w = x.shape[1]
        qp.append(jnp.where(first, carry_ref[0:1, off:off + w], pltpu.roll(x, 1, 0)))
        off += w
    off = 0
    for x in q:
        w = x.shape[1]
        carry_ref[0:1, off:off + w] = x[tt - 1:tt, :]
        off += w
    res = _prep_math(q, qp, mu_ref, wd_ref, wa_ref, wg_ref, db_ref, ab_ref, kk_ref, ka_ref, rk_ref, bd_ref)
    for o_ref, val in zip(outs, res):
        o_ref[...] = val


def _prep_decode_kernel(r_ref, k_ref, v_ref, lo_ref, rp_ref, kp_ref, vp_ref, lop_ref, mu_ref, wd_ref,
                        wa_ref, wg_ref, db_ref, ab_ref, kk_ref, ka_ref, rk_ref, bd_ref, *outs):
    q = [r_ref[...], k_ref[...], v_ref[...], lo_ref[...]]
    qp = [rp_ref[...], kp_ref[...], vp_ref[...], lop_ref[...]]
    res = _prep_math(q, qp, mu_ref, wd_ref, wa_ref, wg_ref, db_ref, ab_ref, kk_ref, ka_ref, rk_ref, bd_ref)
    for o_ref, val in zip(outs, res):
        o_ref[...] = val


def _prep_param_specs(rw, idx):
    full = lambda shape: pl.BlockSpec(shape, idx)
    vec = full((1, rw))
    return [full((1, 3 * rw + LORA_PAD)), full((LANES, rw)), full((LANES, rw)), full((2 * LANES, rw)),
            vec, vec, vec, vec, vec, full((2 * LANES, 2 * LANES))]


def _rwkv_prep_prefill(proj, shift_buf, pp, batch, seq, rw):
    tt = min(256, seq)
    nt = seq // tt
    lora_blk = (2 * rw + 3 * rw) // LORA_PAD
    col = lambda c: (lambda b, t: (b * nt + t, c))
    qw = 3 * rw + LORA_PAD
    in_specs = [pl.BlockSpec((tt, rw), col(2)), pl.BlockSpec((tt, rw), col(3)),
                pl.BlockSpec((tt, rw), col(4)), pl.BlockSpec((tt, LORA_PAD), col(lora_blk)),
                pl.BlockSpec((1, 1, qw), lambda b, t: (b, 0, 0))]
    in_specs += _prep_param_specs(rw, lambda b, t: (0, 0))
    out_spec = pl.BlockSpec((tt, rw), col(0))
    return pl.pallas_call(
        _prep_prefill_kernel,
        grid=(batch, nt),
        in_specs=in_specs,
        out_specs=[out_spec] * 8,
        out_shape=[jax.ShapeDtypeStruct((batch * seq, rw), F32)] * 8,
        scratch_shapes=[pltpu.VMEM((8, qw), F32)],
        compiler_params=_cparams("parallel", "arbitrary"),
        name="rwkv_prep_prefill",
    )(proj, proj, proj, proj, shift_buf.reshape(batch, 1, qw), *pp)


def _rwkv_prep_decode(proj, shift_state, pp, rw):
    batch = proj.shape[0]
    bb = min(128, batch)
    lora_blk = (2 * rw + 3 * rw) // LORA_PAD
    col = lambda c: (lambda i: (i, c))
    in_specs = [pl.BlockSpec((bb, rw), col(2)), pl.BlockSpec((bb, rw), col(3)),
                pl.BlockSpec((bb, rw), col(4)), pl.BlockSpec((bb, LORA_PAD), col(lora_blk)),
                pl.BlockSpec((bb, rw), col(0)), pl.BlockSpec((bb, rw), col(1)),
                pl.BlockSpec((bb, rw), col(2)), pl.BlockSpec((bb, LORA_PAD), col(3 * rw // LORA_PAD))]
    in_specs += _prep_param_specs(rw, lambda i: (0, 0))
    return pl.pallas_call(
        _prep_decode_kernel,
        grid=(batch // bb,),
        in_specs=in_specs,
        out_specs=[pl.BlockSpec((bb, rw), col(0))] * 8,
        out_shape=[jax.ShapeDtypeStruct((batch, rw), F32)] * 8,
        compiler_params=_cparams("parallel"),
        name="rwkv_prep_decode",
    )(proj, proj, proj, proj, shift_state, shift_state, shift_state, shift_state, *pp)


def _stack2(x, smask):
    return jnp.where(smask, jnp.concatenate([x, x], axis=0), 0.0)


def _rwkv_chunk_kernel(r_ref, lw_ref, k_ref, v_ref, kk_ref, a_ref, s0_ref, o_ref, so_ref, s_ref):
    c = pl.program_id(1)
    cs = r_ref.shape[0]
    n_pairs = r_ref.shape[1] // PAIR
    two = 2 * cs

    @pl.when(c == 0)
    def _():
        z = jnp.zeros((HEAD, HEAD), F32)
        for p in range(n_pairs):
            top = jnp.concatenate([s0_ref[0, 2 * p], z], axis=1)
            bot = jnp.concatenate([z, s0_ref[0, 2 * p + 1]], axis=1)
            s_ref[p] = jnp.concatenate([top, bot], axis=0)

    ri = lax.broadcasted_iota(jnp.int32, (two, two), 0)
    ci = lax.broadcasted_iota(jnp.int32, (two, two), 1)
    strict = ci < ri
    incl = ci <= ri
    eye = (ci == ri).astype(F32)
    smask = (lax.broadcasted_iota(jnp.int32, (two, PAIR), 0) < cs) == (
        lax.broadcasted_iota(jnp.int32, (two, PAIR), 1) < HEAD)
    tri = (lax.broadcasted_iota(jnp.int32, (cs, cs), 1)
           <= lax.broadcasted_iota(jnp.int32, (cs, cs), 0)).astype(BF16)

    lw_all = lw_ref[...]
    lw_hi = lw_all.astype(BF16)
    lw_lo = (lw_all - lw_hi.astype(F32)).astype(BF16)
    cum_all = _dot(tri, lw_hi) + _dot(tri, lw_lo)

    pairs = range(n_pairs)
    cat = jnp.concatenate
    prep = []
    for p in pairs:
        sl = slice(p * PAIR, (p + 1) * PAIR)
        lw = lw_all[:, sl]
        cum = cum_all[:, sl]
        tot = cum[cs - 1:cs, :]
        g_inv = jnp.exp(-cum)
        g_end = jnp.exp(tot - cum)
        kk = kk_ref[:, sl]
        k2 = k_ref[:, sl]
        bb = kk * a_ref[:, sl]
        prep.append(dict(
            g_tot=jnp.exp(tot),
            a_b=_stack2(kk * jnp.exp(cum - lw), smask).astype(BF16),
            r_s=_stack2(r_ref[:, sl] * jnp.exp(cum), smask),
            bk=cat([_stack2(bb * g_inv, smask), _stack2(k2 * g_inv, smask)], axis=0).astype(BF16),
            v_s=_stack2(v_ref[:, sl], smask),
            bg_s=_stack2(bb * g_end, smask).astype(BF16),
            kg_s=_stack2(k2 * g_end, smask).astype(BF16)))
    a_b = [q["a_b"] for q in prep]
    r_s = [q["r_s"] for q in prep]
    v_s = [q["v_s"] for q in prep]
    v_b = [x.astype(BF16) for x in v_s]

    gram = [_dot_nt(cat([a_b[p], r_s[p].astype(BF16)], axis=0), prep[p]["bk"]) for p in pairs]
    l_ab = [jnp.where(strict, g[0:two, 0:two], 0.0) for g in gram]
    l_ak = [jnp.where(strict, g[0:two, two:], 0.0).astype(BF16) for g in gram]
    m_rb = [jnp.where(incl, g[two:, 0:two], 0.0).astype(BF16) for g in gram]
    m_rk = [jnp.where(incl, g[two:, two:], 0.0).astype(BF16) for g in gram]

    tm = [eye - x for x in l_ab]
    pw = l_ab
    n = 1
    while 2 * n < cs:
        pw = [_dot(x.astype(BF16), x.astype(BF16)) for x in pw]
        tm = [t + _dot(t.astype(BF16), x.astype(BF16)) for t, x in zip(tm, pw)]
        n *= 2
    tm_b = [t.astype(BF16) for t in tm]

    w1 = [_dot(l_ak[p], v_b[p]) for p in pairs]
    ua = [_dot(tm_b[p], cat([w1[p].astype(BF16), a_b[p]], axis=1)) for p in pairs]
    ua_b = [x.astype(BF16) for x in ua]
    mrb_ua = [_dot(m_rb[p], ua_b[p]) for p in pairs]
    o0 = [_dot(m_rk[p], v_b[p]) - mrb_ua[p][:, 0:PAIR] for p in pairs]
    rt = [(r_s[p] - mrb_ua[p][:, PAIR:]).astype(BF16) for p in pairs]

    s_old = [s_ref[p] for p in pairs]
    s_b = [x.astype(BF16) for x in s_old]
    ua_t = [cat([x[:, 0:PAIR].T, x[:, PAIR:].T], axis=0).astype(BF16) for x in ua]
    uat_bg = [_dot(ua_t[p], prep[p]["bg_s"]) for p in pairs]
    vt_kg = [_dot(v_s[p].T.astype(BF16), prep[p]["kg_s"]) for p in pairs]
    s_new = [s_old[p] * prep[p]["g_tot"] - _dot(s_b[p], uat_bg[p][PAIR:, :].astype(BF16))
             + vt_kg[p] - uat_bg[p][0:PAIR, :] for p in pairs]
    o_st = [o0[p] + _dot_nt(rt[p], s_b[p]) for p in pairs]

    o_ref[...] = cat([x[0:cs, :] + x[cs:two, :] for x in o_st], axis=1)
    s_ref[...] = jnp.stack(s_new, axis=0)

    @pl.when(c == pl.num_programs(1) - 1)
    def _():
        for p in range(n_pairs):
            s = s_ref[p]
            so_ref[0, 2 * p] = s[0:HEAD, 0:HEAD]
            so_ref[0, 2 * p + 1] = s[HEAD:PAIR, HEAD:PAIR]


def _rwkv_chunked(r, logw, k2, v, kk, a, s0, batch, seq):
    rw = r.shape[1]
    nc = seq // CHUNK
    heads = rw // HEAD
    row = pl.BlockSpec((CHUNK, rw), lambda b, c: (b * nc + c, 0))
    st = pl.BlockSpec((1, heads, HEAD, HEAD), lambda b, c: (b, 0, 0, 0))
    return pl.pallas_call(
        _rwkv_chunk_kernel,
        grid=(batch, nc),
        in_specs=[row] * 6 + [st],
        out_specs=[row, st],
        out_shape=[jax.ShapeDtypeStruct((batch * seq, rw), F32),
                   jax.ShapeDtypeStruct((batch, heads, HEAD, HEAD), F32)],
        scratch_shapes=[pltpu.VMEM((rw // PAIR, PAIR, PAIR), F32)],
        compiler_params=_cparams("parallel", "arbitrary"),
        name="rwkv_chunked",
    )(r, logw, k2, v, kk, a, s0)


def _rwkv_step_kernel(r_ref, lw_ref, k_ref, v_ref, kk_ref, a_ref, s_ref, o_ref, so_ref, *, heads):
    bb = s_ref.shape[0]
    eye = (lax.broadcasted_iota(jnp.int32, (HEAD, HEAD), 0)
           == lax.broadcasted_iota(jnp.int32, (HEAD, HEAD), 1))

    def body(bi, carry):
        for h in range(heads):
            row = pl.ds(bi * heads + h, 1)
            kk = kk_ref[row, :]
            s = s_ref[bi, h]
            sa = jnp.sum(s * kk, axis=1, keepdims=True)
            v_col = jnp.sum(jnp.where(eye, v_ref[row, :], 0.0), axis=1, keepdims=True)
            s_new = (s * jnp.exp(lw_ref[row, :]) - sa * (kk * a_ref[row, :]) + v_col * k_ref[row, :])
            so_ref[bi, h] = s_new
            o_col = jnp.sum(s_new * r_ref[row, :], axis=1, keepdims=True)
            o_ref[row, :] = jnp.sum(jnp.where(eye, o_col, 0.0), axis=0, keepdims=True)
        return carry

    lax.fori_loop(0, bb, body, 0)


def _rwkv_step(r, logw, k2, v, kk, a, s0):
    batch, rw = r.shape
    heads = rw // HEAD
    bb = 8
    flat = lambda x: x.reshape(batch * heads, HEAD)
    row = pl.BlockSpec((bb * heads, HEAD), lambda i: (i, 0))
    st = pl.BlockSpec((bb, heads, HEAD, HEAD), lambda i: (i, 0, 0, 0))
    o, s_new = pl.pallas_call(
        functools.partial(_rwkv_step_kernel, heads=heads),
        grid=(batch // bb,),
        in_specs=[row] * 6 + [st],
        out_specs=[row, st],
        out_shape=[jax.ShapeDtypeStruct((batch * heads, HEAD), F32),
                   jax.ShapeDtypeStruct(s0.shape, F32)],
        compiler_params=_cparams("parallel"),
        name="rwkv_step",
    )(flat(r), flat(logw), flat(k2), flat(v), flat(kk), flat(a), s0)
    return o.reshape(batch, rw), s_new


def _rwkv_post_kernel(o_ref, bonus_ref, gate_ref, lg_ref, lb_ref, bd_ref, y_ref):
    o = o_ref[...]
    mu = _head_sum(o, bd_ref) * (1.0 / HEAD)
    d = o - mu
    var = _head_sum(d * d, bd_ref) * (1.0 / HEAD)
    y = d * lax.rsqrt(var + GN_EPS) * lg_ref[...] + lb_ref[...]
    y_ref[...] = ((y + bonus_ref[...]) * gate_ref[...]).astype(y_ref.dtype)


def _rwkv_post(o, bonus, gate, lnx_g, lnx_b, bd):
    m, rw = o.shape
    bm = min(512, m)
    row = pl.BlockSpec((bm, rw), lambda i: (i, 0))
    vec = pl.BlockSpec((1, rw), lambda i: (0, 0))
    return pl.pallas_call(
        _rwkv_post_kernel,
        grid=(m // bm,),
        in_specs=[row, row, row, vec, vec, pl.BlockSpec(bd.shape, lambda i: (0, 0))],
        out_specs=row,
        out_shape=jax.ShapeDtypeStruct((m, rw), BF16),
        compiler_params=_cparams("parallel"),
        name="rwkv_post",
    )(o, bonus, gate, lnx_g.reshape(1, rw), lnx_b.reshape(1, rw), bd)


def _attn_prefill_kernel(q_ref, k_ref, v_ref, o_ref, *, n_heads):
    d = q_ref.shape[1] // n_heads
    scale = d ** -0.5
    for h in range(n_heads):
        sl = slice(h * d, (h + 1) * d)
        s = _dot_nt(q_ref[:, sl].astype(BF16), k_ref[:, h, :].astype(BF16)) * scale
        p = jnp.exp(s - jnp.max(s, axis=-1, keepdims=True))
        att = p / jnp.sum(p, axis=-1, keepdims=True)
        o_ref[:, sl] = _dot(att.astype(BF16), v_ref[:, h, :].astype(BF16)).astype(o_ref.dtype)


def _attn_prefill(q, mem_k, mem_v, batch, seq):
    d = q.shape[1]
    tt = min(512, seq)
    nt = seq // tt
    kv = pl.BlockSpec((N_MEM, X_HEADS, d // X_HEADS), lambda b, t: (b, 0, 0))
    row = pl.BlockSpec((tt, d), lambda b, t: (b * nt + t, 0))
    return pl.pallas_call(
        functools.partial(_attn_prefill_kernel, n_heads=X_HEADS),
        grid=(batch, nt),
        in_specs=[row, kv, kv],
        out_specs=row,
        out_shape=jax.ShapeDtypeStruct((batch * seq, d), BF16),
        compiler_params=_cparams("parallel", "arbitrary"),
        name="attn_prefill",
    )(q, mem_k, mem_v)


def _attn_decode_kernel(q_ref, k_ref, v_ref, o_ref, *, n_heads):
    bb = q_ref.shape[0]
    d = q_ref.shape[2] // n_heads
    scale = d ** -0.5
    for bi in range(bb):
        q = q_ref[bi]
        for h in range(n_heads):
            sl = slice(h * d, (h + 1) * d)
            s = jnp.sum(k_ref[bi, :, h, :] * q[:, sl], axis=1, keepdims=True) * scale
            p = jnp.exp(s - jnp.max(s, axis=0, keepdims=True))
            att = p / jnp.sum(p, axis=0, keepdims=True)
            o_ref[bi, :, sl] = jnp.sum(att * v_ref[bi, :, h, :], axis=0, keepdims=True)


def _attn_decode(q, cache_k, cache_v):
    batch, d = q.shape
    bb = 2
    kv = pl.BlockSpec((bb, N_MEM, X_HEADS, d // X_HEADS), lambda i: (i, 0, 0, 0))
    row = pl.BlockSpec((bb, 1, d), lambda i: (i, 0, 0))
    out = pl.pallas_call(
        functools.partial(_attn_decode_kernel, n_heads=X_HEADS),
        grid=(batch // bb,),
        in_specs=[row, kv, kv],
        out_specs=row,
        out_shape=jax.ShapeDtypeStruct((batch, 1, d), F32),
        compiler_params=_cparams("parallel"),
        name="attn_decode",
    )(q.reshape(batch, 1, d), cache_k, cache_v)
    return out.reshape(batch, d)


def _route(x_ref, g_ref, wh_ref, wl_ref, b_ref):
    h = _rms(x_ref[...], g_ref[...])
    hh = h.astype(BF16)
    hl = (h - hh.astype(F32)).astype(BF16)
    logits = _dot(hh, wh_ref[...]) + _dot(hl, wh_ref[...]) + _dot(hh, wl_ref[...]) + b_ref[...]
    lane = lax.broadcasted_iota(jnp.int32, (1, LANES), 1).astype(F32)
    is_g = (lane >= N_EXPERTS) & (lane < N_EXPERTS + N_GROUPS)
    lgm = jnp.where(is_g, logits, NEG_BIG)
    gmax = jnp.max(lgm, axis=1, keepdims=True)
    gsum = jnp.sum(jnp.where(is_g, jnp.exp(lgm - gmax), 0.0), axis=1, keepdims=True)
    g_val = 1.0 / gsum
    g_idx = jnp.min(jnp.where(is_g & (lgm == gmax), lane - N_EXPERTS, 1e9), axis=1, keepdims=True)
    in_grp = (lane < N_EXPERTS) & (jnp.floor(lane * (1.0 / EXP_PER_GROUP)) == g_idx)
    le = jnp.where(in_grp, logits, NEG_BIG)
    m1 = jnp.max(le, axis=1, keepdims=True)
    i1 = jnp.min(jnp.where(in_grp & (le == m1), lane, 1e9), axis=1, keepdims=True)
    rest = in_grp & (lane != i1)
    le2 = jnp.where(rest, logits, NEG_BIG)
    m2 = jnp.max(le2, axis=1, keepdims=True)
    i2 = jnp.min(jnp.where(rest & (le2 == m2), lane, 1e9), axis=1, keepdims=True)
    e2 = jnp.exp(m2 - m1)
    den = 1.0 + e2
    w1 = (1.0 / den) * g_val
    w2 = (e2 / den) * g_val
    return h, lane, i1, i2, w1, w2


def _router_kernel(x_ref, g_ref, wh_ref, wl_ref, b_ref, h_ref, comb_ref):
    h, lane, i1, i2, w1, w2 = _route(x_ref, g_ref, wh_ref, wl_ref, b_ref)
    h_ref[...] = h.astype(BF16)
    comb_ref[...] = jnp.where(lane == i1, w1, 0.0) + jnp.where(lane == i2, w2, 0.0)


def _router_sorted_kernel(x_ref, g_ref, wh_ref, wl_ref, b_ref, route_ref, cnt_ref, run_ref):
    @pl.when(pl.program_id(0) == 0)
    def _():
        run_ref[...] = jnp.zeros_like(run_ref)

    h, lane, i1, i2, w1, w2 = _route(x_ref, g_ref, wh_ref, wl_ref, b_ref)
    bm = h.shape[0]
    oh1 = lane == i1
    oh2 = lane == i2
    sel = (oh1 | oh2).astype(BF16)
    before = (lax.broadcasted_iota(jnp.int32, (bm, bm), 1)
              < lax.broadcasted_iota(jnp.int32, (bm, bm), 0)).astype(BF16)
    base = run_ref[0:1, :] + _dot(before, sel)
    rank1 = jnp.sum(jnp.where(oh1, base, 0.0), axis=1, keepdims=True)
    rank2 = jnp.sum(jnp.where(oh2, base, 0.0), axis=1, keepdims=True)
    total = run_ref[0:1, :] + jnp.sum(sel.astype(F32), axis=0, keepdims=True)
    run_ref[0:1, :] = total
    cnt_ref[...] = total
    vals = (i1, i2, w1, w2, rank1, rank2)
    route = jnp.zeros((bm, LANES), F32)
    for idx, val in enumerate(vals):
        route = jnp.where(lane == idx, val, route)
    route_ref[...] = route


def _router_sorted(x, g, wh, wl, bias):
    m, d = x.shape
    bm = min(512, m)
    return pl.pallas_call(
        _router_sorted_kernel,
        grid=(m // bm,),
        in_specs=[pl.BlockSpec((bm, d), lambda i: (i, 0)),
                  pl.BlockSpec((1, d), lambda i: (0, 0)),
                  pl.BlockSpec((d, LANES), lambda i: (0, 0)),
                  pl.BlockSpec((d, LANES), lambda i: (0, 0)),
                  pl.BlockSpec((1, LANES), lambda i: (0, 0))],
        out_specs=[pl.BlockSpec((bm, LANES), lambda i: (i, 0)),
                   pl.BlockSpec((1, LANES), lambda i: (0, 0))],
        out_shape=[jax.ShapeDtypeStruct((m, LANES), F32),
                   jax.ShapeDtypeStruct((1, LANES), F32)],
        scratch_shapes=[pltpu.VMEM((8, LANES), F32)],
        compiler_params=_cparams("arbitrary"),
        name="moe_router_sorted",
    )(x, g.reshape(1, d), wh, wl, bias)


def _row_copy(src_hbm, src_row, dst, dst_row, sem):
    return pltpu.make_async_copy(src_hbm.at[pl.ds(src_row, 1)], dst.at[pl.ds(dst_row, 1)], sem)


def _dispatch_kernel(p1_ref, p2_ref, seg_ref, cnt_ref, nrow_ref, x_ref, g_ref, xs_hbm, h_ref, zero_ref,
                     sem, zsem):
    i = pl.program_id(0)
    n = pl.num_programs(0)
    bm = x_ref.shape[0]
    n_max = xs_hbm.shape[0] // MOE_BM

    def zero_copy(row0):
        return pltpu.make_async_copy(zero_ref, xs_hbm.at[pl.ds(pl.multiple_of(row0, MOE_BM), MOE_BM)], zsem)

    @pl.when(i == 0)
    def _():
        zero_ref[...] = jnp.zeros_like(zero_ref)
        first_free = nrow_ref[0] // MOE_BM

        def tail_start(c, carry):
            zero_copy(c * MOE_BM).start()
            return carry

        def tail_wait(c, carry):
            zero_copy(c * MOE_BM).wait()
            return carry

        for e in range(N_EXPERTS):
            @pl.when(cnt_ref[e] > 0)
            def _():
                zero_copy(seg_ref[e] - MOE_BM).start()

        lax.fori_loop(first_free, n_max, tail_start, 0)
        for e in range(N_EXPERTS):
            @pl.when(cnt_ref[e] > 0)
            def _():
                zero_copy(seg_ref[e] - MOE_BM).wait()

        lax.fori_loop(first_free, n_max, tail_wait, 0)

    def wait_rows(slot):
        for _ in range(2):
            pltpu.make_async_copy(h_ref.at[slot], xs_hbm.at[pl.ds(0, bm)], sem.at[slot]).wait()

    slot = i % 2

    @pl.when(i > 0)
    def _():
        wait_rows(1 - slot)

    h_ref[slot] = _rms(x_ref[...], g_ref[...])

    def body(r, carry):
        t = i * bm + r
        src = h_ref.at[slot, pl.ds(r, 1)]
        pltpu.make_async_copy(src, xs_hbm.at[pl.ds(p1_ref[t], 1)], sem.at[slot]).start()
        pltpu.make_async_copy(src, xs_hbm.at[pl.ds(p2_ref[t], 1)], sem.at[slot]).start()
        return carry

    lax.fori_loop(0, bm, body, 0, unroll=8)

    @pl.when(i == n - 1)
    def _():
        wait_rows(slot)


def _dispatch(x, g, pos1, pos2, seg_end, cnt, n_rows_used, n_rows_max):
    m, d = x.shape
    bm = min(MOE_BM, m)
    idx = lambda i, *_: (i, 0)
    fixed = lambda i, *_: (0, 0)
    return pl.pallas_call(
        _dispatch_kernel,
        grid_spec=pltpu.PrefetchScalarGridSpec(
            num_scalar_prefetch=5, grid=(m // bm,),
            in_specs=[pl.BlockSpec((bm, d), idx), pl.BlockSpec((1, d), fixed)],
            out_specs=pl.BlockSpec(memory_space=pl.ANY),
            scratch_shapes=[pltpu.VMEM((2, bm, d), F32), pltpu.VMEM((MOE_BM, d), F32),
                            pltpu.SemaphoreType.DMA((2,)), pltpu.SemaphoreType.DMA(())]),
        out_shape=jax.ShapeDtypeStruct((n_rows_max, d), F32),
        compiler_params=_cparams("arbitrary"),
        name="moe_dispatch",
    )(pos1, pos2, seg_end, cnt, n_rows_used, x, g.reshape(1, d))


def _experts_sorted_kernel(te_ref, nt_ref, xs_ref, wg_ref, wu_ref, wd_ref, ys_ref):
    @pl.when(pl.program_id(0) < nt_ref[0])
    def _():
        x = xs_ref[...].astype(BF16)
        hg = _dot(x, wg_ref[0])
        hu = _dot(x, wu_ref[0])
        act = hg * jax.nn.sigmoid(hg) * hu
        ys_ref[...] = _dot(act.astype(BF16), wd_ref[0])

    @pl.when(pl.program_id(0) >= nt_ref[0])
    def _():
        ys_ref[...] = jnp.zeros_like(ys_ref)


def _experts_sorted(xs, tile_expert, n_tiles_used, wg, wu, wd):
    n_rows, d = xs.shape
    de = wg.shape[2]
    n_tiles = n_rows // MOE_BM
    row_in = lambda j, te, nt: (jnp.minimum(j, nt[0] - 1), 0)
    wsel = lambda j, te, nt: (te[j], 0, 0)
    return pl.pallas_call(
        _experts_sorted_kernel,
        grid_spec=pltpu.PrefetchScalarGridSpec(
            num_scalar_prefetch=2, grid=(n_tiles,),
            in_specs=[pl.BlockSpec((MOE_BM, d), row_in),
                      pl.BlockSpec((1, d, de), wsel),
                      pl.BlockSpec((1, d, de), wsel),
                      pl.BlockSpec((1, de, d), wsel)],
            out_specs=pl.BlockSpec((MOE_BM, d), lambda j, te, nt: (j, 0))),
        out_shape=jax.ShapeDtypeStruct((n_rows, d), F32),
        compiler_params=_cparams("arbitrary"),
        name="moe_experts_sorted",
    )(tile_expert, n_tiles_used, xs, wg, wu, wd)


def _combine_kernel(p1_ref, p2_ref, ys_hbm, x_ref, route_ref, nf_ref, y_ref, buf_ref, sem):
    i = pl.program_id(0)
    n = pl.num_programs(0)
    bm = x_ref.shape[0]

    def issue(tile, slot):
        def body(r, carry):
            t = tile * bm + r
            _row_copy(ys_hbm, p1_ref[t], buf_ref.at[slot, 0], r, sem.at[slot]).start()
            _row_copy(ys_hbm, p2_ref[t], buf_ref.at[slot, 1], r, sem.at[slot]).start()
            return carry

        lax.fori_loop(0, bm, body, 0, unroll=8)

    @pl.when(i == 0)
    def _():
        issue(0, 0)

    @pl.when(i + 1 < n)
    def _():
        issue(i + 1, (i + 1) % 2)

    slot = i % 2
    for k in range(2):
        pltpu.make_async_copy(ys_hbm.at[pl.ds(0, bm)], buf_ref.at[slot, k], sem.at[slot]).wait()
    lane = lax.broadcasted_iota(jnp.int32, (1, LANES), 1)
    route = route_ref[...]
    w1 = jnp.sum(jnp.where(lane == 2, route, 0.0), axis=1, keepdims=True)
    w2 = jnp.sum(jnp.where(lane == 3, route, 0.0), axis=1, keepdims=True)
    x3 = x_ref[...] + w1 * buf_ref[slot, 0] + w2 * buf_ref[slot, 1]
    y_ref[...] = _rms(x3, nf_ref[...])


def _combine(ys, pos1, pos2, x, route, norm_final):
    m, d = x.shape
    bm = min(MOE_BM, m)
    return pl.pallas_call(
        _combine_kernel,
        grid_spec=pltpu.PrefetchScalarGridSpec(
            num_scalar_prefetch=2, grid=(m // bm,),
            in_specs=[pl.BlockSpec(memory_space=pl.ANY),
                      pl.BlockSpec((bm, d), lambda i, p1, p2: (i, 0)),
                      pl.BlockSpec((bm, LANES), lambda i, p1, p2: (i, 0)),
                      pl.BlockSpec((1, d), lambda i, p1, p2: (0, 0))],
            out_specs=pl.BlockSpec((bm, d), lambda i, p1, p2: (i, 0)),
            scratch_shapes=[pltpu.VMEM((2, 2, bm, d), F32), pltpu.SemaphoreType.DMA((2,))]),
        out_shape=jax.ShapeDtypeStruct((m, d), F32),
        compiler_params=_cparams("arbitrary"),
        name="moe_combine",
    )(pos1, pos2, ys, x, route, norm_final.reshape(1, d))


def _moe_sorted(x, g, wh, wl, bias, wg, wu, wd, norm_final):
    m, d = x.shape
    route, counts = _router_sorted(x, g, wh, wl, bias)
    e1 = route[:, 0].astype(jnp.int32)
    e2 = route[:, 1].astype(jnp.int32)
    rank1 = route[:, 4].astype(jnp.int32)
    rank2 = route[:, 5].astype(jnp.int32)
    cnt = counts[0, :N_EXPERTS].astype(jnp.int32)
    padded = (cnt + MOE_BM - 1) // MOE_BM * MOE_BM
    seg_end = jnp.cumsum(padded)
    seg_start = seg_end - padded
    pos1 = seg_start[e1] + rank1
    pos2 = seg_start[e2] + rank2
    n_rows_max = (2 * m // MOE_BM + N_EXPERTS) * MOE_BM
    n_tiles_max = n_rows_max // MOE_BM
    n_rows_used = seg_end[-1:]
    n_tiles_used = n_rows_used // MOE_BM
    tile_start = jnp.arange(n_tiles_max, dtype=jnp.int32) * MOE_BM
    tile_expert = jnp.sum((seg_end[None, :] <= tile_start[:, None]).astype(jnp.int32), axis=1)
    last_expert = jnp.max(jnp.where(cnt > 0, jnp.arange(N_EXPERTS, dtype=jnp.int32), 0))
    tile_expert = jnp.minimum(tile_expert, last_expert)
    xs = _dispatch(x, g, pos1, pos2, seg_end, cnt, n_rows_used, n_rows_max)
    ys = _experts_sorted(xs, tile_expert, n_tiles_used, wg, wu, wd)
    return _combine(ys, pos1, pos2, x, route, norm_final)


def _router(x, g, wh, wl, bias):
    m, d = x.shape
    bm = min(512, m)
    return pl.pallas_call(
        _router_kernel,
        grid=(m // bm,),
        in_specs=[pl.BlockSpec((bm, d), lambda i: (i, 0)),
                  pl.BlockSpec((1, d), lambda i: (0, 0)),
                  pl.BlockSpec((d, LANES), lambda i: (0, 0)),
                  pl.BlockSpec((d, LANES), lambda i: (0, 0)),
                  pl.BlockSpec((1, LANES), lambda i: (0, 0))],
        out_specs=[pl.BlockSpec((bm, d), lambda i: (i, 0)),
                   pl.BlockSpec((bm, LANES), lambda i: (i, 0))],
        out_shape=[jax.ShapeDtypeStruct((m, d), BF16),
                   jax.ShapeDtypeStruct((m, LANES), F32)],
        compiler_params=_cparams("parallel"),
        name="moe_router",
    )(x, g.reshape(1, d), wh, wl, bias)


def _moe_kernel(h_ref, comb_ref, wg_ref, wu_ref, wd_ref, x_ref, nf_ref, y_ref, acc_ref):
    e = pl.program_id(1)

    @pl.when(e == 0)
    def _():
        acc_ref[...] = x_ref[...]

    h = h_ref[...]
    hg = _dot(h, wg_ref[0])
    hu = _dot(h, wu_ref[0])
    lane = lax.broadcasted_iota(jnp.int32, (1, LANES), 1)
    cw = jnp.sum(jnp.where(lane == e, comb_ref[...], 0.0), axis=1, keepdims=True)
    act = hg * jax.nn.sigmoid(hg) * hu * cw
    acc_ref[...] += _dot(act.astype(BF16), wd_ref[0])

    @pl.when(e == pl.num_programs(1) - 1)
    def _():
        y_ref[...] = _rms(acc_ref[...], nf_ref[...])


def _moe_dense(h, comb, wg, wu, wd, x, norm_final):
    m, d = x.shape
    de = wg.shape[2]
    bm = min(512, m)
    return pl.pallas_call(
        _moe_kernel,
        grid=(m // bm, N_EXPERTS),
        in_specs=[pl.BlockSpec((bm, d), lambda i, e: (i, 0)),
                  pl.BlockSpec((bm, LANES), lambda i, e: (i, 0)),
                  pl.BlockSpec((1, d, de), lambda i, e: (e, 0, 0)),
                  pl.BlockSpec((1, d, de), lambda i, e: (e, 0, 0)),
                  pl.BlockSpec((1, de, d), lambda i, e: (e, 0, 0)),
                  pl.BlockSpec((bm, d), lambda i, e: (i, 0)),
                  pl.BlockSpec((1, d), lambda i, e: (0, 0))],
        out_specs=pl.BlockSpec((bm, d), lambda i, e: (i, 0)),
        out_shape=jax.ShapeDtypeStruct((m, d), F32),
        scratch_shapes=[pltpu.VMEM((bm, d), F32)],
        compiler_params=_cparams("parallel", "arbitrary"),
        name="moe_experts",
    )(h, comb, wg, wu, wd, x, norm_final.reshape(1, d))


def _pad_cols(x, n):
    return jnp.pad(x, ((0, 0), (0, n - x.shape[1])))


def _block_diag_ones(n, blk):
    i = jnp.arange(n) // blk
    return (i[:, None] == i[None, :]).astype(BF16)


def kernel(x_prompt, x_sample, mem_prompt, cache_conv, state_shift, state_rwkv, cache_mem_k, cache_mem_v,
           norm_mix, w_in, conv_w, conv_b, conv_ln_g, conv_ln_b, shift_mu, w_decay_up, decay_bias, w_a_up,
           a_bias, w_g_up, k_k, k_a, r_k, lnx_g, lnx_b, w_out, norm_x, norm_mem, w_cq, w_ck, w_cv, w_co,
           norm_ffn, w_route_group, b_route_group, w_route_expert, b_route_expert, w_gate, w_up, w_down,
           norm_final):
    depth = w_in.shape[0]
    batch, seq, d = x_prompt.shape
    dec_batch = x_sample.shape[0]
    assert depth == 1
    assert x_sample.shape[1] == 1 and seq % CHUNK == 0 and seq >= CONV_K - 1
    cw = conv_w.shape[2]
    rw = w_decay_up.shape[2]
    heads = rw // HEAD
    shift_w = shift_mu.shape[1]
    in_w = w_in.shape[2]
    assert in_w == 2 * cw + shift_w and shift_w == 3 * rw + DECAY_LORA + AAA_LORA + GATE_LORA
    assert cw == rw and rw % LORA_PAD == 0
    in_pad = 2 * cw + 3 * rw + LORA_PAD
    qw = 3 * rw + LORA_PAD

    xp = x_prompt.reshape(batch * seq, d)
    xs = x_sample.reshape(dec_batch, d)
    outs = {k: [] for k in ("conv_p", "shift_p", "rwkv_p", "memk_p", "memv_p", "conv_s", "shift_s", "rwkv_s")}
    bd = _block_diag_ones(2 * LANES, HEAD)

    for l in range(depth):
        w_in_b = _pad_cols(w_in[l], in_pad).astype(BF16)
        w_out_c = w_out[l, :cw].astype(BF16)
        w_out_r = w_out[l, cw:].astype(BF16)
        w_cq_b = w_cq[l].astype(BF16)
        w_ck_b = w_ck[l].astype(BF16)
        w_cv_b = w_cv[l].astype(BF16)
        w_co_b = w_co[l].astype(BF16)
        zeros_l = jnp.zeros((DECAY_LORA, rw), F32)
        wd_pad = jnp.concatenate([w_decay_up[l], zeros_l], axis=0).astype(BF16)
        wa_pad = jnp.concatenate([zeros_l, w_a_up[l]], axis=0).astype(BF16)
        wg_pad = jnp.pad(w_g_up[l], ((0, 2 * LANES - GATE_LORA), (0, 0))).astype(BF16)
        mu_pad = _pad_cols(shift_mu[l].reshape(1, shift_w), qw)
        vec = lambda x: x.reshape(1, rw)
        pp = (mu_pad, wd_pad, wa_pad, wg_pad, vec(decay_bias[l]), vec(a_bias[l]), vec(k_k[l]), vec(k_a[l]),
              vec(r_k[l]), bd)
        w_route = jnp.concatenate([w_route_expert[l].reshape(d, N_EXPERTS), w_route_group[l]], axis=1)
        w_route = _pad_cols(w_route, LANES)
        wr_hi = w_route.astype(BF16)
        wr_lo = (w_route - wr_hi.astype(F32)).astype(BF16)
        b_route = _pad_cols(jnp.concatenate([b_route_expert[l].reshape(1, N_EXPERTS),
                                             b_route_group[l].reshape(1, N_GROUPS)], axis=1), LANES)
        de = w_gate.shape[-1]
        wg_e = w_gate[l].reshape(N_EXPERTS, d, de).astype(BF16)
        wu_e = w_up[l].reshape(N_EXPERTS, d, de).astype(BF16)
        wd_e = w_down[l].reshape(N_EXPERTS, de, d).astype(BF16)
        nf = norm_final

        mem2 = mem_prompt.reshape(batch * N_MEM, d)
        mk = _norm_matmul_heads(mem2, norm_mem[l], w_ck_b, X_HEADS, 256)
        mv = _norm_matmul_heads(mem2, norm_mem[l], w_cv_b, X_HEADS, 256)
        proj = _norm_matmul(xp, norm_mix[l], w_in_b, 1024, 512)
        c_p, conv_new = _conv_prefill(proj, jnp.zeros((batch, CONV_K - 1, cw), F32), conv_w[l], conv_b[l],
                                      conv_ln_g[l], conv_ln_b[l], batch, seq)
        prep = _rwkv_prep_prefill(proj, jnp.zeros((batch, qw), F32), pp, batch, seq, rw)
        o_p, s_p = _rwkv_chunked(*prep[:6], jnp.zeros((batch, heads, HEAD, HEAD), F32), batch, seq)
        o_p = _rwkv_post(o_p, prep[6], prep[7], lnx_g[l], lnx_b[l], bd)
        shift_new = proj.reshape(batch, seq, in_pad)[:, -1, 2 * cw:2 * cw + shift_w]
        xp = _matmul_res([c_p, o_p], [w_out_c, w_out_r], xp, 512)
        qx = _norm_matmul(xp, norm_x[l], w_cq_b, 1024, 512)
        ctx = _attn_prefill(qx, mk, mv, batch, seq)
        xp = _matmul_res([ctx], [w_co_b], xp, 512)
        xp_new = _moe_sorted(xp, norm_ffn[l], wr_hi, wr_lo, b_route, wg_e, wu_e, wd_e, nf)
        outs["conv_p"].append(conv_new)
        outs["shift_p"].append(shift_new)
        outs["rwkv_p"].append(s_p)
        outs["memk_p"].append(mk.reshape(batch, N_MEM, X_HEADS, d // X_HEADS))
        outs["memv_p"].append(mv.reshape(batch, N_MEM, X_HEADS, d // X_HEADS))
        xp = xp_new

        proj_s = _norm_matmul(xs, norm_mix[l], w_in_b, 128, 512)
        c_s, conv_new_s = _conv_decode(proj_s, cache_conv[l], conv_w[l], conv_b[l], conv_ln_g[l],
                                       conv_ln_b[l])
        prep_s = _rwkv_prep_decode(proj_s, _pad_cols(state_shift[l], qw), pp, rw)
        o_s, s_s = _rwkv_step(*prep_s[:6], state_rwkv[l].astype(F32))
        o_s = _rwkv_post(o_s, prep_s[6], prep_s[7], lnx_g[l], lnx_b[l], bd)
        xs = _matmul_res([c_s, o_s], [w_out_c, w_out_r], xs, 128)
        qs = _norm_matmul(xs, norm_x[l], w_cq_b, 128, 512)
        ctx_s = _attn_decode(qs, cache_mem_k[l], cache_mem_v[l])
        xs = _matmul_res([ctx_s], [w_co_b], xs, 128)
        h2s, comb_s = _router(xs, norm_ffn[l], wr_hi, wr_lo, b_route)
        xs = _moe_dense(h2s, comb_s, wg_e, wu_e, wd_e, xs, nf)
        outs["conv_s"].append(conv_new_s)
        outs["shift_s"].append(proj_s[:, 2 * cw:2 * cw + shift_w])
        outs["rwkv_s"].append(s_s)

    y_prompt = xp.reshape(batch, seq, d)
    y_sample = xs.reshape(dec_batch, 1, d)
    st = lambda k: jnp.stack(outs[k])
    return (y_prompt, y_sample, st("conv_p"), st("shift_p"), st("rwkv_p"), st("memk_p"), st("memv_p"),
            st("conv_s"), st("shift_s"), st("rwkv_s"))
```

```python
import functools
import math

import jax
import jax.numpy as jnp
from jax import lax
from jax.experimental import pallas as pl
from jax.experimental.pallas import tpu as pltpu

F32 = jnp.float32
BF16 = jnp.bfloat16

CONV_K = 31
HEAD = 64
PAIR = 2 * HEAD
CHUNK = 64
DECAY_LORA = 64
AAA_LORA = 64
GATE_LORA = 160
LORA_PAD = 512
N_MEM = 256
X_HEADS = 4
N_GROUPS = 4
EXP_PER_GROUP = 8
N_EXPERTS = N_GROUPS * EXP_PER_GROUP
RMS_EPS = 1e-6
LN_EPS = 1e-5
GN_EPS = 64e-5
DECAY_SCALE = math.exp(-0.5)
NEG_BIG = -1e30
MOE_BM = 256
LANES = 128
VMEM_LIMIT = 56 * 1024 * 1024


def _cparams(*sem):
    return pltpu.CompilerParams(dimension_semantics=sem, vmem_limit_bytes=VMEM_LIMIT)


def _dot(a, b):
    return jnp.dot(a, b, preferred_element_type=F32)


def _dot_nt(a, b):
    return lax.dot_general(a, b, (((1,), (1,)), ((), ())), preferred_element_type=F32)


def _split_dot(x, w_bf16):
    hi = x.astype(BF16)
    lo = (x - hi.astype(F32)).astype(BF16)
    return _dot(hi, w_bf16) + _dot(lo, w_bf16)


def _rms(x, g, eps=RMS_EPS):
    return x * lax.rsqrt(jnp.mean(x * x, axis=-1, keepdims=True) + eps) * g


def _norm_mm_kernel(x_ref, g_ref, w_ref, o_ref, xn_ref):
    @pl.when(pl.program_id(1) == 0)
    def _():
        xn_ref[...] = _rms(x_ref[...], g_ref[...]).astype(BF16)

    o_ref[...] = _dot(xn_ref[...], w_ref[...].astype(BF16))


def _norm_matmul(x, g, w, bm, bn):
    m, k = x.shape
    n = w.shape[1]
    bm = min(bm, m)
    return pl.pallas_call(
        _norm_mm_kernel,
        grid=(m // bm, n // bn),
        in_specs=[pl.BlockSpec((bm, k), lambda i, j: (i, 0)),
                  pl.BlockSpec((1, k), lambda i, j: (0, 0)),
                  pl.BlockSpec((k, bn), lambda i, j: (0, j))],
        out_specs=pl.BlockSpec((bm, bn), lambda i, j: (i, j)),
        out_shape=jax.ShapeDtypeStruct((m, n), F32),
        scratch_shapes=[pltpu.VMEM((bm, k), BF16)],
        compiler_params=_cparams("parallel", "arbitrary"),
        name="norm_matmul",
    )(x, g.reshape(1, k), w)


def _norm_mm_heads_kernel(x_ref, g_ref, w_ref, o_ref, ob_ref):
    res = _dot(_rms(x_ref[...], g_ref[...]).astype(BF16), w_ref[...])
    ob_ref[...] = res.astype(BF16)
    dh = o_ref.shape[2]
    for h in range(o_ref.shape[1]):
        o_ref[:, h, :] = res[:, h * dh:(h + 1) * dh]


def _norm_matmul_heads(x, g, w, n_heads, bm):
    m, k = x.shape
    n = w.shape[1]
    bm = min(bm, m)
    return pl.pallas_call(
        _norm_mm_heads_kernel,
        grid=(m // bm,),
        in_specs=[pl.BlockSpec((bm, k), lambda i: (i, 0)),
                  pl.BlockSpec((1, k), lambda i: (0, 0)),
                  pl.BlockSpec((k, n), lambda i: (0, 0))],
        out_specs=[pl.BlockSpec((bm, n_heads, n // n_heads), lambda i: (i, 0, 0)),
                   pl.BlockSpec((bm, n), lambda i: (i, 0))],
        out_shape=[jax.ShapeDtypeStruct((m, n_heads, n // n_heads), F32),
                   jax.ShapeDtypeStruct((m, n), BF16)],
        compiler_params=_cparams("parallel"),
        name="norm_matmul_heads",
    )(x, g.reshape(1, k), w)


def _mm_res_kernel(*refs, n_lhs):
    res_ref = refs[2 * n_lhs]
    o_ref = refs[2 * n_lhs + 1]
    acc = res_ref[...]
    for a_ref, w_ref in zip(refs[:n_lhs], refs[n_lhs:2 * n_lhs]):
        acc = acc + _dot(a_ref[...].astype(BF16), w_ref[...])
    o_ref[...] = acc


def _matmul_res(lhs, ws, res, bm):
    m, n = res.shape
    bm = min(bm, m)
    n_lhs = len(lhs)
    in_specs = [pl.BlockSpec((bm, a.shape[1]), lambda i: (i, 0)) for a in lhs]
    in_specs += [pl.BlockSpec(w.shape, lambda i: (0, 0)) for w in ws]
    in_specs += [pl.BlockSpec((bm, n), lambda i: (i, 0))]
    return pl.pallas_call(
        functools.partial(_mm_res_kernel, n_lhs=n_lhs),
        grid=(m // bm,),
        in_specs=in_specs,
        out_specs=pl.BlockSpec((bm, n), lambda i: (i, 0)),
        out_shape=jax.ShapeDtypeStruct((m, n), F32),
        compiler_params=_cparams("parallel"),
        name="matmul_res",
    )(*lhs, *ws, res)


def _ln_silu(cf, lg, lb):
    mu = jnp.mean(cf, axis=-1, keepdims=True)
    d = cf - mu
    var = jnp.mean(d * d, axis=-1, keepdims=True)
    y = d * lax.rsqrt(var + LN_EPS) * lg + lb
    return y * jax.nn.sigmoid(y)


def _conv_prefill_kernel(a_ref, g_ref, buf_ref, w_ref, cb_ref, lg_ref, lb_ref, c_ref, nc_ref,
                         uf_ref, cv_ref, sh_ref, *, tt, halo):
    t = pl.program_id(1)
    pad = 32 - halo

    @pl.when(t == 0)
    def _():
        uf_ref[pad:32, :] = buf_ref[0]

    @pl.when(t > 0)
    def _():
        uf_ref[pad:32, :] = uf_ref[tt + pad:tt + 32, :]

    uf_ref[32:32 + tt, :] = a_ref[...] * jax.nn.sigmoid(g_ref[...])

    for sft in range(8):
        n_rows = sh_ref.shape[1] if sft < 7 else sh_ref.shape[1] - 8
        sh_ref[sft, 0:n_rows, :] = uf_ref[pad + sft:pad + sft + n_rows, :]

    width = uf_ref.shape[1]
    rb = 64
    for r0 in range(0, tt, rb):
        for l0 in range(0, width, LANES):
            acc = jnp.zeros((rb, LANES), F32)
            for j in range(CONV_K):
                base = r0 + j - j % 8
                acc = acc + sh_ref[j % 8, base:base + rb, l0:l0 + LANES] * w_ref[j:j + 1, l0:l0 + LANES]
            cv_ref[r0:r0 + rb, l0:l0 + LANES] = acc

    c_ref[...] = _ln_silu(cv_ref[...] + cb_ref[...], lg_ref[...], lb_ref[...]).astype(c_ref.dtype)

    @pl.when(t == pl.num_programs(1) - 1)
    def _():
        nc_ref[0] = uf_ref[tt + pad:tt + 32, :]


def _conv_prefill(proj, conv_buf, conv_w, conv_b, ln_g, ln_b, batch, seq):
    cw = conv_w.shape[1]
    halo = CONV_K - 1
    tt = min(256, seq)
    nt = seq // tt
    row = lambda b, t: (b * nt + t, 0)
    vec = pl.BlockSpec((1, cw), lambda b, t: (0, 0))
    return pl.pallas_call(
        functools.partial(_conv_prefill_kernel, tt=tt, halo=halo),
        grid=(batch, nt),
        in_specs=[pl.BlockSpec((tt, cw), row),
                  pl.BlockSpec((tt, cw), lambda b, t: (b * nt + t, 1)),
                  pl.BlockSpec((1, halo, cw), lambda b, t: (b, 0, 0)),
                  pl.BlockSpec((CONV_K, cw), lambda b, t: (0, 0)),
                  vec, vec, vec],
        out_specs=[pl.BlockSpec((tt, cw), row),
                   pl.BlockSpec((1, halo, cw), lambda b, t: (b, 0, 0))],
        out_shape=[jax.ShapeDtypeStruct((batch * seq, cw), BF16),
                   jax.ShapeDtypeStruct((batch, halo, cw), F32)],
        scratch_shapes=[pltpu.VMEM((tt + 32, cw), F32), pltpu.VMEM((tt, cw), F32),
                        pltpu.VMEM((8, tt + 24, cw), F32)],
        compiler_params=_cparams("parallel", "arbitrary"),
        name="conv_prefill",
    )(proj, proj, conv_buf, conv_w, conv_b.reshape(1, cw), ln_g.reshape(1, cw), ln_b.reshape(1, cw))


def _conv_decode_kernel(a_ref, g_ref, cache_ref, w_ref, cb_ref, lg_ref, lb_ref, c_ref, nc_ref):
    halo = CONV_K - 1
    u = a_ref[...] * jax.nn.sigmoid(g_ref[...])
    acc = u * w_ref[halo:halo + 1, :]
    for j in range(halo):
        acc = acc + cache_ref[:, j, :] * w_ref[j:j + 1, :]
    c_ref[...] = _ln_silu(acc + cb_ref[...], lg_ref[...], lb_ref[...]).astype(c_ref.dtype)
    nc_ref[:, 0:halo - 1, :] = cache_ref[:, 1:halo, :]
    nc_ref[:, halo - 1, :] = u


def _conv_decode(proj, cache, conv_w, conv_b, ln_g, ln_b):
    batch, halo, cw = cache.shape
    bb = 8
    vec = pl.BlockSpec((1, cw), lambda i: (0, 0))
    return pl.pallas_call(
        _conv_decode_kernel,
        grid=(batch // bb,),
        in_specs=[pl.BlockSpec((bb, cw), lambda i: (i, 0)),
                  pl.BlockSpec((bb, cw), lambda i: (i, 1)),
                  pl.BlockSpec((bb, halo, cw), lambda i: (i, 0, 0)),
                  pl.BlockSpec((CONV_K, cw), lambda i: (0, 0)),
                  vec, vec, vec],
        out_specs=[pl.BlockSpec((bb, cw), lambda i: (i, 0)),
                   pl.BlockSpec((bb, halo, cw), lambda i: (i, 0, 0))],
        out_shape=[jax.ShapeDtypeStruct((batch, cw), BF16),
                   jax.ShapeDtypeStruct((batch, halo, cw), F32)],
        compiler_params=_cparams("parallel"),
        name="conv_decode",
    )(proj, proj, cache, conv_w, conv_b.reshape(1, cw), ln_g.reshape(1, cw), ln_b.reshape(1, cw))


def _head_sum(x, bd_ref):
    blk = bd_ref.shape[0]
    parts = [_split_dot(x[:, l0:l0 + blk], bd_ref[...]) for l0 in range(0, x.shape[1], blk)]
    return jnp.concatenate(parts, axis=1)


def _prep_math(q, qp, mu_ref, wd_ref, wa_ref, wg_ref, db_ref, ab_ref, kk_ref, ka_ref, rk_ref, bd_ref):
    rw = q[0].shape[1]
    offs = (0, rw, 2 * rw, 3 * rw)
    r, k, v, lo = [x + (xp - x) * mu_ref[:, o:o + x.shape[1]] for x, xp, o in zip(q, qp, offs)]
    pwa = lo[:, 0:LANES]
    pg = lo[:, LANES:3 * LANES]
    dec_in = _dot(jnp.tanh(pwa).astype(BF16), wd_ref[...])
    a_in = _dot(pwa.astype(BF16), wa_ref[...])
    gate = _dot(jax.nn.sigmoid(pg).astype(BF16), wg_ref[...])
    logw = -DECAY_SCALE * jax.nn.sigmoid(db_ref[...] + dec_in)
    a = jax.nn.sigmoid(ab_ref[...] + a_in)
    kk = k * kk_ref[...]
    kk = kk / jnp.maximum(jnp.sqrt(_head_sum(kk * kk, bd_ref)), 1e-12)
    k2 = k * (1.0 + (a - 1.0) * ka_ref[...])
    bonus = _head_sum(r * k2 * rk_ref[...], bd_ref) * v
    return r, logw, k2, v, kk, a, bonus, gate


def _prep_prefill_kernel(r_ref, k_ref, v_ref, lo_ref, sb_ref, mu_ref, wd_ref, wa_ref, wg_ref, db_ref,
                         ab_ref, kk_ref, ka_ref, rk_ref, bd_ref, *rest):
    outs = rest[:8]
    carry_ref = rest[8]
    t = pl.program_id(1)

    @pl.when(t == 0)
    def _():
        carry_ref[0:1, :] = sb_ref[0]

    q = [r_ref[...], k_ref[...], v_ref[...], lo_ref[...]]
    tt = q[0].shape[0]
    first = lax.broadcasted_iota(jnp.int32, (tt, 1), 0) == 0
    qp = []
    off = 0
    for x in q:
        w = x.shape[1]
        qp.append(jnp.where(first, carry_ref[0:1, off:off + w], pltpu.roll(x, 1, 0)))
        off += w
    off = 0
    for x in q:
        w = x.shape[1]
        carry_ref[0:1, off:off + w] = x[tt - 1:tt, :]
        off += w
    res = _prep_math(q, qp, mu_ref, wd_ref, wa_ref, wg_ref, db_ref, ab_ref, kk_ref, ka_ref, rk_ref, bd_ref)
    for o_ref, val in zip(outs, res):
        o_ref[...] = val


def _prep_decode_kernel(r_ref, k_ref, v_ref, lo_ref, rp_ref, kp_ref, vp_ref, lop_ref, mu_ref, wd_ref,
                        wa_ref, wg_ref, db_ref, ab_ref, kk_ref, ka_ref, rk_ref, bd_ref, *outs):
    q = [r_ref[...], k_ref[...], v_ref[...], lo_ref[...]]
    qp = [rp_ref[...], kp_ref[...], vp_ref[...], lop_ref[...]]
    res = _prep_math(q, qp, mu_ref, wd_ref, wa_ref, wg_ref, db_ref, ab_ref, kk_ref, ka_ref, rk_ref, bd_ref)
    for o_ref, val in zip(outs, res):
        o_ref[...] = val


def _prep_param_specs(rw, idx):
    full = lambda shape: pl.BlockSpec(shape, idx)
    vec = full((1, rw))
    return [full((1, 3 * rw + LORA_PAD)), full((LANES, rw)), full((LANES, rw)), full((2 * LANES, rw)),
            vec, vec, vec, vec, vec, full((2 * LANES, 2 * LANES))]


def _rwkv_prep_prefill(proj, shift_buf, pp, batch, seq, rw):
    tt = min(256, seq)
    nt = seq // tt
    lora_blk = (2 * rw + 3 * rw) // LORA_PAD
    col = lambda c: (lambda b, t: (b * nt + t, c))
    qw = 3 * rw + LORA_PAD
    in_specs = [pl.BlockSpec((tt, rw), col(2)), pl.BlockSpec((tt, rw), col(3)),
                pl.BlockSpec((tt, rw), col(4)), pl.BlockSpec((tt, LORA_PAD), col(lora_blk)),
                pl.BlockSpec((1, 1, qw), lambda b, t: (b, 0, 0))]
    in_specs += _prep_param_specs(rw, lambda b, t: (0, 0))
    out_spec = pl.BlockSpec((tt, rw), col(0))
    return pl.pallas_call(
        _prep_prefill_kernel,
        grid=(batch, nt),
        in_specs=in_specs,
        out_specs=[out_spec] * 8,
        out_shape=[jax.ShapeDtypeStruct((batch * seq, rw), F32)] * 8,
        scratch_shapes=[pltpu.VMEM((8, qw), F32)],
        compiler_params=_cparams("parallel", "arbitrary"),
        name="rwkv_prep_prefill",
    )(proj, proj, proj, proj, shift_buf.reshape(batch, 1, qw), *pp)


def _rwkv_prep_decode(proj, shift_state, pp, rw):
    batch = proj.shape[0]
    bb = min(128, batch)
    lora_blk = (2 * rw + 3 * rw) // LORA_PAD
    col = lambda c: (lambda i: (i, c))
    in_specs = [pl.BlockSpec((bb, rw), col(2)), pl.BlockSpec((bb, rw), col(3)),
                pl.BlockSpec((bb, rw), col(4)), pl.BlockSpec((bb, LORA_PAD), col(lora_blk)),
                pl.BlockSpec((bb, rw), col(0)), pl.BlockSpec((bb, rw), col(1)),
                pl.BlockSpec((bb, rw), col(2)), pl.BlockSpec((bb, LORA_PAD), col(3 * rw // LORA_PAD))]
    in_specs += _prep_param_specs(rw, lambda i: (0, 0))
    return pl.pallas_call(
        _prep_decode_kernel,
        grid=(batch // bb,),
        in_specs=in_specs,
        out_specs=[pl.BlockSpec((bb, rw), col(0))] * 8,
        out_shape=[jax.ShapeDtypeStruct((batch, rw), F32)] * 8,
        compiler_params=_cparams("parallel"),
        name="rwkv_prep_decode",
    )(proj, proj, proj, proj, shift_state, shift_state, shift_state, shift_state, *pp)


def _stack2(x, smask):
    return jnp.where(smask, jnp.concatenate([x, x], axis=0), 0.0)


def _rwkv_chunk_kernel(r_ref, lw_ref, k_ref, v_ref, kk_ref, a_ref, s0_ref, o_ref, so_ref, s_ref):
    c = pl.program_id(1)
    cs = r_ref.shape[0]
    n_pairs = r_ref.shape[1] // PAIR
    two = 2 * cs

    @pl.when(c == 0)
    def _():
        z = jnp.zeros((HEAD, HEAD), F32)
        for p in range(n_pairs):
            top = jnp.concatenate([s0_ref[0, 2 * p], z], axis=1)
            bot = jnp.concatenate([z, s0_ref[0, 2 * p + 1]], axis=1)
            s_ref[p] = jnp.concatenate([top, bot], axis=0)

    ri = lax.broadcasted_iota(jnp.int32, (two, two), 0)
    ci = lax.broadcasted_iota(jnp.int32, (two, two), 1)
    strict = ci < ri
    incl = ci <= ri
    eye = (ci == ri).astype(F32)
    smask = (lax.broadcasted_iota(jnp.int32, (two, PAIR), 0) < cs) == (
        lax.broadcasted_iota(jnp.int32, (two, PAIR), 1) < HEAD)
    tri = (lax.broadcasted_iota(jnp.int32, (cs, cs), 1)
           <= lax.broadcasted_iota(jnp.int32, (cs, cs), 0)).astype(BF16)

    lw_all = lw_ref[...]
    lw_hi = lw_all.astype(BF16)
    lw_lo = (lw_all - lw_hi.astype(F32)).astype(BF16)
    cum_all = _dot(tri, lw_hi) + _dot(tri, lw_lo)

    pairs = range(n_pairs)
    cat = jnp.concatenate
    prep = []
    for p in pairs:
        sl = slice(p * PAIR, (p + 1) * PAIR)
        lw = lw_all[:, sl]
        cum = cum_all[:, sl]
        tot = cum[cs - 1:cs, :]
        g_inv = jnp.exp(-cum)
        g_end = jnp.exp(tot - cum)
        kk = kk_ref[:, sl]
        k2 = k_ref[:, sl]
        bb = kk * a_ref[:, sl]
        prep.append(dict(
            g_tot=jnp.exp(tot),
            a_b=_stack2(kk * jnp.exp(cum - lw), smask).astype(BF16),
            r_s=_stack2(r_ref[:, sl] * jnp.exp(cum), smask),
            bk=cat([_stack2(bb * g_inv, smask), _stack2(k2 * g_inv, smask)], axis=0).astype(BF16),
            v_s=_stack2(v_ref[:, sl], smask),
            bg_s=_stack2(bb * g_end, smask).astype(BF16),
            kg_s=_stack2(k2 * g_end, smask).astype(BF16)))
    a_b = [q["a_b"] for q in prep]
    r_s = [q["r_s"] for q in prep]
    v_s = [q["v_s"] for q in prep]
    v_b = [x.astype(BF16) for x in v_s]

    gram = [_dot_nt(cat([a_b[p], r_s[p].astype(BF16)], axis=0), prep[p]["bk"]) for p in pairs]
    l_ab = [jnp.where(strict, g[0:two, 0:two], 0.0) for g in gram]
    l_ak = [jnp.where(strict, g[0:two, two:], 0.0).astype(BF16) for g in gram]
    m_rb = [jnp.where(incl, g[two:, 0:two], 0.0).astype(BF16) for g in gram]
    m_rk = [jnp.where(incl, g[two:, two:], 0.0).astype(BF16) for g in gram]

    tm = [eye - x for x in l_ab]
    pw = l_ab
    n = 1
    while 2 * n < cs:
        pw = [_dot(x.astype(BF16), x.astype(BF16)) for x in pw]
        tm = [t + _dot(t.astype(BF16), x.astype(BF16)) for t, x in zip(tm, pw)]
        n *= 2
    tm_b = [t.astype(BF16) for t in tm]

    w1 = [_dot(l_ak[p], v_b[p]) for p in pairs]
    ua = [_dot(tm_b[p], cat([w1[p].astype(BF16), a_b[p]], axis=1)) for p in pairs]
    ua_b = [x.astype(BF16) for x in ua]
    mrb_ua = [_dot(m_rb[p], ua_b[p]) for p in pairs]
    o0 = [_dot(m_rk[p], v_b[p]) - mrb_ua[p][:, 0:PAIR] for p in pairs]
    rt = [(r_s[p] - mrb_ua[p][:, PAIR:]).astype(BF16) for p in pairs]

    s_old = [s_ref[p] for p in pairs]
    s_b = [x.astype(BF16) for x in s_old]
    ua_t = [cat([x[:, 0:PAIR].T, x[:, PAIR:].T], axis=0).astype(BF16) for x in ua]
    uat_bg = [_dot(ua_t[p], prep[p]["bg_s"]) for p in pairs]
    vt_kg = [_dot(v_s[p].T.astype(BF16), prep[p]["kg_s"]) for p in pairs]
    s_new = [s_old[p] * prep[p]["g_tot"] - _dot(s_b[p], uat_bg[p][PAIR:, :].astype(BF16))
             + vt_kg[p] - uat_bg[p][0:PAIR, :] for p in pairs]
    o_st = [o0[p] + _dot_nt(rt[p], s_b[p]) for p in pairs]

    o_ref[...] = cat([x[0:cs, :] + x[cs:two, :] for x in o_st], axis=1)
    s_ref[...] = jnp.stack(s_new, axis=0)

    @pl.when(c == pl.num_programs(1) - 1)
    def _():
        for p in range(n_pairs):
            s = s_ref[p]
            so_ref[0, 2 * p] = s[0:HEAD, 0:HEAD]
            so_ref[0, 2 * p + 1] = s[HEAD:PAIR, HEAD:PAIR]


def _rwkv_chunked(r, logw, k2, v, kk, a, s0, batch, seq):
    rw = r.shape[1]
    nc = seq // CHUNK
    heads = rw // HEAD
    row = pl.BlockSpec((CHUNK, rw), lambda b, c: (b * nc + c, 0))
    st = pl.BlockSpec((1, heads, HEAD, HEAD), lambda b, c: (b, 0, 0, 0))
    return pl.pallas_call(
        _rwkv_chunk_kernel,
        grid=(batch, nc),
        in_specs=[row] * 6 + [st],
        out_specs=[row, st],
        out_shape=[jax.ShapeDtypeStruct((batch * seq, rw), F32),
                   jax.ShapeDtypeStruct((batch, heads, HEAD, HEAD), F32)],
        scratch_shapes=[pltpu.VMEM((rw // PAIR, PAIR, PAIR), F32)],
        compiler_params=_cparams("parallel", "arbitrary"),
        name="rwkv_chunked",
    )(r, logw, k2, v, kk, a, s0)


def _rwkv_step_kernel(r_ref, lw_ref, k_ref, v_ref, kk_ref, a_ref, s_ref, o_ref, so_ref, *, heads):
    bb = s_ref.shape[0]
    eye = (lax.broadcasted_iota(jnp.int32, (HEAD, HEAD), 0)
           == lax.broadcasted_iota(jnp.int32, (HEAD, HEAD), 1))

    def body(bi, carry):
        hs = range(heads)
        rows = [pl.ds(bi * heads + h, 1) for h in hs]
        kk = [kk_ref[r, :] for r in rows]
        s = [s_ref[bi, h] for h in hs]
        sa = [jnp.sum(s[h] * kk[h], axis=1, keepdims=True) for h in hs]
        v_col = [jnp.sum(jnp.where(eye, v_ref[rows[h], :], 0.0), axis=1, keepdims=True) for h in hs]
        s_new = [s[h] * jnp.exp(lw_ref[rows[h], :]) - sa[h] * (kk[h] * a_ref[rows[h], :])
                 + v_col[h] * k_ref[rows[h], :] for h in hs]
        for h in hs:
            so_ref[bi, h] = s_new[h]
        o_col = [jnp.sum(s_new[h] * r_ref[rows[h], :], axis=1, keepdims=True) for h in hs]
        o_row = [jnp.sum(jnp.where(eye, o_col[h], 0.0), axis=0, keepdims=True) for h in hs]
        o_ref[pl.ds(pl.multiple_of(bi * heads, heads), heads), :] = jnp.concatenate(o_row, axis=0)
        return carry

    lax.fori_loop(0, bb, body, 0)


def _rwkv_step(r, logw, k2, v, kk, a, s0):
    batch, rw = r.shape
    heads = rw // HEAD
    bb = 8
    flat = lambda x: x.reshape(batch * heads, HEAD)
    row = pl.BlockSpec((bb * heads, HEAD), lambda i: (i, 0))
    st = pl.BlockSpec((bb, heads, HEAD, HEAD), lambda i: (i, 0, 0, 0))
    o, s_new = pl.pallas_call(
        functools.partial(_rwkv_step_kernel, heads=heads),
        grid=(batch // bb,),
        in_specs=[row] * 6 + [st],
        out_specs=[row, st],
        out_shape=[jax.ShapeDtypeStruct((batch * heads, HEAD), F32),
                   jax.ShapeDtypeStruct(s0.shape, F32)],
        compiler_params=_cparams("parallel"),
        name="rwkv_step",
    )(flat(r), flat(logw), flat(k2), flat(v), flat(kk), flat(a), s0)
    return o.reshape(batch, rw), s_new


def _rwkv_post_kernel(o_ref, bonus_ref, gate_ref, lg_ref, lb_ref, bd_ref, y_ref):
    o = o_ref[...]
    mu = _head_sum(o, bd_ref) * (1.0 / HEAD)
    d = o - mu
    var = _head_sum(d * d, bd_ref) * (1.0 / HEAD)
    y = d * lax.rsqrt(var + GN_EPS) * lg_ref[...] + lb_ref[...]
    y_ref[...] = ((y + bonus_ref[...]) * gate_ref[...]).astype(y_ref.dtype)


def _rwkv_post(o, bonus, gate, lnx_g, lnx_b, bd):
    m, rw = o.shape
    bm = min(512, m)
    row = pl.BlockSpec((bm, rw), lambda i: (i, 0))
    vec = pl.BlockSpec((1, rw), lambda i: (0, 0))
    return pl.pallas_call(
        _rwkv_post_kernel,
        grid=(m // bm,),
        in_specs=[row, row, row, vec, vec, pl.BlockSpec(bd.shape, lambda i: (0, 0))],
        out_specs=row,
        out_shape=jax.ShapeDtypeStruct((m, rw), BF16),
        compiler_params=_cparams("parallel"),
        name="rwkv_post",
    )(o, bonus, gate, lnx_g.reshape(1, rw), lnx_b.reshape(1, rw), bd)


def _attn_prefill_kernel(q_ref, k_ref, v_ref, o_ref, *, n_heads):
    d = q_ref.shape[1] // n_heads
    scale = d ** -0.5
    for h in range(n_heads):
        sl = slice(h * d, (h + 1) * d)
        s = _dot_nt(q_ref[:, sl].astype(BF16), k_ref[:, sl]) * scale
        p = jnp.exp(s - jnp.max(s, axis=-1, keepdims=True))
        att = p / jnp.sum(p, axis=-1, keepdims=True)
        o_ref[:, sl] = _dot(att.astype(BF16), v_ref[:, sl]).astype(o_ref.dtype)


def _attn_prefill(q, mem_k, mem_v, batch, seq):
    d = q.shape[1]
    tt = min(512, seq)
    nt = seq // tt
    kv = pl.BlockSpec((N_MEM, d), lambda b, t: (b, 0))
    row = pl.BlockSpec((tt, d), lambda b, t: (b * nt + t, 0))
    return pl.pallas_call(
        functools.partial(_attn_prefill_kernel, n_heads=X_HEADS),
        grid=(batch, nt),
        in_specs=[row, kv, kv],
        out_specs=row,
        out_shape=jax.ShapeDtypeStruct((batch * seq, d), BF16),
        compiler_params=_cparams("parallel", "arbitrary"),
        name="attn_prefill",
    )(q, mem_k, mem_v)


def _attn_decode_kernel(q_ref, k_ref, v_ref, o_ref, *, n_heads):
    bb = q_ref.shape[0]
    d = q_ref.shape[2] // n_heads
    scale = d ** -0.5
    for bi in range(bb):
        q = q_ref[bi]
        for h in range(n_heads):
            sl = slice(h * d, (h + 1) * d)
            s = jnp.sum(k_ref[bi, :, h, :] * q[:, sl], axis=1, keepdims=True) * scale
            p = jnp.exp(s - jnp.max(s, axis=0, keepdims=True))
            att = p / jnp.sum(p, axis=0, keepdims=True)
            o_ref[bi, :, sl] = jnp.sum(att * v_ref[bi, :, h, :], axis=0, keepdims=True)


def _attn_decode(q, cache_k, cache_v):
    batch, d = q.shape
    bb = 2
    kv = pl.BlockSpec((bb, N_MEM, X_HEADS, d // X_HEADS), lambda i: (i, 0, 0, 0))
    row = pl.BlockSpec((bb, 1, d), lambda i: (i, 0, 0))
    out = pl.pallas_call(
        functools.partial(_attn_decode_kernel, n_heads=X_HEADS),
        grid=(batch // bb,),
        in_specs=[row, kv, kv],
        out_specs=row,
        out_shape=jax.ShapeDtypeStruct((batch, 1, d), F32),
        compiler_params=_cparams("parallel"),
        name="attn_decode",
    )(q.reshape(batch, 1, d), cache_k, cache_v)
    return out.reshape(batch, d)


def _route(x_ref, g_ref, wh_ref, wl_ref, b_ref):
    h = _rms(x_ref[...], g_ref[...])
    hh = h.astype(BF16)
    hl = (h - hh.astype(F32)).astype(BF16)
    logits = _dot(hh, wh_ref[...]) + _dot(hl, wh_ref[...]) + _dot(hh, wl_ref[...]) + b_ref[...]
    lane = lax.broadcasted_iota(jnp.int32, (1, LANES), 1).astype(F32)
    is_g = (lane >= N_EXPERTS) & (lane < N_EXPERTS + N_GROUPS)
    lgm = jnp.where(is_g, logits, NEG_BIG)
    gmax = jnp.max(lgm, axis=1, keepdims=True)
    gsum = jnp.sum(jnp.where(is_g, jnp.exp(lgm - gmax), 0.0), axis=1, keepdims=True)
    g_val = 1.0 / gsum
    g_idx = jnp.min(jnp.where(is_g & (lgm == gmax), lane - N_EXPERTS, 1e9), axis=1, keepdims=True)
    in_grp = (lane < N_EXPERTS) & (jnp.floor(lane * (1.0 / EXP_PER_GROUP)) == g_idx)
    le = jnp.where(in_grp, logits, NEG_BIG)
    m1 = jnp.max(le, axis=1, keepdims=True)
    i1 = jnp.min(jnp.where(in_grp & (le == m1), lane, 1e9), axis=1, keepdims=True)
    rest = in_grp & (lane != i1)
    le2 = jnp.where(rest, logits, NEG_BIG)
    m2 = jnp.max(le2, axis=1, keepdims=True)
    i2 = jnp.min(jnp.where(rest & (le2 == m2), lane, 1e9), axis=1, keepdims=True)
    e2 = jnp.exp(m2 - m1)
    den = 1.0 + e2
    w1 = (1.0 / den) * g_val
    w2 = (e2 / den) * g_val
    return h, lane, i1, i2, w1, w2


def _router_sorted_kernel(x_ref, g_ref, wh_ref, wl_ref, b_ref, init_ref, route_ref, rt_ref, cnt_ref, run_ref):
    @pl.when(pl.program_id(0) == 0)
    def _():
        run_ref[0:1, :] = init_ref[...]

    h, lane, i1, i2, w1, w2 = _route(x_ref, g_ref, wh_ref, wl_ref, b_ref)
    bm = h.shape[0]
    oh1 = lane == i1
    oh2 = lane == i2
    sel = (oh1 | oh2).astype(BF16)
    before = (lax.broadcasted_iota(jnp.int32, (bm, bm), 1)
              < lax.broadcasted_iota(jnp.int32, (bm, bm), 0)).astype(BF16)
    base = run_ref[0:1, :] + _dot(before, sel)
    rank1 = jnp.sum(jnp.where(oh1, base, 0.0), axis=1, keepdims=True)
    rank2 = jnp.sum(jnp.where(oh2, base, 0.0), axis=1, keepdims=True)
    total = run_ref[0:1, :] + jnp.sum(sel.astype(F32), axis=0, keepdims=True)
    run_ref[0:1, :] = total
    cnt_ref[...] = total
    route = jnp.zeros((bm, LANES), F32)
    for idx, val in enumerate((i1, i2, w1, w2, rank1, rank2)):
        route = jnp.where(lane == idx, val, route)
    route_ref[...] = route
    for r0 in range(0, bm, LANES):
        rt_ref[:, r0:r0 + LANES] = route[r0:r0 + LANES, :].T[0:8, :]


def _router_sorted(x, g, wh, wl, bias, init_counts):
    m, d = x.shape
    bm = min(512, m)
    return pl.pallas_call(
        _router_sorted_kernel,
        grid=(m // bm,),
        in_specs=[pl.BlockSpec((bm, d), lambda i: (i, 0)),
                  pl.BlockSpec((1, d), lambda i: (0, 0)),
                  pl.BlockSpec((d, LANES), lambda i: (0, 0)),
                  pl.BlockSpec((d, LANES), lambda i: (0, 0)),
                  pl.BlockSpec((1, LANES), lambda i: (0, 0)),
                  pl.BlockSpec((1, LANES), lambda i: (0, 0))],
        out_specs=[pl.BlockSpec((bm, LANES), lambda i: (i, 0)),
                   pl.BlockSpec((8, bm), lambda i: (0, i)),
                   pl.BlockSpec((1, LANES), lambda i: (0, 0))],
        out_shape=[jax.ShapeDtypeStruct((m, LANES), F32),
                   jax.ShapeDtypeStruct((8, m), F32),
                   jax.ShapeDtypeStruct((1, LANES), F32)],
        scratch_shapes=[pltpu.VMEM((8, LANES), F32)],
        compiler_params=_cparams("arbitrary"),
        name="moe_router_sorted",
    )(x, g.reshape(1, d), wh, wl, bias, init_counts)


def _plan_kernel(seg_ref, rt_ref, pos_ref):
    rt = rt_ref[...]
    rows = []
    for e_row, r_row in ((0, 4), (1, 5)):
        e = rt[e_row:e_row + 1, :]
        start = jnp.zeros_like(e)
        for k in range(N_EXPERTS):
            start = jnp.where(e == k, seg_ref[k].astype(F32), start)
        rows.append((start + rt[r_row:r_row + 1, :]).astype(jnp.int32))
    pos_ref[...] = jnp.concatenate(rows + [jnp.zeros((6, rt.shape[1]), jnp.int32)], axis=0)


def _plan(route_t, seg_start):
    m = route_t.shape[1]
    bt = min(2048, m)
    pos = pl.pallas_call(
        _plan_kernel,
        grid_spec=pltpu.PrefetchScalarGridSpec(
            num_scalar_prefetch=1, grid=(m // bt,),
            in_specs=[pl.BlockSpec((8, bt), lambda i, seg: (0, i))],
            out_specs=pl.BlockSpec((8, bt), lambda i, seg: (0, i))),
        out_shape=jax.ShapeDtypeStruct((8, m), jnp.int32),
        compiler_params=_cparams("arbitrary"),
        name="moe_plan",
    )(seg_start, route_t)
    return pos[0], pos[1]


def _row_copy(src_hbm, src_row, dst, dst_row, sem):
    return pltpu.make_async_copy(src_hbm.at[pl.ds(src_row, 1)], dst.at[pl.ds(dst_row, 1)], sem)


def _dispatch_kernel(p1_ref, p2_ref, seg_ref, cnt_ref, nrow_ref, xa_ref, xb_ref, g_ref, xs_hbm, h_ref,
                     zero_ref, sem, zsem, *, n_a):
    i = pl.program_id(0)
    n = pl.num_programs(0)
    bm = xa_ref.shape[0]
    n_max = xs_hbm.shape[0] // MOE_BM

    def zero_copy(row0):
        return pltpu.make_async_copy(zero_ref, xs_hbm.at[pl.ds(pl.multiple_of(row0, MOE_BM), MOE_BM)], zsem)

    @pl.when(i == 0)
    def _():
        zero_ref[...] = jnp.zeros_like(zero_ref)
        first_free = nrow_ref[0] // MOE_BM

        def tail_start(c, carry):
            zero_copy(c * MOE_BM).start()
            return carry

        def tail_wait(c, carry):
            zero_copy(c * MOE_BM).wait()
            return carry

        for e in range(N_EXPERTS):
            @pl.when(cnt_ref[e] > 0)
            def _():
                zero_copy(seg_ref[e] - MOE_BM).start()

        lax.fori_loop(first_free, n_max, tail_start, 0)
        for e in range(N_EXPERTS):
            @pl.when(cnt_ref[e] > 0)
            def _():
                zero_copy(seg_ref[e] - MOE_BM).wait()

        lax.fori_loop(first_free, n_max, tail_wait, 0)

    def wait_rows(slot):
        for _ in range(2):
            pltpu.make_async_copy(h_ref.at[slot], xs_hbm.at[pl.ds(0, bm)], sem.at[slot]).wait()

    slot = i % 2

    @pl.when(i > 0)
    def _():
        wait_rows(1 - slot)

    @pl.when(i < n_a)
    def _():
        h_ref[slot] = _rms(xa_ref[...], g_ref[...])

    @pl.when(i >= n_a)
    def _():
        h_ref[slot] = _rms(xb_ref[...], g_ref[...])

    def body(r, carry):
        t = i * bm + r
        src = h_ref.at[slot, pl.ds(r, 1)]
        pltpu.make_async_copy(src, xs_hbm.at[pl.ds(p1_ref[t], 1)], sem.at[slot]).start()
        pltpu.make_async_copy(src, xs_hbm.at[pl.ds(p2_ref[t], 1)], sem.at[slot]).start()
        return carry

    lax.fori_loop(0, bm, body, 0, unroll=8)

    @pl.when(i == n - 1)
    def _():
        wait_rows(slot)


def _dispatch(xa, xb, g, pos1, pos2, seg_end, cnt, n_rows_used, n_rows_max):
    (ma, d), mb = xa.shape, xb.shape[0]
    bm = min(LANES, ma, mb)
    assert ma % bm == 0 and mb % bm == 0
    n_a, n_b = ma // bm, mb // bm
    return pl.pallas_call(
        functools.partial(_dispatch_kernel, n_a=n_a),
        grid_spec=pltpu.PrefetchScalarGridSpec(
            num_scalar_prefetch=5, grid=(n_a + n_b,),
            in_specs=[pl.BlockSpec((bm, d), lambda i, *_: (jnp.minimum(i, n_a - 1), 0)),
                      pl.BlockSpec((bm, d), lambda i, *_: (jnp.maximum(i - n_a, 0), 0)),
                      pl.BlockSpec((1, d), lambda i, *_: (0, 0))],
            out_specs=pl.BlockSpec(memory_space=pl.ANY),
            scratch_shapes=[pltpu.VMEM((2, bm, d), F32), pltpu.VMEM((MOE_BM, d), F32),
                            pltpu.SemaphoreType.DMA((2,)), pltpu.SemaphoreType.DMA(())]),
        out_shape=jax.ShapeDtypeStruct((n_rows_max, d), F32),
        compiler_params=_cparams("arbitrary"),
        name="moe_dispatch",
    )(pos1, pos2, seg_end, cnt, n_rows_used, xa, xb, g.reshape(1, d))


def _experts_sorted_kernel(te_ref, nt_ref, xs_ref, wg_ref, wu_ref, wd_ref, ys_ref, wgb_ref, wub_ref, wdb_ref):
    j = pl.program_id(0)
    prev = te_ref[jnp.maximum(j, 1) - 1]

    @pl.when((j == 0) | (te_ref[j] != prev))
    def _():
        wgb_ref[...] = wg_ref[0].astype(BF16)
        wub_ref[...] = wu_ref[0].astype(BF16)
        wdb_ref[...] = wd_ref[0].astype(BF16)

    @pl.when(j < nt_ref[0])
    def _():
        x = xs_ref[...].astype(BF16)
        hg = _dot(x, wgb_ref[...])
        hu = _dot(x, wub_ref[...])
        act = hg * jax.nn.sigmoid(hg) * hu
        ys_ref[...] = _dot(act.astype(BF16), wdb_ref[...])

    @pl.when(j >= nt_ref[0])
    def _():
        ys_ref[...] = jnp.zeros_like(ys_ref)


def _experts_sorted(xs, tile_expert, n_tiles_used, wg, wu, wd):
    n_rows, d = xs.shape
    de = wg.shape[2]
    n_tiles = n_rows // MOE_BM
    row_in = lambda j, te, nt: (jnp.minimum(j, nt[0] - 1), 0)
    wsel = lambda j, te, nt: (te[j], 0, 0)
    return pl.pallas_call(
        _experts_sorted_kernel,
        grid_spec=pltpu.PrefetchScalarGridSpec(
            num_scalar_prefetch=2, grid=(n_tiles,),
            in_specs=[pl.BlockSpec((MOE_BM, d), row_in),
                      pl.BlockSpec((1, d, de), wsel),
                      pl.BlockSpec((1, d, de), wsel),
                      pl.BlockSpec((1, de, d), wsel)],
            out_specs=pl.BlockSpec((MOE_BM, d), lambda j, te, nt: (j, 0)),
            scratch_shapes=[pltpu.VMEM((d, de), BF16), pltpu.VMEM((d, de), BF16), pltpu.VMEM((de, d), BF16)]),
        out_shape=jax.ShapeDtypeStruct((n_rows, d), F32),
        compiler_params=_cparams("arbitrary"),
        name="moe_experts_sorted",
    )(tile_expert, n_tiles_used, xs, wg, wu, wd)


def _combine_kernel(p1_ref, p2_ref, ys_hbm, x_ref, route_ref, nf_ref, y_ref, buf_ref, sem):
    i = pl.program_id(0)
    n = pl.num_programs(0)
    bm = x_ref.shape[0]

    def issue(tile, slot):
        def body(r, carry):
            t = tile * bm + r
            _row_copy(ys_hbm, p1_ref[t], buf_ref.at[slot, 0], r, sem.at[slot]).start()
            _row_copy(ys_hbm, p2_ref[t], buf_ref.at[slot, 1], r, sem.at[slot]).start()
            return carry

        lax.fori_loop(0, bm, body, 0, unroll=8)

    @pl.when(i == 0)
    def _():
        issue(0, 0)

    @pl.when(i + 1 < n)
    def _():
        issue(i + 1, (i + 1) % 2)

    slot = i % 2
    for k in range(2):
        pltpu.make_async_copy(ys_hbm.at[pl.ds(0, bm)], buf_ref.at[slot, k], sem.at[slot]).wait()
    lane = lax.broadcasted_iota(jnp.int32, (1, LANES), 1)
    route = route_ref[...]
    w1 = jnp.sum(jnp.where(lane == 2, route, 0.0), axis=1, keepdims=True)
    w2 = jnp.sum(jnp.where(lane == 3, route, 0.0), axis=1, keepdims=True)
    x3 = x_ref[...] + w1 * buf_ref[slot, 0] + w2 * buf_ref[slot, 1]
    y_ref[...] = _rms(x3, nf_ref[...])


def _combine(ys, pos1, pos2, x, route, norm_final):
    m, d = x.shape
    bm = min(MOE_BM, m)
    return pl.pallas_call(
        _combine_kernel,
        grid_spec=pltpu.PrefetchScalarGridSpec(
            num_scalar_prefetch=2, grid=(m // bm,),
            in_specs=[pl.BlockSpec(memory_space=pl.ANY),
                      pl.BlockSpec((bm, d), lambda i, p1, p2: (i, 0)),
                      pl.BlockSpec((bm, LANES), lambda i, p1, p2: (i, 0)),
                      pl.BlockSpec((1, d), lambda i, p1, p2: (0, 0))],
            out_specs=pl.BlockSpec((bm, d), lambda i, p1, p2: (i, 0)),
            scratch_shapes=[pltpu.VMEM((2, 2, bm, d), F32), pltpu.SemaphoreType.DMA((2,))]),
        out_shape=jax.ShapeDtypeStruct((m, d), F32),
        compiler_params=_cparams("arbitrary"),
        name="moe_combine",
    )(pos1, pos2, ys, x, route, norm_final.reshape(1, d))


def _moe_sorted(xa, xb, g, wh, wl, bias, wg, wu, wd, norm_final):
    ma, mb = xa.shape[0], xb.shape[0]
    route_a, rt_a, cnt_a = _router_sorted(xa, g, wh, wl, bias, jnp.zeros((1, LANES), F32))
    route_b, rt_b, counts = _router_sorted(xb, g, wh, wl, bias, cnt_a)
    cnt = counts[0, :N_EXPERTS].astype(jnp.int32)
    padded = (cnt + MOE_BM - 1) // MOE_BM * MOE_BM
    seg_end = jnp.cumsum(padded)
    seg_start = seg_end - padded
    n_tiles_max = (2 * (ma + mb) + MOE_BM - 1) // MOE_BM + N_EXPERTS
    n_rows_max = n_tiles_max * MOE_BM
    n_rows_used = seg_end[-1:]
    n_tiles_used = n_rows_used // MOE_BM
    tile_start = jnp.arange(n_tiles_max, dtype=jnp.int32) * MOE_BM
    tile_expert = jnp.sum((seg_end[None, :] <= tile_start[:, None]).astype(jnp.int32), axis=1)
    last_expert = jnp.max(jnp.where(cnt > 0, jnp.arange(N_EXPERTS, dtype=jnp.int32), 0))
    tile_expert = jnp.minimum(tile_expert, last_expert)
    pa1, pa2 = _plan(rt_a, seg_start)
    pb1, pb2 = _plan(rt_b, seg_start)
    xs = _dispatch(xa, xb, g, jnp.concatenate([pa1, pb1]), jnp.concatenate([pa2, pb2]), seg_end, cnt,
                   n_rows_used, n_rows_max)
    ys = _experts_sorted(xs, tile_expert, n_tiles_used, wg, wu, wd)
    return (_combine(ys, pa1, pa2, xa, route_a, norm_final),
            _combine(ys, pb1, pb2, xb, route_b, norm_final))


def _pad_cols(x, n):
    return jnp.pad(x, ((0, 0), (0, n - x.shape[1])))


def _block_diag_ones(n, blk):
    i = jnp.arange(n) // blk
    return (i[:, None] == i[None, :]).astype(BF16)


def kernel(x_prompt, x_sample, mem_prompt, cache_conv, state_shift, state_rwkv, cache_mem_k, cache_mem_v,
           norm_mix, w_in, conv_w, conv_b, conv_ln_g, conv_ln_b, shift_mu, w_decay_up, decay_bias, w_a_up,
           a_bias, w_g_up, k_k, k_a, r_k, lnx_g, lnx_b, w_out, norm_x, norm_mem, w_cq, w_ck, w_cv, w_co,
           norm_ffn, w_route_group, b_route_group, w_route_expert, b_route_expert, w_gate, w_up, w_down,
           norm_final):
    depth = w_in.shape[0]
    batch, seq, d = x_prompt.shape
    dec_batch = x_sample.shape[0]
    assert depth == 1
    assert x_sample.shape[1] == 1 and seq % CHUNK == 0 and seq >= CONV_K - 1
    cw = conv_w.shape[2]
    rw = w_decay_up.shape[2]
    heads = rw // HEAD
    shift_w = shift_mu.shape[1]
    in_w = w_in.shape[2]
    assert in_w == 2 * cw + shift_w and shift_w == 3 * rw + DECAY_LORA + AAA_LORA + GATE_LORA
    assert cw == rw and rw % LORA_PAD == 0
    in_pad = 2 * cw + 3 * rw + LORA_PAD
    qw = 3 * rw + LORA_PAD

    xp = x_prompt.reshape(batch * seq, d)
    xs = x_sample.reshape(dec_batch, d)
    outs = {k: [] for k in ("conv_p", "shift_p", "rwkv_p", "memk_p", "memv_p", "conv_s", "shift_s", "rwkv_s")}
    bd = _block_diag_ones(2 * LANES, HEAD)

    for l in range(depth):
        w_in_b = _pad_cols(w_in[l], in_pad)
        w_out_c = w_out[l, :cw].astype(BF16)
        w_out_r = w_out[l, cw:].astype(BF16)
        w_cq_b = w_cq[l].astype(BF16)
        w_ck_b = w_ck[l].astype(BF16)
        w_cv_b = w_cv[l].astype(BF16)
        w_co_b = w_co[l].astype(BF16)
        zeros_l = jnp.zeros((DECAY_LORA, rw), F32)
        wd_pad = jnp.concatenate([w_decay_up[l], zeros_l], axis=0).astype(BF16)
        wa_pad = jnp.concatenate([zeros_l, w_a_up[l]], axis=0).astype(BF16)
        wg_pad = jnp.pad(w_g_up[l], ((0, 2 * LANES - GATE_LORA), (0, 0))).astype(BF16)
        mu_pad = _pad_cols(shift_mu[l].reshape(1, shift_w), qw)
        vec = lambda x: x.reshape(1, rw)
        pp = (mu_pad, wd_pad, wa_pad, wg_pad, vec(decay_bias[l]), vec(a_bias[l]), vec(k_k[l]), vec(k_a[l]),
              vec(r_k[l]), bd)
        w_route = jnp.concatenate([w_route_expert[l].reshape(d, N_EXPERTS), w_route_group[l]], axis=1)
        w_route = _pad_cols(w_route, LANES)
        wr_hi = w_route.astype(BF16)
        wr_lo = (w_route - wr_hi.astype(F32)).astype(BF16)
        b_route = _pad_cols(jnp.concatenate([b_route_expert[l].reshape(1, N_EXPERTS),
                                             b_route_group[l].reshape(1, N_GROUPS)], axis=1), LANES)
        de = w_gate.shape[-1]
        wg_e = w_gate[l].reshape(N_EXPERTS, d, de)
        wu_e = w_up[l].reshape(N_EXPERTS, d, de)
        wd_e = w_down[l].reshape(N_EXPERTS, de, d)

        mem2 = mem_prompt.reshape(batch * N_MEM, d)
        mk, mk_b = _norm_matmul_heads(mem2, norm_mem[l], w_ck_b, X_HEADS, 256)
        mv, mv_b = _norm_matmul_heads(mem2, norm_mem[l], w_cv_b, X_HEADS, 256)
        proj = _norm_matmul(xp, norm_mix[l], w_in_b, 1024, 512)
        c_p, conv_new = _conv_prefill(proj, jnp.zeros((batch, CONV_K - 1, cw), F32), conv_w[l], conv_b[l],
                                      conv_ln_g[l], conv_ln_b[l], batch, seq)
        prep = _rwkv_prep_prefill(proj, jnp.zeros((batch, qw), F32), pp, batch, seq, rw)
        o_p, s_p = _rwkv_chunked(*prep[:6], jnp.zeros((batch, heads, HEAD, HEAD), F32), batch, seq)
        o_p = _rwkv_post(o_p, prep[6], prep[7], lnx_g[l], lnx_b[l], bd)
        shift_new = proj.reshape(batch, seq, in_pad)[:, -1, 2 * cw:2 * cw + shift_w]
        xp = _matmul_res([c_p, o_p], [w_out_c, w_out_r], xp, 512)
        qx = _norm_matmul(xp, norm_x[l], w_cq_b, 512, d)
        ctx = _attn_prefill(qx, mk_b, mv_b, batch, seq)
        xp = _matmul_res([ctx], [w_co_b], xp, 512)
        outs["conv_p"].append(conv_new)
        outs["shift_p"].append(shift_new)
        outs["rwkv_p"].append(s_p)
        outs["memk_p"].append(mk.reshape(batch, N_MEM, X_HEADS, d // X_HEADS))
        outs["memv_p"].append(mv.reshape(batch, N_MEM, X_HEADS, d // X_HEADS))

        proj_s = _norm_matmul(xs, norm_mix[l], w_in_b, 128, 512)
        c_s, conv_new_s = _conv_decode(proj_s, cache_conv[l], conv_w[l], conv_b[l], conv_ln_g[l],
                                       conv_ln_b[l])
        prep_s = _rwkv_prep_decode(proj_s, _pad_cols(state_shift[l], qw), pp, rw)
        o_s, s_s = _rwkv_step(*prep_s[:6], state_rwkv[l].astype(F32))
        o_s = _rwkv_post(o_s, prep_s[6], prep_s[7], lnx_g[l], lnx_b[l], bd)
        xs = _matmul_res([c_s, o_s], [w_out_c, w_out_r], xs, 128)
        qs = _norm_matmul(xs, norm_x[l], w_cq_b, 128, d)
        ctx_s = _attn_decode(qs, cache_mem_k[l], cache_mem_v[l])
        xs = _matmul_res([ctx_s], [w_co_b], xs, 128)
        outs["conv_s"].append(conv_new_s)
        outs["shift_s"].append(proj_s[:, 2 * cw:2 * cw + shift_w])
        outs["rwkv_s"].append(s_s)

        xp, xs = _moe_sorted(xp, xs, norm_ffn[l], wr_hi, wr_lo, b_route, wg_e, wu_e, wd_e, norm_final)

    y_prompt = xp.reshape(batch, seq, d)
    y_sample = xs.reshape(dec_batch, 1, d)
    st = lambda k: jnp.stack(outs[k])
    return (y_prompt, y_sample, st("conv_p"), st("shift_p"), st("rwkv_p"), st("memk_p"), st("memv_p"),
            st("conv_s"), st("shift_s"), st("rwkv_s"))
```

```python
import functools
import math

import jax
import jax.numpy as jnp
from jax import lax
from jax.experimental import pallas as pl
from jax.experimental.pallas import tpu as pltpu

F32 = jnp.float32
BF16 = jnp.bfloat16

CONV_K = 31
HEAD = 64
PAIR = 2 * HEAD
CHUNK = 64
DECAY_LORA = 64
AAA_LORA = 64
GATE_LORA = 160
LORA_PAD = 512
N_MEM = 256
X_HEADS = 4
N_GROUPS = 4
EXP_PER_GROUP = 8
N_EXPERTS = N_GROUPS * EXP_PER_GROUP
RMS_EPS = 1e-6
LN_EPS = 1e-5
GN_EPS = 64e-5
DECAY_SCALE = math.exp(-0.5)
NEG_BIG = -1e30
MOE_BM = 256
LANES = 128
VMEM_LIMIT = 56 * 1024 * 1024


def _cparams(*sem):
    return pltpu.CompilerParams(dimension_semantics=sem, vmem_limit_bytes=VMEM_LIMIT)


def _dot(a, b):
    return jnp.dot(a, b, preferred_element_type=F32)


def _dot_nt(a, b):
    return lax.dot_general(a, b, (((1,), (1,)), ((), ())), preferred_element_type=F32)


def _split_dot(x, w_bf16):
    hi = x.astype(BF16)
    lo = (x - hi.astype(F32)).astype(BF16)
    return _dot(hi, w_bf16) + _dot(lo, w_bf16)


def _rms(x, g, eps=RMS_EPS):
    return x * lax.rsqrt(jnp.mean(x * x, axis=-1, keepdims=True) + eps) * g


def _norm_mm_kernel(x_ref, g_ref, w_ref, o_ref, xn_ref, *, n_valid):
    j = pl.program_id(1)

    @pl.when(j == 0)
    def _():
        xn_ref[...] = _rms(x_ref[...], g_ref[...]).astype(BF16)

    w = w_ref[0]
    bn = w.shape[1]
    if n_valid % bn:
        col = j * bn + lax.broadcasted_iota(jnp.int32, (1, bn), 1)
        w = jnp.where(col < n_valid, w, 0.0)
    o_ref[...] = _dot(xn_ref[...], w.astype(BF16))


def _norm_matmul(x, g, w, layer, bm, bn):
    m, k = x.shape
    n = w.shape[2]
    bm = min(bm, m)
    n_tiles = pl.cdiv(n, bn)
    return pl.pallas_call(
        functools.partial(_norm_mm_kernel, n_valid=n),
        grid=(m // bm, n_tiles),
        in_specs=[pl.BlockSpec((bm, k), lambda i, j: (i, 0)),
                  pl.BlockSpec((1, k), lambda i, j: (0, 0)),
                  pl.BlockSpec((1, k, bn), lambda i, j: (layer, 0, j))],
        out_specs=pl.BlockSpec((bm, bn), lambda i, j: (i, j)),
        out_shape=jax.ShapeDtypeStruct((m, n_tiles * bn), F32),
        scratch_shapes=[pltpu.VMEM((bm, k), BF16)],
        compiler_params=_cparams("parallel", "arbitrary"),
        name="norm_matmul",
    )(x, g.reshape(1, k), w)


def _norm_mm_heads_kernel(x_ref, g_ref, w_ref, o_ref, ob_ref):
    res = _dot(_rms(x_ref[...], g_ref[...]).astype(BF16), w_ref[0].astype(BF16))
    ob_ref[...] = res.astype(BF16)
    dh = o_ref.shape[2]
    for h in range(o_ref.shape[1]):
        o_ref[:, h, :] = res[:, h * dh:(h + 1) * dh]


def _norm_matmul_heads(x, g, w, layer, n_heads, bm):
    m, k = x.shape
    n = w.shape[2]
    bm = min(bm, m)
    return pl.pallas_call(
        _norm_mm_heads_kernel,
        grid=(m // bm,),
        in_specs=[pl.BlockSpec((bm, k), lambda i: (i, 0)),
                  pl.BlockSpec((1, k), lambda i: (0, 0)),
                  pl.BlockSpec((1, k, n), lambda i: (layer, 0, 0))],
        out_specs=[pl.BlockSpec((bm, n_heads, n // n_heads), lambda i: (i, 0, 0)),
                   pl.BlockSpec((bm, n), lambda i: (i, 0))],
        out_shape=[jax.ShapeDtypeStruct((m, n_heads, n // n_heads), F32),
                   jax.ShapeDtypeStruct((m, n), BF16)],
        compiler_params=_cparams("parallel"),
        name="norm_matmul_heads",
    )(x, g.reshape(1, k), w)


def _mm_res_kernel(*refs, n_lhs):
    res_ref = refs[2 * n_lhs]
    o_ref = refs[2 * n_lhs + 1]
    acc = res_ref[...]
    for a_ref, w_ref in zip(refs[:n_lhs], refs[n_lhs:2 * n_lhs]):
        acc = acc + _dot(a_ref[...].astype(BF16), w_ref[0].astype(BF16))
    o_ref[...] = acc


def _matmul_res(lhs, w, layer, res, bm, bn):
    m, n = res.shape
    bm = min(bm, m)
    n_lhs = len(lhs)
    rk = lhs[0].shape[1]
    assert all(a.shape[1] == rk for a in lhs) and n_lhs * rk == w.shape[1]
    in_specs = [pl.BlockSpec((bm, rk), lambda i, j: (i, 0)) for _ in lhs]
    in_specs += [pl.BlockSpec((1, rk, bn), lambda i, j, rb=rb: (layer, rb, j)) for rb in range(n_lhs)]
    in_specs += [pl.BlockSpec((bm, bn), lambda i, j: (i, j))]
    return pl.pallas_call(
        functools.partial(_mm_res_kernel, n_lhs=n_lhs),
        grid=(m // bm, n // bn),
        in_specs=in_specs,
        out_specs=pl.BlockSpec((bm, bn), lambda i, j: (i, j)),
        out_shape=jax.ShapeDtypeStruct((m, n), F32),
        compiler_params=_cparams("parallel", "arbitrary"),
        name="matmul_res",
    )(*lhs, *([w] * n_lhs), res)


def _ln_silu(cf, lg, lb):
    mu = jnp.mean(cf, axis=-1, keepdims=True)
    d = cf - mu
    var = jnp.mean(d * d, axis=-1, keepdims=True)
    y = d * lax.rsqrt(var + LN_EPS) * lg + lb
    return y * jax.nn.sigmoid(y)


def _conv_prefill_kernel(a_ref, g_ref, buf_ref, w_ref, cb_ref, lg_ref, lb_ref, c_ref, nc_ref,
                         uf_ref, cv_ref, sh_ref, *, tt, halo):
    t = pl.program_id(1)
    pad = 32 - halo

    @pl.when(t == 0)
    def _():
        uf_ref[pad:32, :] = buf_ref[0]

    @pl.when(t > 0)
    def _():
        uf_ref[pad:32, :] = uf_ref[tt + pad:tt + 32, :]

    uf_ref[32:32 + tt, :] = a_ref[...] * jax.nn.sigmoid(g_ref[...])

    for sft in range(8):
        n_rows = sh_ref.shape[1] if sft < 7 else sh_ref.shape[1] - 8
        sh_ref[sft, 0:n_rows, :] = uf_ref[pad + sft:pad + sft + n_rows, :]

    width = uf_ref.shape[1]
    rb = 64
    for r0 in range(0, tt, rb):
        for l0 in range(0, width, LANES):
            acc = jnp.zeros((rb, LANES), F32)
            for j in range(CONV_K):
                base = r0 + j - j % 8
                acc = acc + sh_ref[j % 8, base:base + rb, l0:l0 + LANES] * w_ref[j:j + 1, l0:l0 + LANES]
            cv_ref[r0:r0 + rb, l0:l0 + LANES] = acc

    c_ref[...] = _ln_silu(cv_ref[...] + cb_ref[...], lg_ref[...], lb_ref[...]).astype(c_ref.dtype)

    @pl.when(t == pl.num_programs(1) - 1)
    def _():
        nc_ref[0] = uf_ref[tt + pad:tt + 32, :]


def _conv_prefill(proj, conv_buf, conv_w, conv_b, ln_g, ln_b, batch, seq):
    cw = conv_w.shape[1]
    halo = CONV_K - 1
    tt = min(256, seq)
    nt = seq // tt
    row = lambda b, t: (b * nt + t, 0)
    vec = pl.BlockSpec((1, cw), lambda b, t: (0, 0))
    return pl.pallas_call(
        functools.partial(_conv_prefill_kernel, tt=tt, halo=halo),
        grid=(batch, nt),
        in_specs=[pl.BlockSpec((tt, cw), row),
                  pl.BlockSpec((tt, cw), lambda b, t: (b * nt + t, 1)),
                  pl.BlockSpec((1, halo, cw), lambda b, t: (b, 0, 0)),
                  pl.BlockSpec((CONV_K, cw), lambda b, t: (0, 0)),
                  vec, vec, vec],
        out_specs=[pl.BlockSpec((tt, cw), row),
                   pl.BlockSpec((1, halo, cw), lambda b, t: (b, 0, 0))],
        out_shape=[jax.ShapeDtypeStruct((batch * seq, cw), BF16),
                   jax.ShapeDtypeStruct((batch, halo, cw), F32)],
        scratch_shapes=[pltpu.VMEM((tt + 32, cw), F32), pltpu.VMEM((tt, cw), F32),
                        pltpu.VMEM((8, tt + 24, cw), F32)],
        compiler_params=_cparams("parallel", "arbitrary"),
        name="conv_prefill",
    )(proj, proj, conv_buf, conv_w, conv_b.reshape(1, cw), ln_g.reshape(1, cw), ln_b.reshape(1, cw))


def _conv_decode_kernel(a_ref, g_ref, cache_ref, w_ref, cb_ref, lg_ref, lb_ref, c_ref, nc_ref):
    halo = CONV_K - 1
    u = a_ref[...] * jax.nn.sigmoid(g_ref[...])
    acc = u * w_ref[halo:halo + 1, :]
    for j in range(halo):
        acc = acc + cache_ref[0, :, j, :] * w_ref[j:j + 1, :]
    c_ref[...] = _ln_silu(acc + cb_ref[...], lg_ref[...], lb_ref[...]).astype(c_ref.dtype)
    nc_ref[:, 0:halo - 1, :] = cache_ref[0, :, 1:halo, :]
    nc_ref[:, halo - 1, :] = u


def _conv_decode(proj, cache, layer, conv_w, conv_b, ln_g, ln_b):
    _, batch, halo, cw = cache.shape
    bb = 8
    vec = pl.BlockSpec((1, cw), lambda i: (0, 0))
    return pl.pallas_call(
        _conv_decode_kernel,
        grid=(batch // bb,),
        in_specs=[pl.BlockSpec((bb, cw), lambda i: (i, 0)),
                  pl.BlockSpec((bb, cw), lambda i: (i, 1)),
                  pl.BlockSpec((1, bb, halo, cw), lambda i: (layer, i, 0, 0)),
                  pl.BlockSpec((CONV_K, cw), lambda i: (0, 0)),
                  vec, vec, vec],
        out_specs=[pl.BlockSpec((bb, cw), lambda i: (i, 0)),
                   pl.BlockSpec((bb, halo, cw), lambda i: (i, 0, 0))],
        out_shape=[jax.ShapeDtypeStruct((batch, cw), BF16),
                   jax.ShapeDtypeStruct((batch, halo, cw), F32)],
        compiler_params=_cparams("parallel"),
        name="conv_decode",
    )(proj, proj, cache, conv_w, conv_b.reshape(1, cw), ln_g.reshape(1, cw), ln_b.reshape(1, cw))


def _head_sum(x, bd_ref):
    blk = bd_ref.shape[0]
    parts = [_split_dot(x[:, l0:l0 + blk], bd_ref[...]) for l0 in range(0, x.shape[1], blk)]
    return jnp.concatenate(parts, axis=1)


def _prep_math(q, qp, mu_ref, wd_ref, wa_ref, wg_ref, db_ref, ab_ref, kk_ref, ka_ref, rk_ref, bd_ref):
    rw = q[0].shape[1]
    offs = (0, rw, 2 * rw, 3 * rw)
    r, k, v, lo = [x + (xp - x) * mu_ref[:, o:o + x.shape[1]] for x, xp, o in zip(q, qp, offs)]
    pwa = lo[:, 0:LANES]
    pg = lo[:, LANES:3 * LANES]
    dec_in = _dot(jnp.tanh(pwa).astype(BF16), wd_ref[...])
    a_in = _dot(pwa.astype(BF16), wa_ref[...])
    gate = _dot(jax.nn.sigmoid(pg).astype(BF16), wg_ref[...])
    logw = -DECAY_SCALE * jax.nn.sigmoid(db_ref[...] + dec_in)
    a = jax.nn.sigmoid(ab_ref[...] + a_in)
    kk = k * kk_ref[...]
    kk = kk / jnp.maximum(jnp.sqrt(_head_sum(kk * kk, bd_ref)), 1e-12)
    k2 = k * (1.0 + (a - 1.0) * ka_ref[...])
    bonus = _head_sum(r * k2 * rk_ref[...], bd_ref) * v
    return r, logw, k2, v, kk, a, bonus, gate


def _prep_prefill_kernel(r_ref, k_ref, v_ref, lo_ref, sb_ref, mu_ref, wd_ref, wa_ref, wg_ref, db_ref,
                         ab_ref, kk_ref, ka_ref, rk_ref, bd_ref, *rest):
    outs = rest[:8]
    carry_ref = rest[8]
    t = pl.program_id(1)

    @pl.when(t == 0)
    def _():
        carry_ref[0:1, :] = sb_ref[0]

    q = [r_ref[...], k_ref[...], v_ref[...], lo_ref[...]]
    tt = q[0].shape[0]
    first = lax.broadcasted_iota(jnp.int32, (tt, 1), 0) == 0
    qp = []
    off = 0
    for x in q:
        w = x.shape[1]
        qp.append(jnp.where(first, carry_ref[0:1, off:off + w], pltpu.roll(x, 1, 0)))
        off += w
    off = 0
    for x in q:
        w = x.shape[1]
        carry_ref[0:1, off:off + w] = x[tt - 1:tt, :]
        off += w
    res = _prep_math(q, qp, mu_ref, wd_ref, wa_ref, wg_ref, db_ref, ab_ref, kk_ref, ka_ref, rk_ref, bd_ref)
    for o_ref, val in zip(outs, res):
        o_ref[...] = val


def _prep_decode_kernel(r_ref, k_ref, v_ref, lo_ref, rp_ref, kp_ref, vp_ref, lop_ref, mu_ref, wd_ref,
                        wa_ref, wg_ref, db_ref, ab_ref, kk_ref, ka_ref, rk_ref, bd_ref, *outs):
    q = [r_ref[...], k_ref[...], v_ref[...], lo_ref[...]]
    qp = [rp_ref[...], kp_ref[...], vp_ref[...], lop_ref[...]]
    res = _prep_math(q, qp, mu_ref, wd_ref, wa_ref, wg_ref, db_ref, ab_ref, kk_ref, ka_ref, rk_ref, bd_ref)
    for o_ref, val in zip(outs, res):
        o_ref[...] = val


def _prep_param_specs(rw, idx):
    full = lambda shape: pl.BlockSpec(shape, idx)
    vec = full((1, rw))
    return [full((1, 3 * rw + LORA_PAD)), full((LANES, rw)), full((LANES, rw)), full((2 * LANES, rw)),
            vec, vec, vec, vec, vec, full((2 * LANES, 2 * LANES))]


def _rwkv_prep_prefill(proj, shift_buf, pp, batch, seq, rw):
    tt = min(256, seq)
    nt = seq // tt
    lora_blk = (2 * rw + 3 * rw) // LORA_PAD
    col = lambda c: (lambda b, t: (b * nt + t, c))
    qw = 3 * rw + LORA_PAD
    in_specs = [pl.BlockSpec((tt, rw), col(2)), pl.BlockSpec((tt, rw), col(3)),
                pl.BlockSpec((tt, rw), col(4)), pl.BlockSpec((tt, LORA_PAD), col(lora_blk)),
                pl.BlockSpec((1, 1, qw), lambda b, t: (b, 0, 0))]
    in_specs += _prep_param_specs(rw, lambda b, t: (0, 0))
    out_spec = pl.BlockSpec((tt, rw), col(0))
    return pl.pallas_call(
        _prep_prefill_kernel,
        grid=(batch, nt),
        in_specs=in_specs,
        out_specs=[out_spec] * 8,
        out_shape=[jax.ShapeDtypeStruct((batch * seq, rw), F32)] * 8,
        scratch_shapes=[pltpu.VMEM((8, qw), F32)],
        compiler_params=_cparams("parallel", "arbitrary"),
        name="rwkv_prep_prefill",
    )(proj, proj, proj, proj, shift_buf.reshape(batch, 1, qw), *pp)


def _rwkv_prep_decode(proj, shift_state, pp, rw):
    batch = proj.shape[0]
    bb = min(128, batch)
    lora_blk = (2 * rw + 3 * rw) // LORA_PAD
    col = lambda c: (lambda i: (i, c))
    in_specs = [pl.BlockSpec((bb, rw), col(2)), pl.BlockSpec((bb, rw), col(3)),
                pl.BlockSpec((bb, rw), col(4)), pl.BlockSpec((bb, LORA_PAD), col(lora_blk)),
                pl.BlockSpec((bb, rw), col(0)), pl.BlockSpec((bb, rw), col(1)),
                pl.BlockSpec((bb, rw), col(2)), pl.BlockSpec((bb, LORA_PAD), col(3 * rw // LORA_PAD))]
    in_specs += _prep_param_specs(rw, lambda i: (0, 0))
    return pl.pallas_call(
        _prep_decode_kernel,
        grid=(batch // bb,),
        in_specs=in_specs,
        out_specs=[pl.BlockSpec((bb, rw), col(0))] * 8,
        out_shape=[jax.ShapeDtypeStruct((batch, rw), F32)] * 8,
        compiler_params=_cparams("parallel"),
        name="rwkv_prep_decode",
    )(proj, proj, proj, proj, shift_state, shift_state, shift_state, shift_state, *pp)


def _stack2(x, smask):
    return jnp.where(smask, jnp.concatenate([x, x], axis=0), 0.0)


def _rwkv_chunk_kernel(r_ref, lw_ref, k_ref, v_ref, kk_ref, a_ref, s0_ref, o_ref, so_ref, s_ref):
    c = pl.program_id(1)
    cs = r_ref.shape[0]
    n_pairs = r_ref.shape[1] // PAIR
    two = 2 * cs

    @pl.when(c == 0)
    def _():
        z = jnp.zeros((HEAD, HEAD), F32)
        for p in range(n_pairs):
            top = jnp.concatenate([s0_ref[0, 2 * p], z], axis=1)
            bot = jnp.concatenate([z, s0_ref[0, 2 * p + 1]], axis=1)
            s_ref[p] = jnp.concatenate([top, bot], axis=0)

    ri = lax.broadcasted_iota(jnp.int32, (two, two), 0)
    ci = lax.broadcasted_iota(jnp.int32, (two, two), 1)
    strict = ci < ri
    incl = ci <= ri
    eye = (ci == ri).astype(F32)
    smask = (lax.broadcasted_iota(jnp.int32, (two, PAIR), 0) < cs) == (
        lax.broadcasted_iota(jnp.int32, (two, PAIR), 1) < HEAD)
    tri = (lax.broadcasted_iota(jnp.int32, (cs, cs), 1)
           <= lax.broadcasted_iota(jnp.int32, (cs, cs), 0)).astype(BF16)

    lw_all = lw_ref[...]
    lw_hi = lw_all.astype(BF16)
    lw_lo = (lw_all - lw_hi.astype(F32)).astype(BF16)
    cum_all = _dot(tri, lw_hi) + _dot(tri, lw_lo)

    pairs = range(n_pairs)
    cat = jnp.concatenate
    prep = []
    for p in pairs:
        sl = slice(p * PAIR, (p + 1) * PAIR)
        lw = lw_all[:, sl]
        cum = cum_all[:, sl]
        tot = cum[cs - 1:cs, :]
        g_inv = jnp.exp(-cum)
        g_end = jnp.exp(tot - cum)
        kk = kk_ref[:, sl]
        k2 = k_ref[:, sl]
        bb = kk * a_ref[:, sl]
        prep.append(dict(
            g_tot=jnp.exp(tot),
            a_b=_stack2(kk * jnp.exp(cum - lw), smask).astype(BF16),
            r_s=_stack2(r_ref[:, sl] * jnp.exp(cum), smask),
            bk=cat([_stack2(bb * g_inv, smask), _stack2(k2 * g_inv, smask)], axis=0).astype(BF16),
            v_s=_stack2(v_ref[:, sl], smask),
            bg_s=_stack2(bb * g_end, smask).astype(BF16),
            kg_s=_stack2(k2 * g_end, smask).astype(BF16)))
    a_b = [q["a_b"] for q in prep]
    r_s = [q["r_s"] for q in prep]
    v_s = [q["v_s"] for q in prep]
    v_b = [x.astype(BF16) for x in v_s]

    gram = [_dot_nt(cat([a_b[p], r_s[p].astype(BF16)], axis=0), prep[p]["bk"]) for p in pairs]
    l_ab = [jnp.where(strict, g[0:two, 0:two], 0.0) for g in gram]
    l_ak = [jnp.where(strict, g[0:two, two:], 0.0).astype(BF16) for g in gram]
    m_rb = [jnp.where(incl, g[two:, 0:two], 0.0).astype(BF16) for g in gram]
    m_rk = [jnp.where(incl, g[two:, two:], 0.0).astype(BF16) for g in gram]

    tm = [eye - x for x in l_ab]
    pw = l_ab
    n = 1
    while 2 * n < cs:
        pw = [_dot(x.astype(BF16), x.astype(BF16)) for x in pw]
        tm = [t + _dot(t.astype(BF16), x.astype(BF16)) for t, x in zip(tm, pw)]
        n *= 2
    tm_b = [t.astype(BF16) for t in tm]

    w1 = [_dot(l_ak[p], v_b[p]) for p in pairs]
    ua = [_dot(tm_b[p], cat([w1[p].astype(BF16), a_b[p]], axis=1)) for p in pairs]
    ua_b = [x.astype(BF16) for x in ua]
    mrb_ua = [_dot(m_rb[p], ua_b[p]) for p in pairs]
    o0 = [_dot(m_rk[p], v_b[p]) - mrb_ua[p][:, 0:PAIR] for p in pairs]
    rt = [(r_s[p] - mrb_ua[p][:, PAIR:]).astype(BF16) for p in pairs]

    s_old = [s_ref[p] for p in pairs]
    s_b = [x.astype(BF16) for x in s_old]
    ua_t = [cat([x[:, 0:PAIR].T, x[:, PAIR:].T], axis=0).astype(BF16) for x in ua]
    uat_bg = [_dot(ua_t[p], prep[p]["bg_s"]) for p in pairs]
    vt_kg = [_dot(v_s[p].T.astype(BF16), prep[p]["kg_s"]) for p in pairs]
    s_new = [s_old[p] * prep[p]["g_tot"] - _dot(s_b[p], uat_bg[p][PAIR:, :].astype(BF16))
             + vt_kg[p] - uat_bg[p][0:PAIR, :] for p in pairs]
    o_st = [o0[p] + _dot_nt(rt[p], s_b[p]) for p in pairs]

    o_ref[...] = cat([x[0:cs, :] + x[cs:two, :] for x in o_st], axis=1)
    s_ref[...] = jnp.stack(s_new, axis=0)

    @pl.when(c == pl.num_programs(1) - 1)
    def _():
        for p in range(n_pairs):
            s = s_ref[p]
            so_ref[0, 2 * p] = s[0:HEAD, 0:HEAD]
            so_ref[0, 2 * p + 1] = s[HEAD:PAIR, HEAD:PAIR]


def _rwkv_chunked(r, logw, k2, v, kk, a, s0, batch, seq):
    rw = r.shape[1]
    nc = seq // CHUNK
    heads = rw // HEAD
    row = pl.BlockSpec((CHUNK, rw), lambda b, c: (b * nc + c, 0))
    st = pl.BlockSpec((1, heads, HEAD, HEAD), lambda b, c: (b, 0, 0, 0))
    return pl.pallas_call(
        _rwkv_chunk_kernel,
        grid=(batch, nc),
        in_specs=[row] * 6 + [st],
        out_specs=[row, st],
        out_shape=[jax.ShapeDtypeStruct((batch * seq, rw), F32),
                   jax.ShapeDtypeStruct((batch, heads, HEAD, HEAD), F32)],
        scratch_shapes=[pltpu.VMEM((rw // PAIR, PAIR, PAIR), F32)],
        compiler_params=_cparams("parallel", "arbitrary"),
        name="rwkv_chunked",
    )(r, logw, k2, v, kk, a, s0)


def _rwkv_step_kernel(r_ref, lw_ref, k_ref, v_ref, kk_ref, a_ref, s_ref, o_ref, so_ref, *, heads):
    bb = s_ref.shape[1]
    eye = (lax.broadcasted_iota(jnp.int32, (HEAD, HEAD), 0)
           == lax.broadcasted_iota(jnp.int32, (HEAD, HEAD), 1))

    def body(bi, carry):
        hs = range(heads)
        rows = [pl.ds(bi * heads + h, 1) for h in hs]
        kk = [kk_ref[r, :] for r in rows]
        s = [s_ref[0, bi, h].astype(F32) for h in hs]
        sa = [jnp.sum(s[h] * kk[h], axis=1, keepdims=True) for h in hs]
        v_col = [jnp.sum(jnp.where(eye, v_ref[rows[h], :], 0.0), axis=1, keepdims=True) for h in hs]
        s_new = [s[h] * jnp.exp(lw_ref[rows[h], :]) - sa[h] * (kk[h] * a_ref[rows[h], :])
                 + v_col[h] * k_ref[rows[h], :] for h in hs]
        for h in hs:
            so_ref[bi, h] = s_new[h]
        o_col = [jnp.sum(s_new[h] * r_ref[rows[h], :], axis=1, keepdims=True) for h in hs]
        o_row = [jnp.sum(jnp.where(eye, o_col[h], 0.0), axis=0, keepdims=True) for h in hs]
        o_ref[pl.ds(pl.multiple_of(bi * heads, heads), heads), :] = jnp.concatenate(o_row, axis=0)
        return carry

    lax.fori_loop(0, bb, body, 0)


def _rwkv_step(r, logw, k2, v, kk, a, state, layer):
    batch, rw = r.shape
    heads = rw // HEAD
    bb = 8
    flat = lambda x: x.reshape(batch * heads, HEAD)
    row = pl.BlockSpec((bb * heads, HEAD), lambda i: (i, 0))
    st_in = pl.BlockSpec((1, bb, heads, HEAD, HEAD), lambda i: (layer, i, 0, 0, 0))
    st_out = pl.BlockSpec((bb, heads, HEAD, HEAD), lambda i: (i, 0, 0, 0))
    o, s_new = pl.pallas_call(
        functools.partial(_rwkv_step_kernel, heads=heads),
        grid=(batch // bb,),
        in_specs=[row] * 6 + [st_in],
        out_specs=[row, st_out],
        out_shape=[jax.ShapeDtypeStruct((batch * heads, HEAD), F32),
                   jax.ShapeDtypeStruct(state.shape[1:], F32)],
        compiler_params=_cparams("parallel"),
        name="rwkv_step",
    )(flat(r), flat(logw), flat(k2), flat(v), flat(kk), flat(a), state)
    return o.reshape(batch, rw), s_new


def _rwkv_post_kernel(o_ref, bonus_ref, gate_ref, lg_ref, lb_ref, bd_ref, y_ref):
    o = o_ref[...]
    mu = _head_sum(o, bd_ref) * (1.0 / HEAD)
    d = o - mu
    var = _head_sum(d * d, bd_ref) * (1.0 / HEAD)
    y = d * lax.rsqrt(var + GN_EPS) * lg_ref[...] + lb_ref[...]
    y_ref[...] = ((y + bonus_ref[...]) * gate_ref[...]).astype(y_ref.dtype)


def _rwkv_post(o, bonus, gate, lnx_g, lnx_b, bd):
    m, rw = o.shape
    bm = min(512, m)
    row = pl.BlockSpec((bm, rw), lambda i: (i, 0))
    vec = pl.BlockSpec((1, rw), lambda i: (0, 0))
    return pl.pallas_call(
        _rwkv_post_kernel,
        grid=(m // bm,),
        in_specs=[row, row, row, vec, vec, pl.BlockSpec(bd.shape, lambda i: (0, 0))],
        out_specs=row,
        out_shape=jax.ShapeDtypeStruct((m, rw), BF16),
        compiler_params=_cparams("parallel"),
        name="rwkv_post",
    )(o, bonus, gate, lnx_g.reshape(1, rw), lnx_b.reshape(1, rw), bd)


def _attn_prefill_kernel(q_ref, k_ref, v_ref, o_ref, *, n_heads):
    d = q_ref.shape[1] // n_heads
    scale = d ** -0.5
    for h in range(n_heads):
        sl = slice(h * d, (h + 1) * d)
        s = _dot_nt(q_ref[:, sl].astype(BF16), k_ref[:, sl]) * scale
        p = jnp.exp(s - jnp.max(s, axis=-1, keepdims=True))
        att = p / jnp.sum(p, axis=-1, keepdims=True)
        o_ref[:, sl] = _dot(att.astype(BF16), v_ref[:, sl]).astype(o_ref.dtype)


def _attn_prefill(q, mem_k, mem_v, batch, seq):
    d = q.shape[1]
    tt = min(512, seq)
    nt = seq // tt
    kv = pl.BlockSpec((N_MEM, d), lambda b, t: (b, 0))
    row = pl.BlockSpec((tt, d), lambda b, t: (b * nt + t, 0))
    return pl.pallas_call(
        functools.partial(_attn_prefill_kernel, n_heads=X_HEADS),
        grid=(batch, nt),
        in_specs=[row, kv, kv],
        out_specs=row,
        out_shape=jax.ShapeDtypeStruct((batch * seq, d), BF16),
        compiler_params=_cparams("parallel", "arbitrary"),
        name="attn_prefill",
    )(q, mem_k, mem_v)


def _attn_decode_kernel(q_ref, k_ref, v_ref, o_ref, *, n_heads):
    bb = q_ref.shape[0]
    d = q_ref.shape[2] // n_heads
    scale = d ** -0.5
    for bi in range(bb):
        q = q_ref[bi]
        for h in range(n_heads):
            sl = slice(h * d, (h + 1) * d)
            s = jnp.sum(k_ref[bi, :, h, :] * q[:, sl], axis=1, keepdims=True) * scale
            p = jnp.exp(s - jnp.max(s, axis=0, keepdims=True))
            att = p / jnp.sum(p, axis=0, keepdims=True)
            o_ref[bi, :, sl] = jnp.sum(att * v_ref[bi, :, h, :], axis=0, keepdims=True)


def _attn_decode(q, cache_k, cache_v):
    batch, d = q.shape
    bb = 2
    kv = pl.BlockSpec((bb, N_MEM, X_HEADS, d // X_HEADS), lambda i: (i, 0, 0, 0))
    row = pl.BlockSpec((bb, 1, d), lambda i: (i, 0, 0))
    out = pl.pallas_call(
        functools.partial(_attn_decode_kernel, n_heads=X_HEADS),
        grid=(batch // bb,),
        in_specs=[row, kv, kv],
        out_specs=row,
        out_shape=jax.ShapeDtypeStruct((batch, 1, d), F32),
        compiler_params=_cparams("parallel"),
        name="attn_decode",
    )(q.reshape(batch, 1, d), cache_k, cache_v)
    return out.reshape(batch, d)


def _route(x_ref, g_ref, wh_ref, wl_ref, b_ref):
    h = _rms(x_ref[...], g_ref[...])
    hh = h.astype(BF16)
    hl = (h - hh.astype(F32)).astype(BF16)
    logits = _dot(hh, wh_ref[...]) + _dot(hl, wh_ref[...]) + _dot(hh, wl_ref[...]) + b_ref[...]
    lane = lax.broadcasted_iota(jnp.int32, (1, LANES), 1).astype(F32)
    is_g = (lane >= N_EXPERTS) & (lane < N_EXPERTS + N_GROUPS)
    lgm = jnp.where(is_g, logits, NEG_BIG)
    gmax = jnp.max(lgm, axis=1, keepdims=True)
    gsum = jnp.sum(jnp.where(is_g, jnp.exp(lgm - gmax), 0.0), axis=1, keepdims=True)
    g_val = 1.0 / gsum
    g_idx = jnp.min(jnp.where(is_g & (lgm == gmax), lane - N_EXPERTS, 1e9), axis=1, keepdims=True)
    in_grp = (lane < N_EXPERTS) & (jnp.floor(lane * (1.0 / EXP_PER_GROUP)) == g_idx)
    le = jnp.where(in_grp, logits, NEG_BIG)
    m1 = jnp.max(le, axis=1, keepdims=True)
    i1 = jnp.min(jnp.where(in_grp & (le == m1), lane, 1e9), axis=1, keepdims=True)
    rest = in_grp & (lane != i1)
    le2 = jnp.where(rest, logits, NEG_BIG)
    m2 = jnp.max(le2, axis=1, keepdims=True)
    i2 = jnp.min(jnp.where(rest & (le2 == m2), lane, 1e9), axis=1, keepdims=True)
    e2 = jnp.exp(m2 - m1)
    den = 1.0 + e2
    w1 = (1.0 / den) * g_val
    w2 = (e2 / den) * g_val
    return h, lane, i1, i2, w1, w2


def _router_sorted_kernel(x_ref, g_ref, wh_ref, wl_ref, b_ref, init_ref, route_ref, rt_ref, cnt_ref, run_ref):
    @pl.when(pl.program_id(0) == 0)
    def _():
        run_ref[0:1, :] = init_ref[...]

    h, lane, i1, i2, w1, w2 = _route(x_ref, g_ref, wh_ref, wl_ref, b_ref)
    bm = h.shape[0]
    oh1 = lane == i1
    oh2 = lane == i2
    sel = (oh1 | oh2).astype(BF16)
    before = (lax.broadcasted_iota(jnp.int32, (bm, bm), 1)
              < lax.broadcasted_iota(jnp.int32, (bm, bm), 0)).astype(BF16)
    base = run_ref[0:1, :] + _dot(before, sel)
    rank1 = jnp.sum(jnp.where(oh1, base, 0.0), axis=1, keepdims=True)
    rank2 = jnp.sum(jnp.where(oh2, base, 0.0), axis=1, keepdims=True)
    total = run_ref[0:1, :] + jnp.sum(sel.astype(F32), axis=0, keepdims=True)
    run_ref[0:1, :] = total
    cnt_ref[...] = total
    route = jnp.zeros((bm, LANES), F32)
    for idx, val in enumerate((i1, i2, w1, w2, rank1, rank2)):
        route = jnp.where(lane == idx, val, route)
    route_ref[...] = route
    for r0 in range(0, bm, LANES):
        rt_ref[:, r0:r0 + LANES] = route[r0:r0 + LANES, :].T[0:8, :]


def _router_sorted(x, g, wh, wl, bias, init_counts):
    m, d = x.shape
    bm = min(512, m)
    return pl.pallas_call(
        _router_sorted_kernel,
        grid=(m // bm,),
        in_specs=[pl.BlockSpec((bm, d), lambda i: (i, 0)),
                  pl.BlockSpec((1, d), lambda i: (0, 0)),
                  pl.BlockSpec((d, LANES), lambda i: (0, 0)),
                  pl.BlockSpec((d, LANES), lambda i: (0, 0)),
                  pl.BlockSpec((1, LANES), lambda i: (0, 0)),
                  pl.BlockSpec((1, LANES), lambda i: (0, 0))],
        out_specs=[pl.BlockSpec((bm, LANES), lambda i: (i, 0)),
                   pl.BlockSpec((8, bm), lambda i: (0, i)),
                   pl.BlockSpec((1, LANES), lambda i: (0, 0))],
        out_shape=[jax.ShapeDtypeStruct((m, LANES), F32),
                   jax.ShapeDtypeStruct((8, m), F32),
                   jax.ShapeDtypeStruct((1, LANES), F32)],
        scratch_shapes=[pltpu.VMEM((8, LANES), F32)],
        compiler_params=_cparams("arbitrary"),
        name="moe_router_sorted",
    )(x, g.reshape(1, d), wh, wl, bias, init_counts)


def _plan_kernel(seg_ref, rt_ref, pos_ref):
    rt = rt_ref[...]
    rows = []
    for e_row, r_row in ((0, 4), (1, 5)):
        e = rt[e_row:e_row + 1, :]
        start = jnp.zeros_like(e)
        for k in range(N_EXPERTS):
            start = jnp.where(e == k, seg_ref[k].astype(F32), start)
        rows.append((start + rt[r_row:r_row + 1, :]).astype(jnp.int32))
    pos_ref[...] = jnp.concatenate(rows + [jnp.zeros((6, rt.shape[1]), jnp.int32)], axis=0)


def _plan(route_t, seg_start):
    m = route_t.shape[1]
    bt = min(2048, m)
    pos = pl.pallas_call(
        _plan_kernel,
        grid_spec=pltpu.PrefetchScalarGridSpec(
            num_scalar_prefetch=1, grid=(m // bt,),
            in_specs=[pl.BlockSpec((8, bt), lambda i, seg: (0, i))],
            out_specs=pl.BlockSpec((8, bt), lambda i, seg: (0, i))),
        out_shape=jax.ShapeDtypeStruct((8, m), jnp.int32),
        compiler_params=_cparams("arbitrary"),
        name="moe_plan",
    )(seg_start, route_t)
    return pos[0], pos[1]


def _row_copy(src_hbm, src_row, dst, dst_row, sem):
    return pltpu.make_async_copy(src_hbm.at[pl.ds(src_row, 1)], dst.at[pl.ds(dst_row, 1)], sem)


def _dispatch_kernel(p1_ref, p2_ref, seg_ref, cnt_ref, nrow_ref, xa_ref, xb_ref, g_ref, xs_hbm, h_ref,
                     zero_ref, sem, zsem, *, n_a):
    i = pl.program_id(0)
    n = pl.num_programs(0)
    bm = xa_ref.shape[0]
    n_max = xs_hbm.shape[0] // MOE_BM

    def zero_copy(row0):
        return pltpu.make_async_copy(zero_ref, xs_hbm.at[pl.ds(pl.multiple_of(row0, MOE_BM), MOE_BM)], zsem)

    @pl.when(i == 0)
    def _():
        zero_ref[...] = jnp.zeros_like(zero_ref)
        first_free = nrow_ref[0] // MOE_BM

        def tail_start(c, carry):
            zero_copy(c * MOE_BM).start()
            return carry

        def tail_wait(c, carry):
            zero_copy(c * MOE_BM).wait()
            return carry

        for e in range(N_EXPERTS):
            @pl.when(cnt_ref[e] > 0)
            def _():
                zero_copy(seg_ref[e] - MOE_BM).start()

        lax.fori_loop(first_free, n_max, tail_start, 0)
        for e in range(N_EXPERTS):
            @pl.when(cnt_ref[e] > 0)
            def _():
                zero_copy(seg_ref[e] - MOE_BM).wait()

        lax.fori_loop(first_free, n_max, tail_wait, 0)

    def wait_rows(slot):
        for _ in range(2):
            pltpu.make_async_copy(h_ref.at[slot], xs_hbm.at[pl.ds(0, bm)], sem.at[slot]).wait()

    slot = i % 2

    @pl.when(i > 0)
    def _():
        wait_rows(1 - slot)

    @pl.when(i < n_a)
    def _():
        h_ref[slot] = _rms(xa_ref[...], g_ref[...])

    @pl.when(i >= n_a)
    def _():
        h_ref[slot] = _rms(xb_ref[...], g_ref[...])

    def body(r, carry):
        t = i * bm + r
        src = h_ref.at[slot, pl.ds(r, 1)]
        pltpu.make_async_copy(src, xs_hbm.at[pl.ds(p1_ref[t], 1)], sem.at[slot]).start()
        pltpu.make_async_copy(src, xs_hbm.at[pl.ds(p2_ref[t], 1)], sem.at[slot]).start()
        return carry

    lax.fori_loop(0, bm, body, 0, unroll=8)

    @pl.when(i == n - 1)
    def _():
        wait_rows(slot)


def _dispatch(xa, xb, g, pos1, pos2, seg_end, cnt, n_rows_used, n_rows_max):
    (ma, d), mb = xa.shape, xb.shape[0]
    bm = min(LANES, ma, mb)
    assert ma % bm == 0 and mb % bm == 0
    n_a, n_b = ma // bm, mb // bm
    return pl.pallas_call(
        functools.partial(_dispatch_kernel, n_a=n_a),
        grid_spec=pltpu.PrefetchScalarGridSpec(
            num_scalar_prefetch=5, grid=(n_a + n_b,),
            in_specs=[pl.BlockSpec((bm, d), lambda i, *_: (jnp.minimum(i, n_a - 1), 0)),
                      pl.BlockSpec((bm, d), lambda i, *_: (jnp.maximum(i - n_a, 0), 0)),
                      pl.BlockSpec((1, d), lambda i, *_: (0, 0))],
            out_specs=pl.BlockSpec(memory_space=pl.ANY),
            scratch_shapes=[pltpu.VMEM((2, bm, d), F32), pltpu.VMEM((MOE_BM, d), F32),
                            pltpu.SemaphoreType.DMA((2,)), pltpu.SemaphoreType.DMA(())]),
        out_shape=jax.ShapeDtypeStruct((n_rows_max, d), F32),
        compiler_params=_cparams("arbitrary"),
        name="moe_dispatch",
    )(pos1, pos2, seg_end, cnt, n_rows_used, xa, xb, g.reshape(1, d))


def _experts_sorted_kernel(te_ref, nt_ref, nxt_ref, par_ref, xs_ref, wg_hbm, wu_hbm, wd_hbm, ys_ref,
                           wgf_ref, wuf_ref, wdf_ref, wgb_ref, wub_ref, wdb_ref, sem):
    j = pl.program_id(0)
    prev = te_ref[jnp.maximum(j, 1) - 1]

    def copies(e, slot):
        return [pltpu.make_async_copy(hbm.at[e], buf.at[slot], sem.at[slot])
                for hbm, buf in ((wg_hbm, wgf_ref), (wu_hbm, wuf_ref), (wd_hbm, wdf_ref))]

    @pl.when(j == 0)
    def _():
        for c in copies(te_ref[0], par_ref[0]):
            c.start()

    @pl.when((j < nt_ref[0]) & ((j == 0) | (te_ref[j] != prev)))
    def _():
        slot = par_ref[j]
        for c in copies(te_ref[j], slot):
            c.wait()

        @pl.when(nxt_ref[j] >= 0)
        def _():
            for c in copies(nxt_ref[j], 1 - slot):
                c.start()

        wgb_ref[...] = wgf_ref[slot].astype(BF16)
        wub_ref[...] = wuf_ref[slot].astype(BF16)
        wdb_ref[...] = wdf_ref[slot].astype(BF16)

    @pl.when(j < nt_ref[0])
    def _():
        x = xs_ref[...].astype(BF16)
        hg = _dot(x, wgb_ref[...])
        hu = _dot(x, wub_ref[...])
        act = hg * jax.nn.sigmoid(hg) * hu
        ys_ref[...] = _dot(act.astype(BF16), wdb_ref[...])

    @pl.when(j >= nt_ref[0])
    def _():
        ys_ref[...] = jnp.zeros_like(ys_ref)


def _experts_sorted(xs, tile_expert, n_tiles_used, next_expert, slot_parity, wg, wu, wd):
    n_rows, d = xs.shape
    de = wg.shape[2]
    n_tiles = n_rows // MOE_BM
    row_in = lambda j, te, nt, *_: (jnp.minimum(j, nt[0] - 1), 0)
    hbm = pl.BlockSpec(memory_space=pl.ANY)
    return pl.pallas_call(
        _experts_sorted_kernel,
        grid_spec=pltpu.PrefetchScalarGridSpec(
            num_scalar_prefetch=4, grid=(n_tiles,),
            in_specs=[pl.BlockSpec((MOE_BM, d), row_in), hbm, hbm, hbm],
            out_specs=pl.BlockSpec((MOE_BM, d), lambda j, *_: (j, 0)),
            scratch_shapes=[pltpu.VMEM((2, d, de), F32), pltpu.VMEM((2, d, de), F32),
                            pltpu.VMEM((2, de, d), F32),
                            pltpu.VMEM((d, de), BF16), pltpu.VMEM((d, de), BF16), pltpu.VMEM((de, d), BF16),
                            pltpu.SemaphoreType.DMA((2,))]),
        out_shape=jax.ShapeDtypeStruct((n_rows, d), F32),
        compiler_params=_cparams("arbitrary"),
        name="moe_experts_sorted",
    )(tile_expert, n_tiles_used, next_expert, slot_parity, xs, wg, wu, wd)


def _combine_kernel(p1_ref, p2_ref, ys_hbm, x_ref, route_ref, nf_ref, y_ref, buf_ref, sem):
    i = pl.program_id(0)
    n = pl.num_programs(0)
    bm = x_ref.shape[0]

    def issue(tile, slot):
        def body(r, carry):
            t = tile * bm + r
            _row_copy(ys_hbm, p1_ref[t], buf_ref.at[slot, 0], r, sem.at[slot]).start()
            _row_copy(ys_hbm, p2_ref[t], buf_ref.at[slot, 1], r, sem.at[slot]).start()
            return carry

        lax.fori_loop(0, bm, body, 0, unroll=8)

    @pl.when(i == 0)
    def _():
        issue(0, 0)

    @pl.when(i + 1 < n)
    def _():
        issue(i + 1, (i + 1) % 2)

    slot = i % 2
    for k in range(2):
        pltpu.make_async_copy(ys_hbm.at[pl.ds(0, bm)], buf_ref.at[slot, k], sem.at[slot]).wait()
    lane = lax.broadcasted_iota(jnp.int32, (1, LANES), 1)
    route = route_ref[...]
    w1 = jnp.sum(jnp.where(lane == 2, route, 0.0), axis=1, keepdims=True)
    w2 = jnp.sum(jnp.where(lane == 3, route, 0.0), axis=1, keepdims=True)
    x3 = x_ref[...] + w1 * buf_ref[slot, 0] + w2 * buf_ref[slot, 1]
    y_ref[...] = _rms(x3, nf_ref[...])


def _combine(ys, pos1, pos2, x, route, norm_final):
    m, d = x.shape
    bm = min(MOE_BM, m)
    return pl.pallas_call(
        _combine_kernel,
        grid_spec=pltpu.PrefetchScalarGridSpec(
            num_scalar_prefetch=2, grid=(m // bm,),
            in_specs=[pl.BlockSpec(memory_space=pl.ANY),
                      pl.BlockSpec((bm, d), lambda i, p1, p2: (i, 0)),
                      pl.BlockSpec((bm, LANES), lambda i, p1, p2: (i, 0)),
                      pl.BlockSpec((1, d), lambda i, p1, p2: (0, 0))],
            out_specs=pl.BlockSpec((bm, d), lambda i, p1, p2: (i, 0)),
            scratch_shapes=[pltpu.VMEM((2, 2, bm, d), F32), pltpu.SemaphoreType.DMA((2,))]),
        out_shape=jax.ShapeDtypeStruct((m, d), F32),
        compiler_params=_cparams("arbitrary"),
        name="moe_combine",
    )(pos1, pos2, ys, x, route, norm_final.reshape(1, d))


def _moe_sorted(xa, xb, g, wh, wl, bias, wg, wu, wd, norm_final):
    ma, mb = xa.shape[0], xb.shape[0]
    route_a, rt_a, cnt_a = _router_sorted(xa, g, wh, wl, bias, jnp.zeros((1, LANES), F32))
    route_b, rt_b, counts = _router_sorted(xb, g, wh, wl, bias, cnt_a)
    cnt = counts[0, :N_EXPERTS].astype(jnp.int32)
    padded = (cnt + MOE_BM - 1) // MOE_BM * MOE_BM
    seg_end = jnp.cumsum(padded)
    seg_start = seg_end - padded
    n_tiles_max = (2 * (ma + mb) + MOE_BM - 1) // MOE_BM + N_EXPERTS
    n_rows_max = n_tiles_max * MOE_BM
    n_rows_used = seg_end[-1:]
    n_tiles_used = n_rows_used // MOE_BM
    tile_start = jnp.arange(n_tiles_max, dtype=jnp.int32) * MOE_BM
    tile_expert = jnp.sum((seg_end[None, :] <= tile_start[:, None]).astype(jnp.int32), axis=1)
    last_expert = jnp.max(jnp.where(cnt > 0, jnp.arange(N_EXPERTS, dtype=jnp.int32), 0))
    tile_expert = jnp.minimum(tile_expert, last_expert)
    eidx = jnp.arange(N_EXPERTS, dtype=jnp.int32)
    used = cnt > 0
    later = jnp.where((eidx[None, :] > eidx[:, None]) & used[None, :], eidx[None, :], N_EXPERTS)
    next_used = jnp.min(later, axis=1)
    next_used = jnp.where(next_used == N_EXPERTS, -1, next_used)
    ordinal = jnp.cumsum(used.astype(jnp.int32)) - 1
    onehot = (tile_expert[:, None] == eidx[None, :]).astype(jnp.int32)
    next_expert = jnp.sum(onehot * next_used[None, :], axis=1)
    slot_parity = jnp.sum(onehot * ordinal[None, :], axis=1) % 2
    pa1, pa2 = _plan(rt_a, seg_start)
    pb1, pb2 = _plan(rt_b, seg_start)
    xs = _dispatch(xa, xb, g, jnp.concatenate([pa1, pb1]), jnp.concatenate([pa2, pb2]), seg_end, cnt,
                   n_rows_used, n_rows_max)
    ys = _experts_sorted(xs, tile_expert, n_tiles_used, next_expert, slot_parity, wg, wu, wd)
    return (_combine(ys, pa1, pa2, xa, route_a, norm_final),
            _combine(ys, pb1, pb2, xb, route_b, norm_final))


def _pad_cols(x, n):
    return jnp.pad(x, ((0, 0), (0, n - x.shape[1])))


def _block_diag_ones(n, blk):
    i = jnp.arange(n) // blk
    return (i[:, None] == i[None, :]).astype(BF16)


def kernel(x_prompt, x_sample, mem_prompt, cache_conv, state_shift, state_rwkv, cache_mem_k, cache_mem_v,
           norm_mix, w_in, conv_w, conv_b, conv_ln_g, conv_ln_b, shift_mu, w_decay_up, decay_bias, w_a_up,
           a_bias, w_g_up, k_k, k_a, r_k, lnx_g, lnx_b, w_out, norm_x, norm_mem, w_cq, w_ck, w_cv, w_co,
           norm_ffn, w_route_group, b_route_group, w_route_expert, b_route_expert, w_gate, w_up, w_down,
           norm_final):
    depth = w_in.shape[0]
    batch, seq, d = x_prompt.shape
    dec_batch = x_sample.shape[0]
    assert depth == 1
    assert x_sample.shape[1] == 1 and seq % CHUNK == 0 and seq >= CONV_K - 1
    cw = conv_w.shape[2]
    rw = w_decay_up.shape[2]
    heads = rw // HEAD
    shift_w = shift_mu.shape[1]
    in_w = w_in.shape[2]
    assert in_w == 2 * cw + shift_w and shift_w == 3 * rw + DECAY_LORA + AAA_LORA + GATE_LORA
    assert cw == rw and rw % LORA_PAD == 0
    in_pad = 2 * cw + 3 * rw + LORA_PAD
    qw = 3 * rw + LORA_PAD

    xp = x_prompt.reshape(batch * seq, d)
    xs = x_sample.reshape(dec_batch, d)
    outs = {k: [] for k in ("conv_p", "shift_p", "rwkv_p", "memk_p", "memv_p", "conv_s", "shift_s", "rwkv_s")}
    bd = _block_diag_ones(2 * LANES, HEAD)

    for l in range(depth):
        zeros_l = jnp.zeros((DECAY_LORA, rw), F32)
        wd_pad = jnp.concatenate([w_decay_up[l], zeros_l], axis=0).astype(BF16)
        wa_pad = jnp.concatenate([zeros_l, w_a_up[l]], axis=0).astype(BF16)
        wg_pad = jnp.pad(w_g_up[l], ((0, 2 * LANES - GATE_LORA), (0, 0))).astype(BF16)
        mu_pad = _pad_cols(shift_mu[l].reshape(1, shift_w), qw)
        vec = lambda x: x.reshape(1, rw)
        pp = (mu_pad, wd_pad, wa_pad, wg_pad, vec(decay_bias[l]), vec(a_bias[l]), vec(k_k[l]), vec(k_a[l]),
              vec(r_k[l]), bd)
        w_route = jnp.concatenate([w_route_expert[l].reshape(d, N_EXPERTS), w_route_group[l]], axis=1)
        w_route = _pad_cols(w_route, LANES)
        wr_hi = w_route.astype(BF16)
        wr_lo = (w_route - wr_hi.astype(F32)).astype(BF16)
        b_route = _pad_cols(jnp.concatenate([b_route_expert[l].reshape(1, N_EXPERTS),
                                             b_route_group[l].reshape(1, N_GROUPS)], axis=1), LANES)
        de = w_gate.shape[-1]
        wg_e = w_gate[l].reshape(N_EXPERTS, d, de)
        wu_e = w_up[l].reshape(N_EXPERTS, d, de)
        wd_e = w_down[l].reshape(N_EXPERTS, de, d)

        mem2 = mem_prompt.reshape(batch * N_MEM, d)
        mk, mk_b = _norm_matmul_heads(mem2, norm_mem[l], w_ck, l, X_HEADS, 256)
        mv, mv_b = _norm_matmul_heads(mem2, norm_mem[l], w_cv, l, X_HEADS, 256)
        proj = _norm_matmul(xp, norm_mix[l], w_in, l, 1024, LORA_PAD)
        assert proj.shape[1] == in_pad
        c_p, conv_new = _conv_prefill(proj, jnp.zeros((batch, CONV_K - 1, cw), F32), conv_w[l], conv_b[l],
                                      conv_ln_g[l], conv_ln_b[l], batch, seq)
        prep = _rwkv_prep_prefill(proj, jnp.zeros((batch, qw), F32), pp, batch, seq, rw)
        o_p, s_p = _rwkv_chunked(*prep[:6], jnp.zeros((batch, heads, HEAD, HEAD), F32), batch, seq)
        o_p = _rwkv_post(o_p, prep[6], prep[7], lnx_g[l], lnx_b[l], bd)
        shift_new = proj.reshape(batch, seq, in_pad)[:, -1, 2 * cw:2 * cw + shift_w]
        xp = _matmul_res([c_p, o_p], w_out, l, xp, 1024, 1024)
        qx = _norm_matmul(xp, norm_x[l], w_cq, l, 512, 1024)
        ctx = _attn_prefill(qx, mk_b, mv_b, batch, seq)
        xp = _matmul_res([ctx], w_co, l, xp, 1024, 1024)
        outs["conv_p"].append(conv_new)
        outs["shift_p"].append(shift_new)
        outs["rwkv_p"].append(s_p)
        outs["memk_p"].append(mk.reshape(batch, N_MEM, X_HEADS, d // X_HEADS))
        outs["memv_p"].append(mv.reshape(batch, N_MEM, X_HEADS, d // X_HEADS))

        proj_s = _norm_matmul(xs, norm_mix[l], w_in, l, 128, LORA_PAD)
        c_s, conv_new_s = _conv_decode(proj_s, cache_conv, l, conv_w[l], conv_b[l], conv_ln_g[l],
                                       conv_ln_b[l])
        prep_s = _rwkv_prep_decode(proj_s, _pad_cols(state_shift[l], qw), pp, rw)
        o_s, s_s = _rwkv_step(*prep_s[:6], state_rwkv, l)
        o_s = _rwkv_post(o_s, prep_s[6], prep_s[7], lnx_g[l], lnx_b[l], bd)
        xs = _matmul_res([c_s, o_s], w_out, l, xs, 128, 1024)
        qs = _norm_matmul(xs, norm_x[l], w_cq, l, 128, 1024)
        ctx_s = _attn_decode(qs, cache_mem_k[l], cache_mem_v[l])
        xs = _matmul_res([ctx_s], w_co, l, xs, 128, 1024)
        outs["conv_s"].append(conv_new_s)
        outs["shift_s"].append(proj_s[:, 2 * cw:2 * cw + shift_w])
        outs["rwkv_s"].append(s_s)

        xp, xs = _moe_sorted(xp, xs, norm_ffn[l], wr_hi, wr_lo, b_route, wg_e, wu_e, wd_e, norm_final)

    y_prompt = xp.reshape(batch, seq, d)
    y_sample = xs.reshape(dec_batch, 1, d)
    st = lambda k: jnp.stack(outs[k])
    return (y_prompt, y_sample, st("conv_p"), st("shift_p"), st("rwkv_p"), st("memk_p"), st("memv_p"),
            st("conv_s"), st("shift_s"), st("rwkv_s"))
```

```python
import functools
import math

import jax
import jax.numpy as jnp
from jax import lax
from jax.experimental import pallas as pl
from jax.experimental.pallas import tpu as pltpu

F32 = jnp.float32
BF16 = jnp.bfloat16

CONV_K = 31
HEAD = 64
PAIR = 2 * HEAD
CHUNK = 64
DECAY_LORA = 64
AAA_LORA = 64
GATE_LORA = 160
LORA_PAD = 512
N_MEM = 256
X_HEADS = 4
N_GROUPS = 4
EXP_PER_GROUP = 8
N_EXPERTS = N_GROUPS * EXP_PER_GROUP
RMS_EPS = 1e-6
LN_EPS = 1e-5
GN_EPS = 64e-5
DECAY_SCALE = math.exp(-0.5)
NEG_BIG = -1e30
MOE_BM = 256
LANES = 128
VMEM_LIMIT = 56 * 1024 * 1024


def _cparams(*sem):
    return pltpu.CompilerParams(dimension_semantics=sem, vmem_limit_bytes=VMEM_LIMIT)


def _dot(a, b):
    return jnp.dot(a, b, preferred_element_type=F32)


def _dot_nt(a, b):
    return lax.dot_general(a, b, (((1,), (1,)), ((), ())), preferred_element_type=F32)


def _split_dot(x, w_bf16):
    hi = x.astype(BF16)
    lo = (x - hi.astype(F32)).astype(BF16)
    return _dot(hi, w_bf16) + _dot(lo, w_bf16)


def _rms(x, g, eps=RMS_EPS):
    return x * lax.rsqrt(jnp.mean(x * x, axis=-1, keepdims=True) + eps) * g


def _norm_mm_kernel(x_ref, g_ref, w_ref, o_ref, xn_ref, *, n_valid):
    j = pl.program_id(1)

    @pl.when(j == 0)
    def _():
        xn_ref[...] = _rms(x_ref[...], g_ref[...]).astype(BF16)

    w = w_ref[0]
    bn = w.shape[1]
    if n_valid % bn:
        col = j * bn + lax.broadcasted_iota(jnp.int32, (1, bn), 1)
        w = jnp.where(col < n_valid, w, jnp.zeros_like(w))
    o_ref[...] = _dot(xn_ref[...], w)


def _norm_matmul(x, g, w, layer, bm, bn):
    m, k = x.shape
    n = w.shape[2]
    bm = min(bm, m)
    n_tiles = pl.cdiv(n, bn)
    return pl.pallas_call(
        functools.partial(_norm_mm_kernel, n_valid=n),
        grid=(m // bm, n_tiles),
        in_specs=[pl.BlockSpec((bm, k), lambda i, j: (i, 0)),
                  pl.BlockSpec((1, k), lambda i, j: (0, 0)),
                  pl.BlockSpec((1, k, bn), lambda i, j: (layer, 0, j))],
        out_specs=pl.BlockSpec((bm, bn), lambda i, j: (i, j)),
        out_shape=jax.ShapeDtypeStruct((m, n_tiles * bn), F32),
        scratch_shapes=[pltpu.VMEM((bm, k), BF16)],
        compiler_params=_cparams("parallel", "arbitrary"),
        name="norm_matmul",
    )(x, g.reshape(1, k), w)


def _norm_mm_heads_kernel(x_ref, g_ref, w_ref, o_ref, ob_ref):
    res = _dot(_rms(x_ref[...], g_ref[...]).astype(BF16), w_ref[0])
    ob_ref[...] = res.astype(BF16)
    dh = o_ref.shape[2]
    for h in range(o_ref.shape[1]):
        o_ref[:, h, :] = res[:, h * dh:(h + 1) * dh]


def _norm_matmul_heads(x, g, w, layer, n_heads, bm):
    m, k = x.shape
    n = w.shape[2]
    bm = min(bm, m)
    return pl.pallas_call(
        _norm_mm_heads_kernel,
        grid=(m // bm,),
        in_specs=[pl.BlockSpec((bm, k), lambda i: (i, 0)),
                  pl.BlockSpec((1, k), lambda i: (0, 0)),
                  pl.BlockSpec((1, k, n), lambda i: (layer, 0, 0))],
        out_specs=[pl.BlockSpec((bm, n_heads, n // n_heads), lambda i: (i, 0, 0)),
                   pl.BlockSpec((bm, n), lambda i: (i, 0))],
        out_shape=[jax.ShapeDtypeStruct((m, n_heads, n // n_heads), F32),
                   jax.ShapeDtypeStruct((m, n), BF16)],
        compiler_params=_cparams("parallel"),
        name="norm_matmul_heads",
    )(x, g.reshape(1, k), w)


def _mm_res_kernel(a_ref, w_ref, res_ref, o_ref):
    o_ref[...] = res_ref[...] + _dot(a_ref[...].astype(BF16), w_ref[0])


def _matmul_res(a, w, layer, res, bm):
    m, n = res.shape
    k = a.shape[1]
    bm = min(bm, m)
    return pl.pallas_call(
        _mm_res_kernel,
        grid=(m // bm,),
        in_specs=[pl.BlockSpec((bm, k), lambda i: (i, 0)),
                  pl.BlockSpec((1, k, n), lambda i: (layer, 0, 0)),
                  pl.BlockSpec((bm, n), lambda i: (i, 0))],
        out_specs=pl.BlockSpec((bm, n), lambda i: (i, 0)),
        out_shape=jax.ShapeDtypeStruct((m, n), F32),
        compiler_params=_cparams("parallel"),
        name="matmul_res",
    )(a, w, res)


def _ln_silu(cf, lg, lb):
    mu = jnp.mean(cf, axis=-1, keepdims=True)
    d = cf - mu
    var = jnp.mean(d * d, axis=-1, keepdims=True)
    y = d * lax.rsqrt(var + LN_EPS) * lg + lb
    return y * jax.nn.sigmoid(y)


def _conv_prefill_kernel(a_ref, g_ref, buf_ref, w_ref, cb_ref, lg_ref, lb_ref, c_ref, nc_ref,
                         uf_ref, cv_ref, sh_ref, *, tt, halo):
    t = pl.program_id(1)
    pad = 32 - halo

    @pl.when(t == 0)
    def _():
        uf_ref[pad:32, :] = buf_ref[0]

    @pl.when(t > 0)
    def _():
        uf_ref[pad:32, :] = uf_ref[tt + pad:tt + 32, :]

    uf_ref[32:32 + tt, :] = a_ref[...] * jax.nn.sigmoid(g_ref[...])

    for sft in range(8):
        n_rows = sh_ref.shape[1] if sft < 7 else sh_ref.shape[1] - 8
        sh_ref[sft, 0:n_rows, :] = uf_ref[pad + sft:pad + sft + n_rows, :]

    width = uf_ref.shape[1]
    rb = 64
    for r0 in range(0, tt, rb):
        for l0 in range(0, width, LANES):
            acc = jnp.zeros((rb, LANES), F32)
            for j in range(CONV_K):
                base = r0 + j - j % 8
                acc = acc + sh_ref[j % 8, base:base + rb, l0:l0 + LANES] * w_ref[j:j + 1, l0:l0 + LANES]
            cv_ref[r0:r0 + rb, l0:l0 + LANES] = acc

    c_ref[...] = _ln_silu(cv_ref[...] + cb_ref[...], lg_ref[...], lb_ref[...]).astype(c_ref.dtype)

    @pl.when(t == pl.num_programs(1) - 1)
    def _():
        nc_ref[0] = uf_ref[tt + pad:tt + 32, :]


def _conv_prefill(proj, conv_buf, conv_w, conv_b, ln_g, ln_b, batch, seq):
    cw = conv_w.shape[1]
    halo = CONV_K - 1
    tt = min(256, seq)
    nt = seq // tt
    row = lambda b, t: (b * nt + t, 0)
    vec = pl.BlockSpec((1, cw), lambda b, t: (0, 0))
    return pl.pallas_call(
        functools.partial(_conv_prefill_kernel, tt=tt, halo=halo),
        grid=(batch, nt),
        in_specs=[pl.BlockSpec((tt, cw), row),
                  pl.BlockSpec((tt, cw), lambda b, t: (b * nt + t, 1)),
                  pl.BlockSpec((1, halo, cw), lambda b, t: (b, 0, 0)),
                  pl.BlockSpec((CONV_K, cw), lambda b, t: (0, 0)),
                  vec, vec, vec],
        out_specs=[pl.BlockSpec((tt, cw), row),
                   pl.BlockSpec((1, halo, cw), lambda b, t: (b, 0, 0))],
        out_shape=[jax.ShapeDtypeStruct((batch * seq, cw), BF16),
                   jax.ShapeDtypeStruct((batch, halo, cw), F32)],
        scratch_shapes=[pltpu.VMEM((tt + 32, cw), F32), pltpu.VMEM((tt, cw), F32),
                        pltpu.VMEM((8, tt + 24, cw), F32)],
        compiler_params=_cparams("parallel", "arbitrary"),
        name="conv_prefill",
    )(proj, proj, conv_buf, conv_w, conv_b.reshape(1, cw), ln_g.reshape(1, cw), ln_b.reshape(1, cw))


def _conv_decode_kernel(a_ref, g_ref, cache_ref, w_ref, cb_ref, lg_ref, lb_ref, c_ref, nc_ref):
    halo = CONV_K - 1
    u = a_ref[...] * jax.nn.sigmoid(g_ref[...])
    acc = u * w_ref[halo:halo + 1, :]
    for j in range(halo):
        acc = acc + cache_ref[0, :, j, :] * w_ref[j:j + 1, :]
    c_ref[...] = _ln_silu(acc + cb_ref[...], lg_ref[...], lb_ref[...]).astype(c_ref.dtype)
    nc_ref[:, 0:halo - 1, :] = cache_ref[0, :, 1:halo, :]
    nc_ref[:, halo - 1, :] = u


def _conv_decode(proj, cache, layer, conv_w, conv_b, ln_g, ln_b):
    _, batch, halo, cw = cache.shape
    bb = 8
    vec = pl.BlockSpec((1, cw), lambda i: (0, 0))
    return pl.pallas_call(
        _conv_decode_kernel,
        grid=(batch // bb,),
        in_specs=[pl.BlockSpec((bb, cw), lambda i: (i, 0)),
                  pl.BlockSpec((bb, cw), lambda i: (i, 1)),
                  pl.BlockSpec((1, bb, halo, cw), lambda i: (layer, i, 0, 0)),
                  pl.BlockSpec((CONV_K, cw), lambda i: (0, 0)),
                  vec, vec, vec],
        out_specs=[pl.BlockSpec((bb, cw), lambda i: (i, 0)),
                   pl.BlockSpec((bb, halo, cw), lambda i: (i, 0, 0))],
        out_shape=[jax.ShapeDtypeStruct((batch, cw), BF16),
                   jax.ShapeDtypeStruct((batch, halo, cw), F32)],
        compiler_params=_cparams("parallel"),
        name="conv_decode",
    )(proj, proj, cache, conv_w, conv_b.reshape(1, cw), ln_g.reshape(1, cw), ln_b.reshape(1, cw))


def _head_sum(x, bd_ref):
    blk = bd_ref.shape[0]
    parts = [_split_dot(x[:, l0:l0 + blk], bd_ref[...]) for l0 in range(0, x.shape[1], blk)]
    return jnp.concatenate(parts, axis=1)


def _prep_math(q, qp, mu_ref, wd_ref, wa_ref, wg_ref, db_ref, ab_ref, kk_ref, ka_ref, rk_ref, bd_ref):
    rw = q[0].shape[1]
    offs = (0, rw, 2 * rw, 3 * rw)
    r, k, v, lo = [x + (xp - x) * mu_ref[:, o:o + x.shape[1]] for x, xp, o in zip(q, qp, offs)]
    pwa = lo[:, 0:LANES]
    pg = lo[:, LANES:3 * LANES]
    dec_in = _dot(jnp.tanh(pwa).astype(BF16), wd_ref[...])
    a_in = _dot(pwa.astype(BF16), wa_ref[...])
    gate = _dot(jax.nn.sigmoid(pg).astype(BF16), wg_ref[...])
    logw = -DECAY_SCALE * jax.nn.sigmoid(db_ref[...] + dec_in)
    a = jax.nn.sigmoid(ab_ref[...] + a_in)
    kk = k * kk_ref[...]
    kk = kk / jnp.maximum(jnp.sqrt(_head_sum(kk * kk, bd_ref)), 1e-12)
    k2 = k * (1.0 + (a - 1.0) * ka_ref[...])
    bonus = _head_sum(r * k2 * rk_ref[...], bd_ref) * v
    return r, logw, k2, v, kk, a, bonus, gate


def _prep_prefill_kernel(r_ref, k_ref, v_ref, lo_ref, sb_ref, mu_ref, wd_ref, wa_ref, wg_ref, db_ref,
                         ab_ref, kk_ref, ka_ref, rk_ref, bd_ref, *rest):
    outs = rest[:8]
    carry_ref = rest[8]
    t = pl.program_id(1)

    @pl.when(t == 0)
    def _():
        carry_ref[0:1, :] = sb_ref[0]

    q = [r_ref[...], k_ref[...], v_ref[...], lo_ref[...]]
    tt = q[0].shape[0]
    first = lax.broadcasted_iota(jnp.int32, (tt, 1), 0) == 0
    qp = []
    off = 0
    for x in q:
        w = x.shape[1]
        qp.append(jnp.where(first, carry_ref[0:1, off:off + w], pltpu.roll(x, 1, 0)))
        off += w
    off = 0
    for x in q:
        w = x.shape[1]
        carry_ref[0:1, off:off + w] = x[tt - 1:tt, :]
        off += w
    res = _prep_math(q, qp, mu_ref, wd_ref, wa_ref, wg_ref, db_ref, ab_ref, kk_ref, ka_ref, rk_ref, bd_ref)
    for o_ref, val in zip(outs, res):
        o_ref[...] = val


def _prep_decode_kernel(r_ref, k_ref, v_ref, lo_ref, rp_ref, kp_ref, vp_ref, lop_ref, mu_ref, wd_ref,
                        wa_ref, wg_ref, db_ref, ab_ref, kk_ref, ka_ref, rk_ref, bd_ref, *outs):
    q = [r_ref[...], k_ref[...], v_ref[...], lo_ref[...]]
    qp = [rp_ref[...], kp_ref[...], vp_ref[...], lop_ref[...]]
    res = _prep_math(q, qp, mu_ref, wd_ref, wa_ref, wg_ref, db_ref, ab_ref, kk_ref, ka_ref, rk_ref, bd_ref)
    for o_ref, val in zip(outs, res):
        o_ref[...] = val


def _prep_param_specs(rw, idx):
    full = lambda shape: pl.BlockSpec(shape, idx)
    vec = full((1, rw))
    return [full((1, 3 * rw + LORA_PAD)), full((LANES, rw)), full((LANES, rw)), full((2 * LANES, rw)),
            vec, vec, vec, vec, vec, full((2 * LANES, 2 * LANES))]


def _rwkv_prep_prefill(proj, shift_buf, pp, batch, seq, rw):
    tt = min(256, seq)
    nt = seq // tt
    lora_blk = (2 * rw + 3 * rw) // LORA_PAD
    col = lambda c: (lambda b, t: (b * nt + t, c))
    qw = 3 * rw + LORA_PAD
    in_specs = [pl.BlockSpec((tt, rw), col(2)), pl.BlockSpec((tt, rw), col(3)),
                pl.BlockSpec((tt, rw), col(4)), pl.BlockSpec((tt, LORA_PAD), col(lora_blk)),
                pl.BlockSpec((1, 1, qw), lambda b, t: (b, 0, 0))]
    in_specs += _prep_param_specs(rw, lambda b, t: (0, 0))
    out_spec = pl.BlockSpec((tt, rw), col(0))
    return pl.pallas_call(
        _prep_prefill_kernel,
        grid=(batch, nt),
        in_specs=in_specs,
        out_specs=[out_spec] * 8,
        out_shape=[jax.ShapeDtypeStruct((batch * seq, rw), F32)] * 8,
        scratch_shapes=[pltpu.VMEM((8, qw), F32)],
        compiler_params=_cparams("parallel", "arbitrary"),
        name="rwkv_prep_prefill",
    )(proj, proj, proj, proj, shift_buf.reshape(batch, 1, qw), *pp)


def _rwkv_prep_decode(proj, shift_state, pp, rw):
    batch = proj.shape[0]
    bb = min(128, batch)
    lora_blk = (2 * rw + 3 * rw) // LORA_PAD
    col = lambda c: (lambda i: (i, c))
    in_specs = [pl.BlockSpec((bb, rw), col(2)), pl.BlockSpec((bb, rw), col(3)),
                pl.BlockSpec((bb, rw), col(4)), pl.BlockSpec((bb, LORA_PAD), col(lora_blk)),
                pl.BlockSpec((bb, rw), col(0)), pl.BlockSpec((bb, rw), col(1)),
                pl.BlockSpec((bb, rw), col(2)), pl.BlockSpec((bb, LORA_PAD), col(3 * rw // LORA_PAD))]
    in_specs += _prep_param_specs(rw, lambda i: (0, 0))
    return pl.pallas_call(
        _prep_decode_kernel,
        grid=(batch // bb,),
        in_specs=in_specs,
        out_specs=[pl.BlockSpec((bb, rw), col(0))] * 8,
        out_shape=[jax.ShapeDtypeStruct((batch, rw), F32)] * 8,
        compiler_params=_cparams("parallel"),
        name="rwkv_prep_decode",
    )(proj, proj, proj, proj, shift_state, shift_state, shift_state, shift_state, *pp)


def _stack2(x, smask):
    return jnp.where(smask, jnp.concatenate([x, x], axis=0), 0.0)


def _rwkv_chunk_kernel(r_ref, lw_ref, k_ref, v_ref, kk_ref, a_ref, s0_ref, o_ref, so_ref, s_ref):
    c = pl.program_id(1)
    cs = r_ref.shape[0]
    n_pairs = r_ref.shape[1] // PAIR
    two = 2 * cs

    @pl.when(c == 0)
    def _():
        z = jnp.zeros((HEAD, HEAD), F32)
        for p in range(n_pairs):
            top = jnp.concatenate([s0_ref[0, 2 * p], z], axis=1)
            bot = jnp.concatenate([z, s0_ref[0, 2 * p + 1]], axis=1)
            s_ref[p] = jnp.concatenate([top, bot], axis=0)

    ri = lax.broadcasted_iota(jnp.int32, (two, two), 0)
    ci = lax.broadcasted_iota(jnp.int32, (two, two), 1)
    strict = ci < ri
    incl = ci <= ri
    eye = (ci == ri).astype(F32)
    smask = (lax.broadcasted_iota(jnp.int32, (two, PAIR), 0) < cs) == (
        lax.broadcasted_iota(jnp.int32, (two, PAIR), 1) < HEAD)
    tri = (lax.broadcasted_iota(jnp.int32, (cs, cs), 1)
           <= lax.broadcasted_iota(jnp.int32, (cs, cs), 0)).astype(BF16)

    lw_all = lw_ref[...]
    lw_hi = lw_all.astype(BF16)
    lw_lo = (lw_all - lw_hi.astype(F32)).astype(BF16)
    cum_all = _dot(tri, lw_hi) + _dot(tri, lw_lo)

    pairs = range(n_pairs)
    cat = jnp.concatenate
    prep = []
    for p in pairs:
        sl = slice(p * PAIR, (p + 1) * PAIR)
        lw = lw_all[:, sl]
        cum = cum_all[:, sl]
        tot = cum[cs - 1:cs, :]
        g_inv = jnp.exp(-cum)
        g_end = jnp.exp(tot - cum)
        kk = kk_ref[:, sl]
        k2 = k_ref[:, sl]
        bb = kk * a_ref[:, sl]
        prep.append(dict(
            g_tot=jnp.exp(tot),
            a_b=_stack2(kk * jnp.exp(cum - lw), smask).astype(BF16),
            r_s=_stack2(r_ref[:, sl] * jnp.exp(cum), smask),
            bk=cat([_stack2(bb * g_inv, smask), _stack2(k2 * g_inv, smask)], axis=0).astype(BF16),
            v_s=_stack2(v_ref[:, sl], smask),
            bg_s=_stack2(bb * g_end, smask).astype(BF16),
            kg_s=_stack2(k2 * g_end, smask).astype(BF16)))
    a_b = [q["a_b"] for q in prep]
    r_s = [q["r_s"] for q in prep]
    v_s = [q["v_s"] for q in prep]
    v_b = [x.astype(BF16) for x in v_s]

    gram = [_dot_nt(cat([a_b[p], r_s[p].astype(BF16)], axis=0), prep[p]["bk"]) for p in pairs]
    l_ab = [jnp.where(strict, g[0:two, 0:two], 0.0) for g in gram]
    l_ak = [jnp.where(strict, g[0:two, two:], 0.0).astype(BF16) for g in gram]
    m_rb = [jnp.where(incl, g[two:, 0:two], 0.0).astype(BF16) for g in gram]
    m_rk = [jnp.where(incl, g[two:, two:], 0.0).astype(BF16) for g in gram]

    tm = [eye - x for x in l_ab]
    pw = l_ab
    n = 1
    while 2 * n < cs:
        pw = [_dot(x.astype(BF16), x.astype(BF16)) for x in pw]
        tm = [t + _dot(t.astype(BF16), x.astype(BF16)) for t, x in zip(tm, pw)]
        n *= 2
    tm_b = [t.astype(BF16) for t in tm]

    w1 = [_dot(l_ak[p], v_b[p]) for p in pairs]
    ua = [_dot(tm_b[p], cat([w1[p].astype(BF16), a_b[p]], axis=1)) for p in pairs]
    ua_b = [x.astype(BF16) for x in ua]
    mrb_ua = [_dot(m_rb[p], ua_b[p]) for p in pairs]
    o0 = [_dot(m_rk[p], v_b[p]) - mrb_ua[p][:, 0:PAIR] for p in pairs]
    rt = [(r_s[p] - mrb_ua[p][:, PAIR:]).astype(BF16) for p in pairs]

    s_old = [s_ref[p] for p in pairs]
    s_b = [x.astype(BF16) for x in s_old]
    ua_t = [cat([x[:, 0:PAIR].T, x[:, PAIR:].T], axis=0).astype(BF16) for x in ua]
    uat_bg = [_dot(ua_t[p], prep[p]["bg_s"]) for p in pairs]
    vt_kg = [_dot(v_s[p].T.astype(BF16), prep[p]["kg_s"]) for p in pairs]
    s_new = [s_old[p] * prep[p]["g_tot"] - _dot(s_b[p], uat_bg[p][PAIR:, :].astype(BF16))
             + vt_kg[p] - uat_bg[p][0:PAIR, :] for p in pairs]
    o_st = [o0[p] + _dot_nt(rt[p], s_b[p]) for p in pairs]

    o_ref[...] = cat([x[0:cs, :] + x[cs:two, :] for x in o_st], axis=1)
    s_ref[...] = jnp.stack(s_new, axis=0)

    @pl.when(c == pl.num_programs(1) - 1)
    def _():
        for p in range(n_pairs):
            s = s_ref[p]
            so_ref[0, 2 * p] = s[0:HEAD, 0:HEAD]
            so_ref[0, 2 * p + 1] = s[HEAD:PAIR, HEAD:PAIR]


def _rwkv_chunked(r, logw, k2, v, kk, a, s0, batch, seq):
    rw = r.shape[1]
    nc = seq // CHUNK
    heads = rw // HEAD
    row = pl.BlockSpec((CHUNK, rw), lambda b, c: (b * nc + c, 0))
    st = pl.BlockSpec((1, heads, HEAD, HEAD), lambda b, c: (b, 0, 0, 0))
    return pl.pallas_call(
        _rwkv_chunk_kernel,
        grid=(batch, nc),
        in_specs=[row] * 6 + [st],
        out_specs=[row, st],
        out_shape=[jax.ShapeDtypeStruct((batch * seq, rw), F32),
                   jax.ShapeDtypeStruct((batch, heads, HEAD, HEAD), F32)],
        scratch_shapes=[pltpu.VMEM((rw // PAIR, PAIR, PAIR), F32)],
        compiler_params=_cparams("parallel", "arbitrary"),
        name="rwkv_chunked",
    )(r, logw, k2, v, kk, a, s0)


def _rwkv_step_kernel(r_ref, lw_ref, k_ref, v_ref, kk_ref, a_ref, s_ref, o_ref, so_ref, *, heads):
    bb = s_ref.shape[1]
    eye = (lax.broadcasted_iota(jnp.int32, (HEAD, HEAD), 0)
           == lax.broadcasted_iota(jnp.int32, (HEAD, HEAD), 1))

    def body(bi, carry):
        hs = range(heads)
        rows = [pl.ds(bi * heads + h, 1) for h in hs]
        kk = [kk_ref[r, :] for r in rows]
        s = [s_ref[0, bi, h].astype(F32) for h in hs]
        sa = [jnp.sum(s[h] * kk[h], axis=1, keepdims=True) for h in hs]
        v_col = [jnp.sum(jnp.where(eye, v_ref[rows[h], :], 0.0), axis=1, keepdims=True) for h in hs]
        s_new = [s[h] * jnp.exp(lw_ref[rows[h], :]) - sa[h] * (kk[h] * a_ref[rows[h], :])
                 + v_col[h] * k_ref[rows[h], :] for h in hs]
        for h in hs:
            so_ref[bi, h] = s_new[h]
        o_col = [jnp.sum(s_new[h] * r_ref[rows[h], :], axis=1, keepdims=True) for h in hs]
        o_row = [jnp.sum(jnp.where(eye, o_col[h], 0.0), axis=0, keepdims=True) for h in hs]
        o_ref[pl.ds(pl.multiple_of(bi * heads, heads), heads), :] = jnp.concatenate(o_row, axis=0)
        return carry

    lax.fori_loop(0, bb, body, 0)


def _rwkv_step(r, logw, k2, v, kk, a, state, layer):
    batch, rw = r.shape
    heads = rw // HEAD
    bb = 8
    flat = lambda x: x.reshape(batch * heads, HEAD)
    row = pl.BlockSpec((bb * heads, HEAD), lambda i: (i, 0))
    st_in = pl.BlockSpec((1, bb, heads, HEAD, HEAD), lambda i: (layer, i, 0, 0, 0))
    st_out = pl.BlockSpec((bb, heads, HEAD, HEAD), lambda i: (i, 0, 0, 0))
    o, s_new = pl.pallas_call(
        functools.partial(_rwkv_step_kernel, heads=heads),
        grid=(batch // bb,),
        in_specs=[row] * 6 + [st_in],
        out_specs=[row, st_out],
        out_shape=[jax.ShapeDtypeStruct((batch * heads, HEAD), F32),
                   jax.ShapeDtypeStruct(state.shape[1:], F32)],
        compiler_params=_cparams("parallel"),
        name="rwkv_step",
    )(flat(r), flat(logw), flat(k2), flat(v), flat(kk), flat(a), state)
    return o.reshape(batch, rw), s_new


def _mix_out_kernel(c_ref, o_ref, bonus_ref, gate_ref, lg_ref, lb_ref, bd_ref, wc_ref, wo_ref, x_ref, y_ref):
    o = o_ref[...]
    mu = _head_sum(o, bd_ref) * (1.0 / HEAD)
    d = o - mu
    var = _head_sum(d * d, bd_ref) * (1.0 / HEAD)
    y = d * lax.rsqrt(var + GN_EPS) * lg_ref[...] + lb_ref[...]
    om = ((y + bonus_ref[...]) * gate_ref[...]).astype(BF16)
    y_ref[...] = x_ref[...] + _dot(c_ref[...], wc_ref[0]) + _dot(om, wo_ref[0])


def _mix_out(c, o, bonus, gate, lnx_g, lnx_b, bd, w_out, layer, x):
    m, d = x.shape
    cw, rw = c.shape[1], o.shape[1]
    assert cw == rw and w_out.shape[1] == cw + rw
    bm = min(512, m)
    row = lambda w: pl.BlockSpec((bm, w), lambda i: (i, 0))
    vec = pl.BlockSpec((1, rw), lambda i: (0, 0))
    return pl.pallas_call(
        _mix_out_kernel,
        grid=(m // bm,),
        in_specs=[row(cw), row(rw), row(rw), row(rw), vec, vec, pl.BlockSpec(bd.shape, lambda i: (0, 0)),
                  pl.BlockSpec((1, cw, d), lambda i: (layer, 0, 0)),
                  pl.BlockSpec((1, rw, d), lambda i: (layer, 1, 0)),
                  row(d)],
        out_specs=row(d),
        out_shape=jax.ShapeDtypeStruct((m, d), F32),
        compiler_params=_cparams("parallel"),
        name="mix_out",
    )(c, o, bonus, gate, lnx_g.reshape(1, rw), lnx_b.reshape(1, rw), bd, w_out, w_out, x)


def _attn_prefill_kernel(q_ref, k_ref, v_ref, o_ref, *, n_heads):
    d = q_ref.shape[1] // n_heads
    scale = d ** -0.5
    for h in range(n_heads):
        sl = slice(h * d, (h + 1) * d)
        s = _dot_nt(q_ref[:, sl].astype(BF16), k_ref[:, sl]) * scale
        p = jnp.exp(s - jnp.max(s, axis=-1, keepdims=True))
        att = p / jnp.sum(p, axis=-1, keepdims=True)
        o_ref[:, sl] = _dot(att.astype(BF16), v_ref[:, sl]).astype(o_ref.dtype)


def _attn_prefill(q, mem_k, mem_v, batch, seq):
    d = q.shape[1]
    tt = min(512, seq)
    nt = seq // tt
    kv = pl.BlockSpec((N_MEM, d), lambda b, t: (b, 0))
    row = pl.BlockSpec((tt, d), lambda b, t: (b * nt + t, 0))
    return pl.pallas_call(
        functools.partial(_attn_prefill_kernel, n_heads=X_HEADS),
        grid=(batch, nt),
        in_specs=[row, kv, kv],
        out_specs=row,
        out_shape=jax.ShapeDtypeStruct((batch * seq, d), BF16),
        compiler_params=_cparams("parallel", "arbitrary"),
        name="attn_prefill",
    )(q, mem_k, mem_v)


def _attn_decode_kernel(q_ref, k_ref, v_ref, o_ref, *, n_heads):
    bb = q_ref.shape[0]
    d = q_ref.shape[2] // n_heads
    scale = d ** -0.5
    for bi in range(bb):
        q = q_ref[bi]
        for h in range(n_heads):
            sl = slice(h * d, (h + 1) * d)
            s = jnp.sum(k_ref[bi, :, h, :] * q[:, sl], axis=1, keepdims=True) * scale
            p = jnp.exp(s - jnp.max(s, axis=0, keepdims=True))
            att = p / jnp.sum(p, axis=0, keepdims=True)
            o_ref[bi, :, sl] = jnp.sum(att * v_ref[bi, :, h, :], axis=0, keepdims=True)


def _attn_decode(q, cache_k, cache_v):
    batch, d = q.shape
    bb = 2
    kv = pl.BlockSpec((bb, N_MEM, X_HEADS, d // X_HEADS), lambda i: (i, 0, 0, 0))
    row = pl.BlockSpec((bb, 1, d), lambda i: (i, 0, 0))
    out = pl.pallas_call(
        functools.partial(_attn_decode_kernel, n_heads=X_HEADS),
        grid=(batch // bb,),
        in_specs=[row, kv, kv],
        out_specs=row,
        out_shape=jax.ShapeDtypeStruct((batch, 1, d), F32),
        compiler_params=_cparams("parallel"),
        name="attn_decode",
    )(q.reshape(batch, 1, d), cache_k, cache_v)
    return out.reshape(batch, d)


def _route(x_ref, g_ref, wh_ref, wl_ref, b_ref):
    h = _rms(x_ref[...], g_ref[...])
    hh = h.astype(BF16)
    hl = (h - hh.astype(F32)).astype(BF16)
    logits = _dot(hh, wh_ref[...]) + _dot(hl, wh_ref[...]) + _dot(hh, wl_ref[...]) + b_ref[...]
    lane = lax.broadcasted_iota(jnp.int32, (1, LANES), 1).astype(F32)
    is_g = (lane >= N_EXPERTS) & (lane < N_EXPERTS + N_GROUPS)
    lgm = jnp.where(is_g, logits, NEG_BIG)
    gmax = jnp.max(lgm, axis=1, keepdims=True)
    gsum = jnp.sum(jnp.where(is_g, jnp.exp(lgm - gmax), 0.0), axis=1, keepdims=True)
    g_val = 1.0 / gsum
    g_idx = jnp.min(jnp.where(is_g & (lgm == gmax), lane - N_EXPERTS, 1e9), axis=1, keepdims=True)
    in_grp = (lane < N_EXPERTS) & (jnp.floor(lane * (1.0 / EXP_PER_GROUP)) == g_idx)
    le = jnp.where(in_grp, logits, NEG_BIG)
    m1 = jnp.max(le, axis=1, keepdims=True)
    i1 = jnp.min(jnp.where(in_grp & (le == m1), lane, 1e9), axis=1, keepdims=True)
    rest = in_grp & (lane != i1)
    le2 = jnp.where(rest, logits, NEG_BIG)
    m2 = jnp.max(le2, axis=1, keepdims=True)
    i2 = jnp.min(jnp.where(rest & (le2 == m2), lane, 1e9), axis=1, keepdims=True)
    e2 = jnp.exp(m2 - m1)
    den = 1.0 + e2
    w1 = (1.0 / den) * g_val
    w2 = (e2 / den) * g_val
    return h, lane, i1, i2, w1, w2


def _router_sorted_kernel(x_ref, g_ref, wh_ref, wl_ref, b_ref, init_ref, route_ref, rt_ref, cnt_ref, run_ref):
    @pl.when(pl.program_id(0) == 0)
    def _():
        run_ref[0:1, :] = init_ref[...]

    h, lane, i1, i2, w1, w2 = _route(x_ref, g_ref, wh_ref, wl_ref, b_ref)
    bm = h.shape[0]
    oh1 = lane == i1
    oh2 = lane == i2
    sel = (oh1 | oh2).astype(BF16)
    before = (lax.broadcasted_iota(jnp.int32, (bm, bm), 1)
              < lax.broadcasted_iota(jnp.int32, (bm, bm), 0)).astype(BF16)
    base = run_ref[0:1, :] + _dot(before, sel)
    rank1 = jnp.sum(jnp.where(oh1, base, 0.0), axis=1, keepdims=True)
    rank2 = jnp.sum(jnp.where(oh2, base, 0.0), axis=1, keepdims=True)
    total = run_ref[0:1, :] + jnp.sum(sel.astype(F32), axis=0, keepdims=True)
    run_ref[0:1, :] = total
    cnt_ref[...] = total
    route = jnp.zeros((bm, LANES), F32)
    for idx, val in enumerate((i1, i2, w1, w2, rank1, rank2)):
        route = jnp.where(lane == idx, val, route)
    route_ref[...] = route
    for r0 in range(0, bm, LANES):
        rt_ref[:, r0:r0 + LANES] = route[r0:r0 + LANES, :].T[0:8, :]


def _router_sorted(x, g, wh, wl, bias, init_counts):
    m, d = x.shape
    bm = min(512, m)
    return pl.pallas_call(
        _router_sorted_kernel,
        grid=(m // bm,),
        in_specs=[pl.BlockSpec((bm, d), lambda i: (i, 0)),
                  pl.BlockSpec((1, d), lambda i: (0, 0)),
                  pl.BlockSpec((d, LANES), lambda i: (0, 0)),
                  pl.BlockSpec((d, LANES), lambda i: (0, 0)),
                  pl.BlockSpec((1, LANES), lambda i: (0, 0)),
                  pl.BlockSpec((1, LANES), lambda i: (0, 0))],
        out_specs=[pl.BlockSpec((bm, LANES), lambda i: (i, 0)),
                   pl.BlockSpec((8, bm), lambda i: (0, i)),
                   pl.BlockSpec((1, LANES), lambda i: (0, 0))],
        out_shape=[jax.ShapeDtypeStruct((m, LANES), F32),
                   jax.ShapeDtypeStruct((8, m), F32),
                   jax.ShapeDtypeStruct((1, LANES), F32)],
        scratch_shapes=[pltpu.VMEM((8, LANES), F32)],
        compiler_params=_cparams("arbitrary"),
        name="moe_router_sorted",
    )(x, g.reshape(1, d), wh, wl, bias, init_counts)


def _plan_kernel(seg_ref, rt_ref, pos_ref):
    rt = rt_ref[...]
    rows = []
    for e_row, r_row in ((0, 4), (1, 5)):
        e = rt[e_row:e_row + 1, :]
        start = jnp.zeros_like(e)
        for k in range(N_EXPERTS):
            start = jnp.where(e == k, seg_ref[k].astype(F32), start)
        rows.append((start + rt[r_row:r_row + 1, :]).astype(jnp.int32))
    pos_ref[...] = jnp.concatenate(rows + [jnp.zeros((6, rt.shape[1]), jnp.int32)], axis=0)


def _plan(route_t, seg_start):
    m = route_t.shape[1]
    bt = min(2048, m)
    pos = pl.pallas_call(
        _plan_kernel,
        grid_spec=pltpu.PrefetchScalarGridSpec(
            num_scalar_prefetch=1, grid=(m // bt,),
            in_specs=[pl.BlockSpec((8, bt), lambda i, seg: (0, i))],
            out_specs=pl.BlockSpec((8, bt), lambda i, seg: (0, i))),
        out_shape=jax.ShapeDtypeStruct((8, m), jnp.int32),
        compiler_params=_cparams("arbitrary"),
        name="moe_plan",
    )(seg_start, route_t)
    return pos[0], pos[1]


def _row_copy(src_hbm, src_row, dst, dst_row, sem):
    return pltpu.make_async_copy(src_hbm.at[pl.ds(src_row, 1)], dst.at[pl.ds(dst_row, 1)], sem)


def _dispatch_kernel(p1_ref, p2_ref, seg_ref, cnt_ref, nrow_ref, xa_ref, xb_ref, g_ref, xs_hbm, h_ref,
                     zero_ref, sem, zsem, *, n_a):
    i = pl.program_id(0)
    n = pl.num_programs(0)
    bm = xa_ref.shape[0]
    n_max = xs_hbm.shape[0] // MOE_BM

    def zero_copy(row0):
        return pltpu.make_async_copy(zero_ref, xs_hbm.at[pl.ds(pl.multiple_of(row0, MOE_BM), MOE_BM)], zsem)

    @pl.when(i == 0)
    def _():
        zero_ref[...] = jnp.zeros_like(zero_ref)
        first_free = nrow_ref[0] // MOE_BM

        def tail_start(c, carry):
            zero_copy(c * MOE_BM).start()
            return carry

        def tail_wait(c, carry):
            zero_copy(c * MOE_BM).wait()
            return carry

        for e in range(N_EXPERTS):
            @pl.when(cnt_ref[e] > 0)
            def _():
                zero_copy(seg_ref[e] - MOE_BM).start()

        lax.fori_loop(first_free, n_max, tail_start, 0)
        for e in range(N_EXPERTS):
            @pl.when(cnt_ref[e] > 0)
            def _():
                zero_copy(seg_ref[e] - MOE_BM).wait()

        lax.fori_loop(first_free, n_max, tail_wait, 0)

    bm_b = xb_ref.shape[0]
    tok_b0 = n_a * bm

    def wait_rows(slot, rows):
        for _ in range(2):
            pltpu.make_async_copy(h_ref.at[slot, pl.ds(0, rows)], xs_hbm.at[pl.ds(0, rows)],
                                  sem.at[slot]).wait()

    def scatter_rows(slot, rows, tok0):
        def body(r, carry):
            src = h_ref.at[slot, pl.ds(r, 1)]
            pltpu.make_async_copy(src, xs_hbm.at[pl.ds(p1_ref[tok0 + r], 1)], sem.at[slot]).start()
            pltpu.make_async_copy(src, xs_hbm.at[pl.ds(p2_ref[tok0 + r], 1)], sem.at[slot]).start()
            return carry

        lax.fori_loop(0, rows, body, 0, unroll=8)

    slot = i % 2

    @pl.when((i > 0) & (i - 1 < n_a))
    def _():
        wait_rows(1 - slot, bm)

    @pl.when(i - 1 >= n_a)
    def _():
        wait_rows(1 - slot, bm_b)

    @pl.when(i < n_a)
    def _():
        h_ref[slot, 0:bm, :] = _rms(xa_ref[...], g_ref[...])
        scatter_rows(slot, bm, i * bm)

    @pl.when(i >= n_a)
    def _():
        h_ref[slot, 0:bm_b, :] = _rms(xb_ref[...], g_ref[...])
        scatter_rows(slot, bm_b, tok_b0 + (i - n_a) * bm_b)

    @pl.when(i == n - 1)
    def _():
        wait_rows(slot, bm_b)


def _dispatch(xa, xb, g, pos1, pos2, seg_end, cnt, n_rows_used, n_rows_max):
    (ma, d), mb = xa.shape, xb.shape[0]
    bm_a, bm_b = min(MOE_BM, ma), min(MOE_BM, mb)
    assert ma % bm_a == 0 and mb % bm_b == 0 and bm_b <= bm_a
    n_a, n_b = ma // bm_a, mb // bm_b
    return pl.pallas_call(
        functools.partial(_dispatch_kernel, n_a=n_a),
        grid_spec=pltpu.PrefetchScalarGridSpec(
            num_scalar_prefetch=5, grid=(n_a + n_b,),
            in_specs=[pl.BlockSpec((bm_a, d), lambda i, *_: (jnp.minimum(i, n_a - 1), 0)),
                      pl.BlockSpec((bm_b, d), lambda i, *_: (jnp.maximum(i - n_a, 0), 0)),
                      pl.BlockSpec((1, d), lambda i, *_: (0, 0))],
            out_specs=pl.BlockSpec(memory_space=pl.ANY),
            scratch_shapes=[pltpu.VMEM((2, bm_a, d), F32), pltpu.VMEM((MOE_BM, d), F32),
                            pltpu.SemaphoreType.DMA((2,)), pltpu.SemaphoreType.DMA(())]),
        out_shape=jax.ShapeDtypeStruct((n_rows_max, d), F32),
        compiler_params=_cparams("arbitrary"),
        name="moe_dispatch",
    )(pos1, pos2, seg_end, cnt, n_rows_used, xa, xb, g.reshape(1, d))


def _experts_sorted_kernel(te_ref, nt_ref, nxt_ref, par_ref, xs_ref, wg_hbm, wu_hbm, wd_hbm, ys_ref,
                           wgf_ref, wuf_ref, wdf_ref, wgb_ref, wub_ref, wdb_ref, sem):
    j = pl.program_id(0)
    prev = te_ref[jnp.maximum(j, 1) - 1]

    def copies(e, slot):
        return [pltpu.make_async_copy(hbm.at[e], buf.at[slot], sem.at[slot])
                for hbm, buf in ((wg_hbm, wgf_ref), (wu_hbm, wuf_ref), (wd_hbm, wdf_ref))]

    @pl.when(j == 0)
    def _():
        for c in copies(te_ref[0], par_ref[0]):
            c.start()

    @pl.when((j < nt_ref[0]) & ((j == 0) | (te_ref[j] != prev)))
    def _():
        slot = par_ref[j]
        for c in copies(te_ref[j], slot):
            c.wait()

        @pl.when(nxt_ref[j] >= 0)
        def _():
            for c in copies(nxt_ref[j], 1 - slot):
                c.start()

        wgb_ref[...] = wgf_ref[slot].astype(BF16)
        wub_ref[...] = wuf_ref[slot].astype(BF16)
        wdb_ref[...] = wdf_ref[slot].astype(BF16)

    @pl.when(j < nt_ref[0])
    def _():
        x = xs_ref[...].astype(BF16)
        hg = _dot(x, wgb_ref[...])
        hu = _dot(x, wub_ref[...])
        act = hg * jax.nn.sigmoid(hg) * hu
        ys_ref[...] = _dot(act.astype(BF16), wdb_ref[...])

    @pl.when(j >= nt_ref[0])
    def _():
        ys_ref[...] = jnp.zeros_like(ys_ref)


def _experts_sorted(xs, tile_expert, n_tiles_used, next_expert, slot_parity, wg, wu, wd):
    n_rows, d = xs.shape
    de = wg.shape[2]
    n_tiles = n_rows // MOE_BM
    row_in = lambda j, te, nt, *_: (jnp.minimum(j, nt[0] - 1), 0)
    hbm = pl.BlockSpec(memory_space=pl.ANY)
    return pl.pallas_call(
        _experts_sorted_kernel,
        grid_spec=pltpu.PrefetchScalarGridSpec(
            num_scalar_prefetch=4, grid=(n_tiles,),
            in_specs=[pl.BlockSpec((MOE_BM, d), row_in), hbm, hbm, hbm],
            out_specs=pl.BlockSpec((MOE_BM, d), lambda j, *_: (j, 0)),
            scratch_shapes=[pltpu.VMEM((2, d, de), F32), pltpu.VMEM((2, d, de), F32),
                            pltpu.VMEM((2, de, d), F32),
                            pltpu.VMEM((d, de), BF16), pltpu.VMEM((d, de), BF16), pltpu.VMEM((de, d), BF16),
                            pltpu.SemaphoreType.DMA((2,))]),
        out_shape=jax.ShapeDtypeStruct((n_rows, d), F32),
        compiler_params=_cparams("arbitrary"),
        name="moe_experts_sorted",
    )(tile_expert, n_tiles_used, next_expert, slot_parity, xs, wg, wu, wd)


def _combine_kernel(p1_ref, p2_ref, ys_hbm, x_ref, route_ref, nf_ref, y_ref, buf_ref, sem):
    i = pl.program_id(0)
    n = pl.num_programs(0)
    bm = x_ref.shape[0]

    def issue(tile, slot):
        def body(r, carry):
            t = tile * bm + r
            _row_copy(ys_hbm, p1_ref[t], buf_ref.at[slot, 0], r, sem.at[slot]).start()
            _row_copy(ys_hbm, p2_ref[t], buf_ref.at[slot, 1], r, sem.at[slot]).start()
            return carry

        lax.fori_loop(0, bm, body, 0, unroll=8)

    @pl.when(i == 0)
    def _():
        issue(0, 0)

    @pl.when(i + 1 < n)
    def _():
        issue(i + 1, (i + 1) % 2)

    slot = i % 2
    for k in range(2):
        pltpu.make_async_copy(ys_hbm.at[pl.ds(0, bm)], buf_ref.at[slot, k], sem.at[slot]).wait()
    lane = lax.broadcasted_iota(jnp.int32, (1, LANES), 1)
    route = route_ref[...]
    w1 = jnp.sum(jnp.where(lane == 2, route, 0.0), axis=1, keepdims=True)
    w2 = jnp.sum(jnp.where(lane == 3, route, 0.0), axis=1, keepdims=True)
    x3 = x_ref[...] + w1 * buf_ref[slot, 0] + w2 * buf_ref[slot, 1]
    y_ref[...] = _rms(x3, nf_ref[...])


def _combine(ys, pos1, pos2, x, route, norm_final):
    m, d = x.shape
    bm = min(MOE_BM, m)
    return pl.pallas_call(
        _combine_kernel,
        grid_spec=pltpu.PrefetchScalarGridSpec(
            num_scalar_prefetch=2, grid=(m // bm,),
            in_specs=[pl.BlockSpec(memory_space=pl.ANY),
                      pl.BlockSpec((bm, d), lambda i, p1, p2: (i, 0)),
                      pl.BlockSpec((bm, LANES), lambda i, p1, p2: (i, 0)),
                      pl.BlockSpec((1, d), lambda i, p1, p2: (0, 0))],
            out_specs=pl.BlockSpec((bm, d), lambda i, p1, p2: (i, 0)),
            scratch_shapes=[pltpu.VMEM((2, 2, bm, d), F32), pltpu.SemaphoreType.DMA((2,))]),
        out_shape=jax.ShapeDtypeStruct((m, d), F32),
        compiler_params=_cparams("arbitrary"),
        name="moe_combine",
    )(pos1, pos2, ys, x, route, norm_final.reshape(1, d))


def _moe_sorted(xa, xb, g, wh, wl, bias, wg, wu, wd, norm_final):
    ma, mb = xa.shape[0], xb.shape[0]
    route_a, rt_a, cnt_a = _router_sorted(xa, g, wh, wl, bias, jnp.zeros((1, LANES), F32))
    route_b, rt_b, counts = _router_sorted(xb, g, wh, wl, bias, cnt_a)
    cnt = counts[0, :N_EXPERTS].astype(jnp.int32)
    padded = (cnt + MOE_BM - 1) // MOE_BM * MOE_BM
    seg_end = jnp.cumsum(padded)
    seg_start = seg_end - padded
    n_tiles_max = (2 * (ma + mb) + MOE_BM - 1) // MOE_BM + N_EXPERTS
    n_rows_max = n_tiles_max * MOE_BM
    n_rows_used = seg_end[-1:]
    n_tiles_used = n_rows_used // MOE_BM
    tile_start = jnp.arange(n_tiles_max, dtype=jnp.int32) * MOE_BM
    tile_expert = jnp.sum((seg_end[None, :] <= tile_start[:, None]).astype(jnp.int32), axis=1)
    last_expert = jnp.max(jnp.where(cnt > 0, jnp.arange(N_EXPERTS, dtype=jnp.int32), 0))
    tile_expert = jnp.minimum(tile_expert, last_expert)
    eidx = jnp.arange(N_EXPERTS, dtype=jnp.int32)
    used = cnt > 0
    later = jnp.where((eidx[None, :] > eidx[:, None]) & used[None, :], eidx[None, :], N_EXPERTS)
    next_used = jnp.min(later, axis=1)
    next_used = jnp.where(next_used == N_EXPERTS, -1, next_used)
    ordinal = jnp.cumsum(used.astype(jnp.int32)) - 1
    onehot = (tile_expert[:, None] == eidx[None, :]).astype(jnp.int32)
    next_expert = jnp.sum(onehot * next_used[None, :], axis=1)
    slot_parity = jnp.sum(onehot * ordinal[None, :], axis=1) % 2
    pa1, pa2 = _plan(rt_a, seg_start)
    pb1, pb2 = _plan(rt_b, seg_start)
    xs = _dispatch(xa, xb, g, jnp.concatenate([pa1, pb1]), jnp.concatenate([pa2, pb2]), seg_end, cnt,
                   n_rows_used, n_rows_max)
    ys = _experts_sorted(xs, tile_expert, n_tiles_used, next_expert, slot_parity, wg, wu, wd)
    return (_combine(ys, pa1, pa2, xa, route_a, norm_final),
            _combine(ys, pb1, pb2, xb, route_b, norm_final))


def _pad_cols(x, n):
    return jnp.pad(x, ((0, 0), (0, n - x.shape[1])))


def _block_diag_ones(n, blk):
    i = jnp.arange(n) // blk
    return (i[:, None] == i[None, :]).astype(BF16)


def kernel(x_prompt, x_sample, mem_prompt, cache_conv, state_shift, state_rwkv, cache_mem_k, cache_mem_v,
           norm_mix, w_in, conv_w, conv_b, conv_ln_g, conv_ln_b, shift_mu, w_decay_up, decay_bias, w_a_up,
           a_bias, w_g_up, k_k, k_a, r_k, lnx_g, lnx_b, w_out, norm_x, norm_mem, w_cq, w_ck, w_cv, w_co,
           norm_ffn, w_route_group, b_route_group, w_route_expert, b_route_expert, w_gate, w_up, w_down,
           norm_final):
    depth = w_in.shape[0]
    batch, seq, d = x_prompt.shape
    dec_batch = x_sample.shape[0]
    assert depth == 1
    assert x_sample.shape[1] == 1 and seq % CHUNK == 0 and seq >= CONV_K - 1
    cw = conv_w.shape[2]
    rw = w_decay_up.shape[2]
    heads = rw // HEAD
    shift_w = shift_mu.shape[1]
    in_w = w_in.shape[2]
    assert in_w == 2 * cw + shift_w and shift_w == 3 * rw + DECAY_LORA + AAA_LORA + GATE_LORA
    assert cw == rw and rw % LORA_PAD == 0
    in_pad = 2 * cw + 3 * rw + LORA_PAD
    qw = 3 * rw + LORA_PAD

    xp = x_prompt.reshape(batch * seq, d)
    xs = x_sample.reshape(dec_batch, d)
    outs = {k: [] for k in ("conv_p", "shift_p", "rwkv_p", "memk_p", "memv_p", "conv_s", "shift_s", "rwkv_s")}
    bd = _block_diag_ones(2 * LANES, HEAD)

    for l in range(depth):
        w_in_b = w_in.astype(BF16)
        w_out_b = w_out.astype(BF16)
        w_cq_b = w_cq.astype(BF16)
        w_ck_b = w_ck.astype(BF16)
        w_cv_b = w_cv.astype(BF16)
        w_co_b = w_co.astype(BF16)
        zeros_l = jnp.zeros((DECAY_LORA, rw), F32)
        wd_pad = jnp.concatenate([w_decay_up[l], zeros_l], axis=0).astype(BF16)
        wa_pad = jnp.concatenate([zeros_l, w_a_up[l]], axis=0).astype(BF16)
        wg_pad = jnp.pad(w_g_up[l], ((0, 2 * LANES - GATE_LORA), (0, 0))).astype(BF16)
        mu_pad = _pad_cols(shift_mu[l].reshape(1, shift_w), qw)
        vec = lambda x: x.reshape(1, rw)
        pp = (mu_pad, wd_pad, wa_pad, wg_pad, vec(decay_bias[l]), vec(a_bias[l]), vec(k_k[l]), vec(k_a[l]),
              vec(r_k[l]), bd)
        w_route = jnp.concatenate([w_route_expert[l].reshape(d, N_EXPERTS), w_route_group[l]], axis=1)
        w_route = _pad_cols(w_route, LANES)
        wr_hi = w_route.astype(BF16)
        wr_lo = (w_route - wr_hi.astype(F32)).astype(BF16)
        b_route = _pad_cols(jnp.concatenate([b_route_expert[l].reshape(1, N_EXPERTS),
                                             b_route_group[l].reshape(1, N_GROUPS)], axis=1), LANES)
        de = w_gate.shape[-1]
        wg_e = w_gate[l].reshape(N_EXPERTS, d, de)
        wu_e = w_up[l].reshape(N_EXPERTS, d, de)
        wd_e = w_down[l].reshape(N_EXPERTS, de, d)

        mem2 = mem_prompt.reshape(batch * N_MEM, d)
        mk, mk_b = _norm_matmul_heads(mem2, norm_mem[l], w_ck_b, l, X_HEADS, 256)
        mv, mv_b = _norm_matmul_heads(mem2, norm_mem[l], w_cv_b, l, X_HEADS, 256)
        proj = _norm_matmul(xp, norm_mix[l], w_in_b, l, 1024, LORA_PAD)
        assert proj.shape[1] == in_pad
        c_p, conv_new = _conv_prefill(proj, jnp.zeros((batch, CONV_K - 1, cw), F32), conv_w[l], conv_b[l],
                                      conv_ln_g[l], conv_ln_b[l], batch, seq)
        prep = _rwkv_prep_prefill(proj, jnp.zeros((batch, qw), F32), pp, batch, seq, rw)
        o_p, s_p = _rwkv_chunked(*prep[:6], jnp.zeros((batch, heads, HEAD, HEAD), F32), batch, seq)
        shift_new = proj.reshape(batch, seq, in_pad)[:, -1, 2 * cw:2 * cw + shift_w]
        xp = _mix_out(c_p, o_p, prep[6], prep[7], lnx_g[l], lnx_b[l], bd, w_out_b, l, xp)
        qx = _norm_matmul(xp, norm_x[l], w_cq_b, l, 512, d)
        ctx = _attn_prefill(qx, mk_b, mv_b, batch, seq)
        xp = _matmul_res(ctx, w_co_b, l, xp, 512)
        outs["conv_p"].append(conv_new)
        outs["shift_p"].append(shift_new)
        outs["rwkv_p"].append(s_p)
        outs["memk_p"].append(mk.reshape(batch, N_MEM, X_HEADS, d // X_HEADS))
        outs["memv_p"].append(mv.reshape(batch, N_MEM, X_HEADS, d // X_HEADS))

        proj_s = _norm_matmul(xs, norm_mix[l], w_in_b, l, 128, LORA_PAD)
        c_s, conv_new_s = _conv_decode(proj_s, cache_conv, l, conv_w[l], conv_b[l], conv_ln_g[l],
                                       conv_ln_b[l])
        prep_s = _rwkv_prep_decode(proj_s, _pad_cols(state_shift[l], qw), pp, rw)
        o_s, s_s = _rwkv_step(*prep_s[:6], state_rwkv, l)
        xs = _mix_out(c_s, o_s, prep_s[6], prep_s[7], lnx_g[l], lnx_b[l], bd, w_out_b, l, xs)
        qs = _norm_matmul(xs, norm_x[l], w_cq_b, l, 128, d)
        ctx_s = _attn_decode(qs, cache_mem_k[l], cache_mem_v[l])
        xs = _matmul_res(ctx_s, w_co_b, l, xs, 128)
        outs["conv_s"].append(conv_new_s)
        outs["shift_s"].append(proj_s[:, 2 * cw:2 * cw + shift_w])
        outs["rwkv_s"].append(s_s)

        xp, xs = _moe_sorted(xp, xs, norm_ffn[l], wr_hi, wr_lo, b_route, wg_e, wu_e, wd_e, norm_final)

    y_prompt = xp.reshape(batch, seq, d)
    y_sample = xs.reshape(dec_batch, 1, d)
    st = lambda k: jnp.stack(outs[k])
    return (y_prompt, y_sample, st("conv_p"), st("shift_p"), st("rwkv_p"), st("memk_p"), st("memv_p"),
            st("conv_s"), st("shift_s"), st("rwkv_s"))
```

```python
import functools
import math

import jax
import jax.numpy as jnp
from jax import lax
from jax.experimental import pallas as pl
from jax.experimental.pallas import tpu as pltpu

F32 = jnp.float32
BF16 = jnp.bfloat16

CONV_K = 31
HEAD = 64
PAIR = 2 * HEAD
CHUNK = 64
DECAY_LORA = 64
AAA_LORA = 64
GATE_LORA = 160
LORA_PAD = 512
N_MEM = 256
X_HEADS = 4
N_GROUPS = 4
EXP_PER_GROUP = 8
N_EXPERTS = N_GROUPS * EXP_PER_GROUP
RMS_EPS = 1e-6
LN_EPS = 1e-5
GN_EPS = 64e-5
DECAY_SCALE = math.exp(-0.5)
NEG_BIG = -1e30
MOE_BM = 256
LANES = 128
VMEM_LIMIT = 56 * 1024 * 1024


def _cparams(*sem):
    return pltpu.CompilerParams(dimension_semantics=sem, vmem_limit_bytes=VMEM_LIMIT)


def _dot(a, b):
    return jnp.dot(a, b, preferred_element_type=F32)


def _dot_nt(a, b):
    return lax.dot_general(a, b, (((1,), (1,)), ((), ())), preferred_element_type=F32)


def _split_dot(x, w_bf16):
    hi = x.astype(BF16)
    lo = (x - hi.astype(F32)).astype(BF16)
    return _dot(hi, w_bf16) + _dot(lo, w_bf16)


def _rms(x, g, eps=RMS_EPS):
    return x * lax.rsqrt(jnp.mean(x * x, axis=-1, keepdims=True) + eps) * g


def _norm_mm_kernel(x_ref, g_ref, w_ref, o_ref, xn_ref, *, n_valid):
    j = pl.program_id(1)

    @pl.when(j == 0)
    def _():
        xn_ref[...] = _rms(x_ref[...], g_ref[...]).astype(BF16)

    w = w_ref[0]
    bn = w.shape[1]
    if n_valid % bn:
        col = j * bn + lax.broadcasted_iota(jnp.int32, (1, bn), 1)
        w = jnp.where(col < n_valid, w, jnp.zeros_like(w))
    o_ref[...] = _dot(xn_ref[...], w)


def _norm_matmul(x, g, w, layer, bm, bn):
    m, k = x.shape
    n = w.shape[2]
    bm = min(bm, m)
    n_tiles = pl.cdiv(n, bn)
    return pl.pallas_call(
        functools.partial(_norm_mm_kernel, n_valid=n),
        grid=(m // bm, n_tiles),
        in_specs=[pl.BlockSpec((bm, k), lambda i, j: (i, 0)),
                  pl.BlockSpec((1, k), lambda i, j: (0, 0)),
                  pl.BlockSpec((1, k, bn), lambda i, j: (layer, 0, j))],
        out_specs=pl.BlockSpec((bm, bn), lambda i, j: (i, j)),
        out_shape=jax.ShapeDtypeStruct((m, n_tiles * bn), F32),
        scratch_shapes=[pltpu.VMEM((bm, k), BF16)],
        compiler_params=_cparams("parallel", "arbitrary"),
        name="norm_matmul",
    )(x, g.reshape(1, k), w)


def _norm_mm_heads_kernel(x_ref, g_ref, w_ref, o_ref, ob_ref):
    res = _dot(_rms(x_ref[...], g_ref[...]).astype(BF16), w_ref[0])
    ob_ref[...] = res.astype(BF16)
    dh = o_ref.shape[2]
    for h in range(o_ref.shape[1]):
        o_ref[:, h, :] = res[:, h * dh:(h + 1) * dh]


def _norm_matmul_heads(x, g, w, layer, n_heads, bm):
    m, k = x.shape
    n = w.shape[2]
    bm = min(bm, m)
    return pl.pallas_call(
        _norm_mm_heads_kernel,
        grid=(m // bm,),
        in_specs=[pl.BlockSpec((bm, k), lambda i: (i, 0)),
                  pl.BlockSpec((1, k), lambda i: (0, 0)),
                  pl.BlockSpec((1, k, n), lambda i: (layer, 0, 0))],
        out_specs=[pl.BlockSpec((bm, n_heads, n // n_heads), lambda i: (i, 0, 0)),
                   pl.BlockSpec((bm, n), lambda i: (i, 0))],
        out_shape=[jax.ShapeDtypeStruct((m, n_heads, n // n_heads), F32),
                   jax.ShapeDtypeStruct((m, n), BF16)],
        compiler_params=_cparams("parallel"),
        name="norm_matmul_heads",
    )(x, g.reshape(1, k), w)


def _mm_res_kernel(a_ref, w_ref, res_ref, o_ref):
    o_ref[...] = res_ref[...] + _dot(a_ref[...].astype(BF16), w_ref[0])


def _matmul_res(a, w, layer, res, bm):
    m, n = res.shape
    k = a.shape[1]
    bm = min(bm, m)
    return pl.pallas_call(
        _mm_res_kernel,
        grid=(m // bm,),
        in_specs=[pl.BlockSpec((bm, k), lambda i: (i, 0)),
                  pl.BlockSpec((1, k, n), lambda i: (layer, 0, 0)),
                  pl.BlockSpec((bm, n), lambda i: (i, 0))],
        out_specs=pl.BlockSpec((bm, n), lambda i: (i, 0)),
        out_shape=jax.ShapeDtypeStruct((m, n), F32),
        compiler_params=_cparams("parallel"),
        name="matmul_res",
    )(a, w, res)


def _ln_silu(cf, lg, lb):
    mu = jnp.mean(cf, axis=-1, keepdims=True)
    d = cf - mu
    var = jnp.mean(d * d, axis=-1, keepdims=True)
    y = d * lax.rsqrt(var + LN_EPS) * lg + lb
    return y * jax.nn.sigmoid(y)


def _conv_prefill_kernel(a_ref, g_ref, buf_ref, w_ref, cb_ref, lg_ref, lb_ref, c_ref, nc_ref,
                         uf_ref, cv_ref, sh_ref, *, tt, halo):
    t = pl.program_id(1)
    pad = 32 - halo

    @pl.when(t == 0)
    def _():
        uf_ref[pad:32, :] = buf_ref[0]

    @pl.when(t > 0)
    def _():
        uf_ref[pad:32, :] = uf_ref[tt + pad:tt + 32, :]

    uf_ref[32:32 + tt, :] = a_ref[...] * jax.nn.sigmoid(g_ref[...])

    for sft in range(8):
        n_rows = sh_ref.shape[1] if sft < 7 else sh_ref.shape[1] - 8
        sh_ref[sft, 0:n_rows, :] = uf_ref[pad + sft:pad + sft + n_rows, :]

    width = uf_ref.shape[1]
    rb = 64
    for r0 in range(0, tt, rb):
        for l0 in range(0, width, LANES):
            acc = jnp.zeros((rb, LANES), F32)
            for j in range(CONV_K):
                base = r0 + j - j % 8
                acc = acc + sh_ref[j % 8, base:base + rb, l0:l0 + LANES] * w_ref[j:j + 1, l0:l0 + LANES]
            cv_ref[r0:r0 + rb, l0:l0 + LANES] = acc

    c_ref[...] = _ln_silu(cv_ref[...] + cb_ref[...], lg_ref[...], lb_ref[...]).astype(c_ref.dtype)

    @pl.when(t == pl.num_programs(1) - 1)
    def _():
        nc_ref[0] = uf_ref[tt + pad:tt + 32, :]


def _conv_prefill(proj, conv_buf, conv_w, conv_b, ln_g, ln_b, batch, seq):
    cw = conv_w.shape[1]
    halo = CONV_K - 1
    tt = min(256, seq)
    nt = seq // tt
    row = lambda b, t: (b * nt + t, 0)
    vec = pl.BlockSpec((1, cw), lambda b, t: (0, 0))
    return pl.pallas_call(
        functools.partial(_conv_prefill_kernel, tt=tt, halo=halo),
        grid=(batch, nt),
        in_specs=[pl.BlockSpec((tt, cw), row),
                  pl.BlockSpec((tt, cw), lambda b, t: (b * nt + t, 1)),
                  pl.BlockSpec((1, halo, cw), lambda b, t: (b, 0, 0)),
                  pl.BlockSpec((CONV_K, cw), lambda b, t: (0, 0)),
                  vec, vec, vec],
        out_specs=[pl.BlockSpec((tt, cw), row),
                   pl.BlockSpec((1, halo, cw), lambda b, t: (b, 0, 0))],
        out_shape=[jax.ShapeDtypeStruct((batch * seq, cw), BF16),
                   jax.ShapeDtypeStruct((batch, halo, cw), F32)],
        scratch_shapes=[pltpu.VMEM((tt + 32, cw), F32), pltpu.VMEM((tt, cw), F32),
                        pltpu.VMEM((8, tt + 24, cw), F32)],
        compiler_params=_cparams("parallel", "arbitrary"),
        name="conv_prefill",
    )(proj, proj, conv_buf, conv_w, conv_b.reshape(1, cw), ln_g.reshape(1, cw), ln_b.reshape(1, cw))


def _conv_decode_kernel(a_ref, g_ref, cache_ref, w_ref, cb_ref, lg_ref, lb_ref, c_ref, nc_ref):
    halo = CONV_K - 1
    u = a_ref[...] * jax.nn.sigmoid(g_ref[...])
    acc = u * w_ref[halo:halo + 1, :]
    for j in range(halo):
        acc = acc + cache_ref[0, :, j, :] * w_ref[j:j + 1, :]
    c_ref[...] = _ln_silu(acc + cb_ref[...], lg_ref[...], lb_ref[...]).astype(c_ref.dtype)
    nc_ref[:, 0:halo - 1, :] = cache_ref[0, :, 1:halo, :]
    nc_ref[:, halo - 1, :] = u


def _conv_decode(proj, cache, layer, conv_w, conv_b, ln_g, ln_b):
    _, batch, halo, cw = cache.shape
    bb = 8
    vec = pl.BlockSpec((1, cw), lambda i: (0, 0))
    return pl.pallas_call(
        _conv_decode_kernel,
        grid=(batch // bb,),
        in_specs=[pl.BlockSpec((bb, cw), lambda i: (i, 0)),
                  pl.BlockSpec((bb, cw), lambda i: (i, 1)),
                  pl.BlockSpec((1, bb, halo, cw), lambda i: (layer, i, 0, 0)),
                  pl.BlockSpec((CONV_K, cw), lambda i: (0, 0)),
                  vec, vec, vec],
        out_specs=[pl.BlockSpec((bb, cw), lambda i: (i, 0)),
                   pl.BlockSpec((bb, halo, cw), lambda i: (i, 0, 0))],
        out_shape=[jax.ShapeDtypeStruct((batch, cw), BF16),
                   jax.ShapeDtypeStruct((batch, halo, cw), F32)],
        compiler_params=_cparams("parallel"),
        name="conv_decode",
    )(proj, proj, cache, conv_w, conv_b.reshape(1, cw), ln_g.reshape(1, cw), ln_b.reshape(1, cw))


def _head_sum(x, bd_ref):
    blk = bd_ref.shape[0]
    parts = [_split_dot(x[:, l0:l0 + blk], bd_ref[...]) for l0 in range(0, x.shape[1], blk)]
    return jnp.concatenate(parts, axis=1)


def _prep_math(q, qp, mu_ref, wd_ref, wa_ref, wg_ref, db_ref, ab_ref, kk_ref, ka_ref, rk_ref, bd_ref):
    rw = q[0].shape[1]
    offs = (0, rw, 2 * rw, 3 * rw)
    r, k, v, lo = [x + (xp - x) * mu_ref[:, o:o + x.shape[1]] for x, xp, o in zip(q, qp, offs)]
    pwa = lo[:, 0:LANES]
    pg = lo[:, LANES:3 * LANES]
    dec_in = _dot(jnp.tanh(pwa).astype(BF16), wd_ref[...])
    a_in = _dot(pwa.astype(BF16), wa_ref[...])
    gate = _dot(jax.nn.sigmoid(pg).astype(BF16), wg_ref[...])
    logw = -DECAY_SCALE * jax.nn.sigmoid(db_ref[...] + dec_in)
    a = jax.nn.sigmoid(ab_ref[...] + a_in)
    kk = k * kk_ref[...]
    kk = kk / jnp.maximum(jnp.sqrt(_head_sum(kk * kk, bd_ref)), 1e-12)
    k2 = k * (1.0 + (a - 1.0) * ka_ref[...])
    bonus = _head_sum(r * k2 * rk_ref[...], bd_ref) * v
    return r, logw, k2, v, kk, a, bonus, gate


def _prep_prefill_kernel(r_ref, k_ref, v_ref, lo_ref, sb_ref, mu_ref, wd_ref, wa_ref, wg_ref, db_ref,
                         ab_ref, kk_ref, ka_ref, rk_ref, bd_ref, *rest):
    outs = rest[:8]
    carry_ref = rest[8]
    t = pl.program_id(1)

    @pl.when(t == 0)
    def _():
        carry_ref[0:1, :] = sb_ref[0]

    q = [r_ref[...], k_ref[...], v_ref[...], lo_ref[...]]
    tt = q[0].shape[0]
    first = lax.broadcasted_iota(jnp.int32, (tt, 1), 0) == 0
    qp = []
    off = 0
    for x in q:
        w = x.shape[1]
        qp.append(jnp.where(first, carry_ref[0:1, off:off + w], pltpu.roll(x, 1, 0)))
        off += w
    off = 0
    for x in q:
        w = x.shape[1]
        carry_ref[0:1, off:off + w] = x[tt - 1:tt, :]
        off += w
    res = _prep_math(q, qp, mu_ref, wd_ref, wa_ref, wg_ref, db_ref, ab_ref, kk_ref, ka_ref, rk_ref, bd_ref)
    for o_ref, val in zip(outs, res):
        o_ref[...] = val.astype(o_ref.dtype)


def _prep_decode_kernel(r_ref, k_ref, v_ref, lo_ref, rp_ref, kp_ref, vp_ref, lop_ref, mu_ref, wd_ref,
                        wa_ref, wg_ref, db_ref, ab_ref, kk_ref, ka_ref, rk_ref, bd_ref, *outs):
    q = [r_ref[...], k_ref[...], v_ref[...], lo_ref[...]]
    qp = [rp_ref[...], kp_ref[...], vp_ref[...], lop_ref[...]]
    res = _prep_math(q, qp, mu_ref, wd_ref, wa_ref, wg_ref, db_ref, ab_ref, kk_ref, ka_ref, rk_ref, bd_ref)
    for o_ref, val in zip(outs, res):
        o_ref[...] = val


def _prep_param_specs(rw, idx):
    full = lambda shape: pl.BlockSpec(shape, idx)
    vec = full((1, rw))
    return [full((1, 3 * rw + LORA_PAD)), full((LANES, rw)), full((LANES, rw)), full((2 * LANES, rw)),
            vec, vec, vec, vec, vec, full((2 * LANES, 2 * LANES))]


def _rwkv_prep_prefill(proj, shift_buf, pp, batch, seq, rw):
    tt = min(256, seq)
    nt = seq // tt
    lora_blk = (2 * rw + 3 * rw) // LORA_PAD
    col = lambda c: (lambda b, t: (b * nt + t, c))
    qw = 3 * rw + LORA_PAD
    in_specs = [pl.BlockSpec((tt, rw), col(2)), pl.BlockSpec((tt, rw), col(3)),
                pl.BlockSpec((tt, rw), col(4)), pl.BlockSpec((tt, LORA_PAD), col(lora_blk)),
                pl.BlockSpec((1, 1, qw), lambda b, t: (b, 0, 0))]
    in_specs += _prep_param_specs(rw, lambda b, t: (0, 0))
    out_spec = pl.BlockSpec((tt, rw), col(0))
    return pl.pallas_call(
        _prep_prefill_kernel,
        grid=(batch, nt),
        in_specs=in_specs,
        out_specs=[out_spec] * 8,
        out_shape=[jax.ShapeDtypeStruct((batch * seq, rw), BF16 if i == 3 else F32) for i in range(8)],
        scratch_shapes=[pltpu.VMEM((8, qw), F32)],
        compiler_params=_cparams("parallel", "arbitrary"),
        name="rwkv_prep_prefill",
    )(proj, proj, proj, proj, shift_buf.reshape(batch, 1, qw), *pp)


def _rwkv_prep_decode(proj, shift_state, pp, rw):
    batch = proj.shape[0]
    bb = min(128, batch)
    lora_blk = (2 * rw + 3 * rw) // LORA_PAD
    col = lambda c: (lambda i: (i, c))
    in_specs = [pl.BlockSpec((bb, rw), col(2)), pl.BlockSpec((bb, rw), col(3)),
                pl.BlockSpec((bb, rw), col(4)), pl.BlockSpec((bb, LORA_PAD), col(lora_blk)),
                pl.BlockSpec((bb, rw), col(0)), pl.BlockSpec((bb, rw), col(1)),
                pl.BlockSpec((bb, rw), col(2)), pl.BlockSpec((bb, LORA_PAD), col(3 * rw // LORA_PAD))]
    in_specs += _prep_param_specs(rw, lambda i: (0, 0))
    return pl.pallas_call(
        _prep_decode_kernel,
        grid=(batch // bb,),
        in_specs=in_specs,
        out_specs=[pl.BlockSpec((bb, rw), col(0))] * 8,
        out_shape=[jax.ShapeDtypeStruct((batch, rw), F32)] * 8,
        compiler_params=_cparams("parallel"),
        name="rwkv_prep_decode",
    )(proj, proj, proj, proj, shift_state, shift_state, shift_state, shift_state, *pp)


def _stack2(x, smask):
    return jnp.where(smask, jnp.concatenate([x, x], axis=0), 0.0)


def _rwkv_chunk_kernel(r_ref, lw_ref, k_ref, v_ref, kk_ref, a_ref, s0_ref, o_ref, so_ref, s_ref):
    c = pl.program_id(1)
    cs = r_ref.shape[0]
    n_pairs = r_ref.shape[1] // PAIR
    two = 2 * cs

    @pl.when(c == 0)
    def _():
        z = jnp.zeros((HEAD, HEAD), F32)
        for p in range(n_pairs):
            top = jnp.concatenate([s0_ref[0, 2 * p], z], axis=1)
            bot = jnp.concatenate([z, s0_ref[0, 2 * p + 1]], axis=1)
            s_ref[p] = jnp.concatenate([top, bot], axis=0)

    ri = lax.broadcasted_iota(jnp.int32, (two, two), 0)
    ci = lax.broadcasted_iota(jnp.int32, (two, two), 1)
    strict = ci < ri
    incl = ci <= ri
    eye = (ci == ri).astype(F32)
    smask = (lax.broadcasted_iota(jnp.int32, (two, PAIR), 0) < cs) == (
        lax.broadcasted_iota(jnp.int32, (two, PAIR), 1) < HEAD)
    tri = (lax.broadcasted_iota(jnp.int32, (cs, cs), 1)
           <= lax.broadcasted_iota(jnp.int32, (cs, cs), 0)).astype(BF16)

    lw_all = lw_ref[...]
    lw_hi = lw_all.astype(BF16)
    lw_lo = (lw_all - lw_hi.astype(F32)).astype(BF16)
    cum_all = _dot(tri, lw_hi) + _dot(tri, lw_lo)

    pairs = range(n_pairs)
    cat = jnp.concatenate
    prep = []
    for p in pairs:
        sl = slice(p * PAIR, (p + 1) * PAIR)
        lw = lw_all[:, sl]
        cum = cum_all[:, sl]
        tot = cum[cs - 1:cs, :]
        g_inv = jnp.exp(-cum)
        g_end = jnp.exp(tot - cum)
        kk = kk_ref[:, sl]
        k2 = k_ref[:, sl]
        bb = kk * a_ref[:, sl]
        prep.append(dict(
            g_tot=jnp.exp(tot),
            a_b=_stack2(kk * jnp.exp(cum - lw), smask).astype(BF16),
            r_s=_stack2(r_ref[:, sl] * jnp.exp(cum), smask),
            bk=cat([_stack2(bb * g_inv, smask), _stack2(k2 * g_inv, smask)], axis=0).astype(BF16),
            v_s=_stack2(v_ref[:, sl].astype(F32), smask),
            bg_s=_stack2(bb * g_end, smask).astype(BF16),
            kg_s=_stack2(k2 * g_end, smask).astype(BF16)))
    a_b = [q["a_b"] for q in prep]
    r_s = [q["r_s"] for q in prep]
    v_s = [q["v_s"] for q in prep]
    v_b = [x.astype(BF16) for x in v_s]

    gram = [_dot_nt(cat([a_b[p], r_s[p].astype(BF16)], axis=0), prep[p]["bk"]) for p in pairs]
    l_ab = [jnp.where(strict, g[0:two, 0:two], 0.0) for g in gram]
    l_ak = [jnp.where(strict, g[0:two, two:], 0.0).astype(BF16) for g in gram]
    m_rb = [jnp.where(incl, g[two:, 0:two], 0.0).astype(BF16) for g in gram]
    m_rk = [jnp.where(incl, g[two:, two:], 0.0).astype(BF16) for g in gram]

    tm = [eye - x for x in l_ab]
    pw = l_ab
    n = 1
    while 2 * n < cs:
        pw = [_dot(x.astype(BF16), x.astype(BF16)) for x in pw]
        tm = [t + _dot(t.astype(BF16), x.astype(BF16)) for t, x in zip(tm, pw)]
        n *= 2
    tm_b = [t.astype(BF16) for t in tm]

    w1 = [_dot(l_ak[p], v_b[p]) for p in pairs]
    ua = [_dot(tm_b[p], cat([w1[p].astype(BF16), a_b[p]], axis=1)) for p in pairs]
    ua_b = [x.astype(BF16) for x in ua]
    mrb_ua = [_dot(m_rb[p], ua_b[p]) for p in pairs]
    o0 = [_dot(m_rk[p], v_b[p]) - mrb_ua[p][:, 0:PAIR] for p in pairs]
    rt = [(r_s[p] - mrb_ua[p][:, PAIR:]).astype(BF16) for p in pairs]

    s_old = [s_ref[p] for p in pairs]
    s_b = [x.astype(BF16) for x in s_old]
    ua_t = [cat([x[:, 0:PAIR].T, x[:, PAIR:].T], axis=0).astype(BF16) for x in ua]
    uat_bg = [_dot(ua_t[p], prep[p]["bg_s"]) for p in pairs]
    vt_kg = [_dot(v_s[p].T.astype(BF16), prep[p]["kg_s"]) for p in pairs]
    s_new = [s_old[p] * prep[p]["g_tot"] - _dot(s_b[p], uat_bg[p][PAIR:, :].astype(BF16))
             + vt_kg[p] - uat_bg[p][0:PAIR, :] for p in pairs]
    o_st = [o0[p] + _dot_nt(rt[p], s_b[p]) for p in pairs]

    o_ref[...] = cat([x[0:cs, :] + x[cs:two, :] for x in o_st], axis=1)
    s_ref[...] = jnp.stack(s_new, axis=0)

    @pl.when(c == pl.num_programs(1) - 1)
    def _():
        for p in range(n_pairs):
            s = s_ref[p]
            so_ref[0, 2 * p] = s[0:HEAD, 0:HEAD]
            so_ref[0, 2 * p + 1] = s[HEAD:PAIR, HEAD:PAIR]


def _rwkv_chunked(r, logw, k2, v, kk, a, s0, batch, seq):
    rw = r.shape[1]
    nc = seq // CHUNK
    heads = rw // HEAD
    row = pl.BlockSpec((CHUNK, rw), lambda b, c: (b * nc + c, 0))
    st = pl.BlockSpec((1, heads, HEAD, HEAD), lambda b, c: (b, 0, 0, 0))
    return pl.pallas_call(
        _rwkv_chunk_kernel,
        grid=(batch, nc),
        in_specs=[row] * 6 + [st],
        out_specs=[row, st],
        out_shape=[jax.ShapeDtypeStruct((batch * seq, rw), F32),
                   jax.ShapeDtypeStruct((batch, heads, HEAD, HEAD), F32)],
        scratch_shapes=[pltpu.VMEM((rw // PAIR, PAIR, PAIR), F32)],
        compiler_params=_cparams("parallel", "arbitrary"),
        name="rwkv_chunked",
    )(r, logw, k2, v, kk, a, s0)


def _rwkv_step_kernel(r_ref, lw_ref, k_ref, v_ref, kk_ref, a_ref, s_ref, o_ref, so_ref, *, heads):
    bb = s_ref.shape[1]
    eye = (lax.broadcasted_iota(jnp.int32, (HEAD, HEAD), 0)
           == lax.broadcasted_iota(jnp.int32, (HEAD, HEAD), 1))

    def body(bi, carry):
        hs = range(heads)
        rows = [pl.ds(bi * heads + h, 1) for h in hs]
        kk = [kk_ref[r, :] for r in rows]
        s = [s_ref[0, bi, h].astype(F32) for h in hs]
        sa = [jnp.sum(s[h] * kk[h], axis=1, keepdims=True) for h in hs]
        v_col = [jnp.sum(jnp.where(eye, v_ref[rows[h], :], 0.0), axis=1, keepdims=True) for h in hs]
        s_new = [s[h] * jnp.exp(lw_ref[rows[h], :]) - sa[h] * (kk[h] * a_ref[rows[h], :])
                 + v_col[h] * k_ref[rows[h], :] for h in hs]
        for h in hs:
            so_ref[bi, h] = s_new[h]
        o_col = [jnp.sum(s_new[h] * r_ref[rows[h], :], axis=1, keepdims=True) for h in hs]
        o_row = [jnp.sum(jnp.where(eye, o_col[h], 0.0), axis=0, keepdims=True) for h in hs]
        o_ref[pl.ds(pl.multiple_of(bi * heads, heads), heads), :] = jnp.concatenate(o_row, axis=0)
        return carry

    lax.fori_loop(0, bb, body, 0)


def _rwkv_step(r, logw, k2, v, kk, a, state, layer):
    batch, rw = r.shape
    heads = rw // HEAD
    bb = 8
    flat = lambda x: x.reshape(batch * heads, HEAD)
    row = pl.BlockSpec((bb * heads, HEAD), lambda i: (i, 0))
    st_in = pl.BlockSpec((1, bb, heads, HEAD, HEAD), lambda i: (layer, i, 0, 0, 0))
    st_out = pl.BlockSpec((bb, heads, HEAD, HEAD), lambda i: (i, 0, 0, 0))
    o, s_new = pl.pallas_call(
        functools.partial(_rwkv_step_kernel, heads=heads),
        grid=(batch // bb,),
        in_specs=[row] * 6 + [st_in],
        out_specs=[row, st_out],
        out_shape=[jax.ShapeDtypeStruct((batch * heads, HEAD), F32),
                   jax.ShapeDtypeStruct(state.shape[1:], F32)],
        compiler_params=_cparams("parallel"),
        name="rwkv_step",
    )(flat(r), flat(logw), flat(k2), flat(v), flat(kk), flat(a), state)
    return o.reshape(batch, rw), s_new


def _mix_out_kernel(c_ref, o_ref, bonus_ref, gate_ref, lg_ref, lb_ref, bd_ref, wc_ref, wo_ref, x_ref, y_ref):
    o = o_ref[...]
    mu = _head_sum(o, bd_ref) * (1.0 / HEAD)
    d = o - mu
    var = _head_sum(d * d, bd_ref) * (1.0 / HEAD)
    y = d * lax.rsqrt(var + GN_EPS) * lg_ref[...] + lb_ref[...]
    om = ((y + bonus_ref[...]) * gate_ref[...]).astype(BF16)
    y_ref[...] = x_ref[...] + _dot(c_ref[...], wc_ref[0]) + _dot(om, wo_ref[0])


def _mix_out(c, o, bonus, gate, lnx_g, lnx_b, bd, w_out, layer, x):
    m, d = x.shape
    cw, rw = c.shape[1], o.shape[1]
    assert cw == rw and w_out.shape[1] == cw + rw
    bm = min(512, m)
    row = lambda w: pl.BlockSpec((bm, w), lambda i: (i, 0))
    vec = pl.BlockSpec((1, rw), lambda i: (0, 0))
    return pl.pallas_call(
        _mix_out_kernel,
        grid=(m // bm,),
        in_specs=[row(cw), row(rw), row(rw), row(rw), vec, vec, pl.BlockSpec(bd.shape, lambda i: (0, 0)),
                  pl.BlockSpec((1, cw, d), lambda i: (layer, 0, 0)),
                  pl.BlockSpec((1, rw, d), lambda i: (layer, 1, 0)),
                  row(d)],
        out_specs=row(d),
        out_shape=jax.ShapeDtypeStruct((m, d), F32),
        compiler_params=_cparams("parallel"),
        name="mix_out",
    )(c, o, bonus, gate, lnx_g.reshape(1, rw), lnx_b.reshape(1, rw), bd, w_out, w_out, x)


def _attn_prefill_kernel(q_ref, k_ref, v_ref, o_ref, *, n_heads):
    d = q_ref.shape[1] // n_heads
    scale = d ** -0.5
    for h in range(n_heads):
        sl = slice(h * d, (h + 1) * d)
        s = _dot_nt(q_ref[:, sl].astype(BF16), k_ref[:, sl]) * scale
        p = jnp.exp(s - jnp.max(s, axis=-1, keepdims=True))
        att = p / jnp.sum(p, axis=-1, keepdims=True)
        o_ref[:, sl] = _dot(att.astype(BF16), v_ref[:, sl]).astype(o_ref.dtype)


def _attn_prefill(q, mem_k, mem_v, batch, seq):
    d = q.shape[1]
    tt = min(512, seq)
    nt = seq // tt
    kv = pl.BlockSpec((N_MEM, d), lambda b, t: (b, 0))
    row = pl.BlockSpec((tt, d), lambda b, t: (b * nt + t, 0))
    return pl.pallas_call(
        functools.partial(_attn_prefill_kernel, n_heads=X_HEADS),
        grid=(batch, nt),
        in_specs=[row, kv, kv],
        out_specs=row,
        out_shape=jax.ShapeDtypeStruct((batch * seq, d), BF16),
        compiler_params=_cparams("parallel", "arbitrary"),
        name="attn_prefill",
    )(q, mem_k, mem_v)


def _decode_attn_rows(qs, k_at, v_at, n_heads):
    d = qs[0].shape[1] // n_heads
    scale = d ** -0.5
    ids = [(i, h) for i in range(len(qs)) for h in range(n_heads)]
    s = [jnp.sum(k_at(i, h) * qs[i][:, h * d:(h + 1) * d], axis=1, keepdims=True) * scale for i, h in ids]
    p = [jnp.exp(x - jnp.max(x, axis=0, keepdims=True)) for x in s]
    att = [x / jnp.sum(x, axis=0, keepdims=True) for x in p]
    ctx = [jnp.sum(a * v_at(i, h), axis=0, keepdims=True) for a, (i, h) in zip(att, ids)]
    return [ctx[i * n_heads:(i + 1) * n_heads] for i in range(len(qs))]


def _attn_decode_kernel(q_ref, k_ref, v_ref, o_ref):
    n_heads, d = k_ref.shape[2], k_ref.shape[3]
    bb = q_ref.shape[0]
    ctx = _decode_attn_rows([q_ref[bi] for bi in range(bb)], lambda i, h: k_ref[i, :, h, :],
                            lambda i, h: v_ref[i, :, h, :], n_heads)
    for bi in range(bb):
        for h in range(n_heads):
            o_ref[bi, :, h * d:(h + 1) * d] = ctx[bi][h]


def _attn_decode(q, cache_k, cache_v):
    batch, d = q.shape
    bb = 2
    kv = pl.BlockSpec((bb, N_MEM, X_HEADS, d // X_HEADS), lambda i: (i, 0, 0, 0))
    row = pl.BlockSpec((bb, 1, d), lambda i: (i, 0, 0))
    out = pl.pallas_call(
        _attn_decode_kernel,
        grid=(batch // bb,),
        in_specs=[row, kv, kv],
        out_specs=row,
        out_shape=jax.ShapeDtypeStruct((batch, 1, d), F32),
        compiler_params=_cparams("parallel"),
        name="attn_decode",
    )(q.reshape(batch, 1, d), cache_k, cache_v)
    return out.reshape(batch, d)


def _route(x_ref, g_ref, wh_ref, wl_ref, b_ref):
    h = _rms(x_ref[...], g_ref[...])
    hh = h.astype(BF16)
    hl = (h - hh.astype(F32)).astype(BF16)
    logits = _dot(hh, wh_ref[...]) + _dot(hl, wh_ref[...]) + _dot(hh, wl_ref[...]) + b_ref[...]
    lane = lax.broadcasted_iota(jnp.int32, (1, LANES), 1).astype(F32)
    is_g = (lane >= N_EXPERTS) & (lane < N_EXPERTS + N_GROUPS)
    lgm = jnp.where(is_g, logits, NEG_BIG)
    gmax = jnp.max(lgm, axis=1, keepdims=True)
    gsum = jnp.sum(jnp.where(is_g, jnp.exp(lgm - gmax), 0.0), axis=1, keepdims=True)
    g_val = 1.0 / gsum
    g_idx = jnp.min(jnp.where(is_g & (lgm == gmax), lane - N_EXPERTS, 1e9), axis=1, keepdims=True)
    in_grp = (lane < N_EXPERTS) & (jnp.floor(lane * (1.0 / EXP_PER_GROUP)) == g_idx)
    le = jnp.where(in_grp, logits, NEG_BIG)
    m1 = jnp.max(le, axis=1, keepdims=True)
    i1 = jnp.min(jnp.where(in_grp & (le == m1), lane, 1e9), axis=1, keepdims=True)
    rest = in_grp & (lane != i1)
    le2 = jnp.where(rest, logits, NEG_BIG)
    m2 = jnp.max(le2, axis=1, keepdims=True)
    i2 = jnp.min(jnp.where(rest & (le2 == m2), lane, 1e9), axis=1, keepdims=True)
    e2 = jnp.exp(m2 - m1)
    den = 1.0 + e2
    w1 = (1.0 / den) * g_val
    w2 = (e2 / den) * g_val
    return h, lane, i1, i2, w1, w2


def _router_sorted_kernel(x_ref, g_ref, wh_ref, wl_ref, b_ref, init_ref, route_ref, rt_ref, cnt_ref, run_ref):
    @pl.when(pl.program_id(0) == 0)
    def _():
        run_ref[0:1, :] = init_ref[...]

    h, lane, i1, i2, w1, w2 = _route(x_ref, g_ref, wh_ref, wl_ref, b_ref)
    bm = h.shape[0]
    oh1 = lane == i1
    oh2 = lane == i2
    sel = (oh1 | oh2).astype(BF16)
    before = (lax.broadcasted_iota(jnp.int32, (bm, bm), 1)
              < lax.broadcasted_iota(jnp.int32, (bm, bm), 0)).astype(BF16)
    base = run_ref[0:1, :] + _dot(before, sel)
    rank1 = jnp.sum(jnp.where(oh1, base, 0.0), axis=1, keepdims=True)
    rank2 = jnp.sum(jnp.where(oh2, base, 0.0), axis=1, keepdims=True)
    total = run_ref[0:1, :] + jnp.sum(sel.astype(F32), axis=0, keepdims=True)
    run_ref[0:1, :] = total
    cnt_ref[...] = total
    route = jnp.zeros((bm, LANES), F32)
    for idx, val in enumerate((i1, i2, w1, w2, rank1, rank2)):
        route = jnp.where(lane == idx, val, route)
    route_ref[...] = route
    for r0 in range(0, bm, LANES):
        rt_ref[:, r0:r0 + LANES] = route[r0:r0 + LANES, :].T[0:8, :]


def _router_sorted(x, g, wh, wl, bias, init_counts):
    m, d = x.shape
    bm = min(512, m)
    return pl.pallas_call(
        _router_sorted_kernel,
        grid=(m // bm,),
        in_specs=[pl.BlockSpec((bm, d), lambda i: (i, 0)),
                  pl.BlockSpec((1, d), lambda i: (0, 0)),
                  pl.BlockSpec((d, LANES), lambda i: (0, 0)),
                  pl.BlockSpec((d, LANES), lambda i: (0, 0)),
                  pl.BlockSpec((1, LANES), lambda i: (0, 0)),
                  pl.BlockSpec((1, LANES), lambda i: (0, 0))],
        out_specs=[pl.BlockSpec((bm, LANES), lambda i: (i, 0)),
                   pl.BlockSpec((8, bm), lambda i: (0, i)),
                   pl.BlockSpec((1, LANES), lambda i: (0, 0))],
        out_shape=[jax.ShapeDtypeStruct((m, LANES), F32),
                   jax.ShapeDtypeStruct((8, m), F32),
                   jax.ShapeDtypeStruct((1, LANES), F32)],
        scratch_shapes=[pltpu.VMEM((8, LANES), F32)],
        compiler_params=_cparams("arbitrary"),
        name="moe_router_sorted",
    )(x, g.reshape(1, d), wh, wl, bias, init_counts)


def _plan_kernel(seg_ref, rt_ref, pos_ref):
    rt = rt_ref[...]
    rows = []
    for e_row, r_row in ((0, 4), (1, 5)):
        e = rt[e_row:e_row + 1, :]
        start = jnp.zeros_like(e)
        for k in range(N_EXPERTS):
            start = jnp.where(e == k, seg_ref[k].astype(F32), start)
        rows.append((start + rt[r_row:r_row + 1, :]).astype(jnp.int32))
    pos_ref[...] = jnp.concatenate(rows + [jnp.zeros((6, rt.shape[1]), jnp.int32)], axis=0)


def _plan(route_t, seg_start):
    m = route_t.shape[1]
    bt = min(2048, m)
    pos = pl.pallas_call(
        _plan_kernel,
        grid_spec=pltpu.PrefetchScalarGridSpec(
            num_scalar_prefetch=1, grid=(m // bt,),
            in_specs=[pl.BlockSpec((8, bt), lambda i, seg: (0, i))],
            out_specs=pl.BlockSpec((8, bt), lambda i, seg: (0, i))),
        out_shape=jax.ShapeDtypeStruct((8, m), jnp.int32),
        compiler_params=_cparams("arbitrary"),
        name="moe_plan",
    )(seg_start, route_t)
    return pos[0], pos[1]


def _row_copy(src_hbm, src_row, dst, dst_row, sem):
    return pltpu.make_async_copy(src_hbm.at[pl.ds(src_row, 1)], dst.at[pl.ds(dst_row, 1)], sem)


def _dispatch_kernel(p1_ref, p2_ref, seg_ref, cnt_ref, nrow_ref, xa_ref, xb_ref, g_ref, xs_hbm, h_ref,
                     zero_ref, sem, zsem, *, n_a):
    i = pl.program_id(0)
    n = pl.num_programs(0)
    bm = xa_ref.shape[0]
    n_max = xs_hbm.shape[0] // MOE_BM

    def zero_copy(row0):
        return pltpu.make_async_copy(zero_ref, xs_hbm.at[pl.ds(pl.multiple_of(row0, MOE_BM), MOE_BM)], zsem)

    @pl.when(i == 0)
    def _():
        zero_ref[...] = jnp.zeros_like(zero_ref)
        first_free = nrow_ref[0] // MOE_BM

        def tail_start(c, carry):
            zero_copy(c * MOE_BM).start()
            return carry

        def tail_wait(c, carry):
            zero_copy(c * MOE_BM).wait()
            return carry

        for e in range(N_EXPERTS):
            @pl.when(cnt_ref[e] > 0)
            def _():
                zero_copy(seg_ref[e] - MOE_BM).start()

        lax.fori_loop(first_free, n_max, tail_start, 0)
        for e in range(N_EXPERTS):
            @pl.when(cnt_ref[e] > 0)
            def _():
                zero_copy(seg_ref[e] - MOE_BM).wait()

        lax.fori_loop(first_free, n_max, tail_wait, 0)

    bm_b = xb_ref.shape[0]
    tok_b0 = n_a * bm

    def wait_rows(slot, rows):
        for _ in range(2):
            pltpu.make_async_copy(h_ref.at[slot, pl.ds(0, rows)], xs_hbm.at[pl.ds(0, rows)],
                                  sem.at[slot]).wait()

    def scatter_rows(slot, rows, tok0):
        def body(r, carry):
            src = h_ref.at[slot, pl.ds(r, 1)]
            pltpu.make_async_copy(src, xs_hbm.at[pl.ds(p1_ref[tok0 + r], 1)], sem.at[slot]).start(priority=0)
            pltpu.make_async_copy(src, xs_hbm.at[pl.ds(p2_ref[tok0 + r], 1)], sem.at[slot]).start(priority=1)
            return carry

        lax.fori_loop(0, rows, body, 0, unroll=8)

    slot = i % 2

    @pl.when((i > 0) & (i - 1 < n_a))
    def _():
        wait_rows(1 - slot, bm)

    @pl.when(i - 1 >= n_a)
    def _():
        wait_rows(1 - slot, bm_b)

    @pl.when(i < n_a)
    def _():
        h_ref[slot, 0:bm, :] = _rms(xa_ref[...], g_ref[...])
        scatter_rows(slot, bm, i * bm)

    @pl.when(i >= n_a)
    def _():
        h_ref[slot, 0:bm_b, :] = _rms(xb_ref[...], g_ref[...])
        scatter_rows(slot, bm_b, tok_b0 + (i - n_a) * bm_b)

    @pl.when(i == n - 1)
    def _():
        wait_rows(slot, bm_b)


def _dispatch(xa, xb, g, pos1, pos2, seg_end, cnt, n_rows_used, n_rows_max):
    (ma, d), mb = xa.shape, xb.shape[0]
    bm_a, bm_b = min(MOE_BM, ma), min(MOE_BM, mb)
    assert ma % bm_a == 0 and mb % bm_b == 0 and bm_b <= bm_a
    n_a, n_b = ma // bm_a, mb // bm_b
    return pl.pallas_call(
        functools.partial(_dispatch_kernel, n_a=n_a),
        grid_spec=pltpu.PrefetchScalarGridSpec(
            num_scalar_prefetch=5, grid=(n_a + n_b,),
            in_specs=[pl.BlockSpec((bm_a, d), lambda i, *_: (jnp.minimum(i, n_a - 1), 0)),
                      pl.BlockSpec((bm_b, d), lambda i, *_: (jnp.maximum(i - n_a, 0), 0)),
                      pl.BlockSpec((1, d), lambda i, *_: (0, 0))],
            out_specs=pl.BlockSpec(memory_space=pl.ANY),
            scratch_shapes=[pltpu.VMEM((2, bm_a, d), F32), pltpu.VMEM((MOE_BM, d), F32),
                            pltpu.SemaphoreType.DMA((2,)), pltpu.SemaphoreType.DMA(())]),
        out_shape=jax.ShapeDtypeStruct((n_rows_max, d), F32),
        compiler_params=_cparams("arbitrary"),
        name="moe_dispatch",
    )(pos1, pos2, seg_end, cnt, n_rows_used, xa, xb, g.reshape(1, d))


def _experts_sorted_kernel(te_ref, nt_ref, nxt_ref, par_ref, xs_ref, wg_hbm, wu_hbm, wd_hbm, ys_ref,
                           wgf_ref, wuf_ref, wdf_ref, wgb_ref, wub_ref, wdb_ref, sem):
    j = pl.program_id(0)
    prev = te_ref[jnp.maximum(j, 1) - 1]

    def copies(e, slot):
        return [pltpu.make_async_copy(hbm.at[e], buf.at[slot], sem.at[slot])
                for hbm, buf in ((wg_hbm, wgf_ref), (wu_hbm, wuf_ref), (wd_hbm, wdf_ref))]

    @pl.when(j == 0)
    def _():
        for c in copies(te_ref[0], par_ref[0]):
            c.start()

    @pl.when((j < nt_ref[0]) & ((j == 0) | (te_ref[j] != prev)))
    def _():
        slot = par_ref[j]
        for c in copies(te_ref[j], slot):
            c.wait()

        @pl.when(nxt_ref[j] >= 0)
        def _():
            for c in copies(nxt_ref[j], 1 - slot):
                c.start()

        wgb_ref[...] = wgf_ref[slot].astype(BF16)
        wub_ref[...] = wuf_ref[slot].astype(BF16)
        wdb_ref[...] = wdf_ref[slot].astype(BF16)

    @pl.when(j < nt_ref[0])
    def _():
        x = xs_ref[...].astype(BF16)
        hg = _dot(x, wgb_ref[...])
        hu = _dot(x, wub_ref[...])
        act = hg * jax.nn.sigmoid(hg) * hu
        ys_ref[...] = _dot(act.astype(BF16), wdb_ref[...])

    @pl.when(j >= nt_ref[0])
    def _():
        ys_ref[...] = jnp.zeros_like(ys_ref)


def _experts_sorted(xs, tile_expert, n_tiles_used, next_expert, slot_parity, wg, wu, wd):
    n_rows, d = xs.shape
    de = wg.shape[2]
    n_tiles = n_rows // MOE_BM
    row_in = lambda j, te, nt, *_: (jnp.minimum(j, nt[0] - 1), 0)
    hbm = pl.BlockSpec(memory_space=pl.ANY)
    return pl.pallas_call(
        _experts_sorted_kernel,
        grid_spec=pltpu.PrefetchScalarGridSpec(
            num_scalar_prefetch=4, grid=(n_tiles,),
            in_specs=[pl.BlockSpec((MOE_BM, d), row_in), hbm, hbm, hbm],
            out_specs=pl.BlockSpec((MOE_BM, d), lambda j, *_: (j, 0)),
            scratch_shapes=[pltpu.VMEM((2, d, de), F32), pltpu.VMEM((2, d, de), F32),
                            pltpu.VMEM((2, de, d), F32),
                            pltpu.VMEM((d, de), BF16), pltpu.VMEM((d, de), BF16), pltpu.VMEM((de, d), BF16),
                            pltpu.SemaphoreType.DMA((2,))]),
        out_shape=jax.ShapeDtypeStruct((n_rows, d), F32),
        compiler_params=_cparams("arbitrary"),
        name="moe_experts_sorted",
    )(tile_expert, n_tiles_used, next_expert, slot_parity, xs, wg, wu, wd)


def _combine_kernel(p1_ref, p2_ref, ys_hbm, x_ref, route_ref, nf_ref, y_ref, buf_ref, sem):
    i = pl.program_id(0)
    n = pl.num_programs(0)
    bm = x_ref.shape[0]

    def issue(tile, slot):
        def body(r, carry):
            t = tile * bm + r
            _row_copy(ys_hbm, p1_ref[t], buf_ref.at[slot, 0], r, sem.at[slot]).start(priority=0)
            _row_copy(ys_hbm, p2_ref[t], buf_ref.at[slot, 1], r, sem.at[slot]).start(priority=1)
            return carry

        lax.fori_loop(0, bm, body, 0, unroll=8)

    @pl.when(i == 0)
    def _():
        issue(0, 0)

    @pl.when(i + 1 < n)
    def _():
        issue(i + 1, (i + 1) % 2)

    slot = i % 2
    for k in range(2):
        pltpu.make_async_copy(ys_hbm.at[pl.ds(0, bm)], buf_ref.at[slot, k], sem.at[slot]).wait()
    lane = lax.broadcasted_iota(jnp.int32, (1, LANES), 1)
    route = route_ref[...]
    w1 = jnp.sum(jnp.where(lane == 2, route, 0.0), axis=1, keepdims=True)
    w2 = jnp.sum(jnp.where(lane == 3, route, 0.0), axis=1, keepdims=True)
    x3 = x_ref[...] + w1 * buf_ref[slot, 0] + w2 * buf_ref[slot, 1]
    y_ref[...] = _rms(x3, nf_ref[...])


def _combine(ys, pos1, pos2, x, route, norm_final):
    m, d = x.shape
    bm = min(MOE_BM, m)
    return pl.pallas_call(
        _combine_kernel,
        grid_spec=pltpu.PrefetchScalarGridSpec(
            num_scalar_prefetch=2, grid=(m // bm,),
            in_specs=[pl.BlockSpec(memory_space=pl.ANY),
                      pl.BlockSpec((bm, d), lambda i, p1, p2: (i, 0)),
                      pl.BlockSpec((bm, LANES), lambda i, p1, p2: (i, 0)),
                      pl.BlockSpec((1, d), lambda i, p1, p2: (0, 0))],
            out_specs=pl.BlockSpec((bm, d), lambda i, p1, p2: (i, 0)),
            scratch_shapes=[pltpu.VMEM((2, 2, bm, d), F32), pltpu.SemaphoreType.DMA((2,))]),
        out_shape=jax.ShapeDtypeStruct((m, d), F32),
        compiler_params=_cparams("arbitrary"),
        name="moe_combine",
    )(pos1, pos2, ys, x, route, norm_final.reshape(1, d))


def _moe_sorted(xa, xb, g, wh, wl, bias, wg, wu, wd, norm_final):
    ma, mb = xa.shape[0], xb.shape[0]
    route_a, rt_a, cnt_a = _router_sorted(xa, g, wh, wl, bias, jnp.zeros((1, LANES), F32))
    route_b, rt_b, counts = _router_sorted(xb, g, wh, wl, bias, cnt_a)
    cnt = counts[0, :N_EXPERTS].astype(jnp.int32)
    padded = (cnt + MOE_BM - 1) // MOE_BM * MOE_BM
    seg_end = jnp.cumsum(padded)
    seg_start = seg_end - padded
    n_tiles_max = (2 * (ma + mb) + MOE_BM - 1) // MOE_BM + N_EXPERTS
    n_rows_max = n_tiles_max * MOE_BM
    n_rows_used = seg_end[-1:]
    n_tiles_used = n_rows_used // MOE_BM
    tile_start = jnp.arange(n_tiles_max, dtype=jnp.int32) * MOE_BM
    tile_expert = jnp.sum((seg_end[None, :] <= tile_start[:, None]).astype(jnp.int32), axis=1)
    last_expert = jnp.max(jnp.where(cnt > 0, jnp.arange(N_EXPERTS, dtype=jnp.int32), 0))
    tile_expert = jnp.minimum(tile_expert, last_expert)
    eidx = jnp.arange(N_EXPERTS, dtype=jnp.int32)
    used = cnt > 0
    later = jnp.where((eidx[None, :] > eidx[:, None]) & used[None, :], eidx[None, :], N_EXPERTS)
    next_used = jnp.min(later, axis=1)
    next_used = jnp.where(next_used == N_EXPERTS, -1, next_used)
    ordinal = jnp.cumsum(used.astype(jnp.int32)) - 1
    onehot = (tile_expert[:, None] == eidx[None, :]).astype(jnp.int32)
    next_expert = jnp.sum(onehot * next_used[None, :], axis=1)
    slot_parity = jnp.sum(onehot * ordinal[None, :], axis=1) % 2
    pa1, pa2 = _plan(rt_a, seg_start)
    pb1, pb2 = _plan(rt_b, seg_start)
    xs = _dispatch(xa, xb, g, jnp.concatenate([pa1, pb1]), jnp.concatenate([pa2, pb2]), seg_end, cnt,
                   n_rows_used, n_rows_max)
    ys = _experts_sorted(xs, tile_expert, n_tiles_used, next_expert, slot_parity, wg, wu, wd)
    return (_combine(ys, pa1, pa2, xa, route_a, norm_final),
            _combine(ys, pb1, pb2, xb, route_b, norm_final))


def _pad_cols(x, n):
    return jnp.pad(x, ((0, 0), (0, n - x.shape[1])))


def _block_diag_ones(n, blk):
    i = jnp.arange(n) // blk
    return (i[:, None] == i[None, :]).astype(BF16)


def kernel(x_prompt, x_sample, mem_prompt, cache_conv, state_shift, state_rwkv, cache_mem_k, cache_mem_v,
           norm_mix, w_in, conv_w, conv_b, conv_ln_g, conv_ln_b, shift_mu, w_decay_up, decay_bias, w_a_up,
           a_bias, w_g_up, k_k, k_a, r_k, lnx_g, lnx_b, w_out, norm_x, norm_mem, w_cq, w_ck, w_cv, w_co,
           norm_ffn, w_route_group, b_route_group, w_route_expert, b_route_expert, w_gate, w_up, w_down,
           norm_final):
    depth = w_in.shape[0]
    batch, seq, d = x_prompt.shape
    dec_batch = x_sample.shape[0]
    assert depth == 1
    assert x_sample.shape[1] == 1 and seq % CHUNK == 0 and seq >= CONV_K - 1
    cw = conv_w.shape[2]
    rw = w_decay_up.shape[2]
    heads = rw // HEAD
    shift_w = shift_mu.shape[1]
    in_w = w_in.shape[2]
    assert in_w == 2 * cw + shift_w and shift_w == 3 * rw + DECAY_LORA + AAA_LORA + GATE_LORA
    assert cw == rw and rw % LORA_PAD == 0
    in_pad = 2 * cw + 3 * rw + LORA_PAD
    qw = 3 * rw + LORA_PAD

    xp = x_prompt.reshape(batch * seq, d)
    xs = x_sample.reshape(dec_batch, d)
    outs = {k: [] for k in ("conv_p", "shift_p", "rwkv_p", "memk_p", "memv_p", "conv_s", "shift_s", "rwkv_s")}
    bd = _block_diag_ones(2 * LANES, HEAD)

    for l in range(depth):
        w_in_b = w_in.astype(BF16)
        w_out_b = w_out.astype(BF16)
        w_cq_b = w_cq.astype(BF16)
        w_ck_b = w_ck.astype(BF16)
        w_cv_b = w_cv.astype(BF16)
        w_co_b = w_co.astype(BF16)
        zeros_l = jnp.zeros((DECAY_LORA, rw), F32)
        wd_pad = jnp.concatenate([w_decay_up[l], zeros_l], axis=0).astype(BF16)
        wa_pad = jnp.concatenate([zeros_l, w_a_up[l]], axis=0).astype(BF16)
        wg_pad = jnp.pad(w_g_up[l], ((0, 2 * LANES - GATE_LORA), (0, 0))).astype(BF16)
        mu_pad = _pad_cols(shift_mu[l].reshape(1, shift_w), qw)
        vec = lambda x: x.reshape(1, rw)
        pp = (mu_pad, wd_pad, wa_pad, wg_pad, vec(decay_bias[l]), vec(a_bias[l]), vec(k_k[l]), vec(k_a[l]),
              vec(r_k[l]), bd)
        w_route = jnp.concatenate([w_route_expert[l].reshape(d, N_EXPERTS), w_route_group[l]], axis=1)
        w_route = _pad_cols(w_route, LANES)
        wr_hi = w_route.astype(BF16)
        wr_lo = (w_route - wr_hi.astype(F32)).astype(BF16)
        b_route = _pad_cols(jnp.concatenate([b_route_expert[l].reshape(1, N_EXPERTS),
                                             b_route_group[l].reshape(1, N_GROUPS)], axis=1), LANES)
        de = w_gate.shape[-1]
        wg_e = w_gate[l].reshape(N_EXPERTS, d, de)
        wu_e = w_up[l].reshape(N_EXPERTS, d, de)
        wd_e = w_down[l].reshape(N_EXPERTS, de, d)

        proj_s = _norm_matmul(xs, norm_mix[l], w_in_b, l, 128, LORA_PAD)
        c_s, conv_new_s = _conv_decode(proj_s, cache_conv, l, conv_w[l], conv_b[l], conv_ln_g[l],
                                       conv_ln_b[l])
        prep_s = _rwkv_prep_decode(proj_s, _pad_cols(state_shift[l], qw), pp, rw)
        o_s, s_s = _rwkv_step(*prep_s[:6], state_rwkv, l)
        xs = _mix_out(c_s, o_s, prep_s[6], prep_s[7], lnx_g[l], lnx_b[l], bd, w_out_b, l, xs)
        qs = _norm_matmul(xs, norm_x[l], w_cq_b, l, 128, d)
        ctx_s = _attn_decode(qs, cache_mem_k[l], cache_mem_v[l])
        xs = _matmul_res(ctx_s, w_co_b, l, xs, 128)
        outs["conv_s"].append(conv_new_s)
        outs["shift_s"].append(proj_s[:, 2 * cw:2 * cw + shift_w])
        outs["rwkv_s"].append(s_s)

        mem2 = mem_prompt.reshape(batch * N_MEM, d)
        mk, mk_b = _norm_matmul_heads(mem2, norm_mem[l], w_ck_b, l, X_HEADS, 256)
        mv, mv_b = _norm_matmul_heads(mem2, norm_mem[l], w_cv_b, l, X_HEADS, 256)
        proj = _norm_matmul(xp, norm_mix[l], w_in_b, l, 1024, LORA_PAD)
        assert proj.shape[1] == in_pad
        c_p, conv_new = _conv_prefill(proj, jnp.zeros((batch, CONV_K - 1, cw), F32), conv_w[l], conv_b[l],
                                      conv_ln_g[l], conv_ln_b[l], batch, seq)
        prep = _rwkv_prep_prefill(proj, jnp.zeros((batch, qw), F32), pp, batch, seq, rw)
        o_p, s_p = _rwkv_chunked(*prep[:6], jnp.zeros((batch, heads, HEAD, HEAD), F32), batch, seq)
        shift_new = proj.reshape(batch, seq, in_pad)[:, -1, 2 * cw:2 * cw + shift_w]
        xp = _mix_out(c_p, o_p, prep[6], prep[7], lnx_g[l], lnx_b[l], bd, w_out_b, l, xp)
        qx = _norm_matmul(xp, norm_x[l], w_cq_b, l, 512, d)
        ctx = _attn_prefill(qx, mk_b, mv_b, batch, seq)
        xp = _matmul_res(ctx, w_co_b, l, xp, 512)
        outs["conv_p"].append(conv_new)
        outs["shift_p"].append(shift_new)
        outs["rwkv_p"].append(s_p)
        outs["memk_p"].append(mk.reshape(batch, N_MEM, X_HEADS, d // X_HEADS))
        outs["memv_p"].append(mv.reshape(batch, N_MEM, X_HEADS, d // X_HEADS))

        xp, xs = _moe_sorted(xp, xs, norm_ffn[l], wr_hi, wr_lo, b_route, wg_e, wu_e, wd_e, norm_final)

    y_prompt = xp.reshape(batch, seq, d)
    y_sample = xs.reshape(dec_batch, 1, d)
    st = lambda k: jnp.stack(outs[k])
    return (y_prompt, y_sample, st("conv_p"), st("shift_p"), st("rwkv_p"), st("memk_p"), st("memv_p"),
            st("conv_s"), st("shift_s"), st("rwkv_s"))
```

```python
import functools
import math

import jax
import jax.numpy as jnp
from jax import lax
from jax.experimental import pallas as pl
from jax.experimental.pallas import tpu as pltpu

F32 = jnp.float32
BF16 = jnp.bfloat16

CONV_K = 31
HEAD = 64
PAIR = 2 * HEAD
CHUNK = 64
DECAY_LORA = 64
AAA_LORA = 64
GATE_LORA = 160
LORA_PAD = 512
N_MEM = 256
X_HEADS = 4
N_GROUPS = 4
EXP_PER_GROUP = 8
N_EXPERTS = N_GROUPS * EXP_PER_GROUP
RMS_EPS = 1e-6
LN_EPS = 1e-5
GN_EPS = 64e-5
DECAY_SCALE = math.exp(-0.5)
NEG_BIG = -1e30
MOE_BM = 512
LANES = 128
VMEM_LIMIT = 56 * 1024 * 1024


def _cparams(*sem):
    return pltpu.CompilerParams(dimension_semantics=sem, vmem_limit_bytes=VMEM_LIMIT)


def _dot(a, b):
    return jnp.dot(a, b, preferred_element_type=F32)


def _dot_nt(a, b):
    return lax.dot_general(a, b, (((1,), (1,)), ((), ())), preferred_element_type=F32)


def _split_dot(x, w_bf16):
    hi = x.astype(BF16)
    lo = (x - hi.astype(F32)).astype(BF16)
    return _dot(hi, w_bf16) + _dot(lo, w_bf16)


def _rms(x, g, eps=RMS_EPS):
    return x * lax.rsqrt(jnp.mean(x * x, axis=-1, keepdims=True) + eps) * g


def _norm_mm_kernel(x_ref, g_ref, w_ref, o_ref, xn_ref, *, n_valid):
    j = pl.program_id(1)

    @pl.when(j == 0)
    def _():
        xn_ref[...] = _rms(x_ref[...], g_ref[...]).astype(BF16)

    w = w_ref[0]
    bn = w.shape[1]
    if n_valid % bn:
        col = j * bn + lax.broadcasted_iota(jnp.int32, (1, bn), 1)
        w = jnp.where(col < n_valid, w, jnp.zeros_like(w))
    o_ref[...] = _dot(xn_ref[...], w)


def _norm_matmul(x, g, w, layer, bm, bn):
    m, k = x.shape
    n = w.shape[2]
    bm = min(bm, m)
    n_tiles = pl.cdiv(n, bn)
    return pl.pallas_call(
        functools.partial(_norm_mm_kernel, n_valid=n),
        grid=(m // bm, n_tiles),
        in_specs=[pl.BlockSpec((bm, k), lambda i, j: (i, 0)),
                  pl.BlockSpec((1, k), lambda i, j: (0, 0)),
                  pl.BlockSpec((1, k, bn), lambda i, j: (layer, 0, j))],
        out_specs=pl.BlockSpec((bm, bn), lambda i, j: (i, j)),
        out_shape=jax.ShapeDtypeStruct((m, n_tiles * bn), F32),
        scratch_shapes=[pltpu.VMEM((bm, k), BF16)],
        compiler_params=_cparams("parallel", "arbitrary"),
        name="norm_matmul",
    )(x, g.reshape(1, k), w)


def _norm_mm_heads_kernel(x_ref, g_ref, w_ref, o_ref, ob_ref):
    res = _dot(_rms(x_ref[...], g_ref[...]).astype(BF16), w_ref[0])
    ob_ref[...] = res.astype(BF16)
    dh = o_ref.shape[2]
    for h in range(o_ref.shape[1]):
        o_ref[:, h, :] = res[:, h * dh:(h + 1) * dh]


def _norm_matmul_heads(x, g, w, layer, n_heads, bm):
    m, k = x.shape
    n = w.shape[2]
    bm = min(bm, m)
    return pl.pallas_call(
        _norm_mm_heads_kernel,
        grid=(m // bm,),
        in_specs=[pl.BlockSpec((bm, k), lambda i: (i, 0)),
                  pl.BlockSpec((1, k), lambda i: (0, 0)),
                  pl.BlockSpec((1, k, n), lambda i: (layer, 0, 0))],
        out_specs=[pl.BlockSpec((bm, n_heads, n // n_heads), lambda i: (i, 0, 0)),
                   pl.BlockSpec((bm, n), lambda i: (i, 0))],
        out_shape=[jax.ShapeDtypeStruct((m, n_heads, n // n_heads), F32),
                   jax.ShapeDtypeStruct((m, n), BF16)],
        compiler_params=_cparams("parallel"),
        name="norm_matmul_heads",
    )(x, g.reshape(1, k), w)


def _mm_res_kernel(a_ref, w_ref, res_ref, o_ref):
    o_ref[...] = res_ref[...] + _dot(a_ref[...].astype(BF16), w_ref[0])


def _matmul_res(a, w, layer, res, bm):
    m, n = res.shape
    k = a.shape[1]
    bm = min(bm, m)
    return pl.pallas_call(
        _mm_res_kernel,
        grid=(m // bm,),
        in_specs=[pl.BlockSpec((bm, k), lambda i: (i, 0)),
                  pl.BlockSpec((1, k, n), lambda i: (layer, 0, 0)),
                  pl.BlockSpec((bm, n), lambda i: (i, 0))],
        out_specs=pl.BlockSpec((bm, n), lambda i: (i, 0)),
        out_shape=jax.ShapeDtypeStruct((m, n), F32),
        compiler_params=_cparams("parallel"),
        name="matmul_res",
    )(a, w, res)


def _ln_silu(cf, lg, lb):
    mu = jnp.mean(cf, axis=-1, keepdims=True)
    d = cf - mu
    var = jnp.mean(d * d, axis=-1, keepdims=True)
    y = d * lax.rsqrt(var + LN_EPS) * lg + lb
    return y * jax.nn.sigmoid(y)


def _conv_prefill_kernel(a_ref, g_ref, buf_ref, w_ref, cb_ref, lg_ref, lb_ref, c_ref, nc_ref,
                         uf_ref, cv_ref, sh_ref, *, tt, halo):
    t = pl.program_id(1)
    pad = 32 - halo

    @pl.when(t == 0)
    def _():
        uf_ref[pad:32, :] = buf_ref[0]

    @pl.when(t > 0)
    def _():
        uf_ref[pad:32, :] = uf_ref[tt + pad:tt + 32, :]

    uf_ref[32:32 + tt, :] = a_ref[...] * jax.nn.sigmoid(g_ref[...])

    for sft in range(8):
        n_rows = sh_ref.shape[1] if sft < 7 else sh_ref.shape[1] - 8
        sh_ref[sft, 0:n_rows, :] = uf_ref[pad + sft:pad + sft + n_rows, :]

    width = uf_ref.shape[1]
    rb = 64
    for r0 in range(0, tt, rb):
        for l0 in range(0, width, LANES):
            acc = jnp.zeros((rb, LANES), F32)
            for j in range(CONV_K):
                base = r0 + j - j % 8
                acc = acc + sh_ref[j % 8, base:base + rb, l0:l0 + LANES] * w_ref[j:j + 1, l0:l0 + LANES]
            cv_ref[r0:r0 + rb, l0:l0 + LANES] = acc

    c_ref[...] = _ln_silu(cv_ref[...] + cb_ref[...], lg_ref[...], lb_ref[...]).astype(c_ref.dtype)

    @pl.when(t == pl.num_programs(1) - 1)
    def _():
        nc_ref[0] = uf_ref[tt + pad:tt + 32, :]


def _conv_prefill(proj, conv_buf, conv_w, conv_b, ln_g, ln_b, batch, seq):
    cw = conv_w.shape[1]
    halo = CONV_K - 1
    tt = min(256, seq)
    nt = seq // tt
    row = lambda b, t: (b * nt + t, 0)
    vec = pl.BlockSpec((1, cw), lambda b, t: (0, 0))
    return pl.pallas_call(
        functools.partial(_conv_prefill_kernel, tt=tt, halo=halo),
        grid=(batch, nt),
        in_specs=[pl.BlockSpec((tt, cw), row),
                  pl.BlockSpec((tt, cw), lambda b, t: (b * nt + t, 1)),
                  pl.BlockSpec((1, halo, cw), lambda b, t: (b, 0, 0)),
                  pl.BlockSpec((CONV_K, cw), lambda b, t: (0, 0)),
                  vec, vec, vec],
        out_specs=[pl.BlockSpec((tt, cw), row),
                   pl.BlockSpec((1, halo, cw), lambda b, t: (b, 0, 0))],
        out_shape=[jax.ShapeDtypeStruct((batch * seq, cw), BF16),
                   jax.ShapeDtypeStruct((batch, halo, cw), F32)],
        scratch_shapes=[pltpu.VMEM((tt + 32, cw), F32), pltpu.VMEM((tt, cw), F32),
                        pltpu.VMEM((8, tt + 24, cw), F32)],
        compiler_params=_cparams("parallel", "arbitrary"),
        name="conv_prefill",
    )(proj, proj, conv_buf, conv_w, conv_b.reshape(1, cw), ln_g.reshape(1, cw), ln_b.reshape(1, cw))


def _conv_decode_kernel(a_ref, g_ref, cache_ref, w_ref, cb_ref, lg_ref, lb_ref, c_ref, nc_ref):
    halo = CONV_K - 1
    u = a_ref[...] * jax.nn.sigmoid(g_ref[...])
    acc = u * w_ref[halo:halo + 1, :]
    for j in range(halo):
        acc = acc + cache_ref[0, :, j, :] * w_ref[j:j + 1, :]
    c_ref[...] = _ln_silu(acc + cb_ref[...], lg_ref[...], lb_ref[...]).astype(c_ref.dtype)
    nc_ref[:, 0:halo - 1, :] = cache_ref[0, :, 1:halo, :]
    nc_ref[:, halo - 1, :] = u


def _conv_decode(proj, cache, layer, conv_w, conv_b, ln_g, ln_b):
    _, batch, halo, cw = cache.shape
    bb = 8
    vec = pl.BlockSpec((1, cw), lambda i: (0, 0))
    return pl.pallas_call(
        _conv_decode_kernel,
        grid=(batch // bb,),
        in_specs=[pl.BlockSpec((bb, cw), lambda i: (i, 0)),
                  pl.BlockSpec((bb, cw), lambda i: (i, 1)),
                  pl.BlockSpec((1, bb, halo, cw), lambda i: (layer, i, 0, 0)),
                  pl.BlockSpec((CONV_K, cw), lambda i: (0, 0)),
                  vec, vec, vec],
        out_specs=[pl.BlockSpec((bb, cw), lambda i: (i, 0)),
                   pl.BlockSpec((bb, halo, cw), lambda i: (i, 0, 0))],
        out_shape=[jax.ShapeDtypeStruct((batch, cw), BF16),
                   jax.ShapeDtypeStruct((batch, halo, cw), F32)],
        compiler_params=_cparams("parallel"),
        name="conv_decode",
    )(proj, proj, cache, conv_w, conv_b.reshape(1, cw), ln_g.reshape(1, cw), ln_b.reshape(1, cw))


def _head_sum(x, bd_ref):
    blk = bd_ref.shape[0]
    parts = [_split_dot(x[:, l0:l0 + blk], bd_ref[...]) for l0 in range(0, x.shape[1], blk)]
    return jnp.concatenate(parts, axis=1)


def _prep_math(q, qp, mu_ref, wd_ref, wa_ref, wg_ref, db_ref, ab_ref, kk_ref, ka_ref, rk_ref, bd_ref):
    rw = q[0].shape[1]
    offs = (0, rw, 2 * rw, 3 * rw)
    r, k, v, lo = [x + (xp - x) * mu_ref[:, o:o + x.shape[1]] for x, xp, o in zip(q, qp, offs)]
    pwa = lo[:, 0:LANES]
    pg = lo[:, LANES:3 * LANES]
    dec_in = _dot(jnp.tanh(pwa).astype(BF16), wd_ref[...])
    a_in = _dot(pwa.astype(BF16), wa_ref[...])
    gate = _dot(jax.nn.sigmoid(pg).astype(BF16), wg_ref[...])
    logw = -DECAY_SCALE * jax.nn.sigmoid(db_ref[...] + dec_in)
    a = jax.nn.sigmoid(ab_ref[...] + a_in)
    kk = k * kk_ref[...]
    kk = kk / jnp.maximum(jnp.sqrt(_head_sum(kk * kk, bd_ref)), 1e-12)
    k2 = k * (1.0 + (a - 1.0) * ka_ref[...])
    bonus = _head_sum(r * k2 * rk_ref[...], bd_ref) * v
    return r, logw, k2, v, kk, a, bonus, gate


def _prep_prefill_kernel(r_ref, k_ref, v_ref, lo_ref, sb_ref, mu_ref, wd_ref, wa_ref, wg_ref, db_ref,
                         ab_ref, kk_ref, ka_ref, rk_ref, bd_ref, *rest):
    outs = rest[:8]
    carry_ref = rest[8]
    t = pl.program_id(1)

    @pl.when(t == 0)
    def _():
        carry_ref[0:1, :] = sb_ref[0]

    q = [r_ref[...], k_ref[...], v_ref[...], lo_ref[...]]
    tt = q[0].shape[0]
    first = lax.broadcasted_iota(jnp.int32, (tt, 1), 0) == 0
    qp = []
    off = 0
    for x in q:
        w = x.shape[1]
        qp.append(jnp.where(first, carry_ref[0:1, off:off + w], pltpu.roll(x, 1, 0)))
        off += w
    off = 0
    for x in q:
        w = x.shape[1]
        carry_ref[0:1, off:off + w] = x[tt - 1:tt, :]
        off += w
    res = _prep_math(q, qp, mu_ref, wd_ref, wa_ref, wg_ref, db_ref, ab_ref, kk_ref, ka_ref, rk_ref, bd_ref)
    for o_ref, val in zip(outs, res):
        o_ref[...] = val.astype(o_ref.dtype)


def _prep_decode_kernel(r_ref, k_ref, v_ref, lo_ref, rp_ref, kp_ref, vp_ref, lop_ref, mu_ref, wd_ref,
                        wa_ref, wg_ref, db_ref, ab_ref, kk_ref, ka_ref, rk_ref, bd_ref, *outs):
    q = [r_ref[...], k_ref[...], v_ref[...], lo_ref[...]]
    qp = [rp_ref[...], kp_ref[...], vp_ref[...], lop_ref[...]]
    res = _prep_math(q, qp, mu_ref, wd_ref, wa_ref, wg_ref, db_ref, ab_ref, kk_ref, ka_ref, rk_ref, bd_ref)
    for o_ref, val in zip(outs, res):
        o_ref[...] = val


def _prep_param_specs(rw, idx):
    full = lambda shape: pl.BlockSpec(shape, idx)
    vec = full((1, rw))
    return [full((1, 3 * rw + LORA_PAD)), full((LANES, rw)), full((LANES, rw)), full((2 * LANES, rw)),
            vec, vec, vec, vec, vec, full((2 * LANES, 2 * LANES))]


def _rwkv_prep_prefill(proj, shift_buf, pp, batch, seq, rw):
    tt = min(256, seq)
    nt = seq // tt
    lora_blk = (2 * rw + 3 * rw) // LORA_PAD
    col = lambda c: (lambda b, t: (b * nt + t, c))
    qw = 3 * rw + LORA_PAD
    in_specs = [pl.BlockSpec((tt, rw), col(2)), pl.BlockSpec((tt, rw), col(3)),
                pl.BlockSpec((tt, rw), col(4)), pl.BlockSpec((tt, LORA_PAD), col(lora_blk)),
                pl.BlockSpec((1, 1, qw), lambda b, t: (b, 0, 0))]
    in_specs += _prep_param_specs(rw, lambda b, t: (0, 0))
    out_spec = pl.BlockSpec((tt, rw), col(0))
    return pl.pallas_call(
        _prep_prefill_kernel,
        grid=(batch, nt),
        in_specs=in_specs,
        out_specs=[out_spec] * 8,
        out_shape=[jax.ShapeDtypeStruct((batch * seq, rw), BF16 if i == 3 else F32) for i in range(8)],
        scratch_shapes=[pltpu.VMEM((8, qw), F32)],
        compiler_params=_cparams("parallel", "arbitrary"),
        name="rwkv_prep_prefill",
    )(proj, proj, proj, proj, shift_buf.reshape(batch, 1, qw), *pp)


def _rwkv_prep_decode(proj, shift_state, pp, rw):
    batch = proj.shape[0]
    bb = min(128, batch)
    lora_blk = (2 * rw + 3 * rw) // LORA_PAD
    col = lambda c: (lambda i: (i, c))
    in_specs = [pl.BlockSpec((bb, rw), col(2)), pl.BlockSpec((bb, rw), col(3)),
                pl.BlockSpec((bb, rw), col(4)), pl.BlockSpec((bb, LORA_PAD), col(lora_blk)),
                pl.BlockSpec((bb, rw), col(0)), pl.BlockSpec((bb, rw), col(1)),
                pl.BlockSpec((bb, rw), col(2)), pl.BlockSpec((bb, LORA_PAD), col(3 * rw // LORA_PAD))]
    in_specs += _prep_param_specs(rw, lambda i: (0, 0))
    return pl.pallas_call(
        _prep_decode_kernel,
        grid=(batch // bb,),
        in_specs=in_specs,
        out_specs=[pl.BlockSpec((bb, rw), col(0))] * 8,
        out_shape=[jax.ShapeDtypeStruct((batch, rw), F32)] * 8,
        compiler_params=_cparams("parallel"),
        name="rwkv_prep_decode",
    )(proj, proj, proj, proj, shift_state, shift_state, shift_state, shift_state, *pp)


def _stack2(x, smask):
    return jnp.where(smask, jnp.concatenate([x, x], axis=0), 0.0)


def _rwkv_chunk_kernel(r_ref, lw_ref, k_ref, v_ref, kk_ref, a_ref, s0_ref, o_ref, so_ref, s_ref):
    c = pl.program_id(1)
    cs = r_ref.shape[0]
    n_pairs = r_ref.shape[1] // PAIR
    two = 2 * cs

    @pl.when(c == 0)
    def _():
        z = jnp.zeros((HEAD, HEAD), F32)
        for p in range(n_pairs):
            top = jnp.concatenate([s0_ref[0, 2 * p], z], axis=1)
            bot = jnp.concatenate([z, s0_ref[0, 2 * p + 1]], axis=1)
            s_ref[p] = jnp.concatenate([top, bot], axis=0)

    ri = lax.broadcasted_iota(jnp.int32, (two, two), 0)
    ci = lax.broadcasted_iota(jnp.int32, (two, two), 1)
    strict = ci < ri
    incl = ci <= ri
    eye = (ci == ri).astype(F32)
    smask = (lax.broadcasted_iota(jnp.int32, (two, PAIR), 0) < cs) == (
        lax.broadcasted_iota(jnp.int32, (two, PAIR), 1) < HEAD)
    tri = (lax.broadcasted_iota(jnp.int32, (cs, cs), 1)
           <= lax.broadcasted_iota(jnp.int32, (cs, cs), 0)).astype(BF16)

    lw_all = lw_ref[...]
    lw_hi = lw_all.astype(BF16)
    lw_lo = (lw_all - lw_hi.astype(F32)).astype(BF16)
    cum_all = _dot(tri, lw_hi) + _dot(tri, lw_lo)

    pairs = range(n_pairs)
    cat = jnp.concatenate
    prep = []
    for p in pairs:
        sl = slice(p * PAIR, (p + 1) * PAIR)
        lw = lw_all[:, sl]
        cum = cum_all[:, sl]
        tot = cum[cs - 1:cs, :]
        g_inv = jnp.exp(-cum)
        g_end = jnp.exp(tot - cum)
        kk = kk_ref[:, sl]
        k2 = k_ref[:, sl]
        bb = kk * a_ref[:, sl]
        prep.append(dict(
            g_tot=jnp.exp(tot),
            a_b=_stack2(kk * jnp.exp(cum - lw), smask).astype(BF16),
            r_s=_stack2(r_ref[:, sl] * jnp.exp(cum), smask),
            bk=cat([_stack2(bb * g_inv, smask), _stack2(k2 * g_inv, smask)], axis=0).astype(BF16),
            v_s=_stack2(v_ref[:, sl].astype(F32), smask),
            bg_s=_stack2(bb * g_end, smask).astype(BF16),
            kg_s=_stack2(k2 * g_end, smask).astype(BF16)))
    a_b = [q["a_b"] for q in prep]
    r_s = [q["r_s"] for q in prep]
    v_s = [q["v_s"] for q in prep]
    v_b = [x.astype(BF16) for x in v_s]

    gram = [_dot_nt(cat([a_b[p], r_s[p].astype(BF16)], axis=0), prep[p]["bk"]) for p in pairs]
    l_ab = [jnp.where(strict, g[0:two, 0:two], 0.0) for g in gram]
    l_ak = [jnp.where(strict, g[0:two, two:], 0.0).astype(BF16) for g in gram]
    m_rb = [jnp.where(incl, g[two:, 0:two], 0.0).astype(BF16) for g in gram]
    m_rk = [jnp.where(incl, g[two:, two:], 0.0).astype(BF16) for g in gram]

    tm = [eye - x for x in l_ab]
    pw = l_ab
    n = 1
    while 2 * n < cs:
        pw = [_dot(x.astype(BF16), x.astype(BF16)) for x in pw]
        tm = [t + _dot(t.astype(BF16), x.astype(BF16)) for t, x in zip(tm, pw)]
        n *= 2
    tm_b = [t.astype(BF16) for t in tm]

    w1 = [_dot(l_ak[p], v_b[p]) for p in pairs]
    ua = [_dot(tm_b[p], cat([w1[p].astype(BF16), a_b[p]], axis=1)) for p in pairs]
    ua_b = [x.astype(BF16) for x in ua]
    mrb_ua = [_dot(m_rb[p], ua_b[p]) for p in pairs]
    o0 = [_dot(m_rk[p], v_b[p]) - mrb_ua[p][:, 0:PAIR] for p in pairs]
    rt = [(r_s[p] - mrb_ua[p][:, PAIR:]).astype(BF16) for p in pairs]

    s_old = [s_ref[p] for p in pairs]
    s_b = [x.astype(BF16) for x in s_old]
    ua_t = [cat([x[:, 0:PAIR].T, x[:, PAIR:].T], axis=0).astype(BF16) for x in ua]
    uat_bg = [_dot(ua_t[p], prep[p]["bg_s"]) for p in pairs]
    vt_kg = [_dot(v_s[p].T.astype(BF16), prep[p]["kg_s"]) for p in pairs]
    s_new = [s_old[p] * prep[p]["g_tot"] - _dot(s_b[p], uat_bg[p][PAIR:, :].astype(BF16))
             + vt_kg[p] - uat_bg[p][0:PAIR, :] for p in pairs]
    o_st = [o0[p] + _dot_nt(rt[p], s_b[p]) for p in pairs]

    o_ref[...] = cat([x[0:cs, :] + x[cs:two, :] for x in o_st], axis=1)
    s_ref[...] = jnp.stack(s_new, axis=0)

    @pl.when(c == pl.num_programs(1) - 1)
    def _():
        for p in range(n_pairs):
            s = s_ref[p]
            so_ref[0, 2 * p] = s[0:HEAD, 0:HEAD]
            so_ref[0, 2 * p + 1] = s[HEAD:PAIR, HEAD:PAIR]


def _rwkv_chunked(r, logw, k2, v, kk, a, s0, batch, seq):
    rw = r.shape[1]
    nc = seq // CHUNK
    heads = rw // HEAD
    row = pl.BlockSpec((CHUNK, rw), lambda b, c: (b * nc + c, 0))
    st = pl.BlockSpec((1, heads, HEAD, HEAD), lambda b, c: (b, 0, 0, 0))
    return pl.pallas_call(
        _rwkv_chunk_kernel,
        grid=(batch, nc),
        in_specs=[row] * 6 + [st],
        out_specs=[row, st],
        out_shape=[jax.ShapeDtypeStruct((batch * seq, rw), F32),
                   jax.ShapeDtypeStruct((batch, heads, HEAD, HEAD), F32)],
        scratch_shapes=[pltpu.VMEM((rw // PAIR, PAIR, PAIR), F32)],
        compiler_params=_cparams("parallel", "arbitrary"),
        name="rwkv_chunked",
    )(r, logw, k2, v, kk, a, s0)


def _rwkv_step_kernel(r_ref, lw_ref, k_ref, v_ref, kk_ref, a_ref, s_ref, o_ref, so_ref, *, heads):
    bb = s_ref.shape[1]
    eye = (lax.broadcasted_iota(jnp.int32, (HEAD, HEAD), 0)
           == lax.broadcasted_iota(jnp.int32, (HEAD, HEAD), 1))

    def body(bi, carry):
        hs = range(heads)
        rows = [pl.ds(bi * heads + h, 1) for h in hs]
        kk = [kk_ref[r, :] for r in rows]
        s = [s_ref[0, bi, h].astype(F32) for h in hs]
        sa = [jnp.sum(s[h] * kk[h], axis=1, keepdims=True) for h in hs]
        v_col = [jnp.sum(jnp.where(eye, v_ref[rows[h], :], 0.0), axis=1, keepdims=True) for h in hs]
        s_new = [s[h] * jnp.exp(lw_ref[rows[h], :]) - sa[h] * (kk[h] * a_ref[rows[h], :])
                 + v_col[h] * k_ref[rows[h], :] for h in hs]
        for h in hs:
            so_ref[bi, h] = s_new[h]
        o_col = [jnp.sum(s_new[h] * r_ref[rows[h], :], axis=1, keepdims=True) for h in hs]
        o_row = [jnp.sum(jnp.where(eye, o_col[h], 0.0), axis=0, keepdims=True) for h in hs]
        o_ref[pl.ds(pl.multiple_of(bi * heads, heads), heads), :] = jnp.concatenate(o_row, axis=0)
        return carry

    lax.fori_loop(0, bb, body, 0)


def _rwkv_step(r, logw, k2, v, kk, a, state, layer):
    batch, rw = r.shape
    heads = rw // HEAD
    bb = 8
    flat = lambda x: x.reshape(batch * heads, HEAD)
    row = pl.BlockSpec((bb * heads, HEAD), lambda i: (i, 0))
    st_in = pl.BlockSpec((1, bb, heads, HEAD, HEAD), lambda i: (layer, i, 0, 0, 0))
    st_out = pl.BlockSpec((bb, heads, HEAD, HEAD), lambda i: (i, 0, 0, 0))
    o, s_new = pl.pallas_call(
        functools.partial(_rwkv_step_kernel, heads=heads),
        grid=(batch // bb,),
        in_specs=[row] * 6 + [st_in],
        out_specs=[row, st_out],
        out_shape=[jax.ShapeDtypeStruct((batch * heads, HEAD), F32),
                   jax.ShapeDtypeStruct(state.shape[1:], F32)],
        compiler_params=_cparams("parallel"),
        name="rwkv_step",
    )(flat(r), flat(logw), flat(k2), flat(v), flat(kk), flat(a), state)
    return o.reshape(batch, rw), s_new


def _mix_out_kernel(c_ref, o_ref, bonus_ref, gate_ref, lg_ref, lb_ref, bd_ref, wc_ref, wo_ref, x_ref, y_ref):
    o = o_ref[...]
    mu = _head_sum(o, bd_ref) * (1.0 / HEAD)
    d = o - mu
    var = _head_sum(d * d, bd_ref) * (1.0 / HEAD)
    y = d * lax.rsqrt(var + GN_EPS) * lg_ref[...] + lb_ref[...]
    om = ((y + bonus_ref[...]) * gate_ref[...]).astype(BF16)
    y_ref[...] = x_ref[...] + _dot(c_ref[...], wc_ref[0]) + _dot(om, wo_ref[0])


def _mix_out(c, o, bonus, gate, lnx_g, lnx_b, bd, w_out, layer, x):
    m, d = x.shape
    cw, rw = c.shape[1], o.shape[1]
    assert cw == rw and w_out.shape[1] == cw + rw
    bm = min(512, m)
    row = lambda w: pl.BlockSpec((bm, w), lambda i: (i, 0))
    vec = pl.BlockSpec((1, rw), lambda i: (0, 0))
    return pl.pallas_call(
        _mix_out_kernel,
        grid=(m // bm,),
        in_specs=[row(cw), row(rw), row(rw), row(rw), vec, vec, pl.BlockSpec(bd.shape, lambda i: (0, 0)),
                  pl.BlockSpec((1, cw, d), lambda i: (layer, 0, 0)),
                  pl.BlockSpec((1, rw, d), lambda i: (layer, 1, 0)),
                  row(d)],
        out_specs=row(d),
        out_shape=jax.ShapeDtypeStruct((m, d), F32),
        compiler_params=_cparams("parallel"),
        name="mix_out",
    )(c, o, bonus, gate, lnx_g.reshape(1, rw), lnx_b.reshape(1, rw), bd, w_out, w_out, x)


def _attn_prefill_kernel(q_ref, k_ref, v_ref, o_ref, *, n_heads):
    d = q_ref.shape[1] // n_heads
    scale = d ** -0.5
    for h in range(n_heads):
        sl = slice(h * d, (h + 1) * d)
        s = _dot_nt(q_ref[:, sl].astype(BF16), k_ref[:, sl]) * scale
        p = jnp.exp(s - jnp.max(s, axis=-1, keepdims=True))
        att = p / jnp.sum(p, axis=-1, keepdims=True)
        o_ref[:, sl] = _dot(att.astype(BF16), v_ref[:, sl]).astype(o_ref.dtype)


def _attn_prefill(q, mem_k, mem_v, batch, seq):
    d = q.shape[1]
    tt = min(512, seq)
    nt = seq // tt
    kv = pl.BlockSpec((N_MEM, d), lambda b, t: (b, 0))
    row = pl.BlockSpec((tt, d), lambda b, t: (b * nt + t, 0))
    return pl.pallas_call(
        functools.partial(_attn_prefill_kernel, n_heads=X_HEADS),
        grid=(batch, nt),
        in_specs=[row, kv, kv],
        out_specs=row,
        out_shape=jax.ShapeDtypeStruct((batch * seq, d), BF16),
        compiler_params=_cparams("parallel", "arbitrary"),
        name="attn_prefill",
    )(q, mem_k, mem_v)


def _decode_attn_rows(qs, k_at, v_at, n_heads):
    d = qs[0].shape[1] // n_heads
    scale = d ** -0.5
    ids = [(i, h) for i in range(len(qs)) for h in range(n_heads)]
    s = [jnp.sum(k_at(i, h) * qs[i][:, h * d:(h + 1) * d], axis=1, keepdims=True) * scale for i, h in ids]
    p = [jnp.exp(x - jnp.max(x, axis=0, keepdims=True)) for x in s]
    att = [x / jnp.sum(x, axis=0, keepdims=True) for x in p]
    ctx = [jnp.sum(a * v_at(i, h), axis=0, keepdims=True) for a, (i, h) in zip(att, ids)]
    return [ctx[i * n_heads:(i + 1) * n_heads] for i in range(len(qs))]


def _attn_decode_kernel(q_ref, k_ref, v_ref, o_ref):
    n_heads, d = k_ref.shape[2], k_ref.shape[3]
    bb = q_ref.shape[0]
    ctx = _decode_attn_rows([q_ref[bi] for bi in range(bb)], lambda i, h: k_ref[i, :, h, :],
                            lambda i, h: v_ref[i, :, h, :], n_heads)
    for bi in range(bb):
        for h in range(n_heads):
            o_ref[bi, :, h * d:(h + 1) * d] = ctx[bi][h]


def _attn_decode(q, cache_k, cache_v):
    batch, d = q.shape
    bb = 2
    kv = pl.BlockSpec((bb, N_MEM, X_HEADS, d // X_HEADS), lambda i: (i, 0, 0, 0))
    row = pl.BlockSpec((bb, 1, d), lambda i: (i, 0, 0))
    out = pl.pallas_call(
        _attn_decode_kernel,
        grid=(batch // bb,),
        in_specs=[row, kv, kv],
        out_specs=row,
        out_shape=jax.ShapeDtypeStruct((batch, 1, d), F32),
        compiler_params=_cparams("parallel"),
        name="attn_decode",
    )(q.reshape(batch, 1, d), cache_k, cache_v)
    return out.reshape(batch, d)


def _route(x_ref, g_ref, wh_ref, wl_ref, b_ref):
    h = _rms(x_ref[...], g_ref[...])
    hh = h.astype(BF16)
    hl = (h - hh.astype(F32)).astype(BF16)
    logits = _dot(hh, wh_ref[...]) + _dot(hl, wh_ref[...]) + _dot(hh, wl_ref[...]) + b_ref[...]
    lane = lax.broadcasted_iota(jnp.int32, (1, LANES), 1).astype(F32)
    is_g = (lane >= N_EXPERTS) & (lane < N_EXPERTS + N_GROUPS)
    lgm = jnp.where(is_g, logits, NEG_BIG)
    gmax = jnp.max(lgm, axis=1, keepdims=True)
    gsum = jnp.sum(jnp.where(is_g, jnp.exp(lgm - gmax), 0.0), axis=1, keepdims=True)
    g_val = 1.0 / gsum
    g_idx = jnp.min(jnp.where(is_g & (lgm == gmax), lane - N_EXPERTS, 1e9), axis=1, keepdims=True)
    in_grp = (lane < N_EXPERTS) & (jnp.floor(lane * (1.0 / EXP_PER_GROUP)) == g_idx)
    le = jnp.where(in_grp, logits, NEG_BIG)
    m1 = jnp.max(le, axis=1, keepdims=True)
    i1 = jnp.min(jnp.where(in_grp & (le == m1), lane, 1e9), axis=1, keepdims=True)
    rest = in_grp & (lane != i1)
    le2 = jnp.where(rest, logits, NEG_BIG)
    m2 = jnp.max(le2, axis=1, keepdims=True)
    i2 = jnp.min(jnp.where(rest & (le2 == m2), lane, 1e9), axis=1, keepdims=True)
    e2 = jnp.exp(m2 - m1)
    den = 1.0 + e2
    w1 = (1.0 / den) * g_val
    w2 = (e2 / den) * g_val
    return h, lane, i1, i2, w1, w2


def _router_sorted_kernel(x_ref, g_ref, wh_ref, wl_ref, b_ref, init_ref, route_ref, rt_ref, cnt_ref, run_ref):
    @pl.when(pl.program_id(0) == 0)
    def _():
        run_ref[0:1, :] = init_ref[...]

    h, lane, i1, i2, w1, w2 = _route(x_ref, g_ref, wh_ref, wl_ref, b_ref)
    bm = h.shape[0]
    oh1 = lane == i1
    oh2 = lane == i2
    sel = (oh1 | oh2).astype(BF16)
    before = (lax.broadcasted_iota(jnp.int32, (bm, bm), 1)
              < lax.broadcasted_iota(jnp.int32, (bm, bm), 0)).astype(BF16)
    base = run_ref[0:1, :] + _dot(before, sel)
    rank1 = jnp.sum(jnp.where(oh1, base, 0.0), axis=1, keepdims=True)
    rank2 = jnp.sum(jnp.where(oh2, base, 0.0), axis=1, keepdims=True)
    total = run_ref[0:1, :] + jnp.sum(sel.astype(F32), axis=0, keepdims=True)
    run_ref[0:1, :] = total
    cnt_ref[...] = total
    route = jnp.zeros((bm, LANES), F32)
    for idx, val in enumerate((i1, i2, w1, w2, rank1, rank2)):
        route = jnp.where(lane == idx, val, route)
    route_ref[...] = route
    for r0 in range(0, bm, LANES):
        rt_ref[:, r0:r0 + LANES] = route[r0:r0 + LANES, :].T[0:8, :]


def _router_sorted(x, g, wh, wl, bias, init_counts):
    m, d = x.shape
    bm = min(512, m)
    return pl.pallas_call(
        _router_sorted_kernel,
        grid=(m // bm,),
        in_specs=[pl.BlockSpec((bm, d), lambda i: (i, 0)),
                  pl.BlockSpec((1, d), lambda i: (0, 0)),
                  pl.BlockSpec((d, LANES), lambda i: (0, 0)),
                  pl.BlockSpec((d, LANES), lambda i: (0, 0)),
                  pl.BlockSpec((1, LANES), lambda i: (0, 0)),
                  pl.BlockSpec((1, LANES), lambda i: (0, 0))],
        out_specs=[pl.BlockSpec((bm, LANES), lambda i: (i, 0)),
                   pl.BlockSpec((8, bm), lambda i: (0, i)),
                   pl.BlockSpec((1, LANES), lambda i: (0, 0))],
        out_shape=[jax.ShapeDtypeStruct((m, LANES), F32),
                   jax.ShapeDtypeStruct((8, m), F32),
                   jax.ShapeDtypeStruct((1, LANES), F32)],
        scratch_shapes=[pltpu.VMEM((8, LANES), F32)],
        compiler_params=_cparams("arbitrary"),
        name="moe_router_sorted",
    )(x, g.reshape(1, d), wh, wl, bias, init_counts)


def _plan_kernel(seg_ref, rt_ref, pos_ref):
    rt = rt_ref[...]
    rows = []
    for e_row, r_row in ((0, 4), (1, 5)):
        e = rt[e_row:e_row + 1, :]
        start = jnp.zeros_like(e)
        for k in range(N_EXPERTS):
            start = jnp.where(e == k, seg_ref[k].astype(F32), start)
        rows.append((start + rt[r_row:r_row + 1, :]).astype(jnp.int32))
    pos_ref[...] = jnp.concatenate(rows + [jnp.zeros((6, rt.shape[1]), jnp.int32)], axis=0)


def _plan(route_t, seg_start):
    m = route_t.shape[1]
    bt = min(2048, m)
    pos = pl.pallas_call(
        _plan_kernel,
        grid_spec=pltpu.PrefetchScalarGridSpec(
            num_scalar_prefetch=1, grid=(m // bt,),
            in_specs=[pl.BlockSpec((8, bt), lambda i, seg: (0, i))],
            out_specs=pl.BlockSpec((8, bt), lambda i, seg: (0, i))),
        out_shape=jax.ShapeDtypeStruct((8, m), jnp.int32),
        compiler_params=_cparams("arbitrary"),
        name="moe_plan",
    )(seg_start, route_t)
    return pos[0], pos[1]


def _row_copy(src_hbm, src_row, dst, dst_row, sem):
    return pltpu.make_async_copy(src_hbm.at[pl.ds(src_row, 1)], dst.at[pl.ds(dst_row, 1)], sem)


def _dispatch_kernel(p1_ref, p2_ref, seg_ref, cnt_ref, nrow_ref, xa_ref, xb_ref, g_ref, xs_hbm, h_ref,
                     zero_ref, sem, zsem, *, n_a):
    i = pl.program_id(0)
    n = pl.num_programs(0)
    bm = xa_ref.shape[0]
    n_max = xs_hbm.shape[0] // MOE_BM

    def zero_copy(row0):
        return pltpu.make_async_copy(zero_ref, xs_hbm.at[pl.ds(pl.multiple_of(row0, MOE_BM), MOE_BM)], zsem)

    @pl.when(i == 0)
    def _():
        zero_ref[...] = jnp.zeros_like(zero_ref)
        first_free = nrow_ref[0] // MOE_BM

        def tail_start(c, carry):
            zero_copy(c * MOE_BM).start()
            return carry

        def tail_wait(c, carry):
            zero_copy(c * MOE_BM).wait()
            return carry

        for e in range(N_EXPERTS):
            @pl.when(cnt_ref[e] > 0)
            def _():
                zero_copy(seg_ref[e] - MOE_BM).start()

        lax.fori_loop(first_free, n_max, tail_start, 0)
        for e in range(N_EXPERTS):
            @pl.when(cnt_ref[e] > 0)
            def _():
                zero_copy(seg_ref[e] - MOE_BM).wait()

        lax.fori_loop(first_free, n_max, tail_wait, 0)

    bm_b = xb_ref.shape[0]
    tok_b0 = n_a * bm

    def wait_rows(slot, rows):
        for _ in range(2):
            pltpu.make_async_copy(h_ref.at[slot, pl.ds(0, rows)], xs_hbm.at[pl.ds(0, rows)],
                                  sem.at[slot]).wait()

    def scatter_rows(slot, rows, tok0):
        def body(r, carry):
            src = h_ref.at[slot, pl.ds(r, 1)]
            pltpu.make_async_copy(src, xs_hbm.at[pl.ds(p1_ref[tok0 + r], 1)], sem.at[slot]).start(priority=0)
            pltpu.make_async_copy(src, xs_hbm.at[pl.ds(p2_ref[tok0 + r], 1)], sem.at[slot]).start(priority=1)
            return carry

        lax.fori_loop(0, rows, body, 0, unroll=8)

    slot = i % 2

    @pl.when((i > 0) & (i - 1 < n_a))
    def _():
        wait_rows(1 - slot, bm)

    @pl.when(i - 1 >= n_a)
    def _():
        wait_rows(1 - slot, bm_b)

    @pl.when(i < n_a)
    def _():
        h_ref[slot, 0:bm, :] = _rms(xa_ref[...], g_ref[...])
        scatter_rows(slot, bm, i * bm)

    @pl.when(i >= n_a)
    def _():
        h_ref[slot, 0:bm_b, :] = _rms(xb_ref[...], g_ref[...])
        scatter_rows(slot, bm_b, tok_b0 + (i - n_a) * bm_b)

    @pl.when(i == n - 1)
    def _():
        wait_rows(slot, bm_b)


def _dispatch(xa, xb, g, pos1, pos2, seg_end, cnt, n_rows_used, n_rows_max):
    (ma, d), mb = xa.shape, xb.shape[0]
    bm_a, bm_b = min(MOE_BM, ma), min(MOE_BM, mb)
    assert ma % bm_a == 0 and mb % bm_b == 0 and bm_b <= bm_a
    n_a, n_b = ma // bm_a, mb // bm_b
    return pl.pallas_call(
        functools.partial(_dispatch_kernel, n_a=n_a),
        grid_spec=pltpu.PrefetchScalarGridSpec(
            num_scalar_prefetch=5, grid=(n_a + n_b,),
            in_specs=[pl.BlockSpec((bm_a, d), lambda i, *_: (jnp.minimum(i, n_a - 1), 0)),
                      pl.BlockSpec((bm_b, d), lambda i, *_: (jnp.maximum(i - n_a, 0), 0)),
                      pl.BlockSpec((1, d), lambda i, *_: (0, 0))],
            out_specs=pl.BlockSpec(memory_space=pl.ANY),
            scratch_shapes=[pltpu.VMEM((2, bm_a, d), F32), pltpu.VMEM((MOE_BM, d), F32),
                            pltpu.SemaphoreType.DMA((2,)), pltpu.SemaphoreType.DMA(())]),
        out_shape=jax.ShapeDtypeStruct((n_rows_max, d), F32),
        compiler_params=_cparams("arbitrary"),
        name="moe_dispatch",
    )(pos1, pos2, seg_end, cnt, n_rows_used, xa, xb, g.reshape(1, d))


def _experts_sorted_kernel(te_ref, nt_ref, nxt_ref, par_ref, xs_ref, wg_hbm, wu_hbm, wd_hbm, ys_ref,
                           wgf_ref, wuf_ref, wdf_ref, wgb_ref, wub_ref, wdb_ref, sem):
    j = pl.program_id(0)
    prev = te_ref[jnp.maximum(j, 1) - 1]

    def copies(e, slot):
        return [pltpu.make_async_copy(hbm.at[e], buf.at[slot], sem.at[slot])
                for hbm, buf in ((wg_hbm, wgf_ref), (wu_hbm, wuf_ref), (wd_hbm, wdf_ref))]

    @pl.when(j == 0)
    def _():
        for c in copies(te_ref[0], par_ref[0]):
            c.start()

    @pl.when((j < nt_ref[0]) & ((j == 0) | (te_ref[j] != prev)))
    def _():
        slot = par_ref[j]
        for c in copies(te_ref[j], slot):
            c.wait()

        @pl.when(nxt_ref[j] >= 0)
        def _():
            for c in copies(nxt_ref[j], 1 - slot):
                c.start()

        wgb_ref[...] = wgf_ref[slot].astype(BF16)
        wub_ref[...] = wuf_ref[slot].astype(BF16)
        wdb_ref[...] = wdf_ref[slot].astype(BF16)

    @pl.when(j < nt_ref[0])
    def _():
        x = xs_ref[...].astype(BF16)
        hg = _dot(x, wgb_ref[...])
        hu = _dot(x, wub_ref[...])
        act = hg * jax.nn.sigmoid(hg) * hu
        ys_ref[...] = _dot(act.astype(BF16), wdb_ref[...])

    @pl.when(j >= nt_ref[0])
    def _():
        ys_ref[...] = jnp.zeros_like(ys_ref)


def _experts_sorted(xs, tile_expert, n_tiles_used, next_expert, slot_parity, wg, wu, wd):
    n_rows, d = xs.shape
    de = wg.shape[2]
    n_tiles = n_rows // MOE_BM
    row_in = lambda j, te, nt, *_: (jnp.minimum(j, nt[0] - 1), 0)
    hbm = pl.BlockSpec(memory_space=pl.ANY)
    return pl.pallas_call(
        _experts_sorted_kernel,
        grid_spec=pltpu.PrefetchScalarGridSpec(
            num_scalar_prefetch=4, grid=(n_tiles,),
            in_specs=[pl.BlockSpec((MOE_BM, d), row_in), hbm, hbm, hbm],
            out_specs=pl.BlockSpec((MOE_BM, d), lambda j, *_: (j, 0)),
            scratch_shapes=[pltpu.VMEM((2, d, de), F32), pltpu.VMEM((2, d, de), F32),
                            pltpu.VMEM((2, de, d), F32),
                            pltpu.VMEM((d, de), BF16), pltpu.VMEM((d, de), BF16), pltpu.VMEM((de, d), BF16),
                            pltpu.SemaphoreType.DMA((2,))]),
        out_shape=jax.ShapeDtypeStruct((n_rows, d), F32),
        compiler_params=_cparams("arbitrary"),
        name="moe_experts_sorted",
    )(tile_expert, n_tiles_used, next_expert, slot_parity, xs, wg, wu, wd)


def _combine_kernel(p1_ref, p2_ref, ys_hbm, x_ref, route_ref, nf_ref, y_ref, buf_ref, sem):
    i = pl.program_id(0)
    n = pl.num_programs(0)
    bm = x_ref.shape[0]

    def issue(tile, slot):
        def body(r, carry):
            t = tile * bm + r
            _row_copy(ys_hbm, p1_ref[t], buf_ref.at[slot, 0], r, sem.at[slot]).start(priority=0)
            _row_copy(ys_hbm, p2_ref[t], buf_ref.at[slot, 1], r, sem.at[slot]).start(priority=1)
            return carry

        lax.fori_loop(0, bm, body, 0, unroll=8)

    @pl.when(i == 0)
    def _():
        issue(0, 0)

    @pl.when(i + 1 < n)
    def _():
        issue(i + 1, (i + 1) % 2)

    slot = i % 2
    for k in range(2):
        pltpu.make_async_copy(ys_hbm.at[pl.ds(0, bm)], buf_ref.at[slot, k], sem.at[slot]).wait()
    lane = lax.broadcasted_iota(jnp.int32, (1, LANES), 1)
    route = route_ref[...]
    w1 = jnp.sum(jnp.where(lane == 2, route, 0.0), axis=1, keepdims=True)
    w2 = jnp.sum(jnp.where(lane == 3, route, 0.0), axis=1, keepdims=True)
    x3 = x_ref[...] + w1 * buf_ref[slot, 0] + w2 * buf_ref[slot, 1]
    y_ref[...] = _rms(x3, nf_ref[...])


def _combine(ys, pos1, pos2, x, route, norm_final):
    m, d = x.shape
    bm = min(MOE_BM, m)
    return pl.pallas_call(
        _combine_kernel,
        grid_spec=pltpu.PrefetchScalarGridSpec(
            num_scalar_prefetch=2, grid=(m // bm,),
            in_specs=[pl.BlockSpec(memory_space=pl.ANY),
                      pl.BlockSpec((bm, d), lambda i, p1, p2: (i, 0)),
                      pl.BlockSpec((bm, LANES), lambda i, p1, p2: (i, 0)),
                      pl.BlockSpec((1, d), lambda i, p1, p2: (0, 0))],
            out_specs=pl.BlockSpec((bm, d), lambda i, p1, p2: (i, 0)),
            scratch_shapes=[pltpu.VMEM((2, 2, bm, d), F32), pltpu.SemaphoreType.DMA((2,))]),
        out_shape=jax.ShapeDtypeStruct((m, d), F32),
        compiler_params=_cparams("arbitrary"),
        name="moe_combine",
    )(pos1, pos2, ys, x, route, norm_final.reshape(1, d))


def _moe_sorted(xa, xb, g, wh, wl, bias, wg, wu, wd, norm_final):
    ma, mb = xa.shape[0], xb.shape[0]
    route_a, rt_a, cnt_a = _router_sorted(xa, g, wh, wl, bias, jnp.zeros((1, LANES), F32))
    route_b, rt_b, counts = _router_sorted(xb, g, wh, wl, bias, cnt_a)
    cnt = counts[0, :N_EXPERTS].astype(jnp.int32)
    padded = (cnt + MOE_BM - 1) // MOE_BM * MOE_BM
    seg_end = jnp.cumsum(padded)
    seg_start = seg_end - padded
    n_tiles_max = (2 * (ma + mb) + MOE_BM - 1) // MOE_BM + N_EXPERTS
    n_rows_max = n_tiles_max * MOE_BM
    n_rows_used = seg_end[-1:]
    n_tiles_used = n_rows_used // MOE_BM
    tile_start = jnp.arange(n_tiles_max, dtype=jnp.int32) * MOE_BM
    tile_expert = jnp.sum((seg_end[None, :] <= tile_start[:, None]).astype(jnp.int32), axis=1)
    last_expert = jnp.max(jnp.where(cnt > 0, jnp.arange(N_EXPERTS, dtype=jnp.int32), 0))
    tile_expert = jnp.minimum(tile_expert, last_expert)
    eidx = jnp.arange(N_EXPERTS, dtype=jnp.int32)
    used = cnt > 0
    later = jnp.where((eidx[None, :] > eidx[:, None]) & used[None, :], eidx[None, :], N_EXPERTS)
    next_used = jnp.min(later, axis=1)
    next_used = jnp.where(next_used == N_EXPERTS, -1, next_used)
    ordinal = jnp.cumsum(used.astype(jnp.int32)) - 1
    onehot = (tile_expert[:, None] == eidx[None, :]).astype(jnp.int32)
    next_expert = jnp.sum(onehot * next_used[None, :], axis=1)
    slot_parity = jnp.sum(onehot * ordinal[None, :], axis=1) % 2
    pa1, pa2 = _plan(rt_a, seg_start)
    pb1, pb2 = _plan(rt_b, seg_start)
    xs = _dispatch(xa, xb, g, jnp.concatenate([pa1, pb1]), jnp.concatenate([pa2, pb2]), seg_end, cnt,
                   n_rows_used, n_rows_max)
    ys = _experts_sorted(xs, tile_expert, n_tiles_used, next_expert, slot_parity, wg, wu, wd)
    return (_combine(ys, pa1, pa2, xa, route_a, norm_final),
            _combine(ys, pb1, pb2, xb, route_b, norm_final))


def _pad_cols(x, n):
    return jnp.pad(x, ((0, 0), (0, n - x.shape[1])))


def _block_diag_ones(n, blk):
    i = jnp.arange(n) // blk
    return (i[:, None] == i[None, :]).astype(BF16)


def kernel(x_prompt, x_sample, mem_prompt, cache_conv, state_shift, state_rwkv, cache_mem_k, cache_mem_v,
           norm_mix, w_in, conv_w, conv_b, conv_ln_g, conv_ln_b, shift_mu, w_decay_up, decay_bias, w_a_up,
           a_bias, w_g_up, k_k, k_a, r_k, lnx_g, lnx_b, w_out, norm_x, norm_mem, w_cq, w_ck, w_cv, w_co,
           norm_ffn, w_route_group, b_route_group, w_route_expert, b_route_expert, w_gate, w_up, w_down,
           norm_final):
    depth = w_in.shape[0]
    batch, seq, d = x_prompt.shape
    dec_batch = x_sample.shape[0]
    assert depth == 1
    assert x_sample.shape[1] == 1 and seq % CHUNK == 0 and seq >= CONV_K - 1
    cw = conv_w.shape[2]
    rw = w_decay_up.shape[2]
    heads = rw // HEAD
    shift_w = shift_mu.shape[1]
    in_w = w_in.shape[2]
    assert in_w == 2 * cw + shift_w and shift_w == 3 * rw + DECAY_LORA + AAA_LORA + GATE_LORA
    assert cw == rw and rw % LORA_PAD == 0
    in_pad = 2 * cw + 3 * rw + LORA_PAD
    qw = 3 * rw + LORA_PAD

    xp = x_prompt.reshape(batch * seq, d)
    xs = x_sample.reshape(dec_batch, d)
    outs = {k: [] for k in ("conv_p", "shift_p", "rwkv_p", "memk_p", "memv_p", "conv_s", "shift_s", "rwkv_s")}
    bd = _block_diag_ones(2 * LANES, HEAD)

    for l in range(depth):
        w_in_b = w_in.astype(BF16)
        w_out_b = w_out.astype(BF16)
        w_cq_b = w_cq.astype(BF16)
        w_ck_b = w_ck.astype(BF16)
        w_cv_b = w_cv.astype(BF16)
        w_co_b = w_co.astype(BF16)
        zeros_l = jnp.zeros((DECAY_LORA, rw), F32)
        wd_pad = jnp.concatenate([w_decay_up[l], zeros_l], axis=0).astype(BF16)
        wa_pad = jnp.concatenate([zeros_l, w_a_up[l]], axis=0).astype(BF16)
        wg_pad = jnp.pad(w_g_up[l], ((0, 2 * LANES - GATE_LORA), (0, 0))).astype(BF16)
        mu_pad = _pad_cols(shift_mu[l].reshape(1, shift_w), qw)
        vec = lambda x: x.reshape(1, rw)
        pp = (mu_pad, wd_pad, wa_pad, wg_pad, vec(decay_bias[l]), vec(a_bias[l]), vec(k_k[l]), vec(k_a[l]),
              vec(r_k[l]), bd)
        w_route = jnp.concatenate([w_route_expert[l].reshape(d, N_EXPERTS), w_route_group[l]], axis=1)
        w_route = _pad_cols(w_route, LANES)
        wr_hi = w_route.astype(BF16)
        wr_lo = (w_route - wr_hi.astype(F32)).astype(BF16)
        b_route = _pad_cols(jnp.concatenate([b_route_expert[l].reshape(1, N_EXPERTS),
                                             b_route_group[l].reshape(1, N_GROUPS)], axis=1), LANES)
        de = w_gate.shape[-1]
        wg_e = w_gate[l].reshape(N_EXPERTS, d, de)
        wu_e = w_up[l].reshape(N_EXPERTS, d, de)
        wd_e = w_down[l].reshape(N_EXPERTS, de, d)

        proj_s = _norm_matmul(xs, norm_mix[l], w_in_b, l, 128, LORA_PAD)
        c_s, conv_new_s = _conv_decode(proj_s, cache_conv, l, conv_w[l], conv_b[l], conv_ln_g[l],
                                       conv_ln_b[l])
        prep_s = _rwkv_prep_decode(proj_s, _pad_cols(state_shift[l], qw), pp, rw)
        o_s, s_s = _rwkv_step(*prep_s[:6], state_rwkv, l)
        xs = _mix_out(c_s, o_s, prep_s[6], prep_s[7], lnx_g[l], lnx_b[l], bd, w_out_b, l, xs)
        qs = _norm_matmul(xs, norm_x[l], w_cq_b, l, 128, d)
        ctx_s = _attn_decode(qs, cache_mem_k[l], cache_mem_v[l])
        xs = _matmul_res(ctx_s, w_co_b, l, xs, 128)
        outs["conv_s"].append(conv_new_s)
        outs["shift_s"].append(proj_s[:, 2 * cw:2 * cw + shift_w])
        outs["rwkv_s"].append(s_s)

        mem2 = mem_prompt.reshape(batch * N_MEM, d)
        mk, mk_b = _norm_matmul_heads(mem2, norm_mem[l], w_ck_b, l, X_HEADS, 256)
        mv, mv_b = _norm_matmul_heads(mem2, norm_mem[l], w_cv_b, l, X_HEADS, 256)
        proj = _norm_matmul(xp, norm_mix[l], w_in_b, l, 1024, 2 * LORA_PAD)
        assert proj.shape[1] >= in_pad
        c_p, conv_new = _conv_prefill(proj, jnp.zeros((batch, CONV_K - 1, cw), F32), conv_w[l], conv_b[l],
                                      conv_ln_g[l], conv_ln_b[l], batch, seq)
        prep = _rwkv_prep_prefill(proj, jnp.zeros((batch, qw), F32), pp, batch, seq, rw)
        o_p, s_p = _rwkv_chunked(*prep[:6], jnp.zeros((batch, heads, HEAD, HEAD), F32), batch, seq)
        shift_new = proj.reshape(batch, seq, proj.shape[1])[:, -1, 2 * cw:2 * cw + shift_w]
        xp = _mix_out(c_p, o_p, prep[6], prep[7], lnx_g[l], lnx_b[l], bd, w_out_b, l, xp)
        qx = _norm_matmul(xp, norm_x[l], w_cq_b, l, 512, d)
        ctx = _attn_prefill(qx, mk_b, mv_b, batch, seq)
        xp = _matmul_res(ctx, w_co_b, l, xp, 512)
        outs["conv_p"].append(conv_new)
        outs["shift_p"].append(shift_new)
        outs["rwkv_p"].append(s_p)
        outs["memk_p"].append(mk.reshape(batch, N_MEM, X_HEADS, d // X_HEADS))
        outs["memv_p"].append(mv.reshape(batch, N_MEM, X_HEADS, d // X_HEADS))

        xp, xs = _moe_sorted(xp, xs, norm_ffn[l], wr_hi, wr_lo, b_route, wg_e, wu_e, wd_e, norm_final)

    y_prompt = xp.reshape(batch, seq, d)
    y_sample = xs.reshape(dec_batch, 1, d)
    st = lambda k: jnp.stack(outs[k])
    return (y_prompt, y_sample, st("conv_p"), st("shift_p"), st("rwkv_p"), st("memk_p"), st("memv_p"),
            st("conv_s"), st("shift_s"), st("rwkv_s"))
```

```python
import functools
import math

import jax
import jax.numpy as jnp
from jax import lax
from jax.experimental import pallas as pl
from jax.experimental.pallas import tpu as pltpu

F32 = jnp.float32
BF16 = jnp.bfloat16

CONV_K = 31
HEAD = 64
PAIR = 2 * HEAD
CHUNK = 64
DECAY_LORA = 64
AAA_LORA = 64
GATE_LORA = 160
LORA_PAD = 512
N_MEM = 256
X_HEADS = 4
N_GROUPS = 4
EXP_PER_GROUP = 8
N_EXPERTS = N_GROUPS * EXP_PER_GROUP
RMS_EPS = 1e-6
LN_EPS = 1e-5
GN_EPS = 64e-5
DECAY_SCALE = math.exp(-0.5)
NEG_BIG = -1e30
MOE_BM = 256
LANES = 128
VMEM_LIMIT = 56 * 1024 * 1024


def _cparams(*sem):
    return pltpu.CompilerParams(dimension_semantics=sem, vmem_limit_bytes=VMEM_LIMIT)


def _dot(a, b):
    return jnp.dot(a, b, preferred_element_type=F32)


def _dot_nt(a, b):
    return lax.dot_general(a, b, (((1,), (1,)), ((), ())), preferred_element_type=F32)


def _split_dot(x, w_bf16):
    hi = x.astype(BF16)
    lo = (x - hi.astype(F32)).astype(BF16)
    return _dot(hi, w_bf16) + _dot(lo, w_bf16)


def _rms(x, g, eps=RMS_EPS):
    return x * lax.rsqrt(jnp.mean(x * x, axis=-1, keepdims=True) + eps) * g


def _norm_mm_kernel(x_ref, g_ref, w_ref, o_ref, xn_ref, *, n_valid):
    j = pl.program_id(1)

    @pl.when(j == 0)
    def _():
        xn_ref[...] = _rms(x_ref[...], g_ref[...]).astype(BF16)

    w = w_ref[0]
    bn = w.shape[1]
    if n_valid % bn:
        col = j * bn + lax.broadcasted_iota(jnp.int32, (1, bn), 1)
        w = jnp.where(col < n_valid, w, jnp.zeros_like(w))
    o_ref[...] = _dot(xn_ref[...], w)


def _norm_matmul(x, g, w, layer, bm, bn):
    m, k = x.shape
    n = w.shape[2]
    bm = min(bm, m)
    n_tiles = pl.cdiv(n, bn)
    return pl.pallas_call(
        functools.partial(_norm_mm_kernel, n_valid=n),
        grid=(m // bm, n_tiles),
        in_specs=[pl.BlockSpec((bm, k), lambda i, j: (i, 0)),
                  pl.BlockSpec((1, k), lambda i, j: (0, 0)),
                  pl.BlockSpec((1, k, bn), lambda i, j: (layer, 0, j))],
        out_specs=pl.BlockSpec((bm, bn), lambda i, j: (i, j)),
        out_shape=jax.ShapeDtypeStruct((m, n_tiles * bn), F32),
        scratch_shapes=[pltpu.VMEM((bm, k), BF16)],
        compiler_params=_cparams("parallel", "arbitrary"),
        name="norm_matmul",
    )(x, g.reshape(1, k), w)


def _norm_mm_heads_kernel(x_ref, g_ref, w_ref, o_ref, ob_ref):
    res = _dot(_rms(x_ref[...], g_ref[...]).astype(BF16), w_ref[0])
    ob_ref[...] = res.astype(BF16)
    dh = o_ref.shape[2]
    for h in range(o_ref.shape[1]):
        o_ref[:, h, :] = res[:, h * dh:(h + 1) * dh]


def _norm_matmul_heads(x, g, w, layer, n_heads, bm):
    m, k = x.shape
    n = w.shape[2]
    bm = min(bm, m)
    return pl.pallas_call(
        _norm_mm_heads_kernel,
        grid=(m // bm,),
        in_specs=[pl.BlockSpec((bm, k), lambda i: (i, 0)),
                  pl.BlockSpec((1, k), lambda i: (0, 0)),
                  pl.BlockSpec((1, k, n), lambda i: (layer, 0, 0))],
        out_specs=[pl.BlockSpec((bm, n_heads, n // n_heads), lambda i: (i, 0, 0)),
                   pl.BlockSpec((bm, n), lambda i: (i, 0))],
        out_shape=[jax.ShapeDtypeStruct((m, n_heads, n // n_heads), F32),
                   jax.ShapeDtypeStruct((m, n), BF16)],
        compiler_params=_cparams("parallel"),
        name="norm_matmul_heads",
    )(x, g.reshape(1, k), w)


def _mm_res_kernel(a_ref, w_ref, res_ref, o_ref):
    o_ref[...] = res_ref[...] + _dot(a_ref[...].astype(BF16), w_ref[0])


def _matmul_res(a, w, layer, res, bm):
    m, n = res.shape
    k = a.shape[1]
    bm = min(bm, m)
    return pl.pallas_call(
        _mm_res_kernel,
        grid=(m // bm,),
        in_specs=[pl.BlockSpec((bm, k), lambda i: (i, 0)),
                  pl.BlockSpec((1, k, n), lambda i: (layer, 0, 0)),
                  pl.BlockSpec((bm, n), lambda i: (i, 0))],
        out_specs=pl.BlockSpec((bm, n), lambda i: (i, 0)),
        out_shape=jax.ShapeDtypeStruct((m, n), F32),
        compiler_params=_cparams("parallel"),
        name="matmul_res",
    )(a, w, res)


def _ln_silu(cf, lg, lb):
    mu = jnp.mean(cf, axis=-1, keepdims=True)
    d = cf - mu
    var = jnp.mean(d * d, axis=-1, keepdims=True)
    y = d * lax.rsqrt(var + LN_EPS) * lg + lb
    return y * jax.nn.sigmoid(y)


def _conv_prefill_kernel(a_ref, g_ref, buf_ref, w_ref, cb_ref, lg_ref, lb_ref, c_ref, nc_ref,
                         uf_ref, cv_ref, sh_ref, *, tt, halo):
    t = pl.program_id(1)
    pad = 32 - halo

    @pl.when(t == 0)
    def _():
        uf_ref[pad:32, :] = buf_ref[0]

    @pl.when(t > 0)
    def _():
        uf_ref[pad:32, :] = uf_ref[tt + pad:tt + 32, :]

    uf_ref[32:32 + tt, :] = a_ref[...] * jax.nn.sigmoid(g_ref[...])

    for sft in range(8):
        n_rows = sh_ref.shape[1] if sft < 7 else sh_ref.shape[1] - 8
        sh_ref[sft, 0:n_rows, :] = uf_ref[pad + sft:pad + sft + n_rows, :]

    width = uf_ref.shape[1]
    rb = 64
    for r0 in range(0, tt, rb):
        for l0 in range(0, width, LANES):
            acc = jnp.zeros((rb, LANES), F32)
            for j in range(CONV_K):
                base = r0 + j - j % 8
                acc = acc + sh_ref[j % 8, base:base + rb, l0:l0 + LANES] * w_ref[j:j + 1, l0:l0 + LANES]
            cv_ref[r0:r0 + rb, l0:l0 + LANES] = acc

    c_ref[...] = _ln_silu(cv_ref[...] + cb_ref[...], lg_ref[...], lb_ref[...]).astype(c_ref.dtype)

    @pl.when(t == pl.num_programs(1) - 1)
    def _():
        nc_ref[0] = uf_ref[tt + pad:tt + 32, :]


def _conv_prefill(proj, conv_buf, conv_w, conv_b, ln_g, ln_b, batch, seq):
    cw = conv_w.shape[1]
    halo = CONV_K - 1
    tt = min(256, seq)
    nt = seq // tt
    row = lambda b, t: (b * nt + t, 0)
    vec = pl.BlockSpec((1, cw), lambda b, t: (0, 0))
    return pl.pallas_call(
        functools.partial(_conv_prefill_kernel, tt=tt, halo=halo),
        grid=(batch, nt),
        in_specs=[pl.BlockSpec((tt, cw), row),
                  pl.BlockSpec((tt, cw), lambda b, t: (b * nt + t, 1)),
                  pl.BlockSpec((1, halo, cw), lambda b, t: (b, 0, 0)),
                  pl.BlockSpec((CONV_K, cw), lambda b, t: (0, 0)),
                  vec, vec, vec],
        out_specs=[pl.BlockSpec((tt, cw), row),
                   pl.BlockSpec((1, halo, cw), lambda b, t: (b, 0, 0))],
        out_shape=[jax.ShapeDtypeStruct((batch * seq, cw), BF16),
                   jax.ShapeDtypeStruct((batch, halo, cw), F32)],
        scratch_shapes=[pltpu.VMEM((tt + 32, cw), F32), pltpu.VMEM((tt, cw), F32),
                        pltpu.VMEM((8, tt + 24, cw), F32)],
        compiler_params=_cparams("parallel", "arbitrary"),
        name="conv_prefill",
    )(proj, proj, conv_buf, conv_w, conv_b.reshape(1, cw), ln_g.reshape(1, cw), ln_b.reshape(1, cw))


def _conv_decode_kernel(a_ref, g_ref, cache_ref, w_ref, cb_ref, lg_ref, lb_ref, c_ref, nc_ref):
    halo = CONV_K - 1
    u = a_ref[...] * jax.nn.sigmoid(g_ref[...])
    acc = u * w_ref[halo:halo + 1, :]
    for j in range(halo):
        acc = acc + cache_ref[0, :, j, :] * w_ref[j:j + 1, :]
    c_ref[...] = _ln_silu(acc + cb_ref[...], lg_ref[...], lb_ref[...]).astype(c_ref.dtype)
    nc_ref[:, 0:halo - 1, :] = cache_ref[0, :, 1:halo, :]
    nc_ref[:, halo - 1, :] = u


def _conv_decode(proj, cache, layer, conv_w, conv_b, ln_g, ln_b):
    _, batch, halo, cw = cache.shape
    bb = 8
    vec = pl.BlockSpec((1, cw), lambda i: (0, 0))
    return pl.pallas_call(
        _conv_decode_kernel,
        grid=(batch // bb,),
        in_specs=[pl.BlockSpec((bb, cw), lambda i: (i, 0)),
                  pl.BlockSpec((bb, cw), lambda i: (i, 1)),
                  pl.BlockSpec((1, bb, halo, cw), lambda i: (layer, i, 0, 0)),
                  pl.BlockSpec((CONV_K, cw), lambda i: (0, 0)),
                  vec, vec, vec],
        out_specs=[pl.BlockSpec((bb, cw), lambda i: (i, 0)),
                   pl.BlockSpec((bb, halo, cw), lambda i: (i, 0, 0))],
        out_shape=[jax.ShapeDtypeStruct((batch, cw), BF16),
                   jax.ShapeDtypeStruct((batch, halo, cw), F32)],
        compiler_params=_cparams("parallel"),
        name="conv_decode",
    )(proj, proj, cache, conv_w, conv_b.reshape(1, cw), ln_g.reshape(1, cw), ln_b.reshape(1, cw))


def _head_sum(x, bd_ref):
    blk = bd_ref.shape[0]
    parts = [_split_dot(x[:, l0:l0 + blk], bd_ref[...]) for l0 in range(0, x.shape[1], blk)]
    return jnp.concatenate(parts, axis=1)


def _prep_math(q, qp, mu_ref, wd_ref, wa_ref, wg_ref, db_ref, ab_ref, kk_ref, ka_ref, rk_ref, bd_ref):
    rw = q[0].shape[1]
    offs = (0, rw, 2 * rw, 3 * rw)
    r, k, v, lo = [x + (xp - x) * mu_ref[:, o:o + x.shape[1]] for x, xp, o in zip(q, qp, offs)]
    pwa = lo[:, 0:LANES]
    pg = lo[:, LANES:3 * LANES]
    dec_in = _dot(jnp.tanh(pwa).astype(BF16), wd_ref[...])
    a_in = _dot(pwa.astype(BF16), wa_ref[...])
    gate = _dot(jax.nn.sigmoid(pg).astype(BF16), wg_ref[...])
    logw = -DECAY_SCALE * jax.nn.sigmoid(db_ref[...] + dec_in)
    a = jax.nn.sigmoid(ab_ref[...] + a_in)
    kk = k * kk_ref[...]
    kk = kk / jnp.maximum(jnp.sqrt(_head_sum(kk * kk, bd_ref)), 1e-12)
    k2 = k * (1.0 + (a - 1.0) * ka_ref[...])
    bonus = _head_sum(r * k2 * rk_ref[...], bd_ref) * v
    return r, logw, k2, v, kk, a, bonus, gate


def _prep_prefill_kernel(r_ref, k_ref, v_ref, lo_ref, sb_ref, mu_ref, wd_ref, wa_ref, wg_ref, db_ref,
                         ab_ref, kk_ref, ka_ref, rk_ref, bd_ref, *rest):
    outs = rest[:8]
    carry_ref = rest[8]
    t = pl.program_id(1)

    @pl.when(t == 0)
    def _():
        carry_ref[0:1, :] = sb_ref[0]

    q = [r_ref[...], k_ref[...], v_ref[...], lo_ref[...]]
    tt = q[0].shape[0]
    first = lax.broadcasted_iota(jnp.int32, (tt, 1), 0) == 0
    qp = []
    off = 0
    for x in q:
        w = x.shape[1]
        qp.append(jnp.where(first, carry_ref[0:1, off:off + w], pltpu.roll(x, 1, 0)))
        off += w
    off = 0
    for x in q:
        w = x.shape[1]
        carry_ref[0:1, off:off + w] = x[tt - 1:tt, :]
        off += w
    res = _prep_math(q, qp, mu_ref, wd_ref, wa_ref, wg_ref, db_ref, ab_ref, kk_ref, ka_ref, rk_ref, bd_ref)
    for o_ref, val in zip(outs, res):
        o_ref[...] = val.astype(o_ref.dtype)


def _prep_decode_kernel(r_ref, k_ref, v_ref, lo_ref, rp_ref, kp_ref, vp_ref, lop_ref, mu_ref, wd_ref,
                        wa_ref, wg_ref, db_ref, ab_ref, kk_ref, ka_ref, rk_ref, bd_ref, *outs):
    q = [r_ref[...], k_ref[...], v_ref[...], lo_ref[...]]
    qp = [rp_ref[...], kp_ref[...], vp_ref[...], lop_ref[...]]
    res = _prep_math(q, qp, mu_ref, wd_ref, wa_ref, wg_ref, db_ref, ab_ref, kk_ref, ka_ref, rk_ref, bd_ref)
    for o_ref, val in zip(outs, res):
        o_ref[...] = val


def _prep_param_specs(rw, idx):
    full = lambda shape: pl.BlockSpec(shape, idx)
    vec = full((1, rw))
    return [full((1, 3 * rw + LORA_PAD)), full((LANES, rw)), full((LANES, rw)), full((2 * LANES, rw)),
            vec, vec, vec, vec, vec, full((2 * LANES, 2 * LANES))]


def _rwkv_prep_prefill(proj, shift_buf, pp, batch, seq, rw):
    tt = min(256, seq)
    nt = seq // tt
    lora_blk = (2 * rw + 3 * rw) // LORA_PAD
    col = lambda c: (lambda b, t: (b * nt + t, c))
    qw = 3 * rw + LORA_PAD
    in_specs = [pl.BlockSpec((tt, rw), col(2)), pl.BlockSpec((tt, rw), col(3)),
                pl.BlockSpec((tt, rw), col(4)), pl.BlockSpec((tt, LORA_PAD), col(lora_blk)),
                pl.BlockSpec((1, 1, qw), lambda b, t: (b, 0, 0))]
    in_specs += _prep_param_specs(rw, lambda b, t: (0, 0))
    out_spec = pl.BlockSpec((tt, rw), col(0))
    return pl.pallas_call(
        _prep_prefill_kernel,
        grid=(batch, nt),
        in_specs=in_specs,
        out_specs=[out_spec] * 8,
        out_shape=[jax.ShapeDtypeStruct((batch * seq, rw), BF16 if i == 3 else F32) for i in range(8)],
        scratch_shapes=[pltpu.VMEM((8, qw), F32)],
        compiler_params=_cparams("parallel", "arbitrary"),
        name="rwkv_prep_prefill",
    )(proj, proj, proj, proj, shift_buf.reshape(batch, 1, qw), *pp)


def _rwkv_prep_decode(proj, shift_state, pp, rw):
    batch = proj.shape[0]
    bb = min(128, batch)
    lora_blk = (2 * rw + 3 * rw) // LORA_PAD
    col = lambda c: (lambda i: (i, c))
    in_specs = [pl.BlockSpec((bb, rw), col(2)), pl.BlockSpec((bb, rw), col(3)),
                pl.BlockSpec((bb, rw), col(4)), pl.BlockSpec((bb, LORA_PAD), col(lora_blk)),
                pl.BlockSpec((bb, rw), col(0)), pl.BlockSpec((bb, rw), col(1)),
                pl.BlockSpec((bb, rw), col(2)), pl.BlockSpec((bb, LORA_PAD), col(3 * rw // LORA_PAD))]
    in_specs += _prep_param_specs(rw, lambda i: (0, 0))
    return pl.pallas_call(
        _prep_decode_kernel,
        grid=(batch // bb,),
        in_specs=in_specs,
        out_specs=[pl.BlockSpec((bb, rw), col(0))] * 8,
        out_shape=[jax.ShapeDtypeStruct((batch, rw), F32)] * 8,
        compiler_params=_cparams("parallel"),
        name="rwkv_prep_decode",
    )(proj, proj, proj, proj, shift_state, shift_state, shift_state, shift_state, *pp)


def _stack2(x, smask):
    return jnp.where(smask, jnp.concatenate([x, x], axis=0), 0.0)


def _rwkv_chunk_kernel(r_ref, lw_ref, k_ref, v_ref, kk_ref, a_ref, s0_ref, o_ref, so_ref, s_ref):
    c = pl.program_id(1)
    cs = r_ref.shape[0]
    n_pairs = r_ref.shape[1] // PAIR
    two = 2 * cs

    @pl.when(c == 0)
    def _():
        z = jnp.zeros((HEAD, HEAD), F32)
        for p in range(n_pairs):
            top = jnp.concatenate([s0_ref[0, 2 * p], z], axis=1)
            bot = jnp.concatenate([z, s0_ref[0, 2 * p + 1]], axis=1)
            s_ref[p] = jnp.concatenate([top, bot], axis=0)

    ri = lax.broadcasted_iota(jnp.int32, (two, two), 0)
    ci = lax.broadcasted_iota(jnp.int32, (two, two), 1)
    strict = ci < ri
    incl = ci <= ri
    eye = (ci == ri).astype(F32)
    smask = (lax.broadcasted_iota(jnp.int32, (two, PAIR), 0) < cs) == (
        lax.broadcasted_iota(jnp.int32, (two, PAIR), 1) < HEAD)
    tri = (lax.broadcasted_iota(jnp.int32, (cs, cs), 1)
           <= lax.broadcasted_iota(jnp.int32, (cs, cs), 0)).astype(BF16)

    lw_all = lw_ref[...]
    lw_hi = lw_all.astype(BF16)
    lw_lo = (lw_all - lw_hi.astype(F32)).astype(BF16)
    cum_all = _dot(tri, lw_hi) + _dot(tri, lw_lo)

    pairs = range(n_pairs)
    cat = jnp.concatenate
    prep = []
    for p in pairs:
        sl = slice(p * PAIR, (p + 1) * PAIR)
        lw = lw_all[:, sl]
        cum = cum_all[:, sl]
        tot = cum[cs - 1:cs, :]
        g_inv = jnp.exp(-cum)
        g_end = jnp.exp(tot - cum)
        kk = kk_ref[:, sl]
        k2 = k_ref[:, sl]
        bb = kk * a_ref[:, sl]
        prep.append(dict(
            g_tot=jnp.exp(tot),
            a_b=_stack2(kk * jnp.exp(cum - lw), smask).astype(BF16),
            r_s=_stack2(r_ref[:, sl] * jnp.exp(cum), smask),
            bk=cat([_stack2(bb * g_inv, smask), _stack2(k2 * g_inv, smask)], axis=0).astype(BF16),
            v_s=_stack2(v_ref[:, sl].astype(F32), smask),
            bg_s=_stack2(bb * g_end, smask).astype(BF16),
            kg_s=_stack2(k2 * g_end, smask).astype(BF16)))
    a_b = [q["a_b"] for q in prep]
    r_s = [q["r_s"] for q in prep]
    v_s = [q["v_s"] for q in prep]
    v_b = [x.astype(BF16) for x in v_s]

    gram = [_dot_nt(cat([a_b[p], r_s[p].astype(BF16)], axis=0), prep[p]["bk"]) for p in pairs]
    l_ab = [jnp.where(strict, g[0:two, 0:two], 0.0) for g in gram]
    l_ak = [jnp.where(strict, g[0:two, two:], 0.0).astype(BF16) for g in gram]
    m_rb = [jnp.where(incl, g[two:, 0:two], 0.0).astype(BF16) for g in gram]
    m_rk = [jnp.where(incl, g[two:, two:], 0.0).astype(BF16) for g in gram]

    tm = [eye - x for x in l_ab]
    pw = [x.astype(BF16) for x in l_ab]
    pw = [_dot(x, x).astype(BF16) for x in pw]
    n = 2
    while 2 * n < cs:
        both = [_dot(cat([t.astype(BF16), x], axis=0), x) for t, x in zip(tm, pw)]
        tm = [t + b[0:two] for t, b in zip(tm, both)]
        pw = [b[two:].astype(BF16) for b in both]
        n *= 2
    tm_b = [(t + _dot(t.astype(BF16), x)).astype(BF16) for t, x in zip(tm, pw)]

    wv = [_dot(cat([l_ak[p], m_rk[p]], axis=0), v_b[p]) for p in pairs]
    ua = [_dot(tm_b[p], cat([wv[p][0:two].astype(BF16), a_b[p]], axis=1)) for p in pairs]
    ua_b = [x.astype(BF16) for x in ua]
    mrb_ua = [_dot(m_rb[p], ua_b[p]) for p in pairs]
    o0 = [wv[p][two:] - mrb_ua[p][:, 0:PAIR] for p in pairs]
    rt = [(r_s[p] - mrb_ua[p][:, PAIR:]).astype(BF16) for p in pairs]

    s_old = [s_ref[p] for p in pairs]
    s_b = [x.astype(BF16) for x in s_old]
    lhs_t = [cat([ua[p][:, 0:PAIR].T, ua[p][:, PAIR:].T, v_s[p].T], axis=0).astype(BF16) for p in pairs]
    t_all = [_dot(lhs_t[p], cat([prep[p]["bg_s"], prep[p]["kg_s"]], axis=1)) for p in pairs]
    s_new = [s_old[p] * prep[p]["g_tot"] - _dot(s_b[p], t_all[p][PAIR:2 * PAIR, 0:PAIR].astype(BF16))
             + t_all[p][2 * PAIR:, PAIR:] - t_all[p][0:PAIR, 0:PAIR] for p in pairs]
    o_st = [o0[p] + _dot_nt(rt[p], s_b[p]) for p in pairs]

    o_ref[...] = cat([x[0:cs, :] + x[cs:two, :] for x in o_st], axis=1)
    s_ref[...] = jnp.stack(s_new, axis=0)

    @pl.when(c == pl.num_programs(1) - 1)
    def _():
        for p in range(n_pairs):
            s = s_ref[p]
            so_ref[0, 2 * p] = s[0:HEAD, 0:HEAD]
            so_ref[0, 2 * p + 1] = s[HEAD:PAIR, HEAD:PAIR]


def _rwkv_chunked(r, logw, k2, v, kk, a, s0, batch, seq):
    rw = r.shape[1]
    nc = seq // CHUNK
    heads = rw // HEAD
    row = pl.BlockSpec((CHUNK, rw), lambda b, c: (b * nc + c, 0))
    st = pl.BlockSpec((1, heads, HEAD, HEAD), lambda b, c: (b, 0, 0, 0))
    return pl.pallas_call(
        _rwkv_chunk_kernel,
        grid=(batch, nc),
        in_specs=[row] * 6 + [st],
        out_specs=[row, st],
        out_shape=[jax.ShapeDtypeStruct((batch * seq, rw), F32),
                   jax.ShapeDtypeStruct((batch, heads, HEAD, HEAD), F32)],
        scratch_shapes=[pltpu.VMEM((rw // PAIR, PAIR, PAIR), F32)],
        compiler_params=_cparams("parallel", "arbitrary"),
        name="rwkv_chunked",
    )(r, logw, k2, v, kk, a, s0)


def _rwkv_step_kernel(r_ref, lw_ref, k_ref, v_ref, kk_ref, a_ref, s_ref, o_ref, so_ref, *, heads):
    bb = s_ref.shape[1]
    eye = (lax.broadcasted_iota(jnp.int32, (HEAD, HEAD), 0)
           == lax.broadcasted_iota(jnp.int32, (HEAD, HEAD), 1))

    def body(bi, carry):
        hs = range(heads)
        rows = [pl.ds(bi * heads + h, 1) for h in hs]
        kk = [kk_ref[r, :] for r in rows]
        s = [s_ref[0, bi, h].astype(F32) for h in hs]
        sa = [jnp.sum(s[h] * kk[h], axis=1, keepdims=True) for h in hs]
        v_col = [jnp.sum(jnp.where(eye, v_ref[rows[h], :], 0.0), axis=1, keepdims=True) for h in hs]
        s_new = [s[h] * jnp.exp(lw_ref[rows[h], :]) - sa[h] * (kk[h] * a_ref[rows[h], :])
                 + v_col[h] * k_ref[rows[h], :] for h in hs]
        for h in hs:
            so_ref[bi, h] = s_new[h]
        o_col = [jnp.sum(s_new[h] * r_ref[rows[h], :], axis=1, keepdims=True) for h in hs]
        o_row = [jnp.sum(jnp.where(eye, o_col[h], 0.0), axis=0, keepdims=True) for h in hs]
        o_ref[pl.ds(pl.multiple_of(bi * heads, heads), heads), :] = jnp.concatenate(o_row, axis=0)
        return carry

    lax.fori_loop(0, bb, body, 0)


def _rwkv_step(r, logw, k2, v, kk, a, state, layer):
    batch, rw = r.shape
    heads = rw // HEAD
    bb = 8
    flat = lambda x: x.reshape(batch * heads, HEAD)
    row = pl.BlockSpec((bb * heads, HEAD), lambda i: (i, 0))
    st_in = pl.BlockSpec((1, bb, heads, HEAD, HEAD), lambda i: (layer, i, 0, 0, 0))
    st_out = pl.BlockSpec((bb, heads, HEAD, HEAD), lambda i: (i, 0, 0, 0))
    o, s_new = pl.pallas_call(
        functools.partial(_rwkv_step_kernel, heads=heads),
        grid=(batch // bb,),
        in_specs=[row] * 6 + [st_in],
        out_specs=[row, st_out],
        out_shape=[jax.ShapeDtypeStruct((batch * heads, HEAD), F32),
                   jax.ShapeDtypeStruct(state.shape[1:], F32)],
        compiler_params=_cparams("parallel"),
        name="rwkv_step",
    )(flat(r), flat(logw), flat(k2), flat(v), flat(kk), flat(a), state)
    return o.reshape(batch, rw), s_new


def _mix_out_kernel(c_ref, o_ref, bonus_ref, gate_ref, lg_ref, lb_ref, bd_ref, wc_ref, wo_ref, x_ref, y_ref):
    o = o_ref[...]
    mu = _head_sum(o, bd_ref) * (1.0 / HEAD)
    d = o - mu
    var = _head_sum(d * d, bd_ref) * (1.0 / HEAD)
    y = d * lax.rsqrt(var + GN_EPS) * lg_ref[...] + lb_ref[...]
    om = ((y + bonus_ref[...]) * gate_ref[...]).astype(BF16)
    y_ref[...] = x_ref[...] + _dot(c_ref[...], wc_ref[0]) + _dot(om, wo_ref[0])


def _mix_out(c, o, bonus, gate, lnx_g, lnx_b, bd, w_out, layer, x):
    m, d = x.shape
    cw, rw = c.shape[1], o.shape[1]
    assert cw == rw and w_out.shape[1] == cw + rw
    bm = min(512, m)
    row = lambda w: pl.BlockSpec((bm, w), lambda i: (i, 0))
    vec = pl.BlockSpec((1, rw), lambda i: (0, 0))
    return pl.pallas_call(
        _mix_out_kernel,
        grid=(m // bm,),
        in_specs=[row(cw), row(rw), row(rw), row(rw), vec, vec, pl.BlockSpec(bd.shape, lambda i: (0, 0)),
                  pl.BlockSpec((1, cw, d), lambda i: (layer, 0, 0)),
                  pl.BlockSpec((1, rw, d), lambda i: (layer, 1, 0)),
                  row(d)],
        out_specs=row(d),
        out_shape=jax.ShapeDtypeStruct((m, d), F32),
        compiler_params=_cparams("parallel"),
        name="mix_out",
    )(c, o, bonus, gate, lnx_g.reshape(1, rw), lnx_b.reshape(1, rw), bd, w_out, w_out, x)


def _attn_prefill_kernel(q_ref, k_ref, v_ref, w_ref, x_ref, o_ref, *, n_heads):
    d = q_ref.shape[1] // n_heads
    scale = d ** -0.5
    ctx = []
    for h in range(n_heads):
        sl = slice(h * d, (h + 1) * d)
        s = _dot_nt(q_ref[:, sl].astype(BF16), k_ref[:, sl]) * scale
        p = jnp.exp(s - jnp.max(s, axis=-1, keepdims=True))
        att = p / jnp.sum(p, axis=-1, keepdims=True)
        ctx.append(_dot(att.astype(BF16), v_ref[:, sl]).astype(BF16))
    o_ref[...] = x_ref[...] + _dot(jnp.concatenate(ctx, axis=1), w_ref[0])


def _attn_prefill(q, mem_k, mem_v, w_co, layer, x, batch, seq):
    d = q.shape[1]
    tt = min(512, seq)
    nt = seq // tt
    kv = pl.BlockSpec((N_MEM, d), lambda b, t: (b, 0))
    row = pl.BlockSpec((tt, d), lambda b, t: (b * nt + t, 0))
    return pl.pallas_call(
        functools.partial(_attn_prefill_kernel, n_heads=X_HEADS),
        grid=(batch, nt),
        in_specs=[row, kv, kv, pl.BlockSpec((1, d, d), lambda b, t: (layer, 0, 0)), row],
        out_specs=row,
        out_shape=jax.ShapeDtypeStruct((batch * seq, d), F32),
        compiler_params=_cparams("parallel", "arbitrary"),
        name="attn_prefill",
    )(q, mem_k, mem_v, w_co, x)


def _decode_attn_rows(qs, k_at, v_at, n_heads):
    d = qs[0].shape[1] // n_heads
    scale = d ** -0.5
    ids = [(i, h) for i in range(len(qs)) for h in range(n_heads)]
    s = [jnp.sum(k_at(i, h) * qs[i][:, h * d:(h + 1) * d], axis=1, keepdims=True) * scale for i, h in ids]
    p = [jnp.exp(x - jnp.max(x, axis=0, keepdims=True)) for x in s]
    att = [x / jnp.sum(x, axis=0, keepdims=True) for x in p]
    ctx = [jnp.sum(a * v_at(i, h), axis=0, keepdims=True) for a, (i, h) in zip(att, ids)]
    return [ctx[i * n_heads:(i + 1) * n_heads] for i in range(len(qs))]


def _attn_decode_kernel(q_ref, k_ref, v_ref, o_ref):
    n_heads, d = k_ref.shape[2], k_ref.shape[3]
    bb = q_ref.shape[0]
    ctx = _decode_attn_rows([q_ref[bi] for bi in range(bb)], lambda i, h: k_ref[i, :, h, :],
                            lambda i, h: v_ref[i, :, h, :], n_heads)
    for bi in range(bb):
        for h in range(n_heads):
            o_ref[bi, :, h * d:(h + 1) * d] = ctx[bi][h]


def _attn_decode(q, cache_k, cache_v):
    batch, d = q.shape
    bb = 2
    kv = pl.BlockSpec((bb, N_MEM, X_HEADS, d // X_HEADS), lambda i: (i, 0, 0, 0))
    row = pl.BlockSpec((bb, 1, d), lambda i: (i, 0, 0))
    out = pl.pallas_call(
        _attn_decode_kernel,
        grid=(batch // bb,),
        in_specs=[row, kv, kv],
        out_specs=row,
        out_shape=jax.ShapeDtypeStruct((batch, 1, d), F32),
        compiler_params=_cparams("parallel"),
        name="attn_decode",
    )(q.reshape(batch, 1, d), cache_k, cache_v)
    return out.reshape(batch, d)


def _route(x_ref, g_ref, wh_ref, wl_ref, b_ref):
    h = _rms(x_ref[...], g_ref[...])
    hh = h.astype(BF16)
    hl = (h - hh.astype(F32)).astype(BF16)
    logits = _dot(hh, wh_ref[...]) + _dot(hl, wh_ref[...]) + _dot(hh, wl_ref[...]) + b_ref[...]
    lane = lax.broadcasted_iota(jnp.int32, (1, LANES), 1).astype(F32)
    is_g = (lane >= N_EXPERTS) & (lane < N_EXPERTS + N_GROUPS)
    lgm = jnp.where(is_g, logits, NEG_BIG)
    gmax = jnp.max(lgm, axis=1, keepdims=True)
    gsum = jnp.sum(jnp.where(is_g, jnp.exp(lgm - gmax), 0.0), axis=1, keepdims=True)
    g_val = 1.0 / gsum
    g_idx = jnp.min(jnp.where(is_g & (lgm == gmax), lane - N_EXPERTS, 1e9), axis=1, keepdims=True)
    in_grp = (lane < N_EXPERTS) & (jnp.floor(lane * (1.0 / EXP_PER_GROUP)) == g_idx)
    le = jnp.where(in_grp, logits, NEG_BIG)
    m1 = jnp.max(le, axis=1, keepdims=True)
    i1 = jnp.min(jnp.where(in_grp & (le == m1), lane, 1e9), axis=1, keepdims=True)
    rest = in_grp & (lane != i1)
    le2 = jnp.where(rest, logits, NEG_BIG)
    m2 = jnp.max(le2, axis=1, keepdims=True)
    i2 = jnp.min(jnp.where(rest & (le2 == m2), lane, 1e9), axis=1, keepdims=True)
    e2 = jnp.exp(m2 - m1)
    den = 1.0 + e2
    w1 = (1.0 / den) * g_val
    w2 = (e2 / den) * g_val
    return h, lane, i1, i2, w1, w2


def _router_sorted_kernel(x_ref, g_ref, wh_ref, wl_ref, b_ref, init_ref, route_ref, rt_ref, cnt_ref, run_ref):
    @pl.when(pl.program_id(0) == 0)
    def _():
        run_ref[0:1, :] = init_ref[...]

    h, lane, i1, i2, w1, w2 = _route(x_ref, g_ref, wh_ref, wl_ref, b_ref)
    bm = h.shape[0]
    oh1 = lane == i1
    oh2 = lane == i2
    sel = (oh1 | oh2).astype(BF16)
    before = (lax.broadcasted_iota(jnp.int32, (bm, bm), 1)
              < lax.broadcasted_iota(jnp.int32, (bm, bm), 0)).astype(BF16)
    base = run_ref[0:1, :] + _dot(before, sel)
    rank1 = jnp.sum(jnp.where(oh1, base, 0.0), axis=1, keepdims=True)
    rank2 = jnp.sum(jnp.where(oh2, base, 0.0), axis=1, keepdims=True)
    total = run_ref[0:1, :] + jnp.sum(sel.astype(F32), axis=0, keepdims=True)
    run_ref[0:1, :] = total
    cnt_ref[...] = total
    route = jnp.zeros((bm, LANES), F32)
    for idx, val in enumerate((i1, i2, w1, w2, rank1, rank2)):
        route = jnp.where(lane == idx, val, route)
    route_ref[...] = route
    for r0 in range(0, bm, LANES):
        rt_ref[:, r0:r0 + LANES] = route[r0:r0 + LANES, :].T[0:8, :]


def _router_sorted(x, g, wh, wl, bias, init_counts):
    m, d = x.shape
    bm = min(512, m)
    return pl.pallas_call(
        _router_sorted_kernel,
        grid=(m // bm,),
        in_specs=[pl.BlockSpec((bm, d), lambda i: (i, 0)),
                  pl.BlockSpec((1, d), lambda i: (0, 0)),
                  pl.BlockSpec((d, LANES), lambda i: (0, 0)),
                  pl.BlockSpec((d, LANES), lambda i: (0, 0)),
                  pl.BlockSpec((1, LANES), lambda i: (0, 0)),
                  pl.BlockSpec((1, LANES), lambda i: (0, 0))],
        out_specs=[pl.BlockSpec((bm, LANES), lambda i: (i, 0)),
                   pl.BlockSpec((8, bm), lambda i: (0, i)),
                   pl.BlockSpec((1, LANES), lambda i: (0, 0))],
        out_shape=[jax.ShapeDtypeStruct((m, LANES), F32),
                   jax.ShapeDtypeStruct((8, m), F32),
                   jax.ShapeDtypeStruct((1, LANES), F32)],
        scratch_shapes=[pltpu.VMEM((8, LANES), F32)],
        compiler_params=_cparams("arbitrary"),
        name="moe_router_sorted",
    )(x, g.reshape(1, d), wh, wl, bias, init_counts)


def _plan_kernel(seg_ref, rt_ref, pos_ref):
    rt = rt_ref[...]
    rows = []
    for e_row, r_row in ((0, 4), (1, 5)):
        e = rt[e_row:e_row + 1, :]
        start = jnp.zeros_like(e)
        for k in range(N_EXPERTS):
            start = jnp.where(e == k, seg_ref[k].astype(F32), start)
        rows.append((start + rt[r_row:r_row + 1, :]).astype(jnp.int32))
    pos_ref[...] = jnp.concatenate(rows + [jnp.zeros((6, rt.shape[1]), jnp.int32)], axis=0)


def _plan(route_t, seg_start):
    m = route_t.shape[1]
    bt = min(2048, m)
    pos = pl.pallas_call(
        _plan_kernel,
        grid_spec=pltpu.PrefetchScalarGridSpec(
            num_scalar_prefetch=1, grid=(m // bt,),
            in_specs=[pl.BlockSpec((8, bt), lambda i, seg: (0, i))],
            out_specs=pl.BlockSpec((8, bt), lambda i, seg: (0, i))),
        out_shape=jax.ShapeDtypeStruct((8, m), jnp.int32),
        compiler_params=_cparams("arbitrary"),
        name="moe_plan",
    )(seg_start, route_t)
    return pos[0], pos[1]


def _row_copy(src_hbm, src_row, dst, dst_row, sem):
    return pltpu.make_async_copy(src_hbm.at[pl.ds(src_row, 1)], dst.at[pl.ds(dst_row, 1)], sem)


def _dispatch_kernel(p1_ref, p2_ref, seg_ref, cnt_ref, nrow_ref, xa_ref, xb_ref, g_ref, xs_hbm, h_ref,
                     zero_ref, sem, zsem, *, n_a):
    i = pl.program_id(0)
    n = pl.num_programs(0)
    bm = xa_ref.shape[0]
    n_max = xs_hbm.shape[0] // MOE_BM

    def zero_copy(row0):
        return pltpu.make_async_copy(zero_ref, xs_hbm.at[pl.ds(pl.multiple_of(row0, MOE_BM), MOE_BM)], zsem)

    @pl.when(i == 0)
    def _():
        zero_ref[...] = jnp.zeros_like(zero_ref)
        first_free = nrow_ref[0] // MOE_BM

        def tail_start(c, carry):
            zero_copy(c * MOE_BM).start()
            return carry

        def tail_wait(c, carry):
            zero_copy(c * MOE_BM).wait()
            return carry

        for e in range(N_EXPERTS):
            @pl.when(cnt_ref[e] > 0)
            def _():
                zero_copy(seg_ref[e] - MOE_BM).start()

        lax.fori_loop(first_free, n_max, tail_start, 0)
        for e in range(N_EXPERTS):
            @pl.when(cnt_ref[e] > 0)
            def _():
                zero_copy(seg_ref[e] - MOE_BM).wait()

        lax.fori_loop(first_free, n_max, tail_wait, 0)

    bm_b = xb_ref.shape[0]
    tok_b0 = n_a * bm

    def wait_rows(slot, rows):
        for _ in range(2):
            pltpu.make_async_copy(h_ref.at[slot, pl.ds(0, rows)], xs_hbm.at[pl.ds(0, rows)],
                                  sem.at[slot]).wait()

    def scatter_rows(slot, rows, tok0):
        def body(r, carry):
            src = h_ref.at[slot, pl.ds(r, 1)]
            pltpu.make_async_copy(src, xs_hbm.at[pl.ds(p1_ref[tok0 + r], 1)], sem.at[slot]).start(priority=0)
            pltpu.make_async_copy(src, xs_hbm.at[pl.ds(p2_ref[tok0 + r], 1)], sem.at[slot]).start(priority=1)
            return carry

        lax.fori_loop(0, rows, body, 0, unroll=8)

    slot = i % 2

    @pl.when((i > 0) & (i - 1 < n_a))
    def _():
        wait_rows(1 - slot, bm)

    @pl.when(i - 1 >= n_a)
    def _():
        wait_rows(1 - slot, bm_b)

    @pl.when(i < n_a)
    def _():
        h_ref[slot, 0:bm, :] = _rms(xa_ref[...], g_ref[...])
        scatter_rows(slot, bm, i * bm)

    @pl.when(i >= n_a)
    def _():
        h_ref[slot, 0:bm_b, :] = _rms(xb_ref[...], g_ref[...])
        scatter_rows(slot, bm_b, tok_b0 + (i - n_a) * bm_b)

    @pl.when(i == n - 1)
    def _():
        wait_rows(slot, bm_b)


def _dispatch(xa, xb, g, pos1, pos2, seg_end, cnt, n_rows_used, n_rows_max):
    (ma, d), mb = xa.shape, xb.shape[0]
    bm_a, bm_b = min(MOE_BM, ma), min(MOE_BM, mb)
    assert ma % bm_a == 0 and mb % bm_b == 0 and bm_b <= bm_a
    n_a, n_b = ma // bm_a, mb // bm_b
    return pl.pallas_call(
        functools.partial(_dispatch_kernel, n_a=n_a),
        grid_spec=pltpu.PrefetchScalarGridSpec(
            num_scalar_prefetch=5, grid=(n_a + n_b,),
            in_specs=[pl.BlockSpec((bm_a, d), lambda i, *_: (jnp.minimum(i, n_a - 1), 0)),
                      pl.BlockSpec((bm_b, d), lambda i, *_: (jnp.maximum(i - n_a, 0), 0)),
                      pl.BlockSpec((1, d), lambda i, *_: (0, 0))],
            out_specs=pl.BlockSpec(memory_space=pl.ANY),
            scratch_shapes=[pltpu.VMEM((2, bm_a, d), F32), pltpu.VMEM((MOE_BM, d), F32),
                            pltpu.SemaphoreType.DMA((2,)), pltpu.SemaphoreType.DMA(())]),
        out_shape=jax.ShapeDtypeStruct((n_rows_max, d), F32),
        compiler_params=_cparams("arbitrary"),
        name="moe_dispatch",
    )(pos1, pos2, seg_end, cnt, n_rows_used, xa, xb, g.reshape(1, d))


def _experts_sorted_kernel(te_ref, nt_ref, nxt_ref, par_ref, xs_ref, wg_hbm, wu_hbm, wd_hbm, ys_ref,
                           wgf_ref, wuf_ref, wdf_ref, wgb_ref, wub_ref, wdb_ref, sem):
    j = pl.program_id(0)
    prev = te_ref[jnp.maximum(j, 1) - 1]

    def copies(e, slot):
        return [pltpu.make_async_copy(hbm.at[e], buf.at[slot], sem.at[slot])
                for hbm, buf in ((wg_hbm, wgf_ref), (wu_hbm, wuf_ref), (wd_hbm, wdf_ref))]

    @pl.when(j == 0)
    def _():
        for c in copies(te_ref[0], par_ref[0]):
            c.start()

    @pl.when((j < nt_ref[0]) & ((j == 0) | (te_ref[j] != prev)))
    def _():
        slot = par_ref[j]
        for c in copies(te_ref[j], slot):
            c.wait()

        @pl.when(nxt_ref[j] >= 0)
        def _():
            for c in copies(nxt_ref[j], 1 - slot):
                c.start()

        wgb_ref[...] = wgf_ref[slot].astype(BF16)
        wub_ref[...] = wuf_ref[slot].astype(BF16)
        wdb_ref[...] = wdf_ref[slot].astype(BF16)

    @pl.when(j < nt_ref[0])
    def _():
        x = xs_ref[...].astype(BF16)
        hg = _dot(x, wgb_ref[...])
        hu = _dot(x, wub_ref[...])
        act = hg * jax.nn.sigmoid(hg) * hu
        ys_ref[...] = _dot(act.astype(BF16), wdb_ref[...])

    @pl.when(j >= nt_ref[0])
    def _():
        ys_ref[...] = jnp.zeros_like(ys_ref)


def _experts_sorted(xs, tile_expert, n_tiles_used, next_expert, slot_parity, wg, wu, wd):
    n_rows, d = xs.shape
    de = wg.shape[2]
    n_tiles = n_rows // MOE_BM
    row_in = lambda j, te, nt, *_: (jnp.minimum(j, nt[0] - 1), 0)
    hbm = pl.BlockSpec(memory_space=pl.ANY)
    return pl.pallas_call(
        _experts_sorted_kernel,
        grid_spec=pltpu.PrefetchScalarGridSpec(
            num_scalar_prefetch=4, grid=(n_tiles,),
            in_specs=[pl.BlockSpec((MOE_BM, d), row_in), hbm, hbm, hbm],
            out_specs=pl.BlockSpec((MOE_BM, d), lambda j, *_: (j, 0)),
            scratch_shapes=[pltpu.VMEM((2, d, de), F32), pltpu.VMEM((2, d, de), F32),
                            pltpu.VMEM((2, de, d), F32),
                            pltpu.VMEM((d, de), BF16), pltpu.VMEM((d, de), BF16), pltpu.VMEM((de, d), BF16),
                            pltpu.SemaphoreType.DMA((2,))]),
        out_shape=jax.ShapeDtypeStruct((n_rows, d), F32),
        compiler_params=_cparams("arbitrary"),
        name="moe_experts_sorted",
    )(tile_expert, n_tiles_used, next_expert, slot_parity, xs, wg, wu, wd)


def _combine_kernel(p1_ref, p2_ref, ys_hbm, x_ref, route_ref, nf_ref, y_ref, buf_ref, sem):
    i = pl.program_id(0)
    n = pl.num_programs(0)
    bm = x_ref.shape[0]

    def issue(tile, slot):
        def body(r, carry):
            t = tile * bm + r
            _row_copy(ys_hbm, p1_ref[t], buf_ref.at[slot, 0], r, sem.at[slot]).start(priority=0)
            _row_copy(ys_hbm, p2_ref[t], buf_ref.at[slot, 1], r, sem.at[slot]).start(priority=1)
            return carry

        lax.fori_loop(0, bm, body, 0, unroll=8)

    @pl.when(i == 0)
    def _():
        issue(0, 0)

    @pl.when(i + 1 < n)
    def _():
        issue(i + 1, (i + 1) % 2)

    slot = i % 2
    for k in range(2):
        pltpu.make_async_copy(ys_hbm.at[pl.ds(0, bm)], buf_ref.at[slot, k], sem.at[slot]).wait()
    lane = lax.broadcasted_iota(jnp.int32, (1, LANES), 1)
    route = route_ref[...]
    w1 = jnp.sum(jnp.where(lane == 2, route, 0.0), axis=1, keepdims=True)
    w2 = jnp.sum(jnp.where(lane == 3, route, 0.0), axis=1, keepdims=True)
    x3 = x_ref[...] + w1 * buf_ref[slot, 0] + w2 * buf_ref[slot, 1]
    y_ref[...] = _rms(x3, nf_ref[...])


def _combine(ys, pos1, pos2, x, route, norm_final):
    m, d = x.shape
    bm = min(MOE_BM, m)
    return pl.pallas_call(
        _combine_kernel,
        grid_spec=pltpu.PrefetchScalarGridSpec(
            num_scalar_prefetch=2, grid=(m // bm,),
            in_specs=[pl.BlockSpec(memory_space=pl.ANY),
                      pl.BlockSpec((bm, d), lambda i, p1, p2: (i, 0)),
                      pl.BlockSpec((bm, LANES), lambda i, p1, p2: (i, 0)),
                      pl.BlockSpec((1, d), lambda i, p1, p2: (0, 0))],
            out_specs=pl.BlockSpec((bm, d), lambda i, p1, p2: (i, 0)),
            scratch_shapes=[pltpu.VMEM((2, 2, bm, d), F32), pltpu.SemaphoreType.DMA((2,))]),
        out_shape=jax.ShapeDtypeStruct((m, d), F32),
        compiler_params=_cparams("arbitrary"),
        name="moe_combine",
    )(pos1, pos2, ys, x, route, norm_final.reshape(1, d))


def _moe_sorted(xa, xb, g, wh, wl, bias, wg, wu, wd, norm_final):
    ma, mb = xa.shape[0], xb.shape[0]
    route_a, rt_a, cnt_a = _router_sorted(xa, g, wh, wl, bias, jnp.zeros((1, LANES), F32))
    route_b, rt_b, counts = _router_sorted(xb, g, wh, wl, bias, cnt_a)
    cnt = counts[0, :N_EXPERTS].astype(jnp.int32)
    padded = (cnt + MOE_BM - 1) // MOE_BM * MOE_BM
    seg_end = jnp.cumsum(padded)
    seg_start = seg_end - padded
    n_tiles_max = (2 * (ma + mb) + MOE_BM - 1) // MOE_BM + N_EXPERTS
    n_rows_max = n_tiles_max * MOE_BM
    n_rows_used = seg_end[-1:]
    n_tiles_used = n_rows_used // MOE_BM
    tile_start = jnp.arange(n_tiles_max, dtype=jnp.int32) * MOE_BM
    tile_expert = jnp.sum((seg_end[None, :] <= tile_start[:, None]).astype(jnp.int32), axis=1)
    last_expert = jnp.max(jnp.where(cnt > 0, jnp.arange(N_EXPERTS, dtype=jnp.int32), 0))
    tile_expert = jnp.minimum(tile_expert, last_expert)
    eidx = jnp.arange(N_EXPERTS, dtype=jnp.int32)
    used = cnt > 0
    later = jnp.where((eidx[None, :] > eidx[:, None]) & used[None, :], eidx[None, :], N_EXPERTS)
    next_used = jnp.min(later, axis=1)
    next_used = jnp.where(next_used == N_EXPERTS, -1, next_used)
    ordinal = jnp.cumsum(used.astype(jnp.int32)) - 1
    onehot = (tile_expert[:, None] == eidx[None, :]).astype(jnp.int32)
    next_expert = jnp.sum(onehot * next_used[None, :], axis=1)
    slot_parity = jnp.sum(onehot * ordinal[None, :], axis=1) % 2
    pa1, pa2 = _plan(rt_a, seg_start)
    pb1, pb2 = _plan(rt_b, seg_start)
    xs = _dispatch(xa, xb, g, jnp.concatenate([pa1, pb1]), jnp.concatenate([pa2, pb2]), seg_end, cnt,
                   n_rows_used, n_rows_max)
    ys = _experts_sorted(xs, tile_expert, n_tiles_used, next_expert, slot_parity, wg, wu, wd)
    return (_combine(ys, pa1, pa2, xa, route_a, norm_final),
            _combine(ys, pb1, pb2, xb, route_b, norm_final))


def _pad_cols(x, n):
    return jnp.pad(x, ((0, 0), (0, n - x.shape[1])))


def _block_diag_ones(n, blk):
    i = jnp.arange(n) // blk
    return (i[:, None] == i[None, :]).astype(BF16)


def kernel(x_prompt, x_sample, mem_prompt, cache_conv, state_shift, state_rwkv, cache_mem_k, cache_mem_v,
           norm_mix, w_in, conv_w, conv_b, conv_ln_g, conv_ln_b, shift_mu, w_decay_up, decay_bias, w_a_up,
           a_bias, w_g_up, k_k, k_a, r_k, lnx_g, lnx_b, w_out, norm_x, norm_mem, w_cq, w_ck, w_cv, w_co,
           norm_ffn, w_route_group, b_route_group, w_route_expert, b_route_expert, w_gate, w_up, w_down,
           norm_final):
    depth = w_in.shape[0]
    batch, seq, d = x_prompt.shape
    dec_batch = x_sample.shape[0]
    assert depth == 1
    assert x_sample.shape[1] == 1 and seq % CHUNK == 0 and seq >= CONV_K - 1
    cw = conv_w.shape[2]
    rw = w_decay_up.shape[2]
    heads = rw // HEAD
    shift_w = shift_mu.shape[1]
    in_w = w_in.shape[2]
    assert in_w == 2 * cw + shift_w and shift_w == 3 * rw + DECAY_LORA + AAA_LORA + GATE_LORA
    assert cw == rw and rw % LORA_PAD == 0
    in_pad = 2 * cw + 3 * rw + LORA_PAD
    qw = 3 * rw + LORA_PAD

    xp = x_prompt.reshape(batch * seq, d)
    xs = x_sample.reshape(dec_batch, d)
    outs = {k: [] for k in ("conv_p", "shift_p", "rwkv_p", "memk_p", "memv_p", "conv_s", "shift_s", "rwkv_s")}
    bd = _block_diag_ones(2 * LANES, HEAD)

    for l in range(depth):
        w_in_b = w_in.astype(BF16)
        w_out_b = w_out.astype(BF16)
        w_cq_b = w_cq.astype(BF16)
        w_ck_b = w_ck.astype(BF16)
        w_cv_b = w_cv.astype(BF16)
        w_co_b = w_co.astype(BF16)
        zeros_l = jnp.zeros((DECAY_LORA, rw), F32)
        wd_pad = jnp.concatenate([w_decay_up[l], zeros_l], axis=0).astype(BF16)
        wa_pad = jnp.concatenate([zeros_l, w_a_up[l]], axis=0).astype(BF16)
        wg_pad = jnp.pad(w_g_up[l], ((0, 2 * LANES - GATE_LORA), (0, 0))).astype(BF16)
        mu_pad = _pad_cols(shift_mu[l].reshape(1, shift_w), qw)
        vec = lambda x: x.reshape(1, rw)
        pp = (mu_pad, wd_pad, wa_pad, wg_pad, vec(decay_bias[l]), vec(a_bias[l]), vec(k_k[l]), vec(k_a[l]),
              vec(r_k[l]), bd)
        w_route = jnp.concatenate([w_route_expert[l].reshape(d, N_EXPERTS), w_route_group[l]], axis=1)
        w_route = _pad_cols(w_route, LANES)
        wr_hi = w_route.astype(BF16)
        wr_lo = (w_route - wr_hi.astype(F32)).astype(BF16)
        b_route = _pad_cols(jnp.concatenate([b_route_expert[l].reshape(1, N_EXPERTS),
                                             b_route_group[l].reshape(1, N_GROUPS)], axis=1), LANES)
        de = w_gate.shape[-1]
        wg_e = w_gate[l].reshape(N_EXPERTS, d, de)
        wu_e = w_up[l].reshape(N_EXPERTS, d, de)
        wd_e = w_down[l].reshape(N_EXPERTS, de, d)

        proj_s = _norm_matmul(xs, norm_mix[l], w_in_b, l, 128, LORA_PAD)
        c_s, conv_new_s = _conv_decode(proj_s, cache_conv, l, conv_w[l], conv_b[l], conv_ln_g[l],
                                       conv_ln_b[l])
        prep_s = _rwkv_prep_decode(proj_s, _pad_cols(state_shift[l], qw), pp, rw)
        o_s, s_s = _rwkv_step(*prep_s[:6], state_rwkv, l)
        xs = _mix_out(c_s, o_s, prep_s[6], prep_s[7], lnx_g[l], lnx_b[l], bd, w_out_b, l, xs)
        qs = _norm_matmul(xs, norm_x[l], w_cq_b, l, 128, d)
        ctx_s = _attn_decode(qs, cache_mem_k[l], cache_mem_v[l])
        xs = _matmul_res(ctx_s, w_co_b, l, xs, 128)
        outs["conv_s"].append(conv_new_s)
        outs["shift_s"].append(proj_s[:, 2 * cw:2 * cw + shift_w])
        outs["rwkv_s"].append(s_s)

        mem2 = mem_prompt.reshape(batch * N_MEM, d)
        mk, mk_b = _norm_matmul_heads(mem2, norm_mem[l], w_ck_b, l, X_HEADS, 256)
        mv, mv_b = _norm_matmul_heads(mem2, norm_mem[l], w_cv_b, l, X_HEADS, 256)
        proj = _norm_matmul(xp, norm_mix[l], w_in_b, l, 1024, LORA_PAD)
        assert proj.shape[1] == in_pad
        c_p, conv_new = _conv_prefill(proj, jnp.zeros((batch, CONV_K - 1, cw), F32), conv_w[l], conv_b[l],
                                      conv_ln_g[l], conv_ln_b[l], batch, seq)
        prep = _rwkv_prep_prefill(proj, jnp.zeros((batch, qw), F32), pp, batch, seq, rw)
        o_p, s_p = _rwkv_chunked(*prep[:6], jnp.zeros((batch, heads, HEAD, HEAD), F32), batch, seq)
        shift_new = proj.reshape(batch, seq, in_pad)[:, -1, 2 * cw:2 * cw + shift_w]
        xp = _mix_out(c_p, o_p, prep[6], prep[7], lnx_g[l], lnx_b[l], bd, w_out_b, l, xp)
        qx = _norm_matmul(xp, norm_x[l], w_cq_b, l, 512, d)
        xp = _attn_prefill(qx, mk_b, mv_b, w_co_b, l, xp, batch, seq)
        outs["conv_p"].append(conv_new)
        outs["shift_p"].append(shift_new)
        outs["rwkv_p"].append(s_p)
        outs["memk_p"].append(mk.reshape(batch, N_MEM, X_HEADS, d // X_HEADS))
        outs["memv_p"].append(mv.reshape(batch, N_MEM, X_HEADS, d // X_HEADS))

        xp, xs = _moe_sorted(xp, xs, norm_ffn[l], wr_hi, wr_lo, b_route, wg_e, wu_e, wd_e, norm_final)

    y_prompt = xp.reshape(batch, seq, d)
    y_sample = xs.reshape(dec_batch, 1, d)
    st = lambda k: jnp.stack(outs[k])
    return (y_prompt, y_sample, st("conv_p"), st("shift_p"), st("rwkv_p"), st("memk_p"), st("memv_p"),
            st("conv_s"), st("shift_s"), st("rwkv_s"))
```

```python
import functools
import math

import jax
import jax.numpy as jnp
from jax import lax
from jax.experimental import pallas as pl
from jax.experimental.pallas import tpu as pltpu

F32 = jnp.float32
BF16 = jnp.bfloat16

CONV_K = 31
HEAD = 64
PAIR = 2 * HEAD
CHUNK = 64
DECAY_LORA = 64
AAA_LORA = 64
GATE_LORA = 160
LORA_PAD = 512
N_MEM = 256
X_HEADS = 4
N_GROUPS = 4
EXP_PER_GROUP = 8
N_EXPERTS = N_GROUPS * EXP_PER_GROUP
RMS_EPS = 1e-6
LN_EPS = 1e-5
GN_EPS = 64e-5
DECAY_SCALE = math.exp(-0.5)
NEG_BIG = -1e30
MOE_BM = 256
LANES = 128
VMEM_LIMIT = 56 * 1024 * 1024


def _cparams(*sem):
    return pltpu.CompilerParams(dimension_semantics=sem, vmem_limit_bytes=VMEM_LIMIT)


def _dot(a, b):
    return jnp.dot(a, b, preferred_element_type=F32)


def _dot_nt(a, b):
    return lax.dot_general(a, b, (((1,), (1,)), ((), ())), preferred_element_type=F32)


def _split_dot(x, w_bf16):
    hi = x.astype(BF16)
    lo = (x - hi.astype(F32)).astype(BF16)
    return _dot(hi, w_bf16) + _dot(lo, w_bf16)


def _rms(x, g, eps=RMS_EPS):
    return x * lax.rsqrt(jnp.mean(x * x, axis=-1, keepdims=True) + eps) * g


def _norm_mm_kernel(x_ref, g_ref, w_ref, o_ref, xn_ref, *, n_valid):
    j = pl.program_id(1)

    @pl.when(j == 0)
    def _():
        xn_ref[...] = _rms(x_ref[...], g_ref[...]).astype(BF16)

    w = w_ref[0]
    bn = w.shape[1]
    if n_valid % bn:
        col = j * bn + lax.broadcasted_iota(jnp.int32, (1, bn), 1)
        w = jnp.where(col < n_valid, w, jnp.zeros_like(w))
    o_ref[...] = _dot(xn_ref[...], w)


def _norm_matmul(x, g, w, layer, bm, bn):
    m, k = x.shape
    n = w.shape[2]
    bm = min(bm, m)
    n_tiles = pl.cdiv(n, bn)
    return pl.pallas_call(
        functools.partial(_norm_mm_kernel, n_valid=n),
        grid=(m // bm, n_tiles),
        in_specs=[pl.BlockSpec((bm, k), lambda i, j: (i, 0)),
                  pl.BlockSpec((1, k), lambda i, j: (0, 0)),
                  pl.BlockSpec((1, k, bn), lambda i, j: (layer, 0, j))],
        out_specs=pl.BlockSpec((bm, bn), lambda i, j: (i, j)),
        out_shape=jax.ShapeDtypeStruct((m, n_tiles * bn), F32),
        scratch_shapes=[pltpu.VMEM((bm, k), BF16)],
        compiler_params=_cparams("parallel", "arbitrary"),
        name="norm_matmul",
    )(x, g.reshape(1, k), w)


def _norm_mm_heads_kernel(x_ref, g_ref, w_ref, o_ref, ob_ref):
    res = _dot(_rms(x_ref[...], g_ref[...]).astype(BF16), w_ref[0])
    ob_ref[...] = res.astype(BF16)
    dh = o_ref.shape[2]
    for h in range(o_ref.shape[1]):
        o_ref[:, h, :] = res[:, h * dh:(h + 1) * dh]


def _norm_matmul_heads(x, g, w, layer, n_heads, bm):
    m, k = x.shape
    n = w.shape[2]
    bm = min(bm, m)
    return pl.pallas_call(
        _norm_mm_heads_kernel,
        grid=(m // bm,),
        in_specs=[pl.BlockSpec((bm, k), lambda i: (i, 0)),
                  pl.BlockSpec((1, k), lambda i: (0, 0)),
                  pl.BlockSpec((1, k, n), lambda i: (layer, 0, 0))],
        out_specs=[pl.BlockSpec((bm, n_heads, n // n_heads), lambda i: (i, 0, 0)),
                   pl.BlockSpec((bm, n), lambda i: (i, 0))],
        out_shape=[jax.ShapeDtypeStruct((m, n_heads, n // n_heads), F32),
                   jax.ShapeDtypeStruct((m, n), BF16)],
        compiler_params=_cparams("parallel"),
        name="norm_matmul_heads",
    )(x, g.reshape(1, k), w)


def _mm_res_kernel(a_ref, w_ref, res_ref, o_ref):
    o_ref[...] = res_ref[...] + _dot(a_ref[...].astype(BF16), w_ref[0])


def _matmul_res(a, w, layer, res, bm):
    m, n = res.shape
    k = a.shape[1]
    bm = min(bm, m)
    return pl.pallas_call(
        _mm_res_kernel,
        grid=(m // bm,),
        in_specs=[pl.BlockSpec((bm, k), lambda i: (i, 0)),
                  pl.BlockSpec((1, k, n), lambda i: (layer, 0, 0)),
                  pl.BlockSpec((bm, n), lambda i: (i, 0))],
        out_specs=pl.BlockSpec((bm, n), lambda i: (i, 0)),
        out_shape=jax.ShapeDtypeStruct((m, n), F32),
        compiler_params=_cparams("parallel"),
        name="matmul_res",
    )(a, w, res)


def _ln_silu(cf, lg, lb):
    mu = jnp.mean(cf, axis=-1, keepdims=True)
    d = cf - mu
    var = jnp.mean(d * d, axis=-1, keepdims=True)
    y = d * lax.rsqrt(var + LN_EPS) * lg + lb
    return y * jax.nn.sigmoid(y)


def _conv_prefill_kernel(a_ref, g_ref, buf_ref, w_ref, cb_ref, lg_ref, lb_ref, c_ref, nc_ref,
                         uf_ref, cv_ref, sh_ref, *, tt, halo):
    t = pl.program_id(1)
    pad = 32 - halo

    @pl.when(t == 0)
    def _():
        uf_ref[pad:32, :] = buf_ref[0]

    @pl.when(t > 0)
    def _():
        uf_ref[pad:32, :] = uf_ref[tt + pad:tt + 32, :]

    uf_ref[32:32 + tt, :] = a_ref[...] * jax.nn.sigmoid(g_ref[...])

    for sft in range(8):
        n_rows = sh_ref.shape[1] if sft < 7 else sh_ref.shape[1] - 8
        sh_ref[sft, 0:n_rows, :] = uf_ref[pad + sft:pad + sft + n_rows, :]

    width = uf_ref.shape[1]
    rb = 64
    for r0 in range(0, tt, rb):
        for l0 in range(0, width, LANES):
            acc = jnp.zeros((rb, LANES), F32)
            for j in range(CONV_K):
                base = r0 + j - j % 8
                acc = acc + sh_ref[j % 8, base:base + rb, l0:l0 + LANES] * w_ref[j:j + 1, l0:l0 + LANES]
            cv_ref[r0:r0 + rb, l0:l0 + LANES] = acc

    c_ref[...] = _ln_silu(cv_ref[...] + cb_ref[...], lg_ref[...], lb_ref[...]).astype(c_ref.dtype)

    @pl.when(t == pl.num_programs(1) - 1)
    def _():
        nc_ref[0] = uf_ref[tt + pad:tt + 32, :]


def _conv_prefill(proj, conv_buf, conv_w, conv_b, ln_g, ln_b, batch, seq):
    cw = conv_w.shape[1]
    halo = CONV_K - 1
    tt = min(256, seq)
    nt = seq // tt
    row = lambda b, t: (b * nt + t, 0)
    vec = pl.BlockSpec((1, cw), lambda b, t: (0, 0))
    return pl.pallas_call(
        functools.partial(_conv_prefill_kernel, tt=tt, halo=halo),
        grid=(batch, nt),
        in_specs=[pl.BlockSpec((tt, cw), row),
                  pl.BlockSpec((tt, cw), lambda b, t: (b * nt + t, 1)),
                  pl.BlockSpec((1, halo, cw), lambda b, t: (b, 0, 0)),
                  pl.BlockSpec((CONV_K, cw), lambda b, t: (0, 0)),
                  vec, vec, vec],
        out_specs=[pl.BlockSpec((tt, cw), row),
                   pl.BlockSpec((1, halo, cw), lambda b, t: (b, 0, 0))],
        out_shape=[jax.ShapeDtypeStruct((batch * seq, cw), BF16),
                   jax.ShapeDtypeStruct((batch, halo, cw), F32)],
        scratch_shapes=[pltpu.VMEM((tt + 32, cw), F32), pltpu.VMEM((tt, cw), F32),
                        pltpu.VMEM((8, tt + 24, cw), F32)],
        compiler_params=_cparams("parallel", "arbitrary"),
        name="conv_prefill",
    )(proj, proj, conv_buf, conv_w, conv_b.reshape(1, cw), ln_g.reshape(1, cw), ln_b.reshape(1, cw))


def _conv_decode_kernel(a_ref, g_ref, cache_ref, w_ref, cb_ref, lg_ref, lb_ref, c_ref, nc_ref):
    halo = CONV_K - 1
    u = a_ref[...] * jax.nn.sigmoid(g_ref[...])
    acc = u * w_ref[halo:halo + 1, :]
    for j in range(halo):
        acc = acc + cache_ref[0, :, j, :] * w_ref[j:j + 1, :]
    c_ref[...] = _ln_silu(acc + cb_ref[...], lg_ref[...], lb_ref[...]).astype(c_ref.dtype)
    nc_ref[:, 0:halo - 1, :] = cache_ref[0, :, 1:halo, :]
    nc_ref[:, halo - 1, :] = u


def _conv_decode(proj, cache, layer, conv_w, conv_b, ln_g, ln_b):
    _, batch, halo, cw = cache.shape
    bb = 8
    vec = pl.BlockSpec((1, cw), lambda i: (0, 0))
    return pl.pallas_call(
        _conv_decode_kernel,
        grid=(batch // bb,),
        in_specs=[pl.BlockSpec((bb, cw), lambda i: (i, 0)),
                  pl.BlockSpec((bb, cw), lambda i: (i, 1)),
                  pl.BlockSpec((1, bb, halo, cw), lambda i: (layer, i, 0, 0)),
                  pl.BlockSpec((CONV_K, cw), lambda i: (0, 0)),
                  vec, vec, vec],
        out_specs=[pl.BlockSpec((bb, cw), lambda i: (i, 0)),
                   pl.BlockSpec((bb, halo, cw), lambda i: (i, 0, 0))],
        out_shape=[jax.ShapeDtypeStruct((batch, cw), BF16),
                   jax.ShapeDtypeStruct((batch, halo, cw), F32)],
        compiler_params=_cparams("parallel"),
        name="conv_decode",
    )(proj, proj, cache, conv_w, conv_b.reshape(1, cw), ln_g.reshape(1, cw), ln_b.reshape(1, cw))


def _head_sum(x, bd_ref):
    blk = bd_ref.shape[0]
    parts = [_split_dot(x[:, l0:l0 + blk], bd_ref[...]) for l0 in range(0, x.shape[1], blk)]
    return jnp.concatenate(parts, axis=1)


def _prep_math(q, qp, mu_ref, wd_ref, wa_ref, wg_ref, db_ref, ab_ref, kk_ref, ka_ref, rk_ref, bd_ref):
    rw = q[0].shape[1]
    offs = (0, rw, 2 * rw, 3 * rw)
    r, k, v, lo = [x + (xp - x) * mu_ref[:, o:o + x.shape[1]] for x, xp, o in zip(q, qp, offs)]
    pwa = lo[:, 0:LANES]
    pg = lo[:, LANES:3 * LANES]
    dec_in = _dot(jnp.tanh(pwa).astype(BF16), wd_ref[...])
    a_in = _dot(pwa.astype(BF16), wa_ref[...])
    gate = _dot(jax.nn.sigmoid(pg).astype(BF16), wg_ref[...])
    logw = -DECAY_SCALE * jax.nn.sigmoid(db_ref[...] + dec_in)
    a = jax.nn.sigmoid(ab_ref[...] + a_in)
    kk = k * kk_ref[...]
    kk = kk / jnp.maximum(jnp.sqrt(_head_sum(kk * kk, bd_ref)), 1e-12)
    k2 = k * (1.0 + (a - 1.0) * ka_ref[...])
    bonus = _head_sum(r * k2 * rk_ref[...], bd_ref) * v
    return r, logw, k2, v, kk, a, bonus, gate


def _prep_prefill_kernel(r_ref, k_ref, v_ref, lo_ref, sb_ref, mu_ref, wd_ref, wa_ref, wg_ref, db_ref,
                         ab_ref, kk_ref, ka_ref, rk_ref, bd_ref, *rest):
    outs = rest[:8]
    carry_ref = rest[8]
    t = pl.program_id(1)

    @pl.when(t == 0)
    def _():
        carry_ref[0:1, :] = sb_ref[0]

    q = [r_ref[...], k_ref[...], v_ref[...], lo_ref[...]]
    tt = q[0].shape[0]
    first = lax.broadcasted_iota(jnp.int32, (tt, 1), 0) == 0
    qp = []
    off = 0
    for x in q:
        w = x.shape[1]
        qp.append(jnp.where(first, carry_ref[0:1, off:off + w], pltpu.roll(x, 1, 0)))
        off += w
    off = 0
    for x in q:
        w = x.shape[1]
        carry_ref[0:1, off:off + w] = x[tt - 1:tt, :]
        off += w
    res = _prep_math(q, qp, mu_ref, wd_ref, wa_ref, wg_ref, db_ref, ab_ref, kk_ref, ka_ref, rk_ref, bd_ref)
    for o_ref, val in zip(outs, res):
        o_ref[...] = val.astype(o_ref.dtype)


def _prep_decode_kernel(r_ref, k_ref, v_ref, lo_ref, rp_ref, kp_ref, vp_ref, lop_ref, mu_ref, wd_ref,
                        wa_ref, wg_ref, db_ref, ab_ref, kk_ref, ka_ref, rk_ref, bd_ref, *outs):
    q = [r_ref[...], k_ref[...], v_ref[...], lo_ref[...]]
    qp = [rp_ref[...], kp_ref[...], vp_ref[...], lop_ref[...]]
    res = _prep_math(q, qp, mu_ref, wd_ref, wa_ref, wg_ref, db_ref, ab_ref, kk_ref, ka_ref, rk_ref, bd_ref)
    for o_ref, val in zip(outs, res):
        o_ref[...] = val


def _prep_param_specs(rw, idx):
    full = lambda shape: pl.BlockSpec(shape, idx)
    vec = full((1, rw))
    return [full((1, 3 * rw + LORA_PAD)), full((LANES, rw)), full((LANES, rw)), full((2 * LANES, rw)),
            vec, vec, vec, vec, vec, full((2 * LANES, 2 * LANES))]


def _rwkv_prep_prefill(proj, shift_buf, pp, batch, seq, rw):
    tt = min(256, seq)
    nt = seq // tt
    lora_blk = (2 * rw + 3 * rw) // LORA_PAD
    col = lambda c: (lambda b, t: (b * nt + t, c))
    qw = 3 * rw + LORA_PAD
    in_specs = [pl.BlockSpec((tt, rw), col(2)), pl.BlockSpec((tt, rw), col(3)),
                pl.BlockSpec((tt, rw), col(4)), pl.BlockSpec((tt, LORA_PAD), col(lora_blk)),
                pl.BlockSpec((1, 1, qw), lambda b, t: (b, 0, 0))]
    in_specs += _prep_param_specs(rw, lambda b, t: (0, 0))
    out_spec = pl.BlockSpec((tt, rw), col(0))
    return pl.pallas_call(
        _prep_prefill_kernel,
        grid=(batch, nt),
        in_specs=in_specs,
        out_specs=[out_spec] * 8,
        out_shape=[jax.ShapeDtypeStruct((batch * seq, rw), BF16 if i == 3 else F32) for i in range(8)],
        scratch_shapes=[pltpu.VMEM((8, qw), F32)],
        compiler_params=_cparams("parallel", "arbitrary"),
        name="rwkv_prep_prefill",
    )(proj, proj, proj, proj, shift_buf.reshape(batch, 1, qw), *pp)


def _rwkv_prep_decode(proj, shift_state, pp, rw):
    batch = proj.shape[0]
    bb = min(128, batch)
    lora_blk = (2 * rw + 3 * rw) // LORA_PAD
    col = lambda c: (lambda i: (i, c))
    in_specs = [pl.BlockSpec((bb, rw), col(2)), pl.BlockSpec((bb, rw), col(3)),
                pl.BlockSpec((bb, rw), col(4)), pl.BlockSpec((bb, LORA_PAD), col(lora_blk)),
                pl.BlockSpec((bb, rw), col(0)), pl.BlockSpec((bb, rw), col(1)),
                pl.BlockSpec((bb, rw), col(2)), pl.BlockSpec((bb, LORA_PAD), col(3 * rw // LORA_PAD))]
    in_specs += _prep_param_specs(rw, lambda i: (0, 0))
    return pl.pallas_call(
        _prep_decode_kernel,
        grid=(batch // bb,),
        in_specs=in_specs,
        out_specs=[pl.BlockSpec((bb, rw), col(0))] * 8,
        out_shape=[jax.ShapeDtypeStruct((batch, rw), F32)] * 8,
        compiler_params=_cparams("parallel"),
        name="rwkv_prep_decode",
    )(proj, proj, proj, proj, shift_state, shift_state, shift_state, shift_state, *pp)


def _stack2(x, smask):
    return jnp.where(smask, jnp.concatenate([x, x], axis=0), 0.0)


def _rwkv_chunk_kernel(r_ref, lw_ref, k_ref, v_ref, kk_ref, a_ref, s0_ref, o_ref, so_ref, s_ref):
    c = pl.program_id(1)
    cs = r_ref.shape[0]
    n_pairs = r_ref.shape[1] // PAIR
    two = 2 * cs

    @pl.when(c == 0)
    def _():
        z = jnp.zeros((HEAD, HEAD), F32)
        for p in range(n_pairs):
            top = jnp.concatenate([s0_ref[0, 2 * p], z], axis=1)
            bot = jnp.concatenate([z, s0_ref[0, 2 * p + 1]], axis=1)
            s_ref[p] = jnp.concatenate([top, bot], axis=0)

    ri = lax.broadcasted_iota(jnp.int32, (two, two), 0)
    ci = lax.broadcasted_iota(jnp.int32, (two, two), 1)
    strict = ci < ri
    incl = ci <= ri
    eye = (ci == ri).astype(F32)
    smask = (lax.broadcasted_iota(jnp.int32, (two, PAIR), 0) < cs) == (
        lax.broadcasted_iota(jnp.int32, (two, PAIR), 1) < HEAD)
    tri = (lax.broadcasted_iota(jnp.int32, (cs, cs), 1)
           <= lax.broadcasted_iota(jnp.int32, (cs, cs), 0)).astype(BF16)

    lw_all = lw_ref[...]
    lw_hi = lw_all.astype(BF16)
    lw_lo = (lw_all - lw_hi.astype(F32)).astype(BF16)
    cum_all = _dot(tri, lw_hi) + _dot(tri, lw_lo)

    pairs = range(n_pairs)
    cat = jnp.concatenate
    prep = []
    for p in pairs:
        sl = slice(p * PAIR, (p + 1) * PAIR)
        lw = lw_all[:, sl]
        cum = cum_all[:, sl]
        tot = cum[cs - 1:cs, :]
        g_inv = jnp.exp(-cum)
        g_end = jnp.exp(tot - cum)
        kk = kk_ref[:, sl]
        k2 = k_ref[:, sl]
        bb = kk * a_ref[:, sl]
        prep.append(dict(
            g_tot=jnp.exp(tot),
            a_b=_stack2(kk * jnp.exp(cum - lw), smask).astype(BF16),
            r_s=_stack2(r_ref[:, sl] * jnp.exp(cum), smask),
            bk=cat([_stack2(bb * g_inv, smask), _stack2(k2 * g_inv, smask)], axis=0).astype(BF16),
            v_s=_stack2(v_ref[:, sl].astype(F32), smask),
            bg_s=_stack2(bb * g_end, smask).astype(BF16),
            kg_s=_stack2(k2 * g_end, smask).astype(BF16)))
    a_b = [q["a_b"] for q in prep]
    r_s = [q["r_s"] for q in prep]
    v_s = [q["v_s"] for q in prep]
    v_b = [x.astype(BF16) for x in v_s]

    gram = [_dot_nt(cat([a_b[p], r_s[p].astype(BF16)], axis=0), prep[p]["bk"]) for p in pairs]
    l_ab = [jnp.where(strict, g[0:two, 0:two], 0.0) for g in gram]
    l_ak = [jnp.where(strict, g[0:two, two:], 0.0).astype(BF16) for g in gram]
    m_rb = [jnp.where(incl, g[two:, 0:two], 0.0).astype(BF16) for g in gram]
    m_rk = [jnp.where(incl, g[two:, two:], 0.0).astype(BF16) for g in gram]

    tm = [eye - x for x in l_ab]
    pw = [x.astype(BF16) for x in l_ab]
    pw = [_dot(x, x).astype(BF16) for x in pw]
    n = 2
    while 2 * n < cs:
        both = [_dot(cat([t.astype(BF16), x], axis=0), x) for t, x in zip(tm, pw)]
        tm = [t + b[0:two] for t, b in zip(tm, both)]
        pw = [b[two:].astype(BF16) for b in both]
        n *= 2
    tm_b = [(t + _dot(t.astype(BF16), x)).astype(BF16) for t, x in zip(tm, pw)]

    wv = [_dot(cat([l_ak[p], m_rk[p]], axis=0), v_b[p]) for p in pairs]
    ua = [_dot(tm_b[p], cat([wv[p][0:two].astype(BF16), a_b[p]], axis=1)) for p in pairs]
    ua_b = [x.astype(BF16) for x in ua]
    mrb_ua = [_dot(m_rb[p], ua_b[p]) for p in pairs]
    o0 = [wv[p][two:] - mrb_ua[p][:, 0:PAIR] for p in pairs]
    rt = [(r_s[p] - mrb_ua[p][:, PAIR:]).astype(BF16) for p in pairs]

    s_old = [s_ref[p] for p in pairs]
    s_b = [x.astype(BF16) for x in s_old]
    lhs_t = [cat([ua[p][:, 0:PAIR].T, ua[p][:, PAIR:].T, v_s[p].T], axis=0).astype(BF16) for p in pairs]
    t_all = [_dot(lhs_t[p], cat([prep[p]["bg_s"], prep[p]["kg_s"]], axis=1)) for p in pairs]
    s_new = [s_old[p] * prep[p]["g_tot"] - _dot(s_b[p], t_all[p][PAIR:2 * PAIR, 0:PAIR].astype(BF16))
             + t_all[p][2 * PAIR:, PAIR:] - t_all[p][0:PAIR, 0:PAIR] for p in pairs]
    o_st = [o0[p] + _dot_nt(rt[p], s_b[p]) for p in pairs]

    o_ref[...] = cat([x[0:cs, :] + x[cs:two, :] for x in o_st], axis=1)
    s_ref[...] = jnp.stack(s_new, axis=0)

    @pl.when(c == pl.num_programs(1) - 1)
    def _():
        for p in range(n_pairs):
            s = s_ref[p]
            so_ref[0, 2 * p] = s[0:HEAD, 0:HEAD]
            so_ref[0, 2 * p + 1] = s[HEAD:PAIR, HEAD:PAIR]


def _rwkv_chunked(r, logw, k2, v, kk, a, s0, batch, seq):
    rw = r.shape[1]
    nc = seq // CHUNK
    heads = rw // HEAD
    row = pl.BlockSpec((CHUNK, rw), lambda b, c: (b * nc + c, 0))
    st = pl.BlockSpec((1, heads, HEAD, HEAD), lambda b, c: (b, 0, 0, 0))
    return pl.pallas_call(
        _rwkv_chunk_kernel,
        grid=(batch, nc),
        in_specs=[row] * 6 + [st],
        out_specs=[row, st],
        out_shape=[jax.ShapeDtypeStruct((batch * seq, rw), F32),
                   jax.ShapeDtypeStruct((batch, heads, HEAD, HEAD), F32)],
        scratch_shapes=[pltpu.VMEM((rw // PAIR, PAIR, PAIR), F32)],
        compiler_params=_cparams("parallel", "arbitrary"),
        name="rwkv_chunked",
    )(r, logw, k2, v, kk, a, s0)


def _rwkv_step_kernel(r_ref, lw_ref, k_ref, v_ref, kk_ref, a_ref, s_ref, o_ref, so_ref, *, heads):
    bb = s_ref.shape[1]
    eye = (lax.broadcasted_iota(jnp.int32, (HEAD, HEAD), 0)
           == lax.broadcasted_iota(jnp.int32, (HEAD, HEAD), 1))

    def body(bi, carry):
        hs = range(heads)
        rows = [pl.ds(bi * heads + h, 1) for h in hs]
        kk = [kk_ref[r, :] for r in rows]
        s = [s_ref[0, bi, h].astype(F32) for h in hs]
        sa = [jnp.sum(s[h] * kk[h], axis=1, keepdims=True) for h in hs]
        v_col = [jnp.sum(jnp.where(eye, v_ref[rows[h], :], 0.0), axis=1, keepdims=True) for h in hs]
        s_new = [s[h] * jnp.exp(lw_ref[rows[h], :]) - sa[h] * (kk[h] * a_ref[rows[h], :])
                 + v_col[h] * k_ref[rows[h], :] for h in hs]
        for h in hs:
            so_ref[bi, h] = s_new[h]
        o_col = [jnp.sum(s_new[h] * r_ref[rows[h], :], axis=1, keepdims=True) for h in hs]
        o_row = [jnp.sum(jnp.where(eye, o_col[h], 0.0), axis=0, keepdims=True) for h in hs]
        o_ref[pl.ds(pl.multiple_of(bi * heads, heads), heads), :] = jnp.concatenate(o_row, axis=0)
        return carry

    lax.fori_loop(0, bb, body, 0)


def _rwkv_step(r, logw, k2, v, kk, a, state, layer):
    batch, rw = r.shape
    heads = rw // HEAD
    bb = 8
    flat = lambda x: x.reshape(batch * heads, HEAD)
    row = pl.BlockSpec((bb * heads, HEAD), lambda i: (i, 0))
    st_in = pl.BlockSpec((1, bb, heads, HEAD, HEAD), lambda i: (layer, i, 0, 0, 0))
    st_out = pl.BlockSpec((bb, heads, HEAD, HEAD), lambda i: (i, 0, 0, 0))
    o, s_new = pl.pallas_call(
        functools.partial(_rwkv_step_kernel, heads=heads),
        grid=(batch // bb,),
        in_specs=[row] * 6 + [st_in],
        out_specs=[row, st_out],
        out_shape=[jax.ShapeDtypeStruct((batch * heads, HEAD), F32),
                   jax.ShapeDtypeStruct(state.shape[1:], F32)],
        compiler_params=_cparams("parallel"),
        name="rwkv_step",
    )(flat(r), flat(logw), flat(k2), flat(v), flat(kk), flat(a), state)
    return o.reshape(batch, rw), s_new


def _mix_out_kernel(c_ref, o_ref, bonus_ref, gate_ref, lg_ref, lb_ref, bd_ref, wc_ref, wo_ref, x_ref, y_ref):
    o = o_ref[...]
    mu = _head_sum(o, bd_ref) * (1.0 / HEAD)
    d = o - mu
    var = _head_sum(d * d, bd_ref) * (1.0 / HEAD)
    y = d * lax.rsqrt(var + GN_EPS) * lg_ref[...] + lb_ref[...]
    om = ((y + bonus_ref[...]) * gate_ref[...]).astype(BF16)
    y_ref[...] = x_ref[...] + _dot(c_ref[...], wc_ref[0]) + _dot(om, wo_ref[0])


def _mix_out(c, o, bonus, gate, lnx_g, lnx_b, bd, w_out, layer, x):
    m, d = x.shape
    cw, rw = c.shape[1], o.shape[1]
    assert cw == rw and w_out.shape[1] == cw + rw
    bm = min(512, m)
    row = lambda w: pl.BlockSpec((bm, w), lambda i: (i, 0))
    vec = pl.BlockSpec((1, rw), lambda i: (0, 0))
    return pl.pallas_call(
        _mix_out_kernel,
        grid=(m // bm,),
        in_specs=[row(cw), row(rw), row(rw), row(rw), vec, vec, pl.BlockSpec(bd.shape, lambda i: (0, 0)),
                  pl.BlockSpec((1, cw, d), lambda i: (layer, 0, 0)),
                  pl.BlockSpec((1, rw, d), lambda i: (layer, 1, 0)),
                  row(d)],
        out_specs=row(d),
        out_shape=jax.ShapeDtypeStruct((m, d), F32),
        compiler_params=_cparams("parallel"),
        name="mix_out",
    )(c, o, bonus, gate, lnx_g.reshape(1, rw), lnx_b.reshape(1, rw), bd, w_out, w_out, x)


def _attn_prefill_kernel(q_ref, k_ref, v_ref, w_ref, x_ref, o_ref, *, n_heads):
    d = q_ref.shape[1] // n_heads
    scale = d ** -0.5
    ctx = []
    for h in range(n_heads):
        sl = slice(h * d, (h + 1) * d)
        s = _dot_nt(q_ref[:, sl].astype(BF16), k_ref[:, sl]) * scale
        p = jnp.exp(s - jnp.max(s, axis=-1, keepdims=True))
        att = p / jnp.sum(p, axis=-1, keepdims=True)
        ctx.append(_dot(att.astype(BF16), v_ref[:, sl]).astype(BF16))
    o_ref[...] = x_ref[...] + _dot(jnp.concatenate(ctx, axis=1), w_ref[0])


def _attn_prefill(q, mem_k, mem_v, w_co, layer, x, batch, seq):
    d = q.shape[1]
    tt = min(512, seq)
    nt = seq // tt
    kv = pl.BlockSpec((N_MEM, d), lambda b, t: (b, 0))
    row = pl.BlockSpec((tt, d), lambda b, t: (b * nt + t, 0))
    return pl.pallas_call(
        functools.partial(_attn_prefill_kernel, n_heads=X_HEADS),
        grid=(batch, nt),
        in_specs=[row, kv, kv, pl.BlockSpec((1, d, d), lambda b, t: (layer, 0, 0)), row],
        out_specs=row,
        out_shape=jax.ShapeDtypeStruct((batch * seq, d), F32),
        compiler_params=_cparams("parallel", "arbitrary"),
        name="attn_prefill",
    )(q, mem_k, mem_v, w_co, x)


def _decode_attn_rows(qs, k_at, v_at, n_heads):
    d = qs[0].shape[1] // n_heads
    scale = d ** -0.5
    ids = [(i, h) for i in range(len(qs)) for h in range(n_heads)]
    s = [jnp.sum(k_at(i, h) * qs[i][:, h * d:(h + 1) * d], axis=1, keepdims=True) * scale for i, h in ids]
    p = [jnp.exp(x - jnp.max(x, axis=0, keepdims=True)) for x in s]
    att = [x / jnp.sum(x, axis=0, keepdims=True) for x in p]
    ctx = [jnp.sum(a * v_at(i, h), axis=0, keepdims=True) for a, (i, h) in zip(att, ids)]
    return [ctx[i * n_heads:(i + 1) * n_heads] for i in range(len(qs))]


def _attn_decode_kernel(q_ref, k_ref, v_ref, o_ref):
    n_heads, d = k_ref.shape[2], k_ref.shape[3]
    bb = q_ref.shape[0]
    ctx = _decode_attn_rows([q_ref[bi] for bi in range(bb)], lambda i, h: k_ref[i, :, h, :],
                            lambda i, h: v_ref[i, :, h, :], n_heads)
    for bi in range(bb):
        for h in range(n_heads):
            o_ref[bi, :, h * d:(h + 1) * d] = ctx[bi][h]


def _attn_decode(q, cache_k, cache_v):
    batch, d = q.shape
    bb = 2
    kv = pl.BlockSpec((bb, N_MEM, X_HEADS, d // X_HEADS), lambda i: (i, 0, 0, 0))
    row = pl.BlockSpec((bb, 1, d), lambda i: (i, 0, 0))
    out = pl.pallas_call(
        _attn_decode_kernel,
        grid=(batch // bb,),
        in_specs=[row, kv, kv],
        out_specs=row,
        out_shape=jax.ShapeDtypeStruct((batch, 1, d), F32),
        compiler_params=_cparams("parallel"),
        name="attn_decode",
    )(q.reshape(batch, 1, d), cache_k, cache_v)
    return out.reshape(batch, d)


def _route(x_ref, g_ref, wh_ref, wl_ref, b_ref):
    h = _rms(x_ref[...], g_ref[...])
    hh = h.astype(BF16)
    hl = (h - hh.astype(F32)).astype(BF16)
    logits = _dot(hh, wh_ref[...]) + _dot(hl, wh_ref[...]) + _dot(hh, wl_ref[...]) + b_ref[...]
    lane = lax.broadcasted_iota(jnp.int32, (1, LANES), 1).astype(F32)
    is_g = (lane >= N_EXPERTS) & (lane < N_EXPERTS + N_GROUPS)
    lgm = jnp.where(is_g, logits, NEG_BIG)
    gmax = jnp.max(lgm, axis=1, keepdims=True)
    gsum = jnp.sum(jnp.where(is_g, jnp.exp(lgm - gmax), 0.0), axis=1, keepdims=True)
    g_val = 1.0 / gsum
    g_idx = jnp.min(jnp.where(is_g & (lgm == gmax), lane - N_EXPERTS, 1e9), axis=1, keepdims=True)
    in_grp = (lane < N_EXPERTS) & (jnp.floor(lane * (1.0 / EXP_PER_GROUP)) == g_idx)
    le = jnp.where(in_grp, logits, NEG_BIG)
    m1 = jnp.max(le, axis=1, keepdims=True)
    i1 = jnp.min(jnp.where(in_grp & (le == m1), lane, 1e9), axis=1, keepdims=True)
    rest = in_grp & (lane != i1)
    le2 = jnp.where(rest, logits, NEG_BIG)
    m2 = jnp.max(le2, axis=1, keepdims=True)
    i2 = jnp.min(jnp.where(rest & (le2 == m2), lane, 1e9), axis=1, keepdims=True)
    e2 = jnp.exp(m2 - m1)
    den = 1.0 + e2
    w1 = (1.0 / den) * g_val
    w2 = (e2 / den) * g_val
    return h, lane, i1, i2, w1, w2


def _router_sorted_kernel(x_ref, g_ref, wh_ref, wl_ref, b_ref, init_ref, route_ref, rt_ref, cnt_ref, run_ref):
    @pl.when(pl.program_id(0) == 0)
    def _():
        run_ref[0:1, :] = init_ref[...]

    h, lane, i1, i2, w1, w2 = _route(x_ref, g_ref, wh_ref, wl_ref, b_ref)
    bm = h.shape[0]
    oh1 = lane == i1
    oh2 = lane == i2
    sel = (oh1 | oh2).astype(BF16)
    before = (lax.broadcasted_iota(jnp.int32, (bm, bm), 1)
              < lax.broadcasted_iota(jnp.int32, (bm, bm), 0)).astype(BF16)
    base = run_ref[0:1, :] + _dot(before, sel)
    rank1 = jnp.sum(jnp.where(oh1, base, 0.0), axis=1, keepdims=True)
    rank2 = jnp.sum(jnp.where(oh2, base, 0.0), axis=1, keepdims=True)
    total = run_ref[0:1, :] + jnp.sum(sel.astype(F32), axis=0, keepdims=True)
    run_ref[0:1, :] = total
    cnt_ref[...] = total
    route = jnp.zeros((bm, LANES), F32)
    for idx, val in enumerate((i1, i2, w1, w2, rank1, rank2)):
        route = jnp.where(lane == idx, val, route)
    route_ref[...] = route
    for r0 in range(0, bm, LANES):
        rt_ref[:, r0:r0 + LANES] = route[r0:r0 + LANES, :].T[0:8, :]


def _router_sorted(x, g, wh, wl, bias, init_counts):
    m, d = x.shape
    bm = min(512, m)
    return pl.pallas_call(
        _router_sorted_kernel,
        grid=(m // bm,),
        in_specs=[pl.BlockSpec((bm, d), lambda i: (i, 0)),
                  pl.BlockSpec((1, d), lambda i: (0, 0)),
                  pl.BlockSpec((d, LANES), lambda i: (0, 0)),
                  pl.BlockSpec((d, LANES), lambda i: (0, 0)),
                  pl.BlockSpec((1, LANES), lambda i: (0, 0)),
                  pl.BlockSpec((1, LANES), lambda i: (0, 0))],
        out_specs=[pl.BlockSpec((bm, LANES), lambda i: (i, 0)),
                   pl.BlockSpec((8, bm), lambda i: (0, i)),
                   pl.BlockSpec((1, LANES), lambda i: (0, 0))],
        out_shape=[jax.ShapeDtypeStruct((m, LANES), F32),
                   jax.ShapeDtypeStruct((8, m), F32),
                   jax.ShapeDtypeStruct((1, LANES), F32)],
        scratch_shapes=[pltpu.VMEM((8, LANES), F32)],
        compiler_params=_cparams("arbitrary"),
        name="moe_router_sorted",
    )(x, g.reshape(1, d), wh, wl, bias, init_counts)


def _plan_kernel(seg_ref, rt_ref, pos_ref):
    rt = rt_ref[...]
    rows = []
    for e_row, r_row in ((0, 4), (1, 5)):
        e = rt[e_row:e_row + 1, :]
        start = jnp.zeros_like(e)
        for k in range(N_EXPERTS):
            start = jnp.where(e == k, seg_ref[k].astype(F32), start)
        rows.append((start + rt[r_row:r_row + 1, :]).astype(jnp.int32))
    pos_ref[...] = jnp.concatenate(rows + [jnp.zeros((6, rt.shape[1]), jnp.int32)], axis=0)


def _plan(route_t, seg_start):
    m = route_t.shape[1]
    bt = min(2048, m)
    pos = pl.pallas_call(
        _plan_kernel,
        grid_spec=pltpu.PrefetchScalarGridSpec(
            num_scalar_prefetch=1, grid=(m // bt,),
            in_specs=[pl.BlockSpec((8, bt), lambda i, seg: (0, i))],
            out_specs=pl.BlockSpec((8, bt), lambda i, seg: (0, i))),
        out_shape=jax.ShapeDtypeStruct((8, m), jnp.int32),
        compiler_params=_cparams("arbitrary"),
        name="moe_plan",
    )(seg_start, route_t)
    return pos[0], pos[1]


def _row_copy(src_hbm, src_row, dst, dst_row, sem):
    return pltpu.make_async_copy(src_hbm.at[pl.ds(src_row, 1)], dst.at[pl.ds(dst_row, 1)], sem)


def _dispatch_kernel(p1_ref, p2_ref, seg_ref, cnt_ref, nrow_ref, xa_ref, xb_ref, g_ref, xs_hbm, h_ref,
                     zero_ref, sem, zsem, *, n_a):
    i = pl.program_id(0)
    n = pl.num_programs(0)
    bm = xa_ref.shape[0]
    n_max = xs_hbm.shape[0] // MOE_BM

    def zero_copy(row0):
        return pltpu.make_async_copy(zero_ref, xs_hbm.at[pl.ds(pl.multiple_of(row0, MOE_BM), MOE_BM)], zsem)

    @pl.when(i == 0)
    def _():
        zero_ref[...] = jnp.zeros_like(zero_ref)
        first_free = nrow_ref[0] // MOE_BM

        def tail_start(c, carry):
            zero_copy(c * MOE_BM).start()
            return carry

        def tail_wait(c, carry):
            zero_copy(c * MOE_BM).wait()
            return carry

        for e in range(N_EXPERTS):
            @pl.when(cnt_ref[e] > 0)
            def _():
                zero_copy(seg_ref[e] - MOE_BM).start()

        lax.fori_loop(first_free, n_max, tail_start, 0)
        for e in range(N_EXPERTS):
            @pl.when(cnt_ref[e] > 0)
            def _():
                zero_copy(seg_ref[e] - MOE_BM).wait()

        lax.fori_loop(first_free, n_max, tail_wait, 0)

    bm_b = xb_ref.shape[0]
    tok_b0 = n_a * bm

    def wait_rows(slot, rows):
        for _ in range(2):
            pltpu.make_async_copy(h_ref.at[slot, pl.ds(0, rows)], xs_hbm.at[pl.ds(0, rows)],
                                  sem.at[slot]).wait()

    def scatter_rows(slot, rows, tok0):
        def body(r, carry):
            src = h_ref.at[slot, pl.ds(r, 1)]
            pltpu.make_async_copy(src, xs_hbm.at[pl.ds(p1_ref[tok0 + r], 1)], sem.at[slot]).start()
            pltpu.make_async_copy(src, xs_hbm.at[pl.ds(p2_ref[tok0 + r], 1)], sem.at[slot]).start()
            return carry

        lax.fori_loop(0, rows, body, 0, unroll=8)

    slot = i % 2

    @pl.when((i > 0) & (i - 1 < n_a))
    def _():
        wait_rows(1 - slot, bm)

    @pl.when(i - 1 >= n_a)
    def _():
        wait_rows(1 - slot, bm_b)

    @pl.when(i < n_a)
    def _():
        h_ref[slot, 0:bm, :] = _rms(xa_ref[...], g_ref[...])
        scatter_rows(slot, bm, i * bm)

    @pl.when(i >= n_a)
    def _():
        h_ref[slot, 0:bm_b, :] = _rms(xb_ref[...], g_ref[...])
        scatter_rows(slot, bm_b, tok_b0 + (i - n_a) * bm_b)

    @pl.when(i == n - 1)
    def _():
        wait_rows(slot, bm_b)


def _dispatch(xa, xb, g, pos1, pos2, seg_end, cnt, n_rows_used, n_rows_max):
    (ma, d), mb = xa.shape, xb.shape[0]
    bm_a, bm_b = min(MOE_BM, ma), min(MOE_BM, mb)
    assert ma % bm_a == 0 and mb % bm_b == 0 and bm_b <= bm_a
    n_a, n_b = ma // bm_a, mb // bm_b
    return pl.pallas_call(
        functools.partial(_dispatch_kernel, n_a=n_a),
        grid_spec=pltpu.PrefetchScalarGridSpec(
            num_scalar_prefetch=5, grid=(n_a + n_b,),
            in_specs=[pl.BlockSpec((bm_a, d), lambda i, *_: (jnp.minimum(i, n_a - 1), 0)),
                      pl.BlockSpec((bm_b, d), lambda i, *_: (jnp.maximum(i - n_a, 0), 0)),
                      pl.BlockSpec((1, d), lambda i, *_: (0, 0))],
            out_specs=pl.BlockSpec(memory_space=pl.ANY),
            scratch_shapes=[pltpu.VMEM((2, bm_a, d), F32), pltpu.VMEM((MOE_BM, d), F32),
                            pltpu.SemaphoreType.DMA((2,)), pltpu.SemaphoreType.DMA(())]),
        out_shape=jax.ShapeDtypeStruct((n_rows_max, d), F32),
        compiler_params=_cparams("arbitrary"),
        name="moe_dispatch",
    )(pos1, pos2, seg_end, cnt, n_rows_used, xa, xb, g.reshape(1, d))


def _experts_sorted_kernel(te_ref, nt_ref, nxt_ref, par_ref, xs_ref, wg_hbm, wu_hbm, wd_hbm, ys_ref,
                           wgf_ref, wuf_ref, wdf_ref, wgb_ref, wub_ref, wdb_ref, sem):
    j = pl.program_id(0)
    prev = te_ref[jnp.maximum(j, 1) - 1]

    def copies(e, slot):
        return [pltpu.make_async_copy(hbm.at[e], buf.at[slot], sem.at[slot])
                for hbm, buf in ((wg_hbm, wgf_ref), (wu_hbm, wuf_ref), (wd_hbm, wdf_ref))]

    @pl.when(j == 0)
    def _():
        for c in copies(te_ref[0], par_ref[0]):
            c.start()

    @pl.when((j < nt_ref[0]) & ((j == 0) | (te_ref[j] != prev)))
    def _():
        slot = par_ref[j]
        for c in copies(te_ref[j], slot):
            c.wait()

        @pl.when(nxt_ref[j] >= 0)
        def _():
            for c in copies(nxt_ref[j], 1 - slot):
                c.start()

        wgb_ref[...] = wgf_ref[slot].astype(BF16)
        wub_ref[...] = wuf_ref[slot].astype(BF16)
        wdb_ref[...] = wdf_ref[slot].astype(BF16)

    @pl.when(j < nt_ref[0])
    def _():
        x = xs_ref[...].astype(BF16)
        hg = _dot(x, wgb_ref[...])
        hu = _dot(x, wub_ref[...])
        act = hg * jax.nn.sigmoid(hg) * hu
        ys_ref[...] = _dot(act.astype(BF16), wdb_ref[...])

    @pl.when(j >= nt_ref[0])
    def _():
        ys_ref[...] = jnp.zeros_like(ys_ref)


def _experts_sorted(xs, tile_expert, n_tiles_used, next_expert, slot_parity, wg, wu, wd):
    n_rows, d = xs.shape
    de = wg.shape[2]
    n_tiles = n_rows // MOE_BM
    row_in = lambda j, te, nt, *_: (jnp.minimum(j, nt[0] - 1), 0)
    hbm = pl.BlockSpec(memory_space=pl.ANY)
    return pl.pallas_call(
        _experts_sorted_kernel,
        grid_spec=pltpu.PrefetchScalarGridSpec(
            num_scalar_prefetch=4, grid=(n_tiles,),
            in_specs=[pl.BlockSpec((MOE_BM, d), row_in), hbm, hbm, hbm],
            out_specs=pl.BlockSpec((MOE_BM, d), lambda j, *_: (j, 0)),
            scratch_shapes=[pltpu.VMEM((2, d, de), F32), pltpu.VMEM((2, d, de), F32),
                            pltpu.VMEM((2, de, d), F32),
                            pltpu.VMEM((d, de), BF16), pltpu.VMEM((d, de), BF16), pltpu.VMEM((de, d), BF16),
                            pltpu.SemaphoreType.DMA((2,))]),
        out_shape=jax.ShapeDtypeStruct((n_rows, d), F32),
        compiler_params=_cparams("arbitrary"),
        name="moe_experts_sorted",
    )(tile_expert, n_tiles_used, next_expert, slot_parity, xs, wg, wu, wd)


def _combine_kernel(p1_ref, p2_ref, ys_hbm, x_ref, route_ref, nf_ref, y_ref, buf_ref, sem):
    i = pl.program_id(0)
    n = pl.num_programs(0)
    bm = x_ref.shape[0]

    def issue(tile, slot):
        def body(r, carry):
            t = tile * bm + r
            _row_copy(ys_hbm, p1_ref[t], buf_ref.at[slot, 0], r, sem.at[slot]).start()
            _row_copy(ys_hbm, p2_ref[t], buf_ref.at[slot, 1], r, sem.at[slot]).start()
            return carry

        lax.fori_loop(0, bm, body, 0, unroll=8)

    @pl.when(i == 0)
    def _():
        issue(0, 0)

    @pl.when(i + 1 < n)
    def _():
        issue(i + 1, (i + 1) % 2)

    slot = i % 2
    for k in range(2):
        pltpu.make_async_copy(ys_hbm.at[pl.ds(0, bm)], buf_ref.at[slot, k], sem.at[slot]).wait()
    lane = lax.broadcasted_iota(jnp.int32, (1, LANES), 1)
    route = route_ref[...]
    w1 = jnp.sum(jnp.where(lane == 2, route, 0.0), axis=1, keepdims=True)
    w2 = jnp.sum(jnp.where(lane == 3, route, 0.0), axis=1, keepdims=True)
    x3 = x_ref[...] + w1 * buf_ref[slot, 0] + w2 * buf_ref[slot, 1]
    y_ref[...] = _rms(x3, nf_ref[...])


def _combine(ys, pos1, pos2, x, route, norm_final):
    m, d = x.shape
    bm = min(MOE_BM, m)
    return pl.pallas_call(
        _combine_kernel,
        grid_spec=pltpu.PrefetchScalarGridSpec(
            num_scalar_prefetch=2, grid=(m // bm,),
            in_specs=[pl.BlockSpec(memory_space=pl.ANY),
                      pl.BlockSpec((bm, d), lambda i, p1, p2: (i, 0)),
                      pl.BlockSpec((bm, LANES), lambda i, p1, p2: (i, 0)),
                      pl.BlockSpec((1, d), lambda i, p1, p2: (0, 0))],
            out_specs=pl.BlockSpec((bm, d), lambda i, p1, p2: (i, 0)),
            scratch_shapes=[pltpu.VMEM((2, 2, bm, d), F32), pltpu.SemaphoreType.DMA((2,))]),
        out_shape=jax.ShapeDtypeStruct((m, d), F32),
        compiler_params=_cparams("arbitrary"),
        name="moe_combine",
    )(pos1, pos2, ys, x, route, norm_final.reshape(1, d))


def _moe_sorted(xa, xb, g, wh, wl, bias, wg, wu, wd, norm_final):
    ma, mb = xa.shape[0], xb.shape[0]
    route_a, rt_a, cnt_a = _router_sorted(xa, g, wh, wl, bias, jnp.zeros((1, LANES), F32))
    route_b, rt_b, counts = _router_sorted(xb, g, wh, wl, bias, cnt_a)
    cnt = counts[0, :N_EXPERTS].astype(jnp.int32)
    padded = (cnt + MOE_BM - 1) // MOE_BM * MOE_BM
    seg_end = jnp.cumsum(padded)
    seg_start = seg_end - padded
    n_tiles_max = (2 * (ma + mb) + MOE_BM - 1) // MOE_BM + N_EXPERTS
    n_rows_max = n_tiles_max * MOE_BM
    n_rows_used = seg_end[-1:]
    n_tiles_used = n_rows_used // MOE_BM
    tile_start = jnp.arange(n_tiles_max, dtype=jnp.int32) * MOE_BM
    tile_expert = jnp.sum((seg_end[None, :] <= tile_start[:, None]).astype(jnp.int32), axis=1)
    last_expert = jnp.max(jnp.where(cnt > 0, jnp.arange(N_EXPERTS, dtype=jnp.int32), 0))
    tile_expert = jnp.minimum(tile_expert, last_expert)
    eidx = jnp.arange(N_EXPERTS, dtype=jnp.int32)
    used = cnt > 0
    later = jnp.where((eidx[None, :] > eidx[:, None]) & used[None, :], eidx[None, :], N_EXPERTS)
    next_used = jnp.min(later, axis=1)
    next_used = jnp.where(next_used == N_EXPERTS, -1, next_used)
    ordinal = jnp.cumsum(used.astype(jnp.int32)) - 1
    onehot = (tile_expert[:, None] == eidx[None, :]).astype(jnp.int32)
    next_expert = jnp.sum(onehot * next_used[None, :], axis=1)
    slot_parity = jnp.sum(onehot * ordinal[None, :], axis=1) % 2
    pa1, pa2 = _plan(rt_a, seg_start)
    pb1, pb2 = _plan(rt_b, seg_start)
    xs = _dispatch(xa, xb, g, jnp.concatenate([pa1, pb1]), jnp.concatenate([pa2, pb2]), seg_end, cnt,
                   n_rows_used, n_rows_max)
    ys = _experts_sorted(xs, tile_expert, n_tiles_used, next_expert, slot_parity, wg, wu, wd)
    return (_combine(ys, pa1, pa2, xa, route_a, norm_final),
            _combine(ys, pb1, pb2, xb, route_b, norm_final))


def _pad_cols(x, n):
    return jnp.pad(x, ((0, 0), (0, n - x.shape[1])))


def _block_diag_ones(n, blk):
    i = jnp.arange(n) // blk
    return (i[:, None] == i[None, :]).astype(BF16)


def kernel(x_prompt, x_sample, mem_prompt, cache_conv, state_shift, state_rwkv, cache_mem_k, cache_mem_v,
           norm_mix, w_in, conv_w, conv_b, conv_ln_g, conv_ln_b, shift_mu, w_decay_up, decay_bias, w_a_up,
           a_bias, w_g_up, k_k, k_a, r_k, lnx_g, lnx_b, w_out, norm_x, norm_mem, w_cq, w_ck, w_cv, w_co,
           norm_ffn, w_route_group, b_route_group, w_route_expert, b_route_expert, w_gate, w_up, w_down,
           norm_final):
    depth = w_in.shape[0]
    batch, seq, d = x_prompt.shape
    dec_batch = x_sample.shape[0]
    assert depth == 1
    assert x_sample.shape[1] == 1 and seq % CHUNK == 0 and seq >= CONV_K - 1
    cw = conv_w.shape[2]
    rw = w_decay_up.shape[2]
    heads = rw // HEAD
    shift_w = shift_mu.shape[1]
    in_w = w_in.shape[2]
    assert in_w == 2 * cw + shift_w and shift_w == 3 * rw + DECAY_LORA + AAA_LORA + GATE_LORA
    assert cw == rw and rw % LORA_PAD == 0
    in_pad = 2 * cw + 3 * rw + LORA_PAD
    qw = 3 * rw + LORA_PAD

    xp = x_prompt.reshape(batch * seq, d)
    xs = x_sample.reshape(dec_batch, d)
    outs = {k: [] for k in ("conv_p", "shift_p", "rwkv_p", "memk_p", "memv_p", "conv_s", "shift_s", "rwkv_s")}
    bd = _block_diag_ones(2 * LANES, HEAD)

    for l in range(depth):
        w_in_b = w_in.astype(BF16)
        w_out_b = w_out.astype(BF16)
        w_cq_b = w_cq.astype(BF16)
        w_ck_b = w_ck.astype(BF16)
        w_cv_b = w_cv.astype(BF16)
        w_co_b = w_co.astype(BF16)
        zeros_l = jnp.zeros((DECAY_LORA, rw), F32)
        wd_pad = jnp.concatenate([w_decay_up[l], zeros_l], axis=0).astype(BF16)
        wa_pad = jnp.concatenate([zeros_l, w_a_up[l]], axis=0).astype(BF16)
        wg_pad = jnp.pad(w_g_up[l], ((0, 2 * LANES - GATE_LORA), (0, 0))).astype(BF16)
        mu_pad = _pad_cols(shift_mu[l].reshape(1, shift_w), qw)
        vec = lambda x: x.reshape(1, rw)
        pp = (mu_pad, wd_pad, wa_pad, wg_pad, vec(decay_bias[l]), vec(a_bias[l]), vec(k_k[l]), vec(k_a[l]),
              vec(r_k[l]), bd)
        w_route = jnp.concatenate([w_route_expert[l].reshape(d, N_EXPERTS), w_route_group[l]], axis=1)
        w_route = _pad_cols(w_route, LANES)
        wr_hi = w_route.astype(BF16)
        wr_lo = (w_route - wr_hi.astype(F32)).astype(BF16)
        b_route = _pad_cols(jnp.concatenate([b_route_expert[l].reshape(1, N_EXPERTS),
                                             b_route_group[l].reshape(1, N_GROUPS)], axis=1), LANES)
        de = w_gate.shape[-1]
        wg_e = w_gate[l].reshape(N_EXPERTS, d, de)
        wu_e = w_up[l].reshape(N_EXPERTS, d, de)
        wd_e = w_down[l].reshape(N_EXPERTS, de, d)

        proj_s = _norm_matmul(xs, norm_mix[l], w_in_b, l, 128, LORA_PAD)
        c_s, conv_new_s = _conv_decode(proj_s, cache_conv, l, conv_w[l], conv_b[l], conv_ln_g[l],
                                       conv_ln_b[l])
        prep_s = _rwkv_prep_decode(proj_s, _pad_cols(state_shift[l], qw), pp, rw)
        o_s, s_s = _rwkv_step(*prep_s[:6], state_rwkv, l)
        xs = _mix_out(c_s, o_s, prep_s[6], prep_s[7], lnx_g[l], lnx_b[l], bd, w_out_b, l, xs)
        qs = _norm_matmul(xs, norm_x[l], w_cq_b, l, 128, d)
        ctx_s = _attn_decode(qs, cache_mem_k[l], cache_mem_v[l])
        xs = _matmul_res(ctx_s, w_co_b, l, xs, 128)
        outs["conv_s"].append(conv_new_s)
        outs["shift_s"].append(proj_s[:, 2 * cw:2 * cw + shift_w])
        outs["rwkv_s"].append(s_s)

        mem2 = mem_prompt.reshape(batch * N_MEM, d)
        mk, mk_b = _norm_matmul_heads(mem2, norm_mem[l], w_ck_b, l, X_HEADS, 256)
        mv, mv_b = _norm_matmul_heads(mem2, norm_mem[l], w_cv_b, l, X_HEADS, 256)
        proj = _norm_matmul(xp, norm_mix[l], w_in_b, l, 1024, LORA_PAD)
        assert proj.shape[1] == in_pad
        c_p, conv_new = _conv_prefill(proj, jnp.zeros((batch, CONV_K - 1, cw), F32), conv_w[l], conv_b[l],
                                      conv_ln_g[l], conv_ln_b[l], batch, seq)
        prep = _rwkv_prep_prefill(proj, jnp.zeros((batch, qw), F32), pp, batch, seq, rw)
        o_p, s_p = _rwkv_chunked(*prep[:6], jnp.zeros((batch, heads, HEAD, HEAD), F32), batch, seq)
        shift_new = proj.reshape(batch, seq, in_pad)[:, -1, 2 * cw:2 * cw + shift_w]
        xp = _mix_out(c_p, o_p, prep[6], prep[7], lnx_g[l], lnx_b[l], bd, w_out_b, l, xp)
        qx = _norm_matmul(xp, norm_x[l], w_cq_b, l, 512, d)
        xp = _attn_prefill(qx, mk_b, mv_b, w_co_b, l, xp, batch, seq)
        outs["conv_p"].append(conv_new)
        outs["shift_p"].append(shift_new)
        outs["rwkv_p"].append(s_p)
        outs["memk_p"].append(mk.reshape(batch, N_MEM, X_HEADS, d // X_HEADS))
        outs["memv_p"].append(mv.reshape(batch, N_MEM, X_HEADS, d // X_HEADS))

        xp, xs = _moe_sorted(xp, xs, norm_ffn[l], wr_hi, wr_lo, b_route, wg_e, wu_e, wd_e, norm_final)

    y_prompt = xp.reshape(batch, seq, d)
    y_sample = xs.reshape(dec_batch, 1, d)
    st = lambda k: jnp.stack(outs[k])
    return (y_prompt, y_sample, st("conv_p"), st("shift_p"), st("rwkv_p"), st("memk_p"), st("memv_p"),
            st("conv_s"), st("shift_s"), st("rwkv_s"))
```

```python
import functools
import math

import jax
import jax.numpy as jnp
from jax import lax
from jax.experimental import pallas as pl
from jax.experimental.pallas import tpu as pltpu

F32 = jnp.float32
BF16 = jnp.bfloat16

CONV_K = 31
HEAD = 64
PAIR = 2 * HEAD
CHUNK = 64
DECAY_LORA = 64
AAA_LORA = 64
GATE_LORA = 160
LORA_PAD = 512
N_MEM = 256
X_HEADS = 4
N_GROUPS = 4
EXP_PER_GROUP = 8
N_EXPERTS = N_GROUPS * EXP_PER_GROUP
RMS_EPS = 1e-6
LN_EPS = 1e-5
GN_EPS = 64e-5
DECAY_SCALE = math.exp(-0.5)
NEG_BIG = -1e30
MOE_BM = 256
LANES = 128
VMEM_LIMIT = 56 * 1024 * 1024


def _cparams(*sem):
    return pltpu.CompilerParams(dimension_semantics=sem, vmem_limit_bytes=VMEM_LIMIT)


def _dot(a, b):
    return jnp.dot(a, b, preferred_element_type=F32)


def _dot_nt(a, b):
    return lax.dot_general(a, b, (((1,), (1,)), ((), ())), preferred_element_type=F32)


def _split_dot(x, w_bf16):
    hi = x.astype(BF16)
    lo = (x - hi.astype(F32)).astype(BF16)
    return _dot(hi, w_bf16) + _dot(lo, w_bf16)


def _rms(x, g, eps=RMS_EPS):
    return x * lax.rsqrt(jnp.mean(x * x, axis=-1, keepdims=True) + eps) * g


def _norm_mm_kernel(x_ref, g_ref, w_ref, o_ref, xn_ref, *, n_valid):
    j = pl.program_id(1)

    @pl.when(j == 0)
    def _():
        xn_ref[...] = _rms(x_ref[...], g_ref[...]).astype(BF16)

    w = w_ref[0]
    bn = w.shape[1]
    if n_valid % bn:
        col = j * bn + lax.broadcasted_iota(jnp.int32, (1, bn), 1)
        w = jnp.where(col < n_valid, w, jnp.zeros_like(w))
    o_ref[...] = _dot(xn_ref[...], w)


def _norm_matmul(x, g, w, layer, bm, bn):
    m, k = x.shape
    n = w.shape[2]
    bm = min(bm, m)
    n_tiles = pl.cdiv(n, bn)
    return pl.pallas_call(
        functools.partial(_norm_mm_kernel, n_valid=n),
        grid=(m // bm, n_tiles),
        in_specs=[pl.BlockSpec((bm, k), lambda i, j: (i, 0)),
                  pl.BlockSpec((1, k), lambda i, j: (0, 0)),
                  pl.BlockSpec((1, k, bn), lambda i, j: (layer, 0, j))],
        out_specs=pl.BlockSpec((bm, bn), lambda i, j: (i, j)),
        out_shape=jax.ShapeDtypeStruct((m, n_tiles * bn), F32),
        scratch_shapes=[pltpu.VMEM((bm, k), BF16)],
        compiler_params=_cparams("parallel", "arbitrary"),
        name="norm_matmul",
    )(x, g.reshape(1, k), w)


def _norm_mm_heads_kernel(x_ref, g_ref, w_ref, o_ref, ob_ref):
    res = _dot(_rms(x_ref[...], g_ref[...]).astype(BF16), w_ref[0])
    ob_ref[...] = res.astype(BF16)
    dh = o_ref.shape[2]
    for h in range(o_ref.shape[1]):
        o_ref[:, h, :] = res[:, h * dh:(h + 1) * dh]


def _norm_matmul_heads(x, g, w, layer, n_heads, bm):
    m, k = x.shape
    n = w.shape[2]
    bm = min(bm, m)
    return pl.pallas_call(
        _norm_mm_heads_kernel,
        grid=(m // bm,),
        in_specs=[pl.BlockSpec((bm, k), lambda i: (i, 0)),
                  pl.BlockSpec((1, k), lambda i: (0, 0)),
                  pl.BlockSpec((1, k, n), lambda i: (layer, 0, 0))],
        out_specs=[pl.BlockSpec((bm, n_heads, n // n_heads), lambda i: (i, 0, 0)),
                   pl.BlockSpec((bm, n), lambda i: (i, 0))],
        out_shape=[jax.ShapeDtypeStruct((m, n_heads, n // n_heads), F32),
                   jax.ShapeDtypeStruct((m, n), BF16)],
        compiler_params=_cparams("parallel"),
        name="norm_matmul_heads",
    )(x, g.reshape(1, k), w)


def _mm_res_kernel(a_ref, w_ref, res_ref, o_ref):
    o_ref[...] = res_ref[...] + _dot(a_ref[...].astype(BF16), w_ref[0])


def _matmul_res(a, w, layer, res, bm):
    m, n = res.shape
    k = a.shape[1]
    bm = min(bm, m)
    return pl.pallas_call(
        _mm_res_kernel,
        grid=(m // bm,),
        in_specs=[pl.BlockSpec((bm, k), lambda i: (i, 0)),
                  pl.BlockSpec((1, k, n), lambda i: (layer, 0, 0)),
                  pl.BlockSpec((bm, n), lambda i: (i, 0))],
        out_specs=pl.BlockSpec((bm, n), lambda i: (i, 0)),
        out_shape=jax.ShapeDtypeStruct((m, n), F32),
        compiler_params=_cparams("parallel"),
        name="matmul_res",
    )(a, w, res)


def _ln_silu(cf, lg, lb):
    mu = jnp.mean(cf, axis=-1, keepdims=True)
    d = cf - mu
    var = jnp.mean(d * d, axis=-1, keepdims=True)
    y = d * lax.rsqrt(var + LN_EPS) * lg + lb
    return y * jax.nn.sigmoid(y)


def _conv_prefill_kernel(a_ref, g_ref, buf_ref, w_ref, cb_ref, lg_ref, lb_ref, c_ref, nc_ref,
                         uf_ref, cv_ref, sh_ref, *, tt, halo):
    t = pl.program_id(1)
    pad = 32 - halo

    @pl.when(t == 0)
    def _():
        uf_ref[pad:32, :] = buf_ref[0]

    @pl.when(t > 0)
    def _():
        uf_ref[pad:32, :] = uf_ref[tt + pad:tt + 32, :]

    uf_ref[32:32 + tt, :] = a_ref[...] * jax.nn.sigmoid(g_ref[...])

    for sft in range(8):
        n_rows = sh_ref.shape[1] if sft < 7 else sh_ref.shape[1] - 8
        sh_ref[sft, 0:n_rows, :] = uf_ref[pad + sft:pad + sft + n_rows, :]

    width = uf_ref.shape[1]
    rb = 64
    for r0 in range(0, tt, rb):
        for l0 in range(0, width, LANES):
            acc = jnp.zeros((rb, LANES), F32)
            for j in range(CONV_K):
                base = r0 + j - j % 8
                acc = acc + sh_ref[j % 8, base:base + rb, l0:l0 + LANES] * w_ref[j:j + 1, l0:l0 + LANES]
            cv_ref[r0:r0 + rb, l0:l0 + LANES] = acc

    c_ref[...] = _ln_silu(cv_ref[...] + cb_ref[...], lg_ref[...], lb_ref[...]).astype(c_ref.dtype)

    @pl.when(t == pl.num_programs(1) - 1)
    def _():
        nc_ref[0] = uf_ref[tt + pad:tt + 32, :]


def _conv_prefill(proj, conv_buf, conv_w, conv_b, ln_g, ln_b, batch, seq):
    cw = conv_w.shape[1]
    halo = CONV_K - 1
    tt = min(256, seq)
    nt = seq // tt
    row = lambda b, t: (b * nt + t, 0)
    vec = pl.BlockSpec((1, cw), lambda b, t: (0, 0))
    return pl.pallas_call(
        functools.partial(_conv_prefill_kernel, tt=tt, halo=halo),
        grid=(batch, nt),
        in_specs=[pl.BlockSpec((tt, cw), row),
                  pl.BlockSpec((tt, cw), lambda b, t: (b * nt + t, 1)),
                  pl.BlockSpec((1, halo, cw), lambda b, t: (b, 0, 0)),
                  pl.BlockSpec((CONV_K, cw), lambda b, t: (0, 0)),
                  vec, vec, vec],
        out_specs=[pl.BlockSpec((tt, cw), row),
                   pl.BlockSpec((1, halo, cw), lambda b, t: (b, 0, 0))],
        out_shape=[jax.ShapeDtypeStruct((batch * seq, cw), BF16),
                   jax.ShapeDtypeStruct((batch, halo, cw), F32)],
        scratch_shapes=[pltpu.VMEM((tt + 32, cw), F32), pltpu.VMEM((tt, cw), F32),
                        pltpu.VMEM((8, tt + 24, cw), F32)],
        compiler_params=_cparams("parallel", "arbitrary"),
        name="conv_prefill",
    )(proj, proj, conv_buf, conv_w, conv_b.reshape(1, cw), ln_g.reshape(1, cw), ln_b.reshape(1, cw))


def _conv_decode_kernel(a_ref, g_ref, cache_ref, w_ref, cb_ref, lg_ref, lb_ref, c_ref, nc_ref):
    halo = CONV_K - 1
    u = a_ref[...] * jax.nn.sigmoid(g_ref[...])
    acc = u * w_ref[halo:halo + 1, :]
    for j in range(halo):
        acc = acc + cache_ref[0, :, j, :] * w_ref[j:j + 1, :]
    c_ref[...] = _ln_silu(acc + cb_ref[...], lg_ref[...], lb_ref[...]).astype(c_ref.dtype)
    nc_ref[:, 0:halo - 1, :] = cache_ref[0, :, 1:halo, :]
    nc_ref[:, halo - 1, :] = u


def _conv_decode(proj, cache, layer, conv_w, conv_b, ln_g, ln_b):
    _, batch, halo, cw = cache.shape
    bb = 8
    vec = pl.BlockSpec((1, cw), lambda i: (0, 0))
    return pl.pallas_call(
        _conv_decode_kernel,
        grid=(batch // bb,),
        in_specs=[pl.BlockSpec((bb, cw), lambda i: (i, 0)),
                  pl.BlockSpec((bb, cw), lambda i: (i, 1)),
                  pl.BlockSpec((1, bb, halo, cw), lambda i: (layer, i, 0, 0)),
                  pl.BlockSpec((CONV_K, cw), lambda i: (0, 0)),
                  vec, vec, vec],
        out_specs=[pl.BlockSpec((bb, cw), lambda i: (i, 0)),
                   pl.BlockSpec((bb, halo, cw), lambda i: (i, 0, 0))],
        out_shape=[jax.ShapeDtypeStruct((batch, cw), BF16),
                   jax.ShapeDtypeStruct((batch, halo, cw), F32)],
        compiler_params=_cparams("parallel"),
        name="conv_decode",
    )(proj, proj, cache, conv_w, conv_b.reshape(1, cw), ln_g.reshape(1, cw), ln_b.reshape(1, cw))


def _head_sum(x, bd_ref):
    blk = bd_ref.shape[0]
    parts = [_split_dot(x[:, l0:l0 + blk], bd_ref[...]) for l0 in range(0, x.shape[1], blk)]
    return jnp.concatenate(parts, axis=1)


def _prep_math(q, qp, mu_ref, wd_ref, wa_ref, wg_ref, db_ref, ab_ref, kk_ref, ka_ref, rk_ref, bd_ref):
    rw = q[0].shape[1]
    offs = (0, rw, 2 * rw, 3 * rw)
    r, k, v, lo = [x + (xp - x) * mu_ref[:, o:o + x.shape[1]] for x, xp, o in zip(q, qp, offs)]
    pwa = lo[:, 0:LANES]
    pg = lo[:, LANES:3 * LANES]
    dec_in = _dot(jnp.tanh(pwa).astype(BF16), wd_ref[...])
    a_in = _dot(pwa.astype(BF16), wa_ref[...])
    gate = _dot(jax.nn.sigmoid(pg).astype(BF16), wg_ref[...])
    logw = -DECAY_SCALE * jax.nn.sigmoid(db_ref[...] + dec_in)
    a = jax.nn.sigmoid(ab_ref[...] + a_in)
    kk = k * kk_ref[...]
    kk = kk / jnp.maximum(jnp.sqrt(_head_sum(kk * kk, bd_ref)), 1e-12)
    k2 = k * (1.0 + (a - 1.0) * ka_ref[...])
    bonus = _head_sum(r * k2 * rk_ref[...], bd_ref) * v
    return r, logw, k2, v, kk, a, bonus, gate


def _prep_prefill_kernel(r_ref, k_ref, v_ref, lo_ref, sb_ref, mu_ref, wd_ref, wa_ref, wg_ref, db_ref,
                         ab_ref, kk_ref, ka_ref, rk_ref, bd_ref, *rest):
    outs = rest[:8]
    carry_ref = rest[8]
    t = pl.program_id(1)

    @pl.when(t == 0)
    def _():
        carry_ref[0:1, :] = sb_ref[0]

    q = [r_ref[...], k_ref[...], v_ref[...], lo_ref[...]]
    tt = q[0].shape[0]
    first = lax.broadcasted_iota(jnp.int32, (tt, 1), 0) == 0
    qp = []
    off = 0
    for x in q:
        w = x.shape[1]
        qp.append(jnp.where(first, carry_ref[0:1, off:off + w], pltpu.roll(x, 1, 0)))
        off += w
    off = 0
    for x in q:
        w = x.shape[1]
        carry_ref[0:1, off:off + w] = x[tt - 1:tt, :]
        off += w
    res = _prep_math(q, qp, mu_ref, wd_ref, wa_ref, wg_ref, db_ref, ab_ref, kk_ref, ka_ref, rk_ref, bd_ref)
    for o_ref, val in zip(outs, res):
        o_ref[...] = val.astype(o_ref.dtype)


def _prep_decode_kernel(r_ref, k_ref, v_ref, lo_ref, rp_ref, kp_ref, vp_ref, lop_ref, mu_ref, wd_ref,
                        wa_ref, wg_ref, db_ref, ab_ref, kk_ref, ka_ref, rk_ref, bd_ref, *outs):
    q = [r_ref[...], k_ref[...], v_ref[...], lo_ref[...]]
    qp = [rp_ref[...], kp_ref[...], vp_ref[...], lop_ref[...]]
    res = _prep_math(q, qp, mu_ref, wd_ref, wa_ref, wg_ref, db_ref, ab_ref, kk_ref, ka_ref, rk_ref, bd_ref)
    for o_ref, val in zip(outs, res):
        o_ref[...] = val


def _prep_param_specs(rw, idx):
    full = lambda shape: pl.BlockSpec(shape, idx)
    vec = full((1, rw))
    return [full((1, 3 * rw + LORA_PAD)), full((LANES, rw)), full((LANES, rw)), full((2 * LANES, rw)),
            vec, vec, vec, vec, vec, full((2 * LANES, 2 * LANES))]


def _rwkv_prep_prefill(proj, shift_buf, pp, batch, seq, rw):
    tt = min(256, seq)
    nt = seq // tt
    lora_blk = (2 * rw + 3 * rw) // LORA_PAD
    col = lambda c: (lambda b, t: (b * nt + t, c))
    qw = 3 * rw + LORA_PAD
    in_specs = [pl.BlockSpec((tt, rw), col(2)), pl.BlockSpec((tt, rw), col(3)),
                pl.BlockSpec((tt, rw), col(4)), pl.BlockSpec((tt, LORA_PAD), col(lora_blk)),
                pl.BlockSpec((1, 1, qw), lambda b, t: (b, 0, 0))]
    in_specs += _prep_param_specs(rw, lambda b, t: (0, 0))
    out_spec = pl.BlockSpec((tt, rw), col(0))
    return pl.pallas_call(
        _prep_prefill_kernel,
        grid=(batch, nt),
        in_specs=in_specs,
        out_specs=[out_spec] * 8,
        out_shape=[jax.ShapeDtypeStruct((batch * seq, rw), BF16 if i == 3 else F32) for i in range(8)],
        scratch_shapes=[pltpu.VMEM((8, qw), F32)],
        compiler_params=_cparams("parallel", "arbitrary"),
        name="rwkv_prep_prefill",
    )(proj, proj, proj, proj, shift_buf.reshape(batch, 1, qw), *pp)


def _rwkv_prep_decode(proj, shift_state, pp, rw):
    batch = proj.shape[0]
    bb = min(128, batch)
    lora_blk = (2 * rw + 3 * rw) // LORA_PAD
    col = lambda c: (lambda i: (i, c))
    in_specs = [pl.BlockSpec((bb, rw), col(2)), pl.BlockSpec((bb, rw), col(3)),
                pl.BlockSpec((bb, rw), col(4)), pl.BlockSpec((bb, LORA_PAD), col(lora_blk)),
                pl.BlockSpec((bb, rw), col(0)), pl.BlockSpec((bb, rw), col(1)),
                pl.BlockSpec((bb, rw), col(2)), pl.BlockSpec((bb, LORA_PAD), col(3 * rw // LORA_PAD))]
    in_specs += _prep_param_specs(rw, lambda i: (0, 0))
    return pl.pallas_call(
        _prep_decode_kernel,
        grid=(batch // bb,),
        in_specs=in_specs,
        out_specs=[pl.BlockSpec((bb, rw), col(0))] * 8,
        out_shape=[jax.ShapeDtypeStruct((batch, rw), F32)] * 8,
        compiler_params=_cparams("parallel"),
        name="rwkv_prep_decode",
    )(proj, proj, proj, proj, shift_state, shift_state, shift_state, shift_state, *pp)


def _stack2(x, smask):
    return jnp.where(smask, jnp.concatenate([x, x], axis=0), 0.0)


def _rwkv_chunk_kernel(r_ref, lw_ref, k_ref, v_ref, kk_ref, a_ref, s0_ref, o_ref, so_ref, s_ref):
    c = pl.program_id(1)
    cs = r_ref.shape[0]
    n_pairs = r_ref.shape[1] // PAIR
    two = 2 * cs

    @pl.when(c == 0)
    def _():
        z = jnp.zeros((HEAD, HEAD), F32)
        for p in range(n_pairs):
            top = jnp.concatenate([s0_ref[0, 2 * p], z], axis=1)
            bot = jnp.concatenate([z, s0_ref[0, 2 * p + 1]], axis=1)
            s_ref[p] = jnp.concatenate([top, bot], axis=0)

    ri = lax.broadcasted_iota(jnp.int32, (two, two), 0)
    ci = lax.broadcasted_iota(jnp.int32, (two, two), 1)
    strict = ci < ri
    incl = ci <= ri
    eye = (ci == ri).astype(F32)
    smask = (lax.broadcasted_iota(jnp.int32, (two, PAIR), 0) < cs) == (
        lax.broadcasted_iota(jnp.int32, (two, PAIR), 1) < HEAD)
    tri = (lax.broadcasted_iota(jnp.int32, (cs, cs), 1)
           <= lax.broadcasted_iota(jnp.int32, (cs, cs), 0)).astype(BF16)

    lw_all = lw_ref[...]
    lw_hi = lw_all.astype(BF16)
    lw_lo = (lw_all - lw_hi.astype(F32)).astype(BF16)
    cum_all = _dot(tri, lw_hi) + _dot(tri, lw_lo)

    pairs = range(n_pairs)
    cat = jnp.concatenate
    prep = []
    for p in pairs:
        sl = slice(p * PAIR, (p + 1) * PAIR)
        lw = lw_all[:, sl]
        cum = cum_all[:, sl]
        tot = cum[cs - 1:cs, :]
        g_inv = jnp.exp(-cum)
        g_end = jnp.exp(tot - cum)
        kk = kk_ref[:, sl]
        k2 = k_ref[:, sl]
        bb = kk * a_ref[:, sl]
        prep.append(dict(
            g_tot=jnp.exp(tot),
            a_b=_stack2(kk * jnp.exp(cum - lw), smask).astype(BF16),
            r_s=_stack2(r_ref[:, sl] * jnp.exp(cum), smask),
            bk=cat([_stack2(bb * g_inv, smask), _stack2(k2 * g_inv, smask)], axis=0).astype(BF16),
            v_s=_stack2(v_ref[:, sl].astype(F32), smask),
            bg_s=_stack2(bb * g_end, smask).astype(BF16),
            kg_s=_stack2(k2 * g_end, smask).astype(BF16)))
    a_b = [q["a_b"] for q in prep]
    r_s = [q["r_s"] for q in prep]
    v_s = [q["v_s"] for q in prep]
    v_b = [x.astype(BF16) for x in v_s]

    gram = [_dot_nt(cat([a_b[p], r_s[p].astype(BF16)], axis=0), prep[p]["bk"]) for p in pairs]
    l_ab = [jnp.where(strict, g[0:two, 0:two], 0.0) for g in gram]
    l_ak = [jnp.where(strict, g[0:two, two:], 0.0).astype(BF16) for g in gram]
    m_rb = [jnp.where(incl, g[two:, 0:two], 0.0).astype(BF16) for g in gram]
    m_rk = [jnp.where(incl, g[two:, two:], 0.0).astype(BF16) for g in gram]

    tm = [eye - x for x in l_ab]
    pw = [x.astype(BF16) for x in l_ab]
    pw = [_dot(x, x).astype(BF16) for x in pw]
    n = 2
    while 2 * n < cs:
        both = [_dot(cat([t.astype(BF16), x], axis=0), x) for t, x in zip(tm, pw)]
        tm = [t + b[0:two] for t, b in zip(tm, both)]
        pw = [b[two:].astype(BF16) for b in both]
        n *= 2
    tm_b = [(t + _dot(t.astype(BF16), x)).astype(BF16) for t, x in zip(tm, pw)]

    wv = [_dot(cat([l_ak[p], m_rk[p]], axis=0), v_b[p]) for p in pairs]
    ua = [_dot(tm_b[p], cat([wv[p][0:two].astype(BF16), a_b[p]], axis=1)) for p in pairs]
    ua_b = [x.astype(BF16) for x in ua]
    mrb_ua = [_dot(m_rb[p], ua_b[p]) for p in pairs]
    o0 = [wv[p][two:] - mrb_ua[p][:, 0:PAIR] for p in pairs]
    rt = [(r_s[p] - mrb_ua[p][:, PAIR:]).astype(BF16) for p in pairs]

    s_old = [s_ref[p] for p in pairs]
    s_b = [x.astype(BF16) for x in s_old]
    lhs_t = [cat([ua[p][:, 0:PAIR].T, ua[p][:, PAIR:].T, v_s[p].T], axis=0).astype(BF16) for p in pairs]
    t_all = [_dot(lhs_t[p], cat([prep[p]["bg_s"], prep[p]["kg_s"]], axis=1)) for p in pairs]
    s_new = [s_old[p] * prep[p]["g_tot"] - _dot(s_b[p], t_all[p][PAIR:2 * PAIR, 0:PAIR].astype(BF16))
             + t_all[p][2 * PAIR:, PAIR:] - t_all[p][0:PAIR, 0:PAIR] for p in pairs]
    o_st = [o0[p] + _dot_nt(rt[p], s_b[p]) for p in pairs]

    o_ref[...] = cat([x[0:cs, :] + x[cs:two, :] for x in o_st], axis=1)
    s_ref[...] = jnp.stack(s_new, axis=0)

    @pl.when(c == pl.num_programs(1) - 1)
    def _():
        for p in range(n_pairs):
            s = s_ref[p]
            so_ref[0, 2 * p] = s[0:HEAD, 0:HEAD]
            so_ref[0, 2 * p + 1] = s[HEAD:PAIR, HEAD:PAIR]


def _rwkv_chunked(r, logw, k2, v, kk, a, s0, batch, seq):
    rw = r.shape[1]
    nc = seq // CHUNK
    heads = rw // HEAD
    row = pl.BlockSpec((CHUNK, rw), lambda b, c: (b * nc + c, 0))
    st = pl.BlockSpec((1, heads, HEAD, HEAD), lambda b, c: (b, 0, 0, 0))
    return pl.pallas_call(
        _rwkv_chunk_kernel,
        grid=(batch, nc),
        in_specs=[row] * 6 + [st],
        out_specs=[row, st],
        out_shape=[jax.ShapeDtypeStruct((batch * seq, rw), F32),
                   jax.ShapeDtypeStruct((batch, heads, HEAD, HEAD), F32)],
        scratch_shapes=[pltpu.VMEM((rw // PAIR, PAIR, PAIR), F32)],
        compiler_params=_cparams("parallel", "arbitrary"),
        name="rwkv_chunked",
    )(r, logw, k2, v, kk, a, s0)


def _rwkv_step_kernel(r_ref, lw_ref, k_ref, v_ref, kk_ref, a_ref, s_ref, o_ref, so_ref, *, heads):
    bb = s_ref.shape[1]
    eye = (lax.broadcasted_iota(jnp.int32, (HEAD, HEAD), 0)
           == lax.broadcasted_iota(jnp.int32, (HEAD, HEAD), 1))

    def body(bi, carry):
        hs = range(heads)
        rows = [pl.ds(bi * heads + h, 1) for h in hs]
        kk = [kk_ref[r, :] for r in rows]
        s = [s_ref[0, bi, h].astype(F32) for h in hs]
        sa = [jnp.sum(s[h] * kk[h], axis=1, keepdims=True) for h in hs]
        v_col = [jnp.sum(jnp.where(eye, v_ref[rows[h], :], 0.0), axis=1, keepdims=True) for h in hs]
        s_new = [s[h] * jnp.exp(lw_ref[rows[h], :]) - sa[h] * (kk[h] * a_ref[rows[h], :])
                 + v_col[h] * k_ref[rows[h], :] for h in hs]
        for h in hs:
            so_ref[bi, h] = s_new[h]
        o_col = [jnp.sum(s_new[h] * r_ref[rows[h], :], axis=1, keepdims=True) for h in hs]
        o_row = [jnp.sum(jnp.where(eye, o_col[h], 0.0), axis=0, keepdims=True) for h in hs]
        o_ref[pl.ds(pl.multiple_of(bi * heads, heads), heads), :] = jnp.concatenate(o_row, axis=0)
        return carry

    lax.fori_loop(0, bb, body, 0)


def _rwkv_step(r, logw, k2, v, kk, a, state, layer):
    batch, rw = r.shape
    heads = rw // HEAD
    bb = 8
    flat = lambda x: x.reshape(batch * heads, HEAD)
    row = pl.BlockSpec((bb * heads, HEAD), lambda i: (i, 0))
    st_in = pl.BlockSpec((1, bb, heads, HEAD, HEAD), lambda i: (layer, i, 0, 0, 0))
    st_out = pl.BlockSpec((bb, heads, HEAD, HEAD), lambda i: (i, 0, 0, 0))
    o, s_new = pl.pallas_call(
        functools.partial(_rwkv_step_kernel, heads=heads),
        grid=(batch // bb,),
        in_specs=[row] * 6 + [st_in],
        out_specs=[row, st_out],
        out_shape=[jax.ShapeDtypeStruct((batch * heads, HEAD), F32),
                   jax.ShapeDtypeStruct(state.shape[1:], F32)],
        compiler_params=_cparams("parallel"),
        name="rwkv_step",
    )(flat(r), flat(logw), flat(k2), flat(v), flat(kk), flat(a), state)
    return o.reshape(batch, rw), s_new


def _mix_out_kernel(c_ref, o_ref, bonus_ref, gate_ref, lg_ref, lb_ref, bd_ref, wc_ref, wo_ref, x_ref, y_ref):
    o = o_ref[...]
    mu = _head_sum(o, bd_ref) * (1.0 / HEAD)
    d = o - mu
    var = _head_sum(d * d, bd_ref) * (1.0 / HEAD)
    y = d * lax.rsqrt(var + GN_EPS) * lg_ref[...] + lb_ref[...]
    om = ((y + bonus_ref[...]) * gate_ref[...]).astype(BF16)
    y_ref[...] = x_ref[...] + _dot(c_ref[...], wc_ref[0]) + _dot(om, wo_ref[0])


def _mix_out(c, o, bonus, gate, lnx_g, lnx_b, bd, w_out, layer, x):
    m, d = x.shape
    cw, rw = c.shape[1], o.shape[1]
    assert cw == rw and w_out.shape[1] == cw + rw
    bm = min(512, m)
    row = lambda w: pl.BlockSpec((bm, w), lambda i: (i, 0))
    vec = pl.BlockSpec((1, rw), lambda i: (0, 0))
    return pl.pallas_call(
        _mix_out_kernel,
        grid=(m // bm,),
        in_specs=[row(cw), row(rw), row(rw), row(rw), vec, vec, pl.BlockSpec(bd.shape, lambda i: (0, 0)),
                  pl.BlockSpec((1, cw, d), lambda i: (layer, 0, 0)),
                  pl.BlockSpec((1, rw, d), lambda i: (layer, 1, 0)),
                  row(d)],
        out_specs=row(d),
        out_shape=jax.ShapeDtypeStruct((m, d), F32),
        compiler_params=_cparams("parallel"),
        name="mix_out",
    )(c, o, bonus, gate, lnx_g.reshape(1, rw), lnx_b.reshape(1, rw), bd, w_out, w_out, x)


def _attn_prefill_kernel(q_ref, k_ref, v_ref, w_ref, x_ref, o_ref, *, n_heads):
    d = q_ref.shape[1] // n_heads
    scale = d ** -0.5
    ctx = []
    for h in range(n_heads):
        sl = slice(h * d, (h + 1) * d)
        s = _dot_nt(q_ref[:, sl].astype(BF16), k_ref[:, sl]) * scale
        p = jnp.exp(s - jnp.max(s, axis=-1, keepdims=True))
        att = p / jnp.sum(p, axis=-1, keepdims=True)
        ctx.append(_dot(att.astype(BF16), v_ref[:, sl]).astype(BF16))
    o_ref[...] = x_ref[...] + _dot(jnp.concatenate(ctx, axis=1), w_ref[0])


def _attn_prefill(q, mem_k, mem_v, w_co, layer, x, batch, seq):
    d = q.shape[1]
    tt = min(512, seq)
    nt = seq // tt
    kv = pl.BlockSpec((N_MEM, d), lambda b, t: (b, 0))
    row = pl.BlockSpec((tt, d), lambda b, t: (b * nt + t, 0))
    return pl.pallas_call(
        functools.partial(_attn_prefill_kernel, n_heads=X_HEADS),
        grid=(batch, nt),
        in_specs=[row, kv, kv, pl.BlockSpec((1, d, d), lambda b, t: (layer, 0, 0)), row],
        out_specs=row,
        out_shape=jax.ShapeDtypeStruct((batch * seq, d), F32),
        compiler_params=_cparams("parallel", "arbitrary"),
        name="attn_prefill",
    )(q, mem_k, mem_v, w_co, x)


def _decode_attn_rows(qs, k_at, v_at, n_heads):
    d = qs[0].shape[1] // n_heads
    scale = d ** -0.5
    ids = [(i, h) for i in range(len(qs)) for h in range(n_heads)]
    s = [jnp.sum(k_at(i, h) * qs[i][:, h * d:(h + 1) * d], axis=1, keepdims=True) * scale for i, h in ids]
    p = [jnp.exp(x - jnp.max(x, axis=0, keepdims=True)) for x in s]
    att = [x / jnp.sum(x, axis=0, keepdims=True) for x in p]
    ctx = [jnp.sum(a * v_at(i, h), axis=0, keepdims=True) for a, (i, h) in zip(att, ids)]
    return [ctx[i * n_heads:(i + 1) * n_heads] for i in range(len(qs))]


def _attn_decode_kernel(q_ref, k_ref, v_ref, o_ref):
    n_heads, d = k_ref.shape[2], k_ref.shape[3]
    bb = q_ref.shape[0]
    ctx = _decode_attn_rows([q_ref[bi] for bi in range(bb)], lambda i, h: k_ref[i, :, h, :],
                            lambda i, h: v_ref[i, :, h, :], n_heads)
    for bi in range(bb):
        for h in range(n_heads):
            o_ref[bi, :, h * d:(h + 1) * d] = ctx[bi][h]


def _attn_decode(q, cache_k, cache_v):
    batch, d = q.shape
    bb = 2
    kv = pl.BlockSpec((bb, N_MEM, X_HEADS, d // X_HEADS), lambda i: (i, 0, 0, 0))
    row = pl.BlockSpec((bb, 1, d), lambda i: (i, 0, 0))
    out = pl.pallas_call(
        _attn_decode_kernel,
        grid=(batch // bb,),
        in_specs=[row, kv, kv],
        out_specs=row,
        out_shape=jax.ShapeDtypeStruct((batch, 1, d), F32),
        compiler_params=_cparams("parallel"),
        name="attn_decode",
    )(q.reshape(batch, 1, d), cache_k, cache_v)
    return out.reshape(batch, d)


def _route(x_ref, g_ref, wh_ref, wl_ref, b_ref):
    h = _rms(x_ref[...], g_ref[...])
    hh = h.astype(BF16)
    hl = (h - hh.astype(F32)).astype(BF16)
    logits = _dot(hh, wh_ref[...]) + _dot(hl, wh_ref[...]) + _dot(hh, wl_ref[...]) + b_ref[...]
    lane = lax.broadcasted_iota(jnp.int32, (1, LANES), 1).astype(F32)
    is_g = (lane >= N_EXPERTS) & (lane < N_EXPERTS + N_GROUPS)
    lgm = jnp.where(is_g, logits, NEG_BIG)
    gmax = jnp.max(lgm, axis=1, keepdims=True)
    gsum = jnp.sum(jnp.where(is_g, jnp.exp(lgm - gmax), 0.0), axis=1, keepdims=True)
    g_val = 1.0 / gsum
    g_idx = jnp.min(jnp.where(is_g & (lgm == gmax), lane - N_EXPERTS, 1e9), axis=1, keepdims=True)
    in_grp = (lane < N_EXPERTS) & (jnp.floor(lane * (1.0 / EXP_PER_GROUP)) == g_idx)
    le = jnp.where(in_grp, logits, NEG_BIG)
    m1 = jnp.max(le, axis=1, keepdims=True)
    i1 = jnp.min(jnp.where(in_grp & (le == m1), lane, 1e9), axis=1, keepdims=True)
    rest = in_grp & (lane != i1)
    le2 = jnp.where(rest, logits, NEG_BIG)
    m2 = jnp.max(le2, axis=1, keepdims=True)
    i2 = jnp.min(jnp.where(rest & (le2 == m2), lane, 1e9), axis=1, keepdims=True)
    e2 = jnp.exp(m2 - m1)
    den = 1.0 + e2
    w1 = (1.0 / den) * g_val
    w2 = (e2 / den) * g_val
    return h, lane, i1, i2, w1, w2


def _router_sorted_kernel(x_ref, g_ref, wh_ref, wl_ref, b_ref, init_ref, route_ref, rt_ref, cnt_ref, run_ref):
    @pl.when(pl.program_id(0) == 0)
    def _():
        run_ref[0:1, :] = init_ref[...]

    h, lane, i1, i2, w1, w2 = _route(x_ref, g_ref, wh_ref, wl_ref, b_ref)
    bm = h.shape[0]
    oh1 = lane == i1
    oh2 = lane == i2
    sel = (oh1 | oh2).astype(BF16)
    before = (lax.broadcasted_iota(jnp.int32, (bm, bm), 1)
              < lax.broadcasted_iota(jnp.int32, (bm, bm), 0)).astype(BF16)
    base = run_ref[0:1, :] + _dot(before, sel)
    rank1 = jnp.sum(jnp.where(oh1, base, 0.0), axis=1, keepdims=True)
    rank2 = jnp.sum(jnp.where(oh2, base, 0.0), axis=1, keepdims=True)
    total = run_ref[0:1, :] + jnp.sum(sel.astype(F32), axis=0, keepdims=True)
    run_ref[0:1, :] = total
    cnt_ref[...] = total
    route = jnp.zeros((bm, LANES), F32)
    for idx, val in enumerate((i1, i2, w1, w2, rank1, rank2)):
        route = jnp.where(lane == idx, val, route)
    route_ref[...] = route
    for r0 in range(0, bm, LANES):
        rt_ref[:, r0:r0 + LANES] = route[r0:r0 + LANES, :].T[0:8, :]


def _router_sorted(x, g, wh, wl, bias, init_counts):
    m, d = x.shape
    bm = min(512, m)
    return pl.pallas_call(
        _router_sorted_kernel,
        grid=(m // bm,),
        in_specs=[pl.BlockSpec((bm, d), lambda i: (i, 0)),
                  pl.BlockSpec((1, d), lambda i: (0, 0)),
                  pl.BlockSpec((d, LANES), lambda i: (0, 0)),
                  pl.BlockSpec((d, LANES), lambda i: (0, 0)),
                  pl.BlockSpec((1, LANES), lambda i: (0, 0)),
                  pl.BlockSpec((1, LANES), lambda i: (0, 0))],
        out_specs=[pl.BlockSpec((bm, LANES), lambda i: (i, 0)),
                   pl.BlockSpec((8, bm), lambda i: (0, i)),
                   pl.BlockSpec((1, LANES), lambda i: (0, 0))],
        out_shape=[jax.ShapeDtypeStruct((m, LANES), F32),
                   jax.ShapeDtypeStruct((8, m), F32),
                   jax.ShapeDtypeStruct((1, LANES), F32)],
        scratch_shapes=[pltpu.VMEM((8, LANES), F32)],
        compiler_params=_cparams("arbitrary"),
        name="moe_router_sorted",
    )(x, g.reshape(1, d), wh, wl, bias, init_counts)


def _plan_kernel(seg_ref, rt_ref, pos_ref):
    rt = rt_ref[...]
    rows = []
    for e_row, r_row in ((0, 4), (1, 5)):
        e = rt[e_row:e_row + 1, :]
        start = jnp.zeros_like(e)
        for k in range(N_EXPERTS):
            start = jnp.where(e == k, seg_ref[k].astype(F32), start)
        rows.append((start + rt[r_row:r_row + 1, :]).astype(jnp.int32))
    pos_ref[...] = jnp.concatenate(rows + [jnp.zeros((6, rt.shape[1]), jnp.int32)], axis=0)


def _plan(route_t, seg_start):
    m = route_t.shape[1]
    bt = min(2048, m)
    pos = pl.pallas_call(
        _plan_kernel,
        grid_spec=pltpu.PrefetchScalarGridSpec(
            num_scalar_prefetch=1, grid=(m // bt,),
            in_specs=[pl.BlockSpec((8, bt), lambda i, seg: (0, i))],
            out_specs=pl.BlockSpec((8, bt), lambda i, seg: (0, i))),
        out_shape=jax.ShapeDtypeStruct((8, m), jnp.int32),
        compiler_params=_cparams("arbitrary"),
        name="moe_plan",
    )(seg_start, route_t)
    return pos[0], pos[1]


def _row_copy(src_hbm, src_row, dst, dst_row, sem):
    return pltpu.make_async_copy(src_hbm.at[pl.ds(src_row, 1)], dst.at[pl.ds(dst_row, 1)], sem)


def _dispatch_kernel(p1_ref, p2_ref, seg_ref, cnt_ref, nrow_ref, xa_ref, xb_ref, g_ref, xs_hbm, h_ref,
                     zero_ref, sem, zsem, *, n_a):
    i = pl.program_id(0)
    n = pl.num_programs(0)
    bm = xa_ref.shape[0]
    n_max = xs_hbm.shape[0] // MOE_BM

    def zero_copy(row0):
        return pltpu.make_async_copy(zero_ref, xs_hbm.at[pl.ds(pl.multiple_of(row0, MOE_BM), MOE_BM)], zsem)

    @pl.when(i == 0)
    def _():
        zero_ref[...] = jnp.zeros_like(zero_ref)
        first_free = nrow_ref[0] // MOE_BM

        def tail_start(c, carry):
            zero_copy(c * MOE_BM).start()
            return carry

        def tail_wait(c, carry):
            zero_copy(c * MOE_BM).wait()
            return carry

        for e in range(N_EXPERTS):
            @pl.when(cnt_ref[e] > 0)
            def _():
                zero_copy(seg_ref[e] - MOE_BM).start()

        lax.fori_loop(first_free, n_max, tail_start, 0)
        for e in range(N_EXPERTS):
            @pl.when(cnt_ref[e] > 0)
            def _():
                zero_copy(seg_ref[e] - MOE_BM).wait()

        lax.fori_loop(first_free, n_max, tail_wait, 0)

    bm_b = xb_ref.shape[0]
    tok_b0 = n_a * bm

    def wait_rows(slot, rows):
        for _ in range(2):
            pltpu.make_async_copy(h_ref.at[slot, pl.ds(0, rows)], xs_hbm.at[pl.ds(0, rows)],
                                  sem.at[slot]).wait()

    def scatter_rows(slot, rows, tok0):
        def body(r, carry):
            src = h_ref.at[slot, pl.ds(r, 1)]
            pltpu.make_async_copy(src, xs_hbm.at[pl.ds(p1_ref[tok0 + r], 1)], sem.at[slot]).start()
            pltpu.make_async_copy(src, xs_hbm.at[pl.ds(p2_ref[tok0 + r], 1)], sem.at[slot]).start()
            return carry

        lax.fori_loop(0, rows, body, 0, unroll=8)

    slot = i % 2

    @pl.when((i > 0) & (i - 1 < n_a))
    def _():
        wait_rows(1 - slot, bm)

    @pl.when(i - 1 >= n_a)
    def _():
        wait_rows(1 - slot, bm_b)

    @pl.when(i < n_a)
    def _():
        h_ref[slot, 0:bm, :] = _rms(xa_ref[...], g_ref[...])
        scatter_rows(slot, bm, i * bm)

    @pl.when(i >= n_a)
    def _():
        h_ref[slot, 0:bm_b, :] = _rms(xb_ref[...], g_ref[...])
        scatter_rows(slot, bm_b, tok_b0 + (i - n_a) * bm_b)

    @pl.when(i == n - 1)
    def _():
        wait_rows(slot, bm_b)


def _dispatch(xa, xb, g, pos1, pos2, seg_end, cnt, n_rows_used, n_rows_max):
    (ma, d), mb = xa.shape, xb.shape[0]
    bm_a, bm_b = min(MOE_BM, ma), min(MOE_BM, mb)
    assert ma % bm_a == 0 and mb % bm_b == 0 and bm_b <= bm_a
    n_a, n_b = ma // bm_a, mb // bm_b
    return pl.pallas_call(
        functools.partial(_dispatch_kernel, n_a=n_a),
        grid_spec=pltpu.PrefetchScalarGridSpec(
            num_scalar_prefetch=5, grid=(n_a + n_b,),
            in_specs=[pl.BlockSpec((bm_a, d), lambda i, *_: (jnp.minimum(i, n_a - 1), 0)),
                      pl.BlockSpec((bm_b, d), lambda i, *_: (jnp.maximum(i - n_a, 0), 0)),
                      pl.BlockSpec((1, d), lambda i, *_: (0, 0))],
            out_specs=pl.BlockSpec(memory_space=pl.ANY),
            scratch_shapes=[pltpu.VMEM((2, bm_a, d), F32), pltpu.VMEM((MOE_BM, d), F32),
                            pltpu.SemaphoreType.DMA((2,)), pltpu.SemaphoreType.DMA(())]),
        out_shape=jax.ShapeDtypeStruct((n_rows_max, d), F32),
        compiler_params=_cparams("arbitrary"),
        name="moe_dispatch",
    )(pos1, pos2, seg_end, cnt, n_rows_used, xa, xb, g.reshape(1, d))


def _experts_sorted_kernel(te_ref, nt_ref, nxt_ref, par_ref, xs_ref, wg_hbm, wu_hbm, wd_hbm, ys_ref,
                           wgf_ref, wuf_ref, wdf_ref, wgb_ref, wub_ref, wdb_ref, sem):
    j = pl.program_id(0)
    prev = te_ref[jnp.maximum(j, 1) - 1]

    def copies(e, slot):
        return [pltpu.make_async_copy(hbm.at[e], buf.at[slot], sem.at[slot])
                for hbm, buf in ((wg_hbm, wgf_ref), (wu_hbm, wuf_ref), (wd_hbm, wdf_ref))]

    @pl.when(j == 0)
    def _():
        for c in copies(te_ref[0], par_ref[0]):
            c.start()

    @pl.when((j < nt_ref[0]) & ((j == 0) | (te_ref[j] != prev)))
    def _():
        slot = par_ref[j]
        for c in copies(te_ref[j], slot):
            c.wait()

        @pl.when(nxt_ref[j] >= 0)
        def _():
            for c in copies(nxt_ref[j], 1 - slot):
                c.start()

        wgb_ref[...] = wgf_ref[slot].astype(BF16)
        wub_ref[...] = wuf_ref[slot].astype(BF16)
        wdb_ref[...] = wdf_ref[slot].astype(BF16)

    @pl.when(j < nt_ref[0])
    def _():
        x = xs_ref[...].astype(BF16)
        hg = _dot(x, wgb_ref[...])
        hu = _dot(x, wub_ref[...])
        act = hg * jax.nn.sigmoid(hg) * hu
        ys_ref[...] = _dot(act.astype(BF16), wdb_ref[...])

    @pl.when(j >= nt_ref[0])
    def _():
        ys_ref[...] = jnp.zeros_like(ys_ref)


def _experts_sorted(xs, tile_expert, n_tiles_used, next_expert, slot_parity, wg, wu, wd):
    n_rows, d = xs.shape
    de = wg.shape[2]
    n_tiles = n_rows // MOE_BM
    row_in = lambda j, te, nt, *_: (jnp.minimum(j, nt[0] - 1), 0)
    hbm = pl.BlockSpec(memory_space=pl.ANY)
    return pl.pallas_call(
        _experts_sorted_kernel,
        grid_spec=pltpu.PrefetchScalarGridSpec(
            num_scalar_prefetch=4, grid=(n_tiles,),
            in_specs=[pl.BlockSpec((MOE_BM, d), row_in), hbm, hbm, hbm],
            out_specs=pl.BlockSpec((MOE_BM, d), lambda j, *_: (j, 0)),
            scratch_shapes=[pltpu.VMEM((2, d, de), F32), pltpu.VMEM((2, d, de), F32),
                            pltpu.VMEM((2, de, d), F32),
                            pltpu.VMEM((d, de), BF16), pltpu.VMEM((d, de), BF16), pltpu.VMEM((de, d), BF16),
                            pltpu.SemaphoreType.DMA((2,))]),
        out_shape=jax.ShapeDtypeStruct((n_rows, d), F32),
        compiler_params=_cparams("arbitrary"),
        name="moe_experts_sorted",
    )(tile_expert, n_tiles_used, next_expert, slot_parity, xs, wg, wu, wd)


def _combine_kernel(p1_ref, p2_ref, ys_hbm, x_ref, route_ref, nf_ref, y_ref, buf_ref, sem):
    i = pl.program_id(0)
    n = pl.num_programs(0)
    bm = x_ref.shape[0]

    def issue(tile, slot):
        def body(r, carry):
            t = tile * bm + r
            _row_copy(ys_hbm, p1_ref[t], buf_ref.at[slot, 0], r, sem.at[slot]).start()
            _row_copy(ys_hbm, p2_ref[t], buf_ref.at[slot, 1], r, sem.at[slot]).start()
            return carry

        lax.fori_loop(0, bm, body, 0, unroll=8)

    @pl.when(i == 0)
    def _():
        issue(0, 0)

    slot = i % 2
    for k in range(2):
        pltpu.make_async_copy(ys_hbm.at[pl.ds(0, bm)], buf_ref.at[slot, k], sem.at[slot]).wait()
    lane = lax.broadcasted_iota(jnp.int32, (1, LANES), 1)
    chunk = min(32, bm)

    def compute_rows(r0):
        rs = slice(r0, r0 + chunk)
        route = route_ref[rs, :]
        w1 = jnp.sum(jnp.where(lane == 2, route, 0.0), axis=1, keepdims=True)
        w2 = jnp.sum(jnp.where(lane == 3, route, 0.0), axis=1, keepdims=True)
        x3 = x_ref[rs, :] + w1 * buf_ref[slot, 0, rs, :] + w2 * buf_ref[slot, 1, rs, :]
        y_ref[rs, :] = _rms(x3, nf_ref[...])

    @pl.when(i + 1 < n)
    def _():
        nslot = 1 - slot
        for r0 in range(0, bm, chunk):
            for r in range(r0, r0 + chunk):
                t = (i + 1) * bm + r
                _row_copy(ys_hbm, p1_ref[t], buf_ref.at[nslot, 0], r, sem.at[nslot]).start()
                _row_copy(ys_hbm, p2_ref[t], buf_ref.at[nslot, 1], r, sem.at[nslot]).start()
            compute_rows(r0)

    @pl.when(i + 1 >= n)
    def _():
        for r0 in range(0, bm, chunk):
            compute_rows(r0)


def _combine(ys, pos1, pos2, x, route, norm_final):
    m, d = x.shape
    bm = min(MOE_BM, m)
    return pl.pallas_call(
        _combine_kernel,
        grid_spec=pltpu.PrefetchScalarGridSpec(
            num_scalar_prefetch=2, grid=(m // bm,),
            in_specs=[pl.BlockSpec(memory_space=pl.ANY),
                      pl.BlockSpec((bm, d), lambda i, p1, p2: (i, 0)),
                      pl.BlockSpec((bm, LANES), lambda i, p1, p2: (i, 0)),
                      pl.BlockSpec((1, d), lambda i, p1, p2: (0, 0))],
            out_specs=pl.BlockSpec((bm, d), lambda i, p1, p2: (i, 0)),
            scratch_shapes=[pltpu.VMEM((2, 2, bm, d), F32), pltpu.SemaphoreType.DMA((2,))]),
        out_shape=jax.ShapeDtypeStruct((m, d), F32),
        compiler_params=_cparams("arbitrary"),
        name="moe_combine",
    )(pos1, pos2, ys, x, route, norm_final.reshape(1, d))


def _moe_sorted(xa, xb, g, wh, wl, bias, wg, wu, wd, norm_final):
    ma, mb = xa.shape[0], xb.shape[0]
    route_a, rt_a, cnt_a = _router_sorted(xa, g, wh, wl, bias, jnp.zeros((1, LANES), F32))
    route_b, rt_b, counts = _router_sorted(xb, g, wh, wl, bias, cnt_a)
    cnt = counts[0, :N_EXPERTS].astype(jnp.int32)
    padded = (cnt + MOE_BM - 1) // MOE_BM * MOE_BM
    seg_end = jnp.cumsum(padded)
    seg_start = seg_end - padded
    n_tiles_max = (2 * (ma + mb) + MOE_BM - 1) // MOE_BM + N_EXPERTS
    n_rows_max = n_tiles_max * MOE_BM
    n_rows_used = seg_end[-1:]
    n_tiles_used = n_rows_used // MOE_BM
    tile_start = jnp.arange(n_tiles_max, dtype=jnp.int32) * MOE_BM
    tile_expert = jnp.sum((seg_end[None, :] <= tile_start[:, None]).astype(jnp.int32), axis=1)
    last_expert = jnp.max(jnp.where(cnt > 0, jnp.arange(N_EXPERTS, dtype=jnp.int32), 0))
    tile_expert = jnp.minimum(tile_expert, last_expert)
    eidx = jnp.arange(N_EXPERTS, dtype=jnp.int32)
    used = cnt > 0
    later = jnp.where((eidx[None, :] > eidx[:, None]) & used[None, :], eidx[None, :], N_EXPERTS)
    next_used = jnp.min(later, axis=1)
    next_used = jnp.where(next_used == N_EXPERTS, -1, next_used)
    ordinal = jnp.cumsum(used.astype(jnp.int32)) - 1
    onehot = (tile_expert[:, None] == eidx[None, :]).astype(jnp.int32)
    next_expert = jnp.sum(onehot * next_used[None, :], axis=1)
    slot_parity = jnp.sum(onehot * ordinal[None, :], axis=1) % 2
    pa1, pa2 = _plan(rt_a, seg_start)
    pb1, pb2 = _plan(rt_b, seg_start)
    xs = _dispatch(xa, xb, g, jnp.concatenate([pa1, pb1]), jnp.concatenate([pa2, pb2]), seg_end, cnt,
                   n_rows_used, n_rows_max)
    ys = _experts_sorted(xs, tile_expert, n_tiles_used, next_expert, slot_parity, wg, wu, wd)
    return (_combine(ys, pa1, pa2, xa, route_a, norm_final),
            _combine(ys, pb1, pb2, xb, route_b, norm_final))


def _pad_cols(x, n):
    return jnp.pad(x, ((0, 0), (0, n - x.shape[1])))


def _block_diag_ones(n, blk):
    i = jnp.arange(n) // blk
    return (i[:, None] == i[None, :]).astype(BF16)


def kernel(x_prompt, x_sample, mem_prompt, cache_conv, state_shift, state_rwkv, cache_mem_k, cache_mem_v,
           norm_mix, w_in, conv_w, conv_b, conv_ln_g, conv_ln_b, shift_mu, w_decay_up, decay_bias, w_a_up,
           a_bias, w_g_up, k_k, k_a, r_k, lnx_g, lnx_b, w_out, norm_x, norm_mem, w_cq, w_ck, w_cv, w_co,
           norm_ffn, w_route_group, b_route_group, w_route_expert, b_route_expert, w_gate, w_up, w_down,
           norm_final):
    depth = w_in.shape[0]
    batch, seq, d = x_prompt.shape
    dec_batch = x_sample.shape[0]
    assert depth == 1
    assert x_sample.shape[1] == 1 and seq % CHUNK == 0 and seq >= CONV_K - 1
    cw = conv_w.shape[2]
    rw = w_decay_up.shape[2]
    heads = rw // HEAD
    shift_w = shift_mu.shape[1]
    in_w = w_in.shape[2]
    assert in_w == 2 * cw + shift_w and shift_w == 3 * rw + DECAY_LORA + AAA_LORA + GATE_LORA
    assert cw == rw and rw % LORA_PAD == 0
    in_pad = 2 * cw + 3 * rw + LORA_PAD
    qw = 3 * rw + LORA_PAD

    xp = x_prompt.reshape(batch * seq, d)
    xs = x_sample.reshape(dec_batch, d)
    outs = {k: [] for k in ("conv_p", "shift_p", "rwkv_p", "memk_p", "memv_p", "conv_s", "shift_s", "rwkv_s")}
    bd = _block_diag_ones(2 * LANES, HEAD)

    for l in range(depth):
        w_in_b = w_in.astype(BF16)
        w_out_b = w_out.astype(BF16)
        w_cq_b = w_cq.astype(BF16)
        w_ck_b = w_ck.astype(BF16)
        w_cv_b = w_cv.astype(BF16)
        w_co_b = w_co.astype(BF16)
        zeros_l = jnp.zeros((DECAY_LORA, rw), F32)
        wd_pad = jnp.concatenate([w_decay_up[l], zeros_l], axis=0).astype(BF16)
        wa_pad = jnp.concatenate([zeros_l, w_a_up[l]], axis=0).astype(BF16)
        wg_pad = jnp.pad(w_g_up[l], ((0, 2 * LANES - GATE_LORA), (0, 0))).astype(BF16)
        mu_pad = _pad_cols(shift_mu[l].reshape(1, shift_w), qw)
        vec = lambda x: x.reshape(1, rw)
        pp = (mu_pad, wd_pad, wa_pad, wg_pad, vec(decay_bias[l]), vec(a_bias[l]), vec(k_k[l]), vec(k_a[l]),
              vec(r_k[l]), bd)
        w_route = jnp.concatenate([w_route_expert[l].reshape(d, N_EXPERTS), w_route_group[l]], axis=1)
        w_route = _pad_cols(w_route, LANES)
        wr_hi = w_route.astype(BF16)
        wr_lo = (w_route - wr_hi.astype(F32)).astype(BF16)
        b_route = _pad_cols(jnp.concatenate([b_route_expert[l].reshape(1, N_EXPERTS),
                                             b_route_group[l].reshape(1, N_GROUPS)], axis=1), LANES)
        de = w_gate.shape[-1]
        wg_e = w_gate[l].reshape(N_EXPERTS, d, de)
        wu_e = w_up[l].reshape(N_EXPERTS, d, de)
        wd_e = w_down[l].reshape(N_EXPERTS, de, d)

        proj_s = _norm_matmul(xs, norm_mix[l], w_in_b, l, 128, LORA_PAD)
        c_s, conv_new_s = _conv_decode(proj_s, cache_conv, l, conv_w[l], conv_b[l], conv_ln_g[l],
                                       conv_ln_b[l])
        prep_s = _rwkv_prep_decode(proj_s, _pad_cols(state_shift[l], qw), pp, rw)
        o_s, s_s = _rwkv_step(*prep_s[:6], state_rwkv, l)
        xs = _mix_out(c_s, o_s, prep_s[6], prep_s[7], lnx_g[l], lnx_b[l], bd, w_out_b, l, xs)
        qs = _norm_matmul(xs, norm_x[l], w_cq_b, l, 128, d)
        ctx_s = _attn_decode(qs, cache_mem_k[l], cache_mem_v[l])
        xs = _matmul_res(ctx_s, w_co_b, l, xs, 128)
        outs["conv_s"].append(conv_new_s)
        outs["shift_s"].append(proj_s[:, 2 * cw:2 * cw + shift_w])
        outs["rwkv_s"].append(s_s)

        mem2 = mem_prompt.reshape(batch * N_MEM, d)
        mk, mk_b = _norm_matmul_heads(mem2, norm_mem[l], w_ck_b, l, X_HEADS, 256)
        mv, mv_b = _norm_matmul_heads(mem2, norm_mem[l], w_cv_b, l, X_HEADS, 256)
        proj = _norm_matmul(xp, norm_mix[l], w_in_b, l, 1024, LORA_PAD)
        assert proj.shape[1] == in_pad
        c_p, conv_new = _conv_prefill(proj, jnp.zeros((batch, CONV_K - 1, cw), F32), conv_w[l], conv_b[l],
                                      conv_ln_g[l], conv_ln_b[l], batch, seq)
        prep = _rwkv_prep_prefill(proj, jnp.zeros((batch, qw), F32), pp, batch, seq, rw)
        o_p, s_p = _rwkv_chunked(*prep[:6], jnp.zeros((batch, heads, HEAD, HEAD), F32), batch, seq)
        shift_new = proj.reshape(batch, seq, in_pad)[:, -1, 2 * cw:2 * cw + shift_w]
        xp = _mix_out(c_p, o_p, prep[6], prep[7], lnx_g[l], lnx_b[l], bd, w_out_b, l, xp)
        qx = _norm_matmul(xp, norm_x[l], w_cq_b, l, 512, d)
        xp = _attn_prefill(qx, mk_b, mv_b, w_co_b, l, xp, batch, seq)
        outs["conv_p"].append(conv_new)
        outs["shift_p"].append(shift_new)
        outs["rwkv_p"].append(s_p)
        outs["memk_p"].append(mk.reshape(batch, N_MEM, X_HEADS, d // X_HEADS))
        outs["memv_p"].append(mv.reshape(batch, N_MEM, X_HEADS, d // X_HEADS))

        xp, xs = _moe_sorted(xp, xs, norm_ffn[l], wr_hi, wr_lo, b_route, wg_e, wu_e, wd_e, norm_final)

    y_prompt = xp.reshape(batch, seq, d)
    y_sample = xs.reshape(dec_batch, 1, d)
    st = lambda k: jnp.stack(outs[k])
    return (y_prompt, y_sample, st("conv_p"), st("shift_p"), st("rwkv_p"), st("memk_p"), st("memv_p"),
            st("conv_s"), st("shift_s"), st("rwkv_s"))
```
